```python
import math
import jax
import jax.numpy as jnp
from jax import lax
import numpy as np

D_MODEL = 1024
BATCH = 32
SEQ = 256
DEPTH = 2
DEC_BATCH = 2
DEC_SEQ = 2048
PAST_LEN = 512

GRID_W = 64
ATT_HEADS = 4
QK_DIM = 64
V_DIM = 2 * QK_DIM
ATT_Q_W = ATT_HEADS * 2 * QK_DIM
ATT_V_W = ATT_HEADS * V_DIM
CONV_CH = D_MODEL - ATT_V_W
CONV_WIDTH = 31
IN_W = 2 * ATT_Q_W + ATT_V_W + 2 * CONV_CH
D_FF = 2816
N_EXPERTS = 8
TOP_K = 2
D_FF_EXPERT = 3584
N_DENSE = (DEPTH + 1) // 2
N_MOE = DEPTH // 2
ROPE_THETA = 10000.0
Q_BLOCK = 128
NORM_EPS = 1e-6
LN_EPS = 1e-5

kernel_name = 'diffattn_conformer_prefix_flow_step'


def rms_norm(x, g):
    xf = x.astype(jnp.float32)
    y = xf * lax.rsqrt(jnp.mean(xf * xf, axis=-1, keepdims=True) + NORM_EPS)
    return (y * g.astype(jnp.float32)).astype(x.dtype)


def layer_norm(x, g, b):
    xf = x.astype(jnp.float32)
    mu = jnp.mean(xf, axis=-1, keepdims=True)
    var = jnp.mean(jnp.square(xf - mu), axis=-1, keepdims=True)
    y = (xf - mu) * lax.rsqrt(var + LN_EPS)
    return (y * g.astype(jnp.float32) + b.astype(jnp.float32)).astype(x.dtype)


def axial_rope(length, dtype):
    rows = length // GRID_W
    row_pos = jnp.repeat(jnp.arange(rows, dtype=jnp.float32), GRID_W)
    col_pos = jnp.tile(jnp.arange(GRID_W, dtype=jnp.float32), rows)
    half = QK_DIM // 2
    inv_freq = 1.0 / (ROPE_THETA ** (jnp.arange(0, half, 2, dtype=jnp.float32) / half))
    ang_r = row_pos[:, None] * inv_freq
    ang_c = col_pos[:, None] * inv_freq
    ang = jnp.concatenate([ang_r, ang_r, ang_c, ang_c], axis=-1)
    return jnp.cos(ang).astype(dtype), jnp.sin(ang).astype(dtype)


def _rotate_half(p):
    p1, p2 = jnp.split(p, 2, axis=-1)
    return jnp.concatenate([-p2, p1], axis=-1)


def apply_axial_rope(x, cos, sin):
    half = QK_DIM // 2
    rot = jnp.concatenate([_rotate_half(x[..., :half]), _rotate_half(x[..., half:])], axis=-1)
    return x * cos[None, :, None, None, :] + rot * sin[None, :, None, None, :]


def differential_attention(q, k, v, lam):
    bsz, lq = q.shape[0], q.shape[1]
    n_blocks = lq // Q_BLOCK
    qb = jnp.moveaxis(q.reshape(bsz, n_blocks, Q_BLOCK, ATT_HEADS, 2, QK_DIM), 1, 0)
    scale = QK_DIM ** -0.5

    def one_block(q_blk):
        s = jnp.einsum('bqhmd,bkhmd->bhmqk', q_blk, k).astype(jnp.float32) * scale
        p = jax.nn.softmax(s, axis=-1)
        a = p[:, :, 0] - lam * p[:, :, 1]
        return jnp.einsum('bhqk,bkhe->bqhe', a.astype(v.dtype), v)

    o = lax.map(one_block, qb)
    return jnp.moveaxis(o, 0, 1).reshape(bsz, lq, ATT_HEADS, V_DIM)


def conformer_conv(g, dw_w, dw_b, ln_g, ln_b):
    a, b = jnp.split(g, 2, axis=-1)
    u = a * jax.nn.sigmoid(b)
    pad = CONV_WIDTH // 2
    y = lax.conv_general_dilated(u, dw_w[:, None, :].astype(u.dtype), window_strides=(1,),
                                 padding=[(pad, pad)], dimension_numbers=('NWC', 'WIO', 'NWC'),
                                 feature_group_count=CONV_CH)
    y = y + dw_b
    return jax.nn.silu(layer_norm(y, ln_g, ln_b))


def token_mixer(h, lw, rope, k_ext, v_ext):
    bsz, length, _ = h.shape
    proj = h @ lw['w_in']
    q, k, v, g = jnp.split(proj, [ATT_Q_W, 2 * ATT_Q_W, 2 * ATT_Q_W + ATT_V_W], axis=-1)
    q = q.reshape(bsz, length, ATT_HEADS, 2, QK_DIM)
    k = k.reshape(bsz, length, ATT_HEADS, 2, QK_DIM)
    v = v.reshape(bsz, length, ATT_HEADS, V_DIM)
    if rope is not None:
        q = apply_axial_rope(q, rope[0], rope[1])
        k = apply_axial_rope(k, rope[0], rope[1])
    k_out = k.reshape(bsz, length, ATT_HEADS, 2 * QK_DIM)
    if k_ext is not None:
        k_all = jnp.concatenate([k, k_ext.reshape(bsz, -1, ATT_HEADS, 2, QK_DIM)], axis=1)
        v_all = jnp.concatenate([v, v_ext], axis=1)
    else:
        k_all, v_all = k, v
    o = differential_attention(q, k_all, v_all, lw['lam'])
    o = rms_norm(o, lw['subln_g']) * (1.0 - lw['lam_init'])
    conv_out = conformer_conv(g, lw['dw_w'], lw['dw_b'], lw['ln_g'], lw['ln_b'])
    mixed = jnp.concatenate([o.reshape(bsz, length, ATT_V_W), conv_out], axis=-1)
    return mixed @ lw['w_out'], k_out, v


def swiglu(x, w13, w2):
    gt, up = jnp.split(x @ w13, 2, axis=-1)
    return (jax.nn.silu(gt) * up) @ w2


def moe_swiglu(x, router_w, w13, w2):
    bsz, length, d = x.shape
    xf = x.reshape(bsz * length, d)
    logits = (xf @ router_w).astype(jnp.float32)
    top_v, top_i = lax.top_k(logits, TOP_K)
    wts = jax.nn.softmax(top_v, axis=-1)
    gates = jnp.sum(jax.nn.one_hot(top_i, N_EXPERTS, dtype=jnp.float32) * wts[..., None], axis=1)
    out = jnp.zeros_like(xf)
    for e in range(N_EXPERTS):
        out = out + gates[:, e:e + 1].astype(x.dtype) * swiglu(xf, w13[e], w2[e])
    return out.reshape(bsz, length, d)


def ada_modulation(cond, w_ada, b_ada):
    m = jax.nn.silu(cond) @ w_ada + b_ada
    return [t[:, None, :] for t in jnp.split(m, 6, axis=-1)]


def trunk_layer(x, cond, lw, ffn_fn, rope, k_ext, v_ext):
    sh1, sc1, g1, sh2, sc2, g2 = ada_modulation(cond, lw['w_ada'], lw['b_ada'])
    h = rms_norm(x, lw['norm_g'][0]) * (1 + sc1) + sh1
    m, k_out, v_out = token_mixer(h, lw, rope, k_ext, v_ext)
    x = x + g1 * rms_norm(m, lw['norm_g'][1])
    h = rms_norm(x, lw['norm_g'][2]) * (1 + sc2) + sh2
    x = x + g2 * rms_norm(ffn_fn(h), lw['norm_g'][3])
    return x, k_out, v_out


def setup_inputs(seed: int = 0) -> dict:
    key = jax.random.key(seed)
    ks = jax.random.split(key, 22)

    def nrm(k, shape, scale):
        return jax.random.normal(k, shape, jnp.float32) * scale

    return {
        'x_prompt': nrm(ks[0], (BATCH, SEQ, D_MODEL), 1.0),
        'x_sample': nrm(ks[1], (DEC_BATCH, DEC_SEQ, D_MODEL), 1.0),
        'cache_k': nrm(ks[2], (DEC_BATCH, DEPTH, PAST_LEN, ATT_HEADS, 2 * QK_DIM), 1.0),
        'cache_v': nrm(ks[3], (DEC_BATCH, DEPTH, PAST_LEN, ATT_HEADS, V_DIM), 1.0),
        'c': nrm(ks[4], (DEC_BATCH, D_MODEL), 1.0),
        'c_ctx': nrm(ks[5], (D_MODEL,), 1.0),
        'w_ada': nrm(ks[6], (DEPTH, D_MODEL, 6 * D_MODEL), 0.3 * D_MODEL ** -0.5),
        'b_ada': nrm(ks[7], (DEPTH, 6 * D_MODEL), 0.02),
        'norm_g': 1.0 + nrm(ks[8], (DEPTH, 4, D_MODEL), 0.05),
        'w_in': nrm(ks[9], (DEPTH, D_MODEL, IN_W), D_MODEL ** -0.5),
        'w_out': nrm(ks[10], (DEPTH, D_MODEL, D_MODEL), D_MODEL ** -0.5),
        'lam_params': nrm(ks[11], (DEPTH, 4, QK_DIM), 0.1),
        'subln_g': 1.0 + nrm(ks[12], (DEPTH, V_DIM), 0.05),
        'dw_weight': nrm(ks[13], (DEPTH, CONV_WIDTH, CONV_CH), CONV_WIDTH ** -0.5),
        'dw_bias': nrm(ks[14], (DEPTH, CONV_CH), 0.02),
        'conv_ln_g': 1.0 + nrm(ks[15], (DEPTH, CONV_CH), 0.05),
        'conv_ln_b': nrm(ks[16], (DEPTH, CONV_CH), 0.02),
        'dense_w13': nrm(ks[17], (N_DENSE, D_MODEL, 2 * D_FF), D_MODEL ** -0.5),
        'dense_w2': nrm(ks[18], (N_DENSE, D_FF, D_MODEL), D_FF ** -0.5),
        'router_w': nrm(ks[19], (N_MOE, D_MODEL, N_EXPERTS), D_MODEL ** -0.5),
        'moe_w13': nrm(ks[20], (N_MOE, N_EXPERTS, D_MODEL, 2 * D_FF_EXPERT), D_MODEL ** -0.5),
        'moe_w2': nrm(ks[21], (N_MOE, N_EXPERTS, D_FF_EXPERT, D_MODEL), D_FF_EXPERT ** -0.5),
    }


def reference(x_prompt, x_sample, cache_k, cache_v, c, c_ctx, w_ada, b_ada, norm_g, w_in, w_out,
              lam_params, subln_g, dw_weight, dw_bias, conv_ln_g, conv_ln_b, dense_w13, dense_w2,
              router_w, moe_w13, moe_w2):
    rope = axial_rope(x_sample.shape[1], x_sample.dtype)
    yp, ys = x_prompt, x_sample
    new_k, new_v = [], []
    for l in range(DEPTH):
        lam_init = 0.8 - 0.6 * math.exp(-0.3 * l)
        lp = lam_params[l].astype(jnp.float32)
        lam = jnp.exp(jnp.sum(lp[0] * lp[1])) - jnp.exp(jnp.sum(lp[2] * lp[3])) + lam_init
        lw = {'w_ada': w_ada[l], 'b_ada': b_ada[l], 'norm_g': norm_g[l], 'w_in': w_in[l],
              'w_out': w_out[l], 'lam': lam, 'lam_init': lam_init, 'subln_g': subln_g[l],
              'dw_w': dw_weight[l], 'dw_b': dw_bias[l], 'ln_g': conv_ln_g[l], 'ln_b': conv_ln_b[l]}
        i = l // 2
        if l % 2 == 0:
            ffn_fn = functools_partial_dense(dense_w13[i], dense_w2[i])
        else:
            ffn_fn = functools_partial_moe(router_w[i], moe_w13[i], moe_w2[i])
        yp, k_ctx, v_ctx = trunk_layer(yp, c_ctx[None, :], lw, ffn_fn, None, None, None)
        new_k.append(k_ctx)
        new_v.append(v_ctx)
        ys, _, _ = trunk_layer(ys, c, lw, ffn_fn, rope, cache_k[:, l], cache_v[:, l])
    new_cache_k = jnp.stack(new_k, axis=1)
    new_cache_v = jnp.stack(new_v, axis=1)
    return (yp, ys, new_cache_k, new_cache_v)


def functools_partial_dense(w13, w2):
    def fn(h):
        return swiglu(h, w13, w2)
    return fn


def functools_partial_moe(rw, w13, w2):
    def fn(h):
        return moe_swiglu(h, rw, w13, w2)
    return fn
```

```python
import functools
import math

import jax
import jax.numpy as jnp
from jax import lax
from jax.experimental import pallas as pl
from jax.experimental.pallas import tpu as pltpu

F32 = jnp.float32
BF16 = jnp.bfloat16

D_MODEL = 1024
BATCH = 32
SEQ = 256
DEPTH = 2
DEC_BATCH = 2
DEC_SEQ = 2048
PAST_LEN = 512
GRID_W = 64
ATT_HEADS = 4
QK_DIM = 64
V_DIM = 128
ATT_W = ATT_HEADS * V_DIM
IN_W = 5 * ATT_W
CONV_CH = 512
CONV_WIDTH = 31
D_FF = 2816
N_EXPERTS = 8
D_FF_EXPERT = 3584
ROPE_THETA = 10000.0
NORM_EPS = 1e-6
LN_EPS = 1e-5

N_PROMPT = BATCH * SEQ
N_SAMPLE = DEC_BATCH * DEC_SEQ
N_TOK = N_PROMPT + N_SAMPLE

TOK_TILE = 512
N_TILES = N_TOK // TOK_TILE
P_TILES = N_PROMPT // TOK_TILE
S_TILES_PER_BATCH = DEC_SEQ // TOK_TILE

ROW_TILE = 512
SORT_TILES = (2 * N_TOK + N_EXPERTS * ROW_TILE) // ROW_TILE + 1
SORT_ROWS = SORT_TILES * ROW_TILE
UP_TILE = 896
UP_TILES = D_FF_EXPERT // UP_TILE
DOWN_TILE = 512
WIN = 128
N_WIN = 16

VMEM_LIMIT = 56 * 1024 * 1024


def _cparams(n_axes):
    return pltpu.CompilerParams(dimension_semantics=("arbitrary",) * n_axes,
                                vmem_limit_bytes=VMEM_LIMIT)


def _cond_id(i):
    return jnp.where(i < P_TILES, 0, 1 + (i - P_TILES) // S_TILES_PER_BATCH)


def _sigmoid(x):
    return 1.0 / (1.0 + jnp.exp(-x))


def _rms(x, g):
    return x * lax.rsqrt(jnp.mean(x * x, axis=-1, keepdims=True) + NORM_EPS) * g


def _ada_kernel(c_ref, w_ref, b_ref, o_ref):
    c = c_ref[...]
    s = (c * _sigmoid(c)).astype(BF16)
    o_ref[...] = jnp.dot(s, w_ref[...].astype(BF16), preferred_element_type=F32) + b_ref[...]


def _ada_modulation(cond8, w_ada, b_ada):
    tn = 1536
    n = 6 * D_MODEL
    return pl.pallas_call(
        _ada_kernel,
        out_shape=jax.ShapeDtypeStruct((DEPTH, 8, n), F32),
        grid=(DEPTH, n // tn),
        in_specs=[pl.BlockSpec((8, D_MODEL), lambda l, j: (0, 0)),
                  pl.BlockSpec((None, D_MODEL, tn), lambda l, j: (l, 0, j)),
                  pl.BlockSpec((None, 1, tn), lambda l, j: (l, 0, j))],
        out_specs=pl.BlockSpec((None, 8, tn), lambda l, j: (l, 0, j)),
        compiler_params=_cparams(2),
        name="ada_modulation",
    )(cond8, w_ada, b_ada.reshape(DEPTH, 1, n))


def _inproj_kernel(x_ref, mod_ref, g_ref, w_ref, cos_ref, sina_ref, sinb_ref,
                   qkvg_ref, kc_ref, vc_ref, wbf_ref):
    i = pl.program_id(0)

    @pl.when(i == 0)
    def _():
        wbf_ref[...] = w_ref[...].astype(BF16)

    h = _rms(x_ref[...], g_ref[0:1, :]) * (1.0 + mod_ref[1:2, :]) + mod_ref[0:1, :]
    proj = jnp.dot(h.astype(BF16), wbf_ref[...], preferred_element_type=F32)
    qkvg_ref[:, 2 * ATT_W:] = proj[:, 2 * ATT_W:].astype(BF16)

    @pl.when(i < P_TILES)
    def _():
        qkvg_ref[:, :2 * ATT_W] = proj[:, :2 * ATT_W].astype(BF16)
        kc_ref[...] = proj[:, ATT_W:2 * ATT_W]
        vc_ref[...] = proj[:, 2 * ATT_W:3 * ATT_W]

    @pl.when(i >= P_TILES)
    def _():
        cos = cos_ref[...]
        sina = sina_ref[...]
        sinb = sinb_ref[...]
        for c in range(2 * ATT_W // 128):
            xg = proj[:, 128 * c:128 * (c + 1)]
            fwd = pltpu.roll(xg, 128 - 16, 1)
            bwd = pltpu.roll(xg, 16, 1)
            qkvg_ref[:, 128 * c:128 * (c + 1)] = (xg * cos + fwd * sina + bwd * sinb).astype(BF16)


def _inproj(l, x, mod, norm_g, w_in, rope):
    cos, sina, sinb = rope

    def rope_idx(i):
        return (jnp.maximum(i - P_TILES, 0) % S_TILES_PER_BATCH, 0)

    def cache_idx(i):
        return (jnp.minimum(i, P_TILES - 1), 0)

    return pl.pallas_call(
        _inproj_kernel,
        out_shape=(jax.ShapeDtypeStruct((N_TOK, IN_W), BF16),
                   jax.ShapeDtypeStruct((N_PROMPT, ATT_W), F32),
                   jax.ShapeDtypeStruct((N_PROMPT, ATT_W), F32)),
        grid=(N_TILES,),
        in_specs=[pl.BlockSpec((TOK_TILE, D_MODEL), lambda i: (i, 0)),
                  pl.BlockSpec((None, None, 6, D_MODEL), lambda i: (l, _cond_id(i), 0, 0)),
                  pl.BlockSpec((None, 4, D_MODEL), lambda i: (l, 0, 0)),
                  pl.BlockSpec((None, D_MODEL, IN_W), lambda i: (l, 0, 0),
                               pipeline_mode=pl.Buffered(1)),
                  pl.BlockSpec((TOK_TILE, 128), rope_idx),
                  pl.BlockSpec((TOK_TILE, 128), rope_idx),
                  pl.BlockSpec((TOK_TILE, 128), rope_idx)],
        out_specs=(pl.BlockSpec((TOK_TILE, IN_W), lambda i: (i, 0)),
                   pl.BlockSpec((TOK_TILE, ATT_W), cache_idx),
                   pl.BlockSpec((TOK_TILE, ATT_W), cache_idx)),
        scratch_shapes=[pltpu.VMEM((D_MODEL, IN_W), BF16)],
        compiler_params=_cparams(1),
        name=f"inproj_{l}",
    )(x, mod, norm_g, w_in, cos, sina, sinb)


def _rope_tables():
    rows = DEC_SEQ // GRID_W
    row_pos = jnp.repeat(jnp.arange(rows, dtype=F32), GRID_W)
    col_pos = jnp.tile(jnp.arange(GRID_W, dtype=F32), rows)
    half = QK_DIM // 2
    inv_freq = 1.0 / (ROPE_THETA ** (jnp.arange(0, half, 2, dtype=F32) / half))
    ang_r = row_pos[:, None] * inv_freq
    ang_c = col_pos[:, None] * inv_freq
    ang = jnp.concatenate([ang_r, ang_r, ang_c, ang_c], axis=-1)
    cos = jnp.tile(jnp.cos(ang), (1, 2))
    sin = jnp.tile(jnp.sin(ang), (1, 2))
    first = (jnp.arange(128) % 32) < 16
    sina = jnp.where(first[None, :], -sin, 0.0)
    sinb = jnp.where(first[None, :], 0.0, sin)
    return cos, sina, sinb


def _attn_kernel(*refs, lam_init, has_ext):
    if has_ext:
        lamp_ref, sub_ref, q_ref, k_ref, v_ref, ke_ref, ve_ref, o_ref = refs
    else:
        lamp_ref, sub_ref, q_ref, k_ref, v_ref, o_ref = refs
    lp = lamp_ref[...]
    lam = (jnp.exp(jnp.sum(lp[0:1] * lp[1:2], axis=-1, keepdims=True))
           - jnp.exp(jnp.sum(lp[2:3] * lp[3:4], axis=-1, keepdims=True)) + lam_init)
    lane = lax.broadcasted_iota(jnp.int32, (1, V_DIM), 1)
    nt = (((1,), (1,)), ((), ()))
    scale = QK_DIM ** -0.5
    map_scale = [jnp.where(lane < QK_DIM, scale, 0.0).astype(BF16),
                 jnp.where(lane < QK_DIM, 0.0, scale).astype(BF16)]

    for hd in range(ATT_HEADS):
        cols = slice(V_DIM * hd, V_DIM * (hd + 1))
        qh = q_ref[:, cols]
        kh = k_ref[:, cols]
        vh = v_ref[:, cols]
        if has_ext:
            keh = ke_ref[:, cols].astype(BF16)
            veh = ve_ref[:, cols].astype(BF16)
        acc = None
        for m in range(2):
            qm = qh * map_scale[m]
            s = lax.dot_general(qm, kh, nt, preferred_element_type=F32)
            mx = jnp.max(s, axis=-1, keepdims=True)
            if has_ext:
                se = lax.dot_general(qm, keh, nt, preferred_element_type=F32)
                mx = jnp.maximum(mx, jnp.max(se, axis=-1, keepdims=True))
            p = jnp.exp(s - mx)
            den = jnp.sum(p, axis=-1, keepdims=True)
            if has_ext:
                pe = jnp.exp(se - mx)
                den = den + jnp.sum(pe, axis=-1, keepdims=True)
            coef = 1.0 / den if m == 0 else -lam / den
            contrib = p * coef
            acc = contrib if acc is None else acc + contrib
            if has_ext:
                contrib_e = pe * coef
                acc_e = contrib_e if m == 0 else acc_e + contrib_e
        o = jnp.dot(acc.astype(BF16), vh, preferred_element_type=F32)
        if has_ext:
            o = o + jnp.dot(acc_e.astype(BF16), veh, preferred_element_type=F32)
        o = _rms(o, sub_ref[...]) * (1.0 - lam_init)
        o_ref[:, cols] = o.astype(BF16)


def _attention(l, lam_init, lam_params, subln_g, qkvg, cache_k, cache_v):
    small = [pl.BlockSpec((None, 4, QK_DIM), lambda *_: (l, 0, 0)),
             pl.BlockSpec((None, 1, V_DIM), lambda *_: (l, 0, 0))]
    sub3 = subln_g.reshape(DEPTH, 1, V_DIM)

    o_ctx = pl.pallas_call(
        functools.partial(_attn_kernel, lam_init=lam_init, has_ext=False),
        out_shape=jax.ShapeDtypeStruct((N_PROMPT, ATT_W), BF16),
        grid=(BATCH,),
        in_specs=small + [pl.BlockSpec((SEQ, ATT_W), lambda b: (b, 0)),
                          pl.BlockSpec((SEQ, ATT_W), lambda b: (b, 1)),
                          pl.BlockSpec((SEQ, ATT_W), lambda b: (b, 2))],
        out_specs=pl.BlockSpec((SEQ, ATT_W), lambda b: (b, 0)),
        compiler_params=_cparams(1),
        name=f"attn_ctx_{l}",
    )(lam_params, sub3, qkvg, qkvg, qkvg)

    tq = 256
    q_tiles = DEC_SEQ // tq
    q_base = N_PROMPT // tq
    kv_base = N_PROMPT // DEC_SEQ

    def q_idx(b, i):
        return (q_base + b * q_tiles + i, 0)

    o_lat = pl.pallas_call(
        functools.partial(_attn_kernel, lam_init=lam_init, has_ext=True),
        out_shape=jax.ShapeDtypeStruct((N_SAMPLE, ATT_W), BF16),
        grid=(DEC_BATCH, q_tiles),
        in_specs=small + [pl.BlockSpec((tq, ATT_W), q_idx),
                          pl.BlockSpec((DEC_SEQ, ATT_W), lambda b, i: (kv_base + b, 1)),
                          pl.BlockSpec((DEC_SEQ, ATT_W), lambda b, i: (kv_base + b, 2)),
                          pl.BlockSpec((None, None, PAST_LEN, ATT_W), lambda b, i: (b, l, 0, 0)),
                          pl.BlockSpec((None, None, PAST_LEN, ATT_W), lambda b, i: (b, l, 0, 0))],
        out_specs=pl.BlockSpec((tq, ATT_W), lambda b, i: (b * q_tiles + i, 0)),
        compiler_params=_cparams(2),
        name=f"attn_lat_{l}",
    )(lam_params, sub3, qkvg, qkvg, qkvg, cache_k, cache_v)
    return o_ctx, o_lat


CONV_TILE = 256
CONV_HALO = 16
CONV_SUB = 64
CONV_TILES = N_TOK // CONV_TILE
CONV_P_TILES = N_PROMPT // CONV_TILE
CONV_S_TILES = DEC_SEQ // CONV_TILE


def _glu(a_ref, b_ref):
    return a_ref[...].astype(F32) * _sigmoid(b_ref[...].astype(F32))


def _conv_kernel(a_ref, b_ref, ap_ref, bp_ref, an_ref, bn_ref, w_ref, bias_ref, lg_ref, lb_ref,
                 o_ref, upad_ref):
    i = pl.program_id(0)
    q = (i - CONV_P_TILES) % CONV_S_TILES
    latent = i >= CONV_P_TILES
    has_prev = jnp.logical_and(latent, q != 0)
    has_next = jnp.logical_and(latent, q != CONV_S_TILES - 1)
    upad_ref[0:CONV_HALO, :] = jnp.where(has_prev, _glu(ap_ref, bp_ref), 0.0)
    upad_ref[CONV_HALO:CONV_HALO + CONV_TILE, :] = _glu(a_ref, b_ref)
    upad_ref[CONV_HALO + CONV_TILE:, :] = jnp.where(has_next, _glu(an_ref, bn_ref), 0.0)
    first_tap = CONV_HALO - CONV_WIDTH // 2

    for t in range(CONV_TILE // CONV_SUB):
        base = t * CONV_SUB
        acc = jnp.zeros((CONV_SUB, CONV_CH), F32)
        for j in range(CONV_WIDTH):
            lo = base + first_tap + j
            acc = acc + upad_ref[lo:lo + CONV_SUB, :] * w_ref[j:j + 1, :]
        y = acc + bias_ref[...]
        mu = jnp.mean(y, axis=-1, keepdims=True)
        yc = y - mu
        var = jnp.mean(yc * yc, axis=-1, keepdims=True)
        z = yc * lax.rsqrt(var + LN_EPS) * lg_ref[...] + lb_ref[...]
        o_ref[base:base + CONV_SUB, :] = (z * _sigmoid(z)).astype(BF16)


def _conformer_conv(l, qkvg, dw_w, dw_b, ln_g, ln_b):
    def vec(a):
        return a.reshape(DEPTH, 1, CONV_CH)

    per = CONV_TILE // CONV_HALO
    last = N_TOK // CONV_HALO - 1

    def prev_idx(col):
        return lambda i: (jnp.maximum(i * per - 1, 0), col)

    def next_idx(col):
        return lambda i: (jnp.minimum((i + 1) * per, last), col)

    return pl.pallas_call(
        _conv_kernel,
        out_shape=jax.ShapeDtypeStruct((N_TOK, CONV_CH), BF16),
        grid=(CONV_TILES,),
        in_specs=[pl.BlockSpec((CONV_TILE, ATT_W), lambda i: (i, 3)),
                  pl.BlockSpec((CONV_TILE, ATT_W), lambda i: (i, 4)),
                  pl.BlockSpec((CONV_HALO, ATT_W), prev_idx(3)),
                  pl.BlockSpec((CONV_HALO, ATT_W), prev_idx(4)),
                  pl.BlockSpec((CONV_HALO, ATT_W), next_idx(3)),
                  pl.BlockSpec((CONV_HALO, ATT_W), next_idx(4)),
                  pl.BlockSpec((None, CONV_WIDTH, CONV_CH), lambda i: (l, 0, 0)),
                  pl.BlockSpec((None, 1, CONV_CH), lambda i: (l, 0, 0)),
                  pl.BlockSpec((None, 1, CONV_CH), lambda i: (l, 0, 0)),
                  pl.BlockSpec((None, 1, CONV_CH), lambda i: (l, 0, 0))],
        out_specs=pl.BlockSpec((CONV_TILE, CONV_CH), lambda i: (i, 0)),
        scratch_shapes=[pltpu.VMEM((CONV_TILE + 2 * CONV_HALO, CONV_CH), F32)],
        compiler_params=_cparams(1),
        name=f"conv_{l}",
    )(qkvg, qkvg, qkvg, qkvg, qkvg, qkvg, dw_w, vec(dw_b), vec(ln_g), vec(ln_b))


def _outproj_kernel(oc_ref, ol_ref, cv_ref, x_ref, mod_ref, g_ref, w_ref, xo_ref, h2_ref, wbf_ref):
    i = pl.program_id(0)

    @pl.when(i == 0)
    def _():
        wbf_ref[...] = w_ref[...].astype(BF16)

    o = jnp.where(i < P_TILES, oc_ref[...], ol_ref[...])
    m = (jnp.dot(o, wbf_ref[0:ATT_W, :], preferred_element_type=F32)
         + jnp.dot(cv_ref[...], wbf_ref[ATT_W:, :], preferred_element_type=F32))
    xn = x_ref[...] + mod_ref[2:3, :] * _rms(m, g_ref[1:2, :])
    xo_ref[...] = xn
    h2_ref[...] = (_rms(xn, g_ref[2:3, :]) * (1.0 + mod_ref[4:5, :]) + mod_ref[3:4, :]).astype(BF16)


def _outproj(l, o_ctx, o_lat, cv, x, mod, norm_g, w_out):
    return pl.pallas_call(
        _outproj_kernel,
        out_shape=(jax.ShapeDtypeStruct((N_TOK, D_MODEL), F32),
                   jax.ShapeDtypeStruct((N_TOK, D_MODEL), BF16)),
        grid=(N_TILES,),
        in_specs=[pl.BlockSpec((TOK_TILE, ATT_W), lambda i: (jnp.minimum(i, P_TILES - 1), 0)),
                  pl.BlockSpec((TOK_TILE, ATT_W), lambda i: (jnp.maximum(i - P_TILES, 0), 0)),
                  pl.BlockSpec((TOK_TILE, CONV_CH), lambda i: (i, 0)),
                  pl.BlockSpec((TOK_TILE, D_MODEL), lambda i: (i, 0)),
                  pl.BlockSpec((None, None, 6, D_MODEL), lambda i: (l, _cond_id(i), 0, 0)),
                  pl.BlockSpec((None, 4, D_MODEL), lambda i: (l, 0, 0)),
                  pl.BlockSpec((None, D_MODEL, D_MODEL), lambda i: (l, 0, 0),
                               pipeline_mode=pl.Buffered(1))],
        out_specs=(pl.BlockSpec((TOK_TILE, D_MODEL), lambda i: (i, 0)),
                   pl.BlockSpec((TOK_TILE, D_MODEL), lambda i: (i, 0))),
        scratch_shapes=[pltpu.VMEM((D_MODEL, D_MODEL), BF16)],
        compiler_params=_cparams(1),
        name=f"outproj_{l}",
    )(o_ctx, o_lat, cv, x, mod, norm_g, w_out)


DENSE_CHUNK = D_FF // 2


def _dense_ffn_kernel(h_ref, x_ref, mod_ref, g_ref, w13_ref, w2_ref, xo_ref):
    h = h_ref[...]
    acc = None
    for c in range(D_FF // DENSE_CHUNK):
        lo = c * DENSE_CHUNK
        gt = jnp.dot(h, w13_ref[:, lo:lo + DENSE_CHUNK], preferred_element_type=F32)
        up = jnp.dot(h, w13_ref[:, D_FF + lo:D_FF + lo + DENSE_CHUNK], preferred_element_type=F32)
        a = (gt * _sigmoid(gt) * up).astype(BF16)
        part = jnp.dot(a, w2_ref[lo:lo + DENSE_CHUNK, :], preferred_element_type=F32)
        acc = part if acc is None else acc + part
    xo_ref[...] = x_ref[...] + mod_ref[5:6, :] * _rms(acc, g_ref[3:4, :])


def _dense_ffn(l, h2, x, mod, norm_g, w13, w2):
    return pl.pallas_call(
        _dense_ffn_kernel,
        out_shape=jax.ShapeDtypeStruct((N_TOK, D_MODEL), F32),
        grid=(N_TILES,),
        in_specs=[pl.BlockSpec((TOK_TILE, D_MODEL), lambda i: (i, 0)),
                  pl.BlockSpec((TOK_TILE, D_MODEL), lambda i: (i, 0)),
                  pl.BlockSpec((None, None, 6, D_MODEL), lambda i: (l, _cond_id(i), 0, 0)),
                  pl.BlockSpec((None, 4, D_MODEL), lambda i: (l, 0, 0)),
                  pl.BlockSpec((D_MODEL, 2 * D_FF), lambda i: (0, 0), pipeline_mode=pl.Buffered(1)),
                  pl.BlockSpec((D_FF, D_MODEL), lambda i: (0, 0), pipeline_mode=pl.Buffered(1))],
        out_specs=pl.BlockSpec((TOK_TILE, D_MODEL), lambda i: (i, 0)),
        compiler_params=_cparams(1),
        name=f"dense_ffn_{l}",
    )(h2, x, mod, norm_g, w13, w2)


def _router_kernel(h_ref, rw_ref, info_ref, cb_ref, tot_ref, carry_ref):
    i = pl.program_id(0)

    @pl.when(i == 0)
    def _():
        carry_ref[...] = jnp.zeros_like(carry_ref)

    lane = lax.broadcasted_iota(jnp.int32, (TOK_TILE, 128), 1)
    lanef = lane.astype(F32)
    logits = jnp.dot(h_ref[...], rw_ref[...], preferred_element_type=F32)
    logits = jnp.where(lane < N_EXPERTS, logits, -jnp.inf)
    big = jnp.asarray(128.0, F32)
    m1 = jnp.max(logits, axis=-1, keepdims=True)
    e1 = jnp.min(jnp.where(logits == m1, lanef, big), axis=-1, keepdims=True)
    oh1 = lanef == e1
    rest = jnp.where(oh1, -jnp.inf, logits)
    m2 = jnp.max(rest, axis=-1, keepdims=True)
    e2 = jnp.min(jnp.where(rest == m2, lanef, big), axis=-1, keepdims=True)
    oh2 = lanef == e2
    ex = jnp.exp(m2 - m1)
    g1 = 1.0 / (1.0 + ex)
    g2 = ex / (1.0 + ex)

    oh = jnp.where(jnp.logical_or(oh1, oh2), 1.0, 0.0)
    r = lax.broadcasted_iota(jnp.int32, (TOK_TILE, TOK_TILE), 0)
    c = lax.broadcasted_iota(jnp.int32, (TOK_TILE, TOK_TILE), 1)
    tri = jnp.where(c <= r, 1.0, 0.0).astype(BF16)
    incl = jnp.dot(tri, oh.astype(BF16), preferred_element_type=F32)
    carry = carry_ref[0:1, :]
    excl = incl - oh + carry
    rank1 = jnp.sum(jnp.where(oh1, excl, 0.0), axis=-1, keepdims=True)
    rank2 = jnp.sum(jnp.where(oh2, excl, 0.0), axis=-1, keepdims=True)

    info = jnp.where(lane == 0, e1, 0.0)
    for k, col in enumerate((e2, rank1, rank2, g1, g2), start=1):
        info = jnp.where(lane == k, col, info)
    info_ref[...] = info

    cb_ref[...] = carry_ref[...]
    new_carry = carry + incl[TOK_TILE - 1:TOK_TILE, :]
    carry_ref[...] = jnp.broadcast_to(new_carry, carry_ref.shape)
    tot_ref[...] = jnp.broadcast_to(new_carry, tot_ref.shape)


def _router(h2, rw_pad):
    return pl.pallas_call(
        _router_kernel,
        out_shape=(jax.ShapeDtypeStruct((N_TOK, 128), F32),
                   jax.ShapeDtypeStruct((N_TILES, 8, 128), F32),
                   jax.ShapeDtypeStruct((8, 128), F32)),
        grid=(N_TILES,),
        in_specs=[pl.BlockSpec((TOK_TILE, D_MODEL), lambda i: (i, 0)),
                  pl.BlockSpec((D_MODEL, 128), lambda i: (0, 0))],
        out_specs=(pl.BlockSpec((TOK_TILE, 128), lambda i: (i, 0)),
                   pl.BlockSpec((None, 8, 128), lambda i: (i, 0, 0)),
                   pl.BlockSpec((8, 128), lambda i: (0, 0))),
        scratch_shapes=[pltpu.VMEM((8, 128), F32)],
        compiler_params=_cparams(1),
        name="moe_router",
    )(h2, rw_pad)


def _dispatch_kernel(clo_ref, cn_ref, pos_ref, x_ref, o_ref, acc_ref):
    r = pl.program_id(0)
    acc_ref[...] = jnp.zeros_like(acc_ref)
    rows = r * ROW_TILE + lax.broadcasted_iota(jnp.int32, (ROW_TILE, TOK_TILE), 0)

    def body(k, carry):
        c = clo_ref[r] + k
        pos = pos_ref[c]
        hit = jnp.logical_or(rows == pos[0:1, :], rows == pos[1:2, :])
        sel = jnp.where(hit, 1.0, 0.0).astype(BF16)
        off = pl.multiple_of(c * TOK_TILE, TOK_TILE)
        acc_ref[...] += jnp.dot(sel, x_ref[pl.ds(off, TOK_TILE), :], preferred_element_type=F32)
        return carry

    lax.fori_loop(0, cn_ref[r], body, 0)
    o_ref[...] = acc_ref[...].astype(BF16)


def _dispatch(c_lo, c_n, pos_rows, h2):
    return pl.pallas_call(
        _dispatch_kernel,
        out_shape=jax.ShapeDtypeStruct((SORT_ROWS, D_MODEL), BF16),
        grid_spec=pltpu.PrefetchScalarGridSpec(
            num_scalar_prefetch=2,
            grid=(SORT_TILES,),
            in_specs=[pl.BlockSpec((N_TILES, 8, TOK_TILE), lambda r, *_: (0, 0, 0)),
                      pl.BlockSpec((N_TOK, D_MODEL), lambda r, *_: (0, 0),
                                   pipeline_mode=pl.Buffered(1))],
            out_specs=pl.BlockSpec((ROW_TILE, D_MODEL), lambda r, *_: (r, 0)),
            scratch_shapes=[pltpu.VMEM((ROW_TILE, D_MODEL), F32)]),
        compiler_params=_cparams(1),
        name="moe_dispatch",
    )(c_lo, c_n, pos_rows, h2)


def _moe_up_kernel(te_ref, tv_ref, tf_ref, x_ref, wg_ref, wu_ref, h_ref, wgb_ref, wub_ref):
    r = pl.program_id(1)

    @pl.when(tf_ref[r] == 1)
    def _():
        wgb_ref[...] = wg_ref[...].astype(BF16)
        wub_ref[...] = wu_ref[...].astype(BF16)

    @pl.when(tv_ref[r] == 1)
    def _():
        x = x_ref[...]
        gt = jnp.dot(x, wgb_ref[...], preferred_element_type=F32)
        up = jnp.dot(x, wub_ref[...], preferred_element_type=F32)
        h_ref[...] = (gt * _sigmoid(gt) * up).astype(BF16)

    @pl.when(tv_ref[r] == 0)
    def _():
        h_ref[...] = jnp.zeros_like(h_ref)


def _moe_up(i_moe, tile_e, tile_valid, tile_first, xs, w13):
    return pl.pallas_call(
        _moe_up_kernel,
        out_shape=jax.ShapeDtypeStruct((SORT_ROWS, D_FF_EXPERT), BF16),
        grid_spec=pltpu.PrefetchScalarGridSpec(
            num_scalar_prefetch=3,
            grid=(UP_TILES, SORT_TILES),
            in_specs=[pl.BlockSpec((ROW_TILE, D_MODEL), lambda f, r, *_: (r, 0)),
                      pl.BlockSpec((None, None, D_MODEL, UP_TILE),
                                   lambda f, r, te, tv, tf: (i_moe, te[r], 0, f)),
                      pl.BlockSpec((None, None, D_MODEL, UP_TILE),
                                   lambda f, r, te, tv, tf: (i_moe, te[r], 0, UP_TILES + f))],
            out_specs=pl.BlockSpec((ROW_TILE, UP_TILE), lambda f, r, *_: (r, f)),
            scratch_shapes=[pltpu.VMEM((D_MODEL, UP_TILE), BF16),
                            pltpu.VMEM((D_MODEL, UP_TILE), BF16)]),
        compiler_params=_cparams(2),
        name="moe_up",
    )(tile_e, tile_valid, tile_first, xs, w13, w13)


def _moe_down_kernel(te_ref, tv_ref, tf_ref, h_ref, w_ref, y_ref, wb_ref):
    r = pl.program_id(1)

    @pl.when(tf_ref[r] == 1)
    def _():
        wb_ref[...] = w_ref[...].astype(BF16)

    @pl.when(tv_ref[r] == 1)
    def _():
        y_ref[...] = jnp.dot(h_ref[...], wb_ref[...], preferred_element_type=F32).astype(BF16)

    @pl.when(tv_ref[r] == 0)
    def _():
        y_ref[...] = jnp.zeros_like(y_ref)


def _moe_down(i_moe, tile_e, tile_valid, tile_first, hs, w2):
    return pl.pallas_call(
        _moe_down_kernel,
        out_shape=jax.ShapeDtypeStruct((SORT_ROWS, D_MODEL), BF16),
        grid_spec=pltpu.PrefetchScalarGridSpec(
            num_scalar_prefetch=3,
            grid=(D_MODEL // DOWN_TILE, SORT_TILES),
            in_specs=[pl.BlockSpec((ROW_TILE, D_FF_EXPERT), lambda n, r, *_: (r, 0)),
                      pl.BlockSpec((None, None, D_FF_EXPERT, DOWN_TILE),
                                   lambda n, r, te, tv, tf: (i_moe, te[r], 0, n))],
            out_specs=pl.BlockSpec((ROW_TILE, DOWN_TILE), lambda n, r, *_: (r, n)),
            scratch_shapes=[pltpu.VMEM((D_FF_EXPERT, DOWN_TILE), BF16)]),
        compiler_params=_cparams(2),
        name="moe_down",
    )(tile_e, tile_valid, tile_first, hs, w2)


def _window_copy(y_hbm, ybuf_ref, sem_ref, src, slot):
    return pltpu.make_async_copy(y_hbm.at[pl.ds(src, WIN), :],
                                 ybuf_ref.at[pl.ds(slot * WIN, WIN), :], sem_ref.at[slot])


def _combine_kernel(src_ref, lo_ref, hi_ref, info_ref, x_ref, mod_ref, g_ref, y_hbm,
                    xo_ref, ybuf_ref, sem_ref):
    j = pl.program_id(0)
    for s in range(N_WIN):
        src = pl.multiple_of(src_ref[j * N_WIN + s], 16)
        _window_copy(y_hbm, ybuf_ref, sem_ref, src, s).start()

    info = info_ref[...]
    pos1, pos2 = info[:, 0:1], info[:, 1:2]
    g1, g2 = info[:, 2:3], info[:, 3:4]
    iota = lax.broadcasted_iota(jnp.int32, (1, WIN), 1)
    ids = []
    for s in range(N_WIN):
        row = src_ref[j * N_WIN + s] + iota
        ok = jnp.logical_and(row >= lo_ref[j * N_WIN + s], row < hi_ref[j * N_WIN + s])
        ids.append(jnp.where(ok, row, -1))
    row_id = jnp.concatenate(ids, axis=1).astype(F32)
    sel = (jnp.where(pos1 == row_id, g1, 0.0) + jnp.where(pos2 == row_id, g2, 0.0)).astype(BF16)

    for s in range(N_WIN):
        _window_copy(y_hbm, ybuf_ref, sem_ref, 0, s).wait()
    ffn = jnp.dot(sel, ybuf_ref[...], preferred_element_type=F32)
    xo_ref[...] = x_ref[...] + mod_ref[5:6, :] * _rms(ffn, g_ref[3:4, :])


def _combine(l, win_src, win_lo, win_hi, info_tok, x, mod, norm_g, ys):
    return pl.pallas_call(
        _combine_kernel,
        out_shape=jax.ShapeDtypeStruct((N_TOK, D_MODEL), F32),
        grid_spec=pltpu.PrefetchScalarGridSpec(
            num_scalar_prefetch=3,
            grid=(N_TILES,),
            in_specs=[pl.BlockSpec((TOK_TILE, 128), lambda i, *_: (i, 0)),
                      pl.BlockSpec((TOK_TILE, D_MODEL), lambda i, *_: (i, 0)),
                      pl.BlockSpec((None, None, 6, D_MODEL), lambda i, *_: (l, _cond_id(i), 0, 0)),
                      pl.BlockSpec((None, 4, D_MODEL), lambda i, *_: (l, 0, 0)),
                      pl.BlockSpec(memory_space=pl.ANY)],
            out_specs=pl.BlockSpec((TOK_TILE, D_MODEL), lambda i, *_: (i, 0)),
            scratch_shapes=[pltpu.VMEM((N_WIN * WIN, D_MODEL), BF16),
                            pltpu.SemaphoreType.DMA((N_WIN,))]),
        compiler_params=_cparams(1),
        name="moe_combine",
    )(win_src, win_lo, win_hi, info_tok, x, mod, norm_g, ys)


def _moe_ffn(l, i_moe, h2, x, mod, norm_g, router_w, moe_w13, moe_w2):
    rw_pad = jnp.zeros((D_MODEL, 128), BF16).at[:, :N_EXPERTS].set(router_w[i_moe].astype(BF16))
    info, cb, tot = _router(h2, rw_pad)

    counts = tot[0, :N_EXPERTS].astype(jnp.int32)
    padded = (counts + ROW_TILE - 1) // ROW_TILE * ROW_TILE
    seg_end = jnp.cumsum(padded)
    base = seg_end - padded
    cbx = cb[:, 0, :N_EXPERTS].astype(jnp.int32)
    cb_end = jnp.concatenate([cbx[1:], counts[None, :]], axis=0)

    e1 = info[:, 0].astype(jnp.int32)
    e2 = info[:, 1].astype(jnp.int32)
    pos1 = base[e1] + info[:, 2].astype(jnp.int32)
    pos2 = base[e2] + info[:, 3].astype(jnp.int32)
    pos_rows = jnp.zeros((N_TILES, 8, TOK_TILE), jnp.int32)
    pos_rows = pos_rows.at[:, 0, :].set(pos1.reshape(N_TILES, TOK_TILE))
    pos_rows = pos_rows.at[:, 1, :].set(pos2.reshape(N_TILES, TOK_TILE))
    info_tok = jnp.zeros((N_TOK, 128), F32)
    info_tok = info_tok.at[:, 0].set(pos1.astype(F32)).at[:, 1].set(pos2.astype(F32))
    info_tok = info_tok.at[:, 2].set(info[:, 4]).at[:, 3].set(info[:, 5])

    row0 = jnp.arange(SORT_TILES, dtype=jnp.int32) * ROW_TILE
    tile_e = jnp.minimum(jnp.sum(seg_end[None, :] <= row0[:, None], axis=1), N_EXPERTS - 1).astype(jnp.int32)
    k0 = row0 - base[tile_e]
    tile_valid = jnp.logical_and(k0 >= 0, k0 < counts[tile_e])
    tile_first = jnp.logical_and(tile_valid, k0 == 0)
    k_end = jnp.minimum(k0 + ROW_TILE, counts[tile_e])
    cb_t = cbx.T[tile_e]
    c_lo = jnp.sum(cb_t <= k0[:, None], axis=1) - 1
    c_hi = jnp.sum(cb_t < k_end[:, None], axis=1) - 1
    c_n = jnp.where(tile_valid, c_hi - c_lo + 1, 0).astype(jnp.int32)
    c_lo = jnp.where(tile_valid, c_lo, 0).astype(jnp.int32)
    tile_valid = tile_valid.astype(jnp.int32)
    tile_first = tile_first.astype(jnp.int32)

    seg_lo = base[None, :] + cbx
    seg_hi = base[None, :] + cb_end
    seg_n = seg_hi - seg_lo
    w0 = seg_lo // 16 * 16
    nw = jnp.where(seg_n > 0, (seg_lo - w0 + seg_n + WIN - 1) // WIN, 0)
    nw_end = jnp.cumsum(nw, axis=1)
    slot = jnp.arange(N_WIN, dtype=jnp.int32)
    slot_e = jnp.minimum(jnp.sum(nw_end[:, None, :] <= slot[None, :, None], axis=2), N_EXPERTS - 1)
    take = lambda a: jnp.take_along_axis(a, slot_e, axis=1)
    slot_k = slot[None, :] - (take(nw_end) - take(nw))
    slot_ok = slot[None, :] < nw_end[:, -1:]
    slot_src = take(w0) + WIN * slot_k
    win_src = jnp.where(slot_ok, slot_src, 0).astype(jnp.int32).reshape(-1)
    win_lo = jnp.where(slot_ok, take(seg_lo), 0).astype(jnp.int32).reshape(-1)
    win_hi = jnp.where(slot_ok, take(seg_hi), 0).astype(jnp.int32).reshape(-1)

    xs = _dispatch(c_lo, c_n, pos_rows, h2)
    hs = _moe_up(i_moe, tile_e, tile_valid, tile_first, xs, moe_w13)
    ys = _moe_down(i_moe, tile_e, tile_valid, tile_first, hs, moe_w2)
    return _combine(l, win_src, win_lo, win_hi, info_tok, x, mod, norm_g, ys)


def kernel(x_prompt, x_sample, cache_k, cache_v, c, c_ctx, w_ada, b_ada, norm_g, w_in, w_out,
           lam_params, subln_g, dw_weight, dw_bias, conv_ln_g, conv_ln_b, dense_w13, dense_w2,
           router_w, moe_w13, moe_w2):
    x = jnp.concatenate([x_prompt.reshape(N_PROMPT, D_MODEL), x_sample.reshape(N_SAMPLE, D_MODEL)], axis=0)
    cond8 = jnp.zeros((8, D_MODEL), F32).at[0].set(c_ctx).at[1:1 + DEC_BATCH].set(c)
    mod = _ada_modulation(cond8, w_ada, b_ada).reshape(DEPTH, 8, 6, D_MODEL)
    rope = _rope_tables()
    ck = cache_k.reshape(DEC_BATCH, DEPTH, PAST_LEN, ATT_W)
    cv = cache_v.reshape(DEC_BATCH, DEPTH, PAST_LEN, ATT_W)

    new_k, new_v = [], []
    for l in range(DEPTH):
        lam_init = 0.8 - 0.6 * math.exp(-0.3 * l)
        qkvg, kc, vc = _inproj(l, x, mod, norm_g, w_in, rope)
        new_k.append(kc.reshape(BATCH, SEQ, ATT_HEADS, V_DIM))
        new_v.append(vc.reshape(BATCH, SEQ, ATT_HEADS, V_DIM))
        o_ctx, o_lat = _attention(l, lam_init, lam_params, subln_g, qkvg, ck, cv)
        cvo = _conformer_conv(l, qkvg, dw_weight, dw_bias, conv_ln_g, conv_ln_b)
        x, h2 = _outproj(l, o_ctx, o_lat, cvo, x, mod, norm_g, w_out)
        i = l // 2
        if l % 2 == 0:
            x = _dense_ffn(l, h2, x, mod, norm_g, dense_w13[i].astype(BF16), dense_w2[i].astype(BF16))
        else:
            x = _moe_ffn(l, i, h2, x, mod, norm_g, router_w, moe_w13, moe_w2)

    y_prompt = x[:N_PROMPT].reshape(BATCH, SEQ, D_MODEL)
    y_sample = x[N_PROMPT:].reshape(DEC_BATCH, DEC_SEQ, D_MODEL)
    return (y_prompt, y_sample, jnp.stack(new_k, axis=1), jnp.stack(new_v, axis=1))
```

```python
import functools
import math

import jax
import jax.numpy as jnp
from jax import lax
from jax.experimental import pallas as pl
from jax.experimental.pallas import tpu as pltpu

F32 = jnp.float32
BF16 = jnp.bfloat16

D_MODEL = 1024
BATCH = 32
SEQ = 256
DEPTH = 2
DEC_BATCH = 2
DEC_SEQ = 2048
PAST_LEN = 512
GRID_W = 64
ATT_HEADS = 4
QK_DIM = 64
V_DIM = 128
ATT_W = ATT_HEADS * V_DIM
IN_W = 5 * ATT_W
CONV_CH = 512
CONV_WIDTH = 31
D_FF = 2816
N_EXPERTS = 8
D_FF_EXPERT = 3584
ROPE_THETA = 10000.0
NORM_EPS = 1e-6
LN_EPS = 1e-5

N_PROMPT = BATCH * SEQ
N_SAMPLE = DEC_BATCH * DEC_SEQ
N_TOK = N_PROMPT + N_SAMPLE

TOK_TILE = 512
N_TILES = N_TOK // TOK_TILE
P_TILES = N_PROMPT // TOK_TILE
S_TILES_PER_BATCH = DEC_SEQ // TOK_TILE

ROW_TILE = 512
SORT_TILES = (2 * N_TOK + N_EXPERTS * ROW_TILE) // ROW_TILE + 1
SORT_ROWS = SORT_TILES * ROW_TILE
UP_TILE = 896
UP_TILES = D_FF_EXPERT // UP_TILE
DOWN_TILE = 512
WIN = 128
N_WIN = 16

VMEM_LIMIT = 56 * 1024 * 1024


def _cparams(n_axes):
    return pltpu.CompilerParams(dimension_semantics=("arbitrary",) * n_axes,
                                vmem_limit_bytes=VMEM_LIMIT)


def _cond_id(i):
    return jnp.where(i < P_TILES, 0, 1 + (i - P_TILES) // S_TILES_PER_BATCH)


def _sigmoid(x):
    return 1.0 / (1.0 + jnp.exp(-x))


def _rms(x, g):
    return x * lax.rsqrt(jnp.mean(x * x, axis=-1, keepdims=True) + NORM_EPS) * g


def _ada_kernel(c_ref, w_ref, b_ref, o_ref):
    c = c_ref[...]
    s = (c * _sigmoid(c)).astype(BF16)
    o_ref[...] = jnp.dot(s, w_ref[...].astype(BF16), preferred_element_type=F32) + b_ref[...]


def _ada_modulation(cond8, w_ada, b_ada):
    tn = 1536
    n = 6 * D_MODEL
    return pl.pallas_call(
        _ada_kernel,
        out_shape=jax.ShapeDtypeStruct((DEPTH, 8, n), F32),
        grid=(DEPTH, n // tn),
        in_specs=[pl.BlockSpec((8, D_MODEL), lambda l, j: (0, 0)),
                  pl.BlockSpec((None, D_MODEL, tn), lambda l, j: (l, 0, j)),
                  pl.BlockSpec((None, 1, tn), lambda l, j: (l, 0, j))],
        out_specs=pl.BlockSpec((None, 8, tn), lambda l, j: (l, 0, j)),
        compiler_params=_cparams(2),
        name="ada_modulation",
    )(cond8, w_ada, b_ada.reshape(DEPTH, 1, n))


def _tile_x(xa_ref, xb_ref):
    return jnp.where(pl.program_id(0) < P_TILES, xa_ref[...], xb_ref[...])


def _x_specs(x):
    if isinstance(x, tuple):
        xa, xb = x
        b_idx = lambda i, *_: (jnp.maximum(i - P_TILES, 0), 0)
    else:
        xa = xb = x
        b_idx = lambda i, *_: (jnp.maximum(i, P_TILES), 0)
    a_idx = lambda i, *_: (jnp.minimum(i, P_TILES - 1), 0)
    return [pl.BlockSpec((TOK_TILE, D_MODEL), a_idx), pl.BlockSpec((TOK_TILE, D_MODEL), b_idx)], (xa, xb)


SEQ_PER_TILE = TOK_TILE // SEQ
CACHE_ROWS = SEQ * ATT_HEADS


def _inproj_kernel(*refs, layer):
    if layer == 0:
        (xa_ref, xb_ref, mod_ref, g_ref, w_ref, cos_ref, sina_ref, sinb_ref,
         qkvg_ref, kc_ref, vc_ref, wbf_ref) = refs
    else:
        (xa_ref, xb_ref, mod_ref, g_ref, w_ref, cos_ref, sina_ref, sinb_ref, _, _,
         qkvg_ref, kc_ref, vc_ref, wbf_ref) = refs
    i = pl.program_id(0)

    @pl.when(i == 0)
    def _():
        wbf_ref[...] = w_ref[...].astype(BF16)

    h = _rms(_tile_x(xa_ref, xb_ref), g_ref[0:1, :]) * (1.0 + mod_ref[1:2, :]) + mod_ref[0:1, :]
    proj = jnp.dot(h.astype(BF16), wbf_ref[...], preferred_element_type=F32)
    qkvg_ref[:, 2 * ATT_W:] = proj[:, 2 * ATT_W:].astype(BF16)

    @pl.when(i < P_TILES)
    def _():
        qkvg_ref[:, :2 * ATT_W] = proj[:, :2 * ATT_W].astype(BF16)
        for ref, col0 in ((kc_ref, ATT_W), (vc_ref, 2 * ATT_W)):
            for s in range(SEQ_PER_TILE):
                for hd in range(ATT_HEADS):
                    val = proj[SEQ * s:SEQ * (s + 1), col0 + V_DIM * hd:col0 + V_DIM * (hd + 1)]
                    rows = pl.ds(hd, SEQ, stride=ATT_HEADS)
                    if layer == 0:
                        ref[s, 0, rows, :] = val
                    else:
                        ref[s, rows, :] = val
            if layer == 0:
                ref[:, 1:] = jnp.zeros((SEQ_PER_TILE, DEPTH - 1, CACHE_ROWS, V_DIM), F32)

    @pl.when(i >= P_TILES)
    def _():
        cos = cos_ref[...]
        sina = sina_ref[...]
        sinb = sinb_ref[...]
        for c in range(2 * ATT_W // 128):
            xg = proj[:, 128 * c:128 * (c + 1)]
            fwd = pltpu.roll(xg, 128 - 16, 1)
            bwd = pltpu.roll(xg, 16, 1)
            qkvg_ref[:, 128 * c:128 * (c + 1)] = (xg * cos + fwd * sina + bwd * sinb).astype(BF16)


def _inproj(l, x, mod, norm_g, w_in, rope, caches):
    cos, sina, sinb = rope
    x_specs, x_args = _x_specs(x)

    def rope_idx(i):
        return (jnp.maximum(i - P_TILES, 0) % S_TILES_PER_BATCH, 0)

    cache_shape = jax.ShapeDtypeStruct((BATCH, DEPTH, CACHE_ROWS, V_DIM), F32)
    if l == 0:
        cache_spec = pl.BlockSpec((SEQ_PER_TILE, DEPTH, CACHE_ROWS, V_DIM),
                                  lambda i: (jnp.minimum(i, P_TILES - 1), 0, 0, 0))
        extra_specs, extra_args, aliases = [], (), {}
    else:
        cache_spec = pl.BlockSpec((SEQ_PER_TILE, None, CACHE_ROWS, V_DIM),
                                  lambda i: (jnp.minimum(i, P_TILES - 1), l, 0, 0))
        extra_specs = [pl.BlockSpec(memory_space=pl.ANY)] * 2
        extra_args = tuple(caches)
        aliases = {8: 1, 9: 2}

    return pl.pallas_call(
        functools.partial(_inproj_kernel, layer=l),
        out_shape=(jax.ShapeDtypeStruct((N_TOK, IN_W), BF16), cache_shape, cache_shape),
        grid=(N_TILES,),
        in_specs=x_specs + [
            pl.BlockSpec((None, None, 6, D_MODEL), lambda i: (l, _cond_id(i), 0, 0)),
            pl.BlockSpec((None, 4, D_MODEL), lambda i: (l, 0, 0)),
            pl.BlockSpec((None, D_MODEL, IN_W), lambda i: (l, 0, 0), pipeline_mode=pl.Buffered(1)),
            pl.BlockSpec((TOK_TILE, 128), rope_idx),
            pl.BlockSpec((TOK_TILE, 128), rope_idx),
            pl.BlockSpec((TOK_TILE, 128), rope_idx)] + extra_specs,
        out_specs=(pl.BlockSpec((TOK_TILE, IN_W), lambda i: (i, 0)), cache_spec, cache_spec),
        scratch_shapes=[pltpu.VMEM((D_MODEL, IN_W), BF16)],
        input_output_aliases=aliases,
        compiler_params=_cparams(1),
        name=f"inproj_{l}",
    )(*x_args, mod, norm_g, w_in, cos, sina, sinb, *extra_args)


def _rope_tables():
    rows = DEC_SEQ // GRID_W
    row_pos = jnp.repeat(jnp.arange(rows, dtype=F32), GRID_W)
    col_pos = jnp.tile(jnp.arange(GRID_W, dtype=F32), rows)
    half = QK_DIM // 2
    inv_freq = 1.0 / (ROPE_THETA ** (jnp.arange(0, half, 2, dtype=F32) / half))
    ang_r = row_pos[:, None] * inv_freq
    ang_c = col_pos[:, None] * inv_freq
    ang = jnp.concatenate([ang_r, ang_r, ang_c, ang_c], axis=-1)
    cos = jnp.tile(jnp.cos(ang), (1, 2))
    sin = jnp.tile(jnp.sin(ang), (1, 2))
    first = (jnp.arange(128) % 32) < 16
    sina = jnp.where(first[None, :], -sin, 0.0)
    sinb = jnp.where(first[None, :], 0.0, sin)
    return cos, sina, sinb


def _attn_kernel(*refs, lam_init, has_ext):
    if has_ext:
        lamp_ref, sub_ref, q_ref, k_ref, v_ref, ke_ref, ve_ref, o_ref = refs
    else:
        lamp_ref, sub_ref, q_ref, k_ref, v_ref, o_ref = refs
    lp = lamp_ref[...]
    lam = (jnp.exp(jnp.sum(lp[0:1] * lp[1:2], axis=-1, keepdims=True))
           - jnp.exp(jnp.sum(lp[2:3] * lp[3:4], axis=-1, keepdims=True)) + lam_init)
    lane = lax.broadcasted_iota(jnp.int32, (1, V_DIM), 1)
    nt = (((1,), (1,)), ((), ()))
    scale = QK_DIM ** -0.5
    map_scale = [jnp.where(lane < QK_DIM, scale, 0.0).astype(BF16),
                 jnp.where(lane < QK_DIM, 0.0, scale).astype(BF16)]

    for hd in range(ATT_HEADS):
        cols = slice(V_DIM * hd, V_DIM * (hd + 1))
        qh = q_ref[:, cols]
        kh = k_ref[:, cols]
        vh = v_ref[:, cols]
        if has_ext:
            head_rows = pl.ds(hd, PAST_LEN, stride=ATT_HEADS)
            keh = ke_ref[head_rows, :].astype(BF16)
            veh = ve_ref[head_rows, :].astype(BF16)
        acc = None
        for m in range(2):
            qm = qh * map_scale[m]
            s = lax.dot_general(qm, kh, nt, preferred_element_type=F32)
            mx = jnp.max(s, axis=-1, keepdims=True)
            if has_ext:
                se = lax.dot_general(qm, keh, nt, preferred_element_type=F32)
                mx = jnp.maximum(mx, jnp.max(se, axis=-1, keepdims=True))
            p = jnp.exp(s - mx)
            den = jnp.sum(p, axis=-1, keepdims=True)
            if has_ext:
                pe = jnp.exp(se - mx)
                den = den + jnp.sum(pe, axis=-1, keepdims=True)
            coef = 1.0 / den if m == 0 else -lam / den
            contrib = p * coef
            acc = contrib if acc is None else acc + contrib
            if has_ext:
                contrib_e = pe * coef
                acc_e = contrib_e if m == 0 else acc_e + contrib_e
        o = jnp.dot(acc.astype(BF16), vh, preferred_element_type=F32)
        if has_ext:
            o = o + jnp.dot(acc_e.astype(BF16), veh, preferred_element_type=F32)
        o = _rms(o, sub_ref[...]) * (1.0 - lam_init)
        o_ref[:, cols] = o.astype(BF16)


def _attention(l, lam_init, lam_params, subln_g, qkvg, cache_k, cache_v):
    small = [pl.BlockSpec((None, 4, QK_DIM), lambda *_: (l, 0, 0)),
             pl.BlockSpec((None, 1, V_DIM), lambda *_: (l, 0, 0))]
    sub3 = subln_g.reshape(DEPTH, 1, V_DIM)

    o_ctx = pl.pallas_call(
        functools.partial(_attn_kernel, lam_init=lam_init, has_ext=False),
        out_shape=jax.ShapeDtypeStruct((N_PROMPT, ATT_W), BF16),
        grid=(BATCH,),
        in_specs=small + [pl.BlockSpec((SEQ, ATT_W), lambda b: (b, 0)),
                          pl.BlockSpec((SEQ, ATT_W), lambda b: (b, 1)),
                          pl.BlockSpec((SEQ, ATT_W), lambda b: (b, 2))],
        out_specs=pl.BlockSpec((SEQ, ATT_W), lambda b: (b, 0)),
        compiler_params=_cparams(1),
        name=f"attn_ctx_{l}",
    )(lam_params, sub3, qkvg, qkvg, qkvg)

    tq = 256
    q_tiles = DEC_SEQ // tq
    q_base = N_PROMPT // tq
    kv_base = N_PROMPT // DEC_SEQ

    def q_idx(b, i):
        return (q_base + b * q_tiles + i, 0)

    o_lat = pl.pallas_call(
        functools.partial(_attn_kernel, lam_init=lam_init, has_ext=True),
        out_shape=jax.ShapeDtypeStruct((N_SAMPLE, ATT_W), BF16),
        grid=(DEC_BATCH, q_tiles),
        in_specs=small + [pl.BlockSpec((tq, ATT_W), q_idx),
                          pl.BlockSpec((DEC_SEQ, ATT_W), lambda b, i: (kv_base + b, 1)),
                          pl.BlockSpec((DEC_SEQ, ATT_W), lambda b, i: (kv_base + b, 2)),
                          pl.BlockSpec((None, None, PAST_LEN * ATT_HEADS, V_DIM), lambda b, i: (b, l, 0, 0)),
                          pl.BlockSpec((None, None, PAST_LEN * ATT_HEADS, V_DIM), lambda b, i: (b, l, 0, 0))],
        out_specs=pl.BlockSpec((tq, ATT_W), lambda b, i: (b * q_tiles + i, 0)),
        compiler_params=_cparams(2),
        name=f"attn_lat_{l}",
    )(lam_params, sub3, qkvg, qkvg, qkvg, cache_k, cache_v)
    return o_ctx, o_lat


CONV_TILE = 256
CONV_HALO = 16
CONV_SUB = 64
CONV_TILES = N_TOK // CONV_TILE
CONV_P_TILES = N_PROMPT // CONV_TILE
CONV_S_TILES = DEC_SEQ // CONV_TILE


def _glu(a_ref, b_ref):
    return a_ref[...].astype(F32) * _sigmoid(b_ref[...].astype(F32))


def _conv_kernel(a_ref, b_ref, ap_ref, bp_ref, an_ref, bn_ref, w_ref, bias_ref, lg_ref, lb_ref,
                 o_ref, upad_ref):
    i = pl.program_id(0)
    q = (i - CONV_P_TILES) % CONV_S_TILES
    latent = i >= CONV_P_TILES
    has_prev = jnp.logical_and(latent, q != 0)
    has_next = jnp.logical_and(latent, q != CONV_S_TILES - 1)
    upad_ref[0:CONV_HALO, :] = jnp.where(has_prev, _glu(ap_ref, bp_ref), 0.0)
    upad_ref[CONV_HALO:CONV_HALO + CONV_TILE, :] = _glu(a_ref, b_ref)
    upad_ref[CONV_HALO + CONV_TILE:, :] = jnp.where(has_next, _glu(an_ref, bn_ref), 0.0)
    first_tap = CONV_HALO - CONV_WIDTH // 2

    for t in range(CONV_TILE // CONV_SUB):
        base = t * CONV_SUB
        acc = jnp.zeros((CONV_SUB, CONV_CH), F32)
        for j in range(CONV_WIDTH):
            lo = base + first_tap + j
            acc = acc + upad_ref[lo:lo + CONV_SUB, :] * w_ref[j:j + 1, :]
        y = acc + bias_ref[...]
        mu = jnp.mean(y, axis=-1, keepdims=True)
        yc = y - mu
        var = jnp.mean(yc * yc, axis=-1, keepdims=True)
        z = yc * lax.rsqrt(var + LN_EPS) * lg_ref[...] + lb_ref[...]
        o_ref[base:base + CONV_SUB, :] = (z * _sigmoid(z)).astype(BF16)


def _conformer_conv(l, qkvg, dw_w, dw_b, ln_g, ln_b):
    def vec(a):
        return a.reshape(DEPTH, 1, CONV_CH)

    per = CONV_TILE // CONV_HALO
    last = N_TOK // CONV_HALO - 1

    def prev_idx(col):
        return lambda i: (jnp.maximum(i * per - 1, 0), col)

    def next_idx(col):
        return lambda i: (jnp.minimum((i + 1) * per, last), col)

    return pl.pallas_call(
        _conv_kernel,
        out_shape=jax.ShapeDtypeStruct((N_TOK, CONV_CH), BF16),
        grid=(CONV_TILES,),
        in_specs=[pl.BlockSpec((CONV_TILE, ATT_W), lambda i: (i, 3)),
                  pl.BlockSpec((CONV_TILE, ATT_W), lambda i: (i, 4)),
                  pl.BlockSpec((CONV_HALO, ATT_W), prev_idx(3)),
                  pl.BlockSpec((CONV_HALO, ATT_W), prev_idx(4)),
                  pl.BlockSpec((CONV_HALO, ATT_W), next_idx(3)),
                  pl.BlockSpec((CONV_HALO, ATT_W), next_idx(4)),
                  pl.BlockSpec((None, CONV_WIDTH, CONV_CH), lambda i: (l, 0, 0)),
                  pl.BlockSpec((None, 1, CONV_CH), lambda i: (l, 0, 0)),
                  pl.BlockSpec((None, 1, CONV_CH), lambda i: (l, 0, 0)),
                  pl.BlockSpec((None, 1, CONV_CH), lambda i: (l, 0, 0))],
        out_specs=pl.BlockSpec((CONV_TILE, CONV_CH), lambda i: (i, 0)),
        scratch_shapes=[pltpu.VMEM((CONV_TILE + 2 * CONV_HALO, CONV_CH), F32)],
        compiler_params=_cparams(1),
        name=f"conv_{l}",
    )(qkvg, qkvg, qkvg, qkvg, qkvg, qkvg, dw_w, vec(dw_b), vec(ln_g), vec(ln_b))


def _outproj_kernel(oc_ref, ol_ref, cv_ref, xa_ref, xb_ref, mod_ref, g_ref, w_ref, xo_ref, h2_ref, wbf_ref):
    i = pl.program_id(0)

    @pl.when(i == 0)
    def _():
        wbf_ref[...] = w_ref[...].astype(BF16)

    o = jnp.where(i < P_TILES, oc_ref[...], ol_ref[...])
    m = (jnp.dot(o, wbf_ref[0:ATT_W, :], preferred_element_type=F32)
         + jnp.dot(cv_ref[...], wbf_ref[ATT_W:, :], preferred_element_type=F32))
    xn = _tile_x(xa_ref, xb_ref) + mod_ref[2:3, :] * _rms(m, g_ref[1:2, :])
    xo_ref[...] = xn
    h2_ref[...] = (_rms(xn, g_ref[2:3, :]) * (1.0 + mod_ref[4:5, :]) + mod_ref[3:4, :]).astype(BF16)


def _outproj(l, o_ctx, o_lat, cv, x, mod, norm_g, w_out):
    x_specs, x_args = _x_specs(x)
    return pl.pallas_call(
        _outproj_kernel,
        out_shape=(jax.ShapeDtypeStruct((N_TOK, D_MODEL), F32),
                   jax.ShapeDtypeStruct((N_TOK, D_MODEL), BF16)),
        grid=(N_TILES,),
        in_specs=[pl.BlockSpec((TOK_TILE, ATT_W), lambda i: (jnp.minimum(i, P_TILES - 1), 0)),
                  pl.BlockSpec((TOK_TILE, ATT_W), lambda i: (jnp.maximum(i - P_TILES, 0), 0)),
                  pl.BlockSpec((TOK_TILE, CONV_CH), lambda i: (i, 0))] + x_specs + [
                  pl.BlockSpec((None, None, 6, D_MODEL), lambda i: (l, _cond_id(i), 0, 0)),
                  pl.BlockSpec((None, 4, D_MODEL), lambda i: (l, 0, 0)),
                  pl.BlockSpec((None, D_MODEL, D_MODEL), lambda i: (l, 0, 0),
                               pipeline_mode=pl.Buffered(1))],
        out_specs=(pl.BlockSpec((TOK_TILE, D_MODEL), lambda i: (i, 0)),
                   pl.BlockSpec((TOK_TILE, D_MODEL), lambda i: (i, 0))),
        scratch_shapes=[pltpu.VMEM((D_MODEL, D_MODEL), BF16)],
        compiler_params=_cparams(1),
        name=f"outproj_{l}",
    )(o_ctx, o_lat, cv, *x_args, mod, norm_g, w_out)


DENSE_CHUNK = D_FF // 2


def _dense_ffn_kernel(h_ref, x_ref, mod_ref, g_ref, w13_ref, w2_ref, xo_ref):
    h = h_ref[...]
    acc = None
    for c in range(D_FF // DENSE_CHUNK):
        lo = c * DENSE_CHUNK
        gt = jnp.dot(h, w13_ref[:, lo:lo + DENSE_CHUNK], preferred_element_type=F32)
        up = jnp.dot(h, w13_ref[:, D_FF + lo:D_FF + lo + DENSE_CHUNK], preferred_element_type=F32)
        a = (gt * _sigmoid(gt) * up).astype(BF16)
        part = jnp.dot(a, w2_ref[lo:lo + DENSE_CHUNK, :], preferred_element_type=F32)
        acc = part if acc is None else acc + part
    xo_ref[...] = x_ref[...] + mod_ref[5:6, :] * _rms(acc, g_ref[3:4, :])


def _dense_ffn(l, h2, x, mod, norm_g, w13, w2):
    return pl.pallas_call(
        _dense_ffn_kernel,
        out_shape=jax.ShapeDtypeStruct((N_TOK, D_MODEL), F32),
        grid=(N_TILES,),
        in_specs=[pl.BlockSpec((TOK_TILE, D_MODEL), lambda i: (i, 0)),
                  pl.BlockSpec((TOK_TILE, D_MODEL), lambda i: (i, 0)),
                  pl.BlockSpec((None, None, 6, D_MODEL), lambda i: (l, _cond_id(i), 0, 0)),
                  pl.BlockSpec((None, 4, D_MODEL), lambda i: (l, 0, 0)),
                  pl.BlockSpec((D_MODEL, 2 * D_FF), lambda i: (0, 0), pipeline_mode=pl.Buffered(1)),
                  pl.BlockSpec((D_FF, D_MODEL), lambda i: (0, 0), pipeline_mode=pl.Buffered(1))],
        out_specs=pl.BlockSpec((TOK_TILE, D_MODEL), lambda i: (i, 0)),
        compiler_params=_cparams(1),
        name=f"dense_ffn_{l}",
    )(h2, x, mod, norm_g, w13, w2)


def _router_kernel(h_ref, rw_ref, info_ref, info_t_ref, cb_ref, tot_ref, carry_ref):
    i = pl.program_id(0)

    @pl.when(i == 0)
    def _():
        carry_ref[...] = jnp.zeros_like(carry_ref)

    lane = lax.broadcasted_iota(jnp.int32, (TOK_TILE, 128), 1)
    lanef = lane.astype(F32)
    logits = jnp.dot(h_ref[...], rw_ref[...], preferred_element_type=F32)
    logits = jnp.where(lane < N_EXPERTS, logits, -jnp.inf)
    big = jnp.asarray(128.0, F32)
    m1 = jnp.max(logits, axis=-1, keepdims=True)
    e1 = jnp.min(jnp.where(logits == m1, lanef, big), axis=-1, keepdims=True)
    oh1 = lanef == e1
    rest = jnp.where(oh1, -jnp.inf, logits)
    m2 = jnp.max(rest, axis=-1, keepdims=True)
    e2 = jnp.min(jnp.where(rest == m2, lanef, big), axis=-1, keepdims=True)
    oh2 = lanef == e2
    ex = jnp.exp(m2 - m1)
    g1 = 1.0 / (1.0 + ex)
    g2 = ex / (1.0 + ex)

    oh = jnp.where(jnp.logical_or(oh1, oh2), 1.0, 0.0)
    r = lax.broadcasted_iota(jnp.int32, (TOK_TILE, TOK_TILE), 0)
    c = lax.broadcasted_iota(jnp.int32, (TOK_TILE, TOK_TILE), 1)
    tri = jnp.where(c <= r, 1.0, 0.0).astype(BF16)
    incl = jnp.dot(tri, oh.astype(BF16), preferred_element_type=F32)
    carry = carry_ref[0:1, :]
    excl = incl - oh + carry
    rank1 = jnp.sum(jnp.where(oh1, excl, 0.0), axis=-1, keepdims=True)
    rank2 = jnp.sum(jnp.where(oh2, excl, 0.0), axis=-1, keepdims=True)

    info = jnp.where(lane == 0, e1, 0.0)
    for k, col in enumerate((e2, rank1, rank2, g1, g2), start=1):
        info = jnp.where(lane == k, col, info)
    info_ref[...] = info
    info_t_ref[...] = info.T[0:8, :]

    cb_ref[...] = carry_ref[...]
    new_carry = carry + incl[TOK_TILE - 1:TOK_TILE, :]
    carry_ref[...] = jnp.broadcast_to(new_carry, carry_ref.shape)
    tot_ref[...] = jnp.broadcast_to(new_carry, tot_ref.shape)


def _router(h2, rw_pad):
    return pl.pallas_call(
        _router_kernel,
        out_shape=(jax.ShapeDtypeStruct((N_TOK, 128), F32),
                   jax.ShapeDtypeStruct((N_TILES, 8, TOK_TILE), F32),
                   jax.ShapeDtypeStruct((N_TILES, 8, 128), F32),
                   jax.ShapeDtypeStruct((8, 128), F32)),
        grid=(N_TILES,),
        in_specs=[pl.BlockSpec((TOK_TILE, D_MODEL), lambda i: (i, 0)),
                  pl.BlockSpec((D_MODEL, 128), lambda i: (0, 0))],
        out_specs=(pl.BlockSpec((TOK_TILE, 128), lambda i: (i, 0)),
                   pl.BlockSpec((None, 8, TOK_TILE), lambda i: (i, 0, 0)),
                   pl.BlockSpec((None, 8, 128), lambda i: (i, 0, 0)),
                   pl.BlockSpec((8, 128), lambda i: (0, 0))),
        scratch_shapes=[pltpu.VMEM((8, 128), F32)],
        compiler_params=_cparams(1),
        name="moe_router",
    )(h2, rw_pad)


def _sorted_pos(expert, rank, base_ref):
    start = jnp.zeros_like(rank)
    for e in range(N_EXPERTS):
        start = jnp.where(expert == float(e), base_ref[e].astype(F32), start)
    return start + rank


def _dispatch_kernel(clo_ref, cn_ref, base_ref, info_t_ref, x_ref, o_ref, acc_ref):
    r = pl.program_id(0)
    acc_ref[...] = jnp.zeros_like(acc_ref)
    rows = (r * ROW_TILE + lax.broadcasted_iota(jnp.int32, (ROW_TILE, TOK_TILE), 0)).astype(F32)

    def body(k, carry):
        c = clo_ref[r] + k
        it = info_t_ref[c]
        pos1 = _sorted_pos(it[0:1, :], it[2:3, :], base_ref)
        pos2 = _sorted_pos(it[1:2, :], it[3:4, :], base_ref)
        hit = jnp.logical_or(rows == pos1, rows == pos2)
        sel = jnp.where(hit, 1.0, 0.0).astype(BF16)
        off = pl.multiple_of(c * TOK_TILE, TOK_TILE)
        acc_ref[...] += jnp.dot(sel, x_ref[pl.ds(off, TOK_TILE), :], preferred_element_type=F32)
        return carry

    lax.fori_loop(0, cn_ref[r], body, 0)
    o_ref[...] = acc_ref[...].astype(BF16)


def _dispatch(c_lo, c_n, base, info_t, h2):
    return pl.pallas_call(
        _dispatch_kernel,
        out_shape=jax.ShapeDtypeStruct((SORT_ROWS, D_MODEL), BF16),
        grid_spec=pltpu.PrefetchScalarGridSpec(
            num_scalar_prefetch=3,
            grid=(SORT_TILES,),
            in_specs=[pl.BlockSpec((N_TILES, 8, TOK_TILE), lambda r, *_: (0, 0, 0)),
                      pl.BlockSpec((N_TOK, D_MODEL), lambda r, *_: (0, 0),
                                   pipeline_mode=pl.Buffered(1))],
            out_specs=pl.BlockSpec((ROW_TILE, D_MODEL), lambda r, *_: (r, 0)),
            scratch_shapes=[pltpu.VMEM((ROW_TILE, D_MODEL), F32)]),
        compiler_params=_cparams(1),
        name="moe_dispatch",
    )(c_lo, c_n, base, info_t, h2)


def _moe_up_kernel(te_ref, tv_ref, tf_ref, x_ref, wg_ref, wu_ref, h_ref, wgb_ref, wub_ref):
    r = pl.program_id(1)

    @pl.when(tf_ref[r] == 1)
    def _():
        wgb_ref[...] = wg_ref[...].astype(BF16)
        wub_ref[...] = wu_ref[...].astype(BF16)

    @pl.when(tv_ref[r] == 1)
    def _():
        x = x_ref[...]
        gt = jnp.dot(x, wgb_ref[...], preferred_element_type=F32)
        up = jnp.dot(x, wub_ref[...], preferred_element_type=F32)
        h_ref[...] = (gt * _sigmoid(gt) * up).astype(BF16)

    @pl.when(tv_ref[r] == 0)
    def _():
        h_ref[...] = jnp.zeros_like(h_ref)


def _moe_up(i_moe, tile_e, tile_valid, tile_first, xs, w13):
    return pl.pallas_call(
        _moe_up_kernel,
        out_shape=jax.ShapeDtypeStruct((SORT_ROWS, D_FF_EXPERT), BF16),
        grid_spec=pltpu.PrefetchScalarGridSpec(
            num_scalar_prefetch=3,
            grid=(UP_TILES, SORT_TILES),
            in_specs=[pl.BlockSpec((ROW_TILE, D_MODEL), lambda f, r, *_: (r, 0)),
                      pl.BlockSpec((None, None, D_MODEL, UP_TILE),
                                   lambda f, r, te, tv, tf: (i_moe, te[r], 0, f)),
                      pl.BlockSpec((None, None, D_MODEL, UP_TILE),
                                   lambda f, r, te, tv, tf: (i_moe, te[r], 0, UP_TILES + f))],
            out_specs=pl.BlockSpec((ROW_TILE, UP_TILE), lambda f, r, *_: (r, f)),
            scratch_shapes=[pltpu.VMEM((D_MODEL, UP_TILE), BF16),
                            pltpu.VMEM((D_MODEL, UP_TILE), BF16)]),
        compiler_params=_cparams(2),
        name="moe_up",
    )(tile_e, tile_valid, tile_first, xs, w13, w13)


def _moe_down_kernel(te_ref, tv_ref, tf_ref, h_ref, w_ref, y_ref, wb_ref):
    r = pl.program_id(1)

    @pl.when(tf_ref[r] == 1)
    def _():
        wb_ref[...] = w_ref[...].astype(BF16)

    @pl.when(tv_ref[r] == 1)
    def _():
        y_ref[...] = jnp.dot(h_ref[...], wb_ref[...], preferred_element_type=F32).astype(BF16)

    @pl.when(tv_ref[r] == 0)
    def _():
        y_ref[...] = jnp.zeros_like(y_ref)


def _moe_down(i_moe, tile_e, tile_valid, tile_first, hs, w2):
    return pl.pallas_call(
        _moe_down_kernel,
        out_shape=jax.ShapeDtypeStruct((SORT_ROWS, D_MODEL), BF16),
        grid_spec=pltpu.PrefetchScalarGridSpec(
            num_scalar_prefetch=3,
            grid=(D_MODEL // DOWN_TILE, SORT_TILES),
            in_specs=[pl.BlockSpec((ROW_TILE, D_FF_EXPERT), lambda n, r, *_: (r, 0)),
                      pl.BlockSpec((None, None, D_FF_EXPERT, DOWN_TILE),
                                   lambda n, r, te, tv, tf: (i_moe, te[r], 0, n))],
            out_specs=pl.BlockSpec((ROW_TILE, DOWN_TILE), lambda n, r, *_: (r, n)),
            scratch_shapes=[pltpu.VMEM((D_FF_EXPERT, DOWN_TILE), BF16)]),
        compiler_params=_cparams(2),
        name="moe_down",
    )(tile_e, tile_valid, tile_first, hs, w2)


def _window_copy(y_hbm, ybuf_ref, sem_ref, src, slot):
    return pltpu.make_async_copy(y_hbm.at[pl.ds(src, WIN), :],
                                 ybuf_ref.at[pl.ds(slot * WIN, WIN), :], sem_ref.at[slot])


def _combine_kernel(src_ref, lo_ref, hi_ref, base_ref, info_ref, x_ref, mod_ref, g_ref, y_hbm,
                    yp_ref, ys_ref, ybuf_ref, sem_ref):
    j = pl.program_id(0)
    for s in range(N_WIN):
        src = pl.multiple_of(src_ref[j * N_WIN + s], 16)
        _window_copy(y_hbm, ybuf_ref, sem_ref, src, s).start()

    info = info_ref[...]
    pos1 = _sorted_pos(info[:, 0:1], info[:, 2:3], base_ref)
    pos2 = _sorted_pos(info[:, 1:2], info[:, 3:4], base_ref)
    g1, g2 = info[:, 4:5], info[:, 5:6]
    iota = lax.broadcasted_iota(jnp.int32, (1, WIN), 1)
    ids = []
    for s in range(N_WIN):
        row = src_ref[j * N_WIN + s] + iota
        ok = jnp.logical_and(row >= lo_ref[j * N_WIN + s], row < hi_ref[j * N_WIN + s])
        ids.append(jnp.where(ok, row, -1))
    row_id = jnp.concatenate(ids, axis=1).astype(F32)
    sel = (jnp.where(pos1 == row_id, g1, 0.0) + jnp.where(pos2 == row_id, g2, 0.0)).astype(BF16)

    for s in range(N_WIN):
        _window_copy(y_hbm, ybuf_ref, sem_ref, 0, s).wait()
    ffn = jnp.dot(sel, ybuf_ref[...], preferred_element_type=F32)
    out = x_ref[...] + mod_ref[5:6, :] * _rms(ffn, g_ref[3:4, :])

    @pl.when(j < P_TILES)
    def _():
        yp_ref[...] = out

    @pl.when(j >= P_TILES)
    def _():
        ys_ref[...] = out


def _combine(l, win_src, win_lo, win_hi, base, info, x, mod, norm_g, ys):
    return pl.pallas_call(
        _combine_kernel,
        out_shape=(jax.ShapeDtypeStruct((N_PROMPT, D_MODEL), F32),
                   jax.ShapeDtypeStruct((N_SAMPLE, D_MODEL), F32)),
        grid_spec=pltpu.PrefetchScalarGridSpec(
            num_scalar_prefetch=4,
            grid=(N_TILES,),
            in_specs=[pl.BlockSpec((TOK_TILE, 128), lambda i, *_: (i, 0)),
                      pl.BlockSpec((TOK_TILE, D_MODEL), lambda i, *_: (i, 0)),
                      pl.BlockSpec((None, None, 6, D_MODEL), lambda i, *_: (l, _cond_id(i), 0, 0)),
                      pl.BlockSpec((None, 4, D_MODEL), lambda i, *_: (l, 0, 0)),
                      pl.BlockSpec(memory_space=pl.ANY)],
            out_specs=(pl.BlockSpec((TOK_TILE, D_MODEL), lambda i, *_: (jnp.minimum(i, P_TILES - 1), 0)),
                       pl.BlockSpec((TOK_TILE, D_MODEL), lambda i, *_: (jnp.maximum(i - P_TILES, 0), 0))),
            scratch_shapes=[pltpu.VMEM((N_WIN * WIN, D_MODEL), BF16),
                            pltpu.SemaphoreType.DMA((N_WIN,))]),
        compiler_params=_cparams(1),
        name="moe_combine",
    )(win_src, win_lo, win_hi, base, info, x, mod, norm_g, ys)


def _moe_ffn(l, i_moe, h2, x, mod, norm_g, router_w, moe_w13, moe_w2):
    rw_pad = jnp.zeros((D_MODEL, 128), BF16).at[:, :N_EXPERTS].set(router_w[i_moe].astype(BF16))
    info, info_t, cb, tot = _router(h2, rw_pad)

    counts = tot[0, :N_EXPERTS].astype(jnp.int32)
    padded = (counts + ROW_TILE - 1) // ROW_TILE * ROW_TILE
    seg_end = jnp.cumsum(padded)
    base = (seg_end - padded).astype(jnp.int32)
    cbx = cb[:, 0, :N_EXPERTS].astype(jnp.int32)
    cb_end = jnp.concatenate([cbx[1:], counts[None, :]], axis=0)

    row0 = jnp.arange(SORT_TILES, dtype=jnp.int32) * ROW_TILE
    tile_e = jnp.minimum(jnp.sum(seg_end[None, :] <= row0[:, None], axis=1), N_EXPERTS - 1).astype(jnp.int32)
    k0 = row0 - base[tile_e]
    tile_valid = jnp.logical_and(k0 >= 0, k0 < counts[tile_e])
    tile_first = jnp.logical_and(tile_valid, k0 == 0)
    k_end = jnp.minimum(k0 + ROW_TILE, counts[tile_e])
    cb_t = cbx.T[tile_e]
    c_lo = jnp.sum(cb_t <= k0[:, None], axis=1) - 1
    c_hi = jnp.sum(cb_t < k_end[:, None], axis=1) - 1
    c_n = jnp.where(tile_valid, c_hi - c_lo + 1, 0).astype(jnp.int32)
    c_lo = jnp.where(tile_valid, c_lo, 0).astype(jnp.int32)
    tile_valid = tile_valid.astype(jnp.int32)
    tile_first = tile_first.astype(jnp.int32)

    seg_lo = base[None, :] + cbx
    seg_hi = base[None, :] + cb_end
    seg_n = seg_hi - seg_lo
    w0 = seg_lo // 16 * 16
    nw = jnp.where(seg_n > 0, (seg_lo - w0 + seg_n + WIN - 1) // WIN, 0)
    nw_end = jnp.cumsum(nw, axis=1)
    slot = jnp.arange(N_WIN, dtype=jnp.int32)
    slot_e = jnp.minimum(jnp.sum(nw_end[:, None, :] <= slot[None, :, None], axis=2), N_EXPERTS - 1)
    take = lambda a: jnp.take_along_axis(a, slot_e, axis=1)
    slot_k = slot[None, :] - (take(nw_end) - take(nw))
    slot_ok = slot[None, :] < nw_end[:, -1:]
    slot_src = take(w0) + WIN * slot_k
    win_src = jnp.where(slot_ok, slot_src, 0).astype(jnp.int32).reshape(-1)
    win_lo = jnp.where(slot_ok, take(seg_lo), 0).astype(jnp.int32).reshape(-1)
    win_hi = jnp.where(slot_ok, take(seg_hi), 0).astype(jnp.int32).reshape(-1)

    xs = _dispatch(c_lo, c_n, base, info_t, h2)
    hs = _moe_up(i_moe, tile_e, tile_valid, tile_first, xs, moe_w13)
    ys = _moe_down(i_moe, tile_e, tile_valid, tile_first, hs, moe_w2)
    return _combine(l, win_src, win_lo, win_hi, base, info, x, mod, norm_g, ys)


def kernel(x_prompt, x_sample, cache_k, cache_v, c, c_ctx, w_ada, b_ada, norm_g, w_in, w_out,
           lam_params, subln_g, dw_weight, dw_bias, conv_ln_g, conv_ln_b, dense_w13, dense_w2,
           router_w, moe_w13, moe_w2):
    x = (x_prompt.reshape(N_PROMPT, D_MODEL), x_sample.reshape(N_SAMPLE, D_MODEL))
    cond8 = jnp.zeros((8, D_MODEL), F32).at[0].set(c_ctx).at[1:1 + DEC_BATCH].set(c)
    mod = _ada_modulation(cond8, w_ada, b_ada).reshape(DEPTH, 8, 6, D_MODEL)
    rope = _rope_tables()
    ck = cache_k.reshape(DEC_BATCH, DEPTH, PAST_LEN * ATT_HEADS, V_DIM)
    cv = cache_v.reshape(DEC_BATCH, DEPTH, PAST_LEN * ATT_HEADS, V_DIM)

    caches = None
    for l in range(DEPTH):
        lam_init = 0.8 - 0.6 * math.exp(-0.3 * l)
        qkvg, *caches = _inproj(l, x, mod, norm_g, w_in, rope, caches)
        o_ctx, o_lat = _attention(l, lam_init, lam_params, subln_g, qkvg, ck, cv)
        cvo = _conformer_conv(l, qkvg, dw_weight, dw_bias, conv_ln_g, conv_ln_b)
        x, h2 = _outproj(l, o_ctx, o_lat, cvo, x, mod, norm_g, w_out)
        i = l // 2
        if l % 2 == 0:
            x = _dense_ffn(l, h2, x, mod, norm_g, dense_w13[i].astype(BF16), dense_w2[i].astype(BF16))
        else:
            x = _moe_ffn(l, i, h2, x, mod, norm_g, router_w, moe_w13, moe_w2)

    xp, xs = x if isinstance(x, tuple) else (x[:N_PROMPT], x[N_PROMPT:])
    new_k, new_v = (a.reshape(BATCH, DEPTH, SEQ, ATT_HEADS, V_DIM) for a in caches)
    return (xp.reshape(BATCH, SEQ, D_MODEL), xs.reshape(DEC_BATCH, DEC_SEQ, D_MODEL), new_k, new_v)
```

```python
import functools
import math

import jax
import jax.numpy as jnp
from jax import lax
from jax.experimental import pallas as pl
from jax.experimental.pallas import tpu as pltpu

F32 = jnp.float32
BF16 = jnp.bfloat16

D_MODEL = 1024
BATCH = 32
SEQ = 256
DEPTH = 2
DEC_BATCH = 2
DEC_SEQ = 2048
PAST_LEN = 512
GRID_W = 64
ATT_HEADS = 4
QK_DIM = 64
V_DIM = 128
ATT_W = ATT_HEADS * V_DIM
IN_W = 5 * ATT_W
CONV_CH = 512
CONV_WIDTH = 31
D_FF = 2816
N_EXPERTS = 8
D_FF_EXPERT = 3584
ROPE_THETA = 10000.0
NORM_EPS = 1e-6
LN_EPS = 1e-5

N_PROMPT = BATCH * SEQ
N_SAMPLE = DEC_BATCH * DEC_SEQ
N_TOK = N_PROMPT + N_SAMPLE

TOK_TILE = 512
N_TILES = N_TOK // TOK_TILE
P_TILES = N_PROMPT // TOK_TILE
S_TILES_PER_BATCH = DEC_SEQ // TOK_TILE

ROW_TILE = 512
SORT_TILES = (2 * N_TOK + N_EXPERTS * ROW_TILE) // ROW_TILE + 1
SORT_ROWS = SORT_TILES * ROW_TILE
UP_TILE = 896
UP_TILES = D_FF_EXPERT // UP_TILE
DOWN_TILE = 512
WIN = 128
N_WIN = 16

VMEM_LIMIT = 56 * 1024 * 1024


def _cparams(n_axes):
    return pltpu.CompilerParams(dimension_semantics=("arbitrary",) * n_axes,
                                vmem_limit_bytes=VMEM_LIMIT)


def _cond_id(i):
    return jnp.where(i < P_TILES, 0, 1 + (i - P_TILES) // S_TILES_PER_BATCH)


def _sigmoid(x):
    return 1.0 / (1.0 + jnp.exp(-x))


def _rms(x, g):
    return x * lax.rsqrt(jnp.mean(x * x, axis=-1, keepdims=True) + NORM_EPS) * g


def _ada_kernel(c_ref, w_ref, b_ref, o_ref):
    c = c_ref[...]
    s = (c * _sigmoid(c)).astype(BF16)
    o_ref[...] = jnp.dot(s, w_ref[...].astype(BF16), preferred_element_type=F32) + b_ref[...]


def _ada_modulation(cond8, w_ada, b_ada):
    tn = 1536
    n = 6 * D_MODEL
    return pl.pallas_call(
        _ada_kernel,
        out_shape=jax.ShapeDtypeStruct((DEPTH, 8, n), F32),
        grid=(DEPTH, n // tn),
        in_specs=[pl.BlockSpec((8, D_MODEL), lambda l, j: (0, 0)),
                  pl.BlockSpec((None, D_MODEL, tn), lambda l, j: (l, 0, j)),
                  pl.BlockSpec((None, 1, tn), lambda l, j: (l, 0, j))],
        out_specs=pl.BlockSpec((None, 8, tn), lambda l, j: (l, 0, j)),
        compiler_params=_cparams(2),
        name="ada_modulation",
    )(cond8, w_ada, b_ada.reshape(DEPTH, 1, n))


def _tile_x(xa_ref, xb_ref):
    return jnp.where(pl.program_id(0) < P_TILES, xa_ref[...], xb_ref[...])


def _x_specs(x):
    if isinstance(x, tuple):
        xa, xb = x
        b_idx = lambda i, *_: (jnp.maximum(i - P_TILES, 0), 0)
    else:
        xa = xb = x
        b_idx = lambda i, *_: (jnp.maximum(i, P_TILES), 0)
    a_idx = lambda i, *_: (jnp.minimum(i, P_TILES - 1), 0)
    return [pl.BlockSpec((TOK_TILE, D_MODEL), a_idx), pl.BlockSpec((TOK_TILE, D_MODEL), b_idx)], (xa, xb)


SEQ_PER_TILE = TOK_TILE // SEQ
CACHE_ROWS = SEQ * ATT_HEADS


def _inproj_kernel(*refs, layer):
    if layer == 0:
        (xa_ref, xb_ref, mod_ref, g_ref, w_ref, cos_ref, sina_ref, sinb_ref,
         qkvg_ref, kc_ref, vc_ref, wbf_ref) = refs
    else:
        (xa_ref, xb_ref, mod_ref, g_ref, w_ref, cos_ref, sina_ref, sinb_ref, _, _,
         qkvg_ref, kc_ref, vc_ref, wbf_ref) = refs
    i = pl.program_id(0)

    @pl.when(i == 0)
    def _():
        wbf_ref[...] = w_ref[...].astype(BF16)

    h = _rms(_tile_x(xa_ref, xb_ref), g_ref[0:1, :]) * (1.0 + mod_ref[1:2, :]) + mod_ref[0:1, :]
    proj = jnp.dot(h.astype(BF16), wbf_ref[...], preferred_element_type=F32)
    qkvg_ref[:, 2 * ATT_W:] = proj[:, 2 * ATT_W:].astype(BF16)

    @pl.when(i < P_TILES)
    def _():
        qkvg_ref[:, :2 * ATT_W] = proj[:, :2 * ATT_W].astype(BF16)
        for ref, col0 in ((kc_ref, ATT_W), (vc_ref, 2 * ATT_W)):
            for s in range(SEQ_PER_TILE):
                for hd in range(ATT_HEADS):
                    val = proj[SEQ * s:SEQ * (s + 1), col0 + V_DIM * hd:col0 + V_DIM * (hd + 1)]
                    rows = pl.ds(hd, SEQ, stride=ATT_HEADS)
                    if layer == 0:
                        ref[s, 0, rows, :] = val
                    else:
                        ref[s, rows, :] = val
            if layer == 0:
                ref[:, 1:] = jnp.zeros((SEQ_PER_TILE, DEPTH - 1, CACHE_ROWS, V_DIM), F32)

    @pl.when(i >= P_TILES)
    def _():
        cos = cos_ref[...]
        sina = sina_ref[...]
        sinb = sinb_ref[...]
        for c in range(2 * ATT_W // 128):
            xg = proj[:, 128 * c:128 * (c + 1)]
            fwd = pltpu.roll(xg, 128 - 16, 1)
            bwd = pltpu.roll(xg, 16, 1)
            qkvg_ref[:, 128 * c:128 * (c + 1)] = (xg * cos + fwd * sina + bwd * sinb).astype(BF16)


def _inproj(l, x, mod, norm_g, w_in, rope, caches):
    cos, sina, sinb = rope
    x_specs, x_args = _x_specs(x)

    def rope_idx(i):
        return (jnp.maximum(i - P_TILES, 0) % S_TILES_PER_BATCH, 0)

    cache_shape = jax.ShapeDtypeStruct((BATCH, DEPTH, CACHE_ROWS, V_DIM), F32)
    if l == 0:
        cache_spec = pl.BlockSpec((SEQ_PER_TILE, DEPTH, CACHE_ROWS, V_DIM),
                                  lambda i: (jnp.minimum(i, P_TILES - 1), 0, 0, 0))
        extra_specs, extra_args, aliases = [], (), {}
    else:
        cache_spec = pl.BlockSpec((SEQ_PER_TILE, None, CACHE_ROWS, V_DIM),
                                  lambda i: (jnp.minimum(i, P_TILES - 1), l, 0, 0))
        extra_specs = [pl.BlockSpec(memory_space=pl.ANY)] * 2
        extra_args = tuple(caches)
        aliases = {8: 1, 9: 2}

    return pl.pallas_call(
        functools.partial(_inproj_kernel, layer=l),
        out_shape=(jax.ShapeDtypeStruct((N_TOK, IN_W), BF16), cache_shape, cache_shape),
        grid=(N_TILES,),
        in_specs=x_specs + [
            pl.BlockSpec((None, None, 6, D_MODEL), lambda i: (l, _cond_id(i), 0, 0)),
            pl.BlockSpec((None, 4, D_MODEL), lambda i: (l, 0, 0)),
            pl.BlockSpec((None, D_MODEL, IN_W), lambda i: (l, 0, 0), pipeline_mode=pl.Buffered(1)),
            pl.BlockSpec((TOK_TILE, 128), rope_idx),
            pl.BlockSpec((TOK_TILE, 128), rope_idx),
            pl.BlockSpec((TOK_TILE, 128), rope_idx)] + extra_specs,
        out_specs=(pl.BlockSpec((TOK_TILE, IN_W), lambda i: (i, 0)), cache_spec, cache_spec),
        scratch_shapes=[pltpu.VMEM((D_MODEL, IN_W), BF16)],
        input_output_aliases=aliases,
        compiler_params=_cparams(1),
        name=f"inproj_{l}",
    )(*x_args, mod, norm_g, w_in, cos, sina, sinb, *extra_args)


def _rope_tables():
    rows = DEC_SEQ // GRID_W
    row_pos = jnp.repeat(jnp.arange(rows, dtype=F32), GRID_W)
    col_pos = jnp.tile(jnp.arange(GRID_W, dtype=F32), rows)
    half = QK_DIM // 2
    inv_freq = 1.0 / (ROPE_THETA ** (jnp.arange(0, half, 2, dtype=F32) / half))
    ang_r = row_pos[:, None] * inv_freq
    ang_c = col_pos[:, None] * inv_freq
    ang = jnp.concatenate([ang_r, ang_r, ang_c, ang_c], axis=-1)
    cos = jnp.tile(jnp.cos(ang), (1, 2))
    sin = jnp.tile(jnp.sin(ang), (1, 2))
    first = (jnp.arange(128) % 32) < 16
    sina = jnp.where(first[None, :], -sin, 0.0)
    sinb = jnp.where(first[None, :], 0.0, sin)
    return cos, sina, sinb


def _attn_kernel(*refs, lam_init, has_ext, maps_per_pass):
    if has_ext:
        lamp_ref, sub_ref, q_ref, k_ref, v_ref, ke_ref, ve_ref, o_ref = refs
    else:
        lamp_ref, sub_ref, q_ref, k_ref, v_ref, o_ref = refs
    lp = lamp_ref[...]
    lam = (jnp.exp(jnp.sum(lp[0:1] * lp[1:2], axis=-1, keepdims=True))
           - jnp.exp(jnp.sum(lp[2:3] * lp[3:4], axis=-1, keepdims=True)) + lam_init)
    lane = lax.broadcasted_iota(jnp.int32, (1, V_DIM), 1)
    nt = (((1,), (1,)), ((), ()))
    scale = QK_DIM ** -0.5
    map_scale = [jnp.where(lane < QK_DIM, scale, 0.0).astype(BF16),
                 jnp.where(lane < QK_DIM, 0.0, scale).astype(BF16)]

    tq = q_ref.shape[0]
    kh, keh, v_aug, ve_aug = {}, {}, {}, {}
    for hd in range(ATT_HEADS):
        cols = slice(V_DIM * hd, V_DIM * (hd + 1))
        kh[hd] = k_ref[:, cols]
        vh = v_ref[:, cols]
        v_aug[hd] = jnp.concatenate([vh, jnp.ones_like(vh)], axis=1)
        if has_ext:
            head_rows = pl.ds(hd, PAST_LEN, stride=ATT_HEADS)
            keh[hd] = ke_ref[head_rows, :].astype(BF16)
            veh = ve_ref[head_rows, :].astype(BF16)
            ve_aug[hd] = jnp.concatenate([veh, jnp.ones_like(veh)], axis=1)

    pairs = [(hd, m) for hd in range(ATT_HEADS) for m in range(2)]
    ratio = {}
    for g0 in range(0, len(pairs), maps_per_pass):
        group = pairs[g0:g0 + maps_per_pass]
        qm = {pr: q_ref[:, V_DIM * pr[0]:V_DIM * (pr[0] + 1)] * map_scale[pr[1]] for pr in group}
        s = jnp.concatenate([lax.dot_general(qm[pr], kh[pr[0]], nt, preferred_element_type=F32)
                             for pr in group], axis=0)
        mx = jnp.max(s, axis=-1, keepdims=True)
        if has_ext:
            se = jnp.concatenate([lax.dot_general(qm[pr], keh[pr[0]], nt, preferred_element_type=F32)
                                  for pr in group], axis=0)
            mx = jnp.maximum(mx, jnp.max(se, axis=-1, keepdims=True))
            pe = jnp.exp((se - mx).astype(BF16))
        p = jnp.exp((s - mx).astype(BF16))
        i = 0
        while i < len(group):
            hd = group[i][0]
            n = sum(1 for pr in group[i:] if pr[0] == hd)
            rows = slice(tq * i, tq * (i + n))
            pv = jnp.dot(p[rows], v_aug[hd], preferred_element_type=F32)
            if has_ext:
                pv = pv + jnp.dot(pe[rows], ve_aug[hd], preferred_element_type=F32)
            r = pv[:, :V_DIM] / pv[:, V_DIM:V_DIM + 1]
            for k in range(n):
                ratio[group[i + k]] = r[tq * k:tq * (k + 1)]
            i += n

    heads = []
    for hd in range(ATT_HEADS):
        o = ratio[(hd, 0)] - lam * ratio[(hd, 1)]
        heads.append((_rms(o, sub_ref[...]) * (1.0 - lam_init)).astype(BF16))
    o_ref[...] = jnp.concatenate(heads, axis=1)


def _attention(l, lam_init, lam_params, subln_g, qkvg, cache_k, cache_v):
    small = [pl.BlockSpec((None, 4, QK_DIM), lambda *_: (l, 0, 0)),
             pl.BlockSpec((None, 1, V_DIM), lambda *_: (l, 0, 0))]
    sub3 = subln_g.reshape(DEPTH, 1, V_DIM)

    o_ctx = pl.pallas_call(
        functools.partial(_attn_kernel, lam_init=lam_init, has_ext=False, maps_per_pass=2 * ATT_HEADS),
        out_shape=jax.ShapeDtypeStruct((N_PROMPT, ATT_W), BF16),
        grid=(BATCH,),
        in_specs=small + [pl.BlockSpec((SEQ, ATT_W), lambda b: (b, 0)),
                          pl.BlockSpec((SEQ, ATT_W), lambda b: (b, 1)),
                          pl.BlockSpec((SEQ, ATT_W), lambda b: (b, 2))],
        out_specs=pl.BlockSpec((SEQ, ATT_W), lambda b: (b, 0)),
        compiler_params=_cparams(1),
        name=f"attn_ctx_{l}",
    )(lam_params, sub3, qkvg, qkvg, qkvg)

    tq = 256
    q_tiles = DEC_SEQ // tq
    q_base = N_PROMPT // tq
    kv_base = N_PROMPT // DEC_SEQ

    def q_idx(b, i):
        return (q_base + b * q_tiles + i, 0)

    o_lat = pl.pallas_call(
        functools.partial(_attn_kernel, lam_init=lam_init, has_ext=True, maps_per_pass=1),
        out_shape=jax.ShapeDtypeStruct((N_SAMPLE, ATT_W), BF16),
        grid=(DEC_BATCH, q_tiles),
        in_specs=small + [pl.BlockSpec((tq, ATT_W), q_idx),
                          pl.BlockSpec((DEC_SEQ, ATT_W), lambda b, i: (kv_base + b, 1)),
                          pl.BlockSpec((DEC_SEQ, ATT_W), lambda b, i: (kv_base + b, 2)),
                          pl.BlockSpec((None, None, PAST_LEN * ATT_HEADS, V_DIM), lambda b, i: (b, l, 0, 0)),
                          pl.BlockSpec((None, None, PAST_LEN * ATT_HEADS, V_DIM), lambda b, i: (b, l, 0, 0))],
        out_specs=pl.BlockSpec((tq, ATT_W), lambda b, i: (b * q_tiles + i, 0)),
        compiler_params=_cparams(2),
        name=f"attn_lat_{l}",
    )(lam_params, sub3, qkvg, qkvg, qkvg, cache_k, cache_v)
    return o_ctx, o_lat


CONV_TILE = 256
CONV_HALO = 16
CONV_SUB = 64
CONV_TILES = N_TOK // CONV_TILE
CONV_P_TILES = N_PROMPT // CONV_TILE
CONV_S_TILES = DEC_SEQ // CONV_TILE


def _glu(a_ref, b_ref):
    return a_ref[...].astype(F32) * _sigmoid(b_ref[...].astype(F32))


def _conv_kernel(a_ref, b_ref, ap_ref, bp_ref, an_ref, bn_ref, w_ref, bias_ref, lg_ref, lb_ref,
                 o_ref, upad_ref):
    i = pl.program_id(0)
    q = (i - CONV_P_TILES) % CONV_S_TILES
    latent = i >= CONV_P_TILES
    has_prev = jnp.logical_and(latent, q != 0)
    has_next = jnp.logical_and(latent, q != CONV_S_TILES - 1)
    upad_ref[0, 0:CONV_HALO, :] = jnp.where(has_prev, _glu(ap_ref, bp_ref), 0.0)
    upad_ref[0, CONV_HALO:CONV_HALO + CONV_TILE, :] = _glu(a_ref, b_ref)
    upad_ref[0, CONV_HALO + CONV_TILE:, :] = jnp.where(has_next, _glu(an_ref, bn_ref), 0.0)
    rows = CONV_TILE + 2 * CONV_HALO - 8
    for s in range(1, 8):
        upad_ref[s, 0:rows, :] = upad_ref[0, s:s + rows, :]
    first_tap = CONV_HALO - CONV_WIDTH // 2

    for t in range(CONV_TILE // CONV_SUB):
        base = t * CONV_SUB
        acc = jnp.zeros((CONV_SUB, CONV_CH), F32)
        for j in range(CONV_WIDTH):
            lo = base + (first_tap + j) // 8 * 8
            acc = acc + upad_ref[(first_tap + j) % 8, lo:lo + CONV_SUB, :] * w_ref[j:j + 1, :]
        y = acc + bias_ref[...]
        mu = jnp.mean(y, axis=-1, keepdims=True)
        yc = y - mu
        var = jnp.mean(yc * yc, axis=-1, keepdims=True)
        z = yc * lax.rsqrt(var + LN_EPS) * lg_ref[...] + lb_ref[...]
        o_ref[base:base + CONV_SUB, :] = (z * _sigmoid(z)).astype(BF16)


def _conformer_conv(l, qkvg, dw_w, dw_b, ln_g, ln_b):
    def vec(a):
        return a.reshape(DEPTH, 1, CONV_CH)

    per = CONV_TILE // CONV_HALO
    last = N_TOK // CONV_HALO - 1

    def prev_idx(col):
        return lambda i: (jnp.maximum(i * per - 1, 0), col)

    def next_idx(col):
        return lambda i: (jnp.minimum((i + 1) * per, last), col)

    return pl.pallas_call(
        _conv_kernel,
        out_shape=jax.ShapeDtypeStruct((N_TOK, CONV_CH), BF16),
        grid=(CONV_TILES,),
        in_specs=[pl.BlockSpec((CONV_TILE, ATT_W), lambda i: (i, 3)),
                  pl.BlockSpec((CONV_TILE, ATT_W), lambda i: (i, 4)),
                  pl.BlockSpec((CONV_HALO, ATT_W), prev_idx(3)),
                  pl.BlockSpec((CONV_HALO, ATT_W), prev_idx(4)),
                  pl.BlockSpec((CONV_HALO, ATT_W), next_idx(3)),
                  pl.BlockSpec((CONV_HALO, ATT_W), next_idx(4)),
                  pl.BlockSpec((None, CONV_WIDTH, CONV_CH), lambda i: (l, 0, 0)),
                  pl.BlockSpec((None, 1, CONV_CH), lambda i: (l, 0, 0)),
                  pl.BlockSpec((None, 1, CONV_CH), lambda i: (l, 0, 0)),
                  pl.BlockSpec((None, 1, CONV_CH), lambda i: (l, 0, 0))],
        out_specs=pl.BlockSpec((CONV_TILE, CONV_CH), lambda i: (i, 0)),
        scratch_shapes=[pltpu.VMEM((8, CONV_TILE + 2 * CONV_HALO, CONV_CH), F32)],
        compiler_params=_cparams(1),
        name=f"conv_{l}",
    )(qkvg, qkvg, qkvg, qkvg, qkvg, qkvg, dw_w, vec(dw_b), vec(ln_g), vec(ln_b))


def _outproj_kernel(oc_ref, ol_ref, cv_ref, xa_ref, xb_ref, mod_ref, g_ref, w_ref, xo_ref, h2_ref, wbf_ref):
    i = pl.program_id(0)

    @pl.when(i == 0)
    def _():
        wbf_ref[...] = w_ref[...].astype(BF16)

    o = jnp.where(i < P_TILES, oc_ref[...], ol_ref[...])
    m = (jnp.dot(o, wbf_ref[0:ATT_W, :], preferred_element_type=F32)
         + jnp.dot(cv_ref[...], wbf_ref[ATT_W:, :], preferred_element_type=F32))
    xn = _tile_x(xa_ref, xb_ref) + mod_ref[2:3, :] * _rms(m, g_ref[1:2, :])
    xo_ref[...] = xn
    h2_ref[...] = (_rms(xn, g_ref[2:3, :]) * (1.0 + mod_ref[4:5, :]) + mod_ref[3:4, :]).astype(BF16)


def _outproj(l, o_ctx, o_lat, cv, x, mod, norm_g, w_out):
    x_specs, x_args = _x_specs(x)
    return pl.pallas_call(
        _outproj_kernel,
        out_shape=(jax.ShapeDtypeStruct((N_TOK, D_MODEL), F32),
                   jax.ShapeDtypeStruct((N_TOK, D_MODEL), BF16)),
        grid=(N_TILES,),
        in_specs=[pl.BlockSpec((TOK_TILE, ATT_W), lambda i: (jnp.minimum(i, P_TILES - 1), 0)),
                  pl.BlockSpec((TOK_TILE, ATT_W), lambda i: (jnp.maximum(i - P_TILES, 0), 0)),
                  pl.BlockSpec((TOK_TILE, CONV_CH), lambda i: (i, 0))] + x_specs + [
                  pl.BlockSpec((None, None, 6, D_MODEL), lambda i: (l, _cond_id(i), 0, 0)),
                  pl.BlockSpec((None, 4, D_MODEL), lambda i: (l, 0, 0)),
                  pl.BlockSpec((None, D_MODEL, D_MODEL), lambda i: (l, 0, 0),
                               pipeline_mode=pl.Buffered(1))],
        out_specs=(pl.BlockSpec((TOK_TILE, D_MODEL), lambda i: (i, 0)),
                   pl.BlockSpec((TOK_TILE, D_MODEL), lambda i: (i, 0))),
        scratch_shapes=[pltpu.VMEM((D_MODEL, D_MODEL), BF16)],
        compiler_params=_cparams(1),
        name=f"outproj_{l}",
    )(o_ctx, o_lat, cv, *x_args, mod, norm_g, w_out)


DENSE_CHUNK = D_FF // 2


def _dense_ffn_kernel(h_ref, x_ref, mod_ref, g_ref, w13_ref, w2_ref, xo_ref):
    h = h_ref[...]
    acc = None
    for c in range(D_FF // DENSE_CHUNK):
        lo = c * DENSE_CHUNK
        gt = jnp.dot(h, w13_ref[:, lo:lo + DENSE_CHUNK], preferred_element_type=F32)
        up = jnp.dot(h, w13_ref[:, D_FF + lo:D_FF + lo + DENSE_CHUNK], preferred_element_type=F32)
        a = (gt * _sigmoid(gt) * up).astype(BF16)
        part = jnp.dot(a, w2_ref[lo:lo + DENSE_CHUNK, :], preferred_element_type=F32)
        acc = part if acc is None else acc + part
    xo_ref[...] = x_ref[...] + mod_ref[5:6, :] * _rms(acc, g_ref[3:4, :])


def _dense_ffn(l, h2, x, mod, norm_g, w13, w2):
    return pl.pallas_call(
        _dense_ffn_kernel,
        out_shape=jax.ShapeDtypeStruct((N_TOK, D_MODEL), F32),
        grid=(N_TILES,),
        in_specs=[pl.BlockSpec((TOK_TILE, D_MODEL), lambda i: (i, 0)),
                  pl.BlockSpec((TOK_TILE, D_MODEL), lambda i: (i, 0)),
                  pl.BlockSpec((None, None, 6, D_MODEL), lambda i: (l, _cond_id(i), 0, 0)),
                  pl.BlockSpec((None, 4, D_MODEL), lambda i: (l, 0, 0)),
                  pl.BlockSpec((D_MODEL, 2 * D_FF), lambda i: (0, 0), pipeline_mode=pl.Buffered(1)),
                  pl.BlockSpec((D_FF, D_MODEL), lambda i: (0, 0), pipeline_mode=pl.Buffered(1))],
        out_specs=pl.BlockSpec((TOK_TILE, D_MODEL), lambda i: (i, 0)),
        compiler_params=_cparams(1),
        name=f"dense_ffn_{l}",
    )(h2, x, mod, norm_g, w13, w2)


def _router_kernel(h_ref, rw_ref, info_ref, info_t_ref, cb_ref, tot_ref, carry_ref):
    i = pl.program_id(0)

    @pl.when(i == 0)
    def _():
        carry_ref[...] = jnp.zeros_like(carry_ref)

    lane = lax.broadcasted_iota(jnp.int32, (TOK_TILE, 128), 1)
    lanef = lane.astype(F32)
    logits = jnp.dot(h_ref[...], rw_ref[...], preferred_element_type=F32)
    logits = jnp.where(lane < N_EXPERTS, logits, -jnp.inf)
    big = jnp.asarray(128.0, F32)
    m1 = jnp.max(logits, axis=-1, keepdims=True)
    e1 = jnp.min(jnp.where(logits == m1, lanef, big), axis=-1, keepdims=True)
    oh1 = lanef == e1
    rest = jnp.where(oh1, -jnp.inf, logits)
    m2 = jnp.max(rest, axis=-1, keepdims=True)
    e2 = jnp.min(jnp.where(rest == m2, lanef, big), axis=-1, keepdims=True)
    oh2 = lanef == e2
    ex = jnp.exp(m2 - m1)
    g1 = 1.0 / (1.0 + ex)
    g2 = ex / (1.0 + ex)

    oh = jnp.where(jnp.logical_or(oh1, oh2), 1.0, 0.0)
    r = lax.broadcasted_iota(jnp.int32, (TOK_TILE, TOK_TILE), 0)
    c = lax.broadcasted_iota(jnp.int32, (TOK_TILE, TOK_TILE), 1)
    tri = jnp.where(c <= r, 1.0, 0.0).astype(BF16)
    incl = jnp.dot(tri, oh.astype(BF16), preferred_element_type=F32)
    carry = carry_ref[0:1, :]
    excl = incl - oh + carry
    rank1 = jnp.sum(jnp.where(oh1, excl, 0.0), axis=-1, keepdims=True)
    rank2 = jnp.sum(jnp.where(oh2, excl, 0.0), axis=-1, keepdims=True)

    info = jnp.where(lane == 0, e1, 0.0)
    for k, col in enumerate((e2, rank1, rank2, g1, g2), start=1):
        info = jnp.where(lane == k, col, info)
    info_ref[...] = info
    info_t_ref[...] = info.T[0:8, :]

    cb_ref[...] = carry_ref[...]
    new_carry = carry + incl[TOK_TILE - 1:TOK_TILE, :]
    carry_ref[...] = jnp.broadcast_to(new_carry, carry_ref.shape)
    tot_ref[...] = jnp.broadcast_to(new_carry, tot_ref.shape)


def _router(h2, rw_pad):
    return pl.pallas_call(
        _router_kernel,
        out_shape=(jax.ShapeDtypeStruct((N_TOK, 128), F32),
                   jax.ShapeDtypeStruct((N_TILES, 8, TOK_TILE), F32),
                   jax.ShapeDtypeStruct((N_TILES, 8, 128), F32),
                   jax.ShapeDtypeStruct((8, 128), F32)),
        grid=(N_TILES,),
        in_specs=[pl.BlockSpec((TOK_TILE, D_MODEL), lambda i: (i, 0)),
                  pl.BlockSpec((D_MODEL, 128), lambda i: (0, 0))],
        out_specs=(pl.BlockSpec((TOK_TILE, 128), lambda i: (i, 0)),
                   pl.BlockSpec((None, 8, TOK_TILE), lambda i: (i, 0, 0)),
                   pl.BlockSpec((None, 8, 128), lambda i: (i, 0, 0)),
                   pl.BlockSpec((8, 128), lambda i: (0, 0))),
        scratch_shapes=[pltpu.VMEM((8, 128), F32)],
        compiler_params=_cparams(1),
        name="moe_router",
    )(h2, rw_pad)


def _sorted_pos(expert, rank, base_ref):
    start = jnp.zeros_like(rank)
    for e in range(N_EXPERTS):
        start = jnp.where(expert == float(e), base_ref[e].astype(F32), start)
    return start + rank


def _dispatch_kernel(clo_ref, cn_ref, base_ref, info_t_ref, x_ref, o_ref, acc_ref):
    r = pl.program_id(0)
    acc_ref[...] = jnp.zeros_like(acc_ref)
    rows = (r * ROW_TILE + lax.broadcasted_iota(jnp.int32, (ROW_TILE, TOK_TILE), 0)).astype(F32)

    def body(k, carry):
        c = clo_ref[r] + k
        it = info_t_ref[c]
        pos1 = _sorted_pos(it[0:1, :], it[2:3, :], base_ref)
        pos2 = _sorted_pos(it[1:2, :], it[3:4, :], base_ref)
        hit = jnp.logical_or(rows == pos1, rows == pos2)
        sel = jnp.where(hit, 1.0, 0.0).astype(BF16)
        off = pl.multiple_of(c * TOK_TILE, TOK_TILE)
        acc_ref[...] += jnp.dot(sel, x_ref[pl.ds(off, TOK_TILE), :], preferred_element_type=F32)
        return carry

    lax.fori_loop(0, cn_ref[r], body, 0)
    o_ref[...] = acc_ref[...].astype(BF16)


def _dispatch(c_lo, c_n, base, info_t, h2):
    return pl.pallas_call(
        _dispatch_kernel,
        out_shape=jax.ShapeDtypeStruct((SORT_ROWS, D_MODEL), BF16),
        grid_spec=pltpu.PrefetchScalarGridSpec(
            num_scalar_prefetch=3,
            grid=(SORT_TILES,),
            in_specs=[pl.BlockSpec((N_TILES, 8, TOK_TILE), lambda r, *_: (0, 0, 0)),
                      pl.BlockSpec((N_TOK, D_MODEL), lambda r, *_: (0, 0),
                                   pipeline_mode=pl.Buffered(1))],
            out_specs=pl.BlockSpec((ROW_TILE, D_MODEL), lambda r, *_: (r, 0)),
            scratch_shapes=[pltpu.VMEM((ROW_TILE, D_MODEL), F32)]),
        compiler_params=_cparams(1),
        name="moe_dispatch",
    )(c_lo, c_n, base, info_t, h2)


def _expert_weight_ring(c, r, n_c, te_ref, tf_ref, nx_ref, copies, cast):
    @pl.when(jnp.logical_and(c == 0, r == 0))
    def _():
        for cp in copies(te_ref[0], 0):
            cp.start()

    @pl.when(tf_ref[r] == 1)
    def _():
        for cp in copies(te_ref[r], c):
            cp.wait()
        cast()
        nr = nx_ref[r]
        nc = c + (nr <= r).astype(jnp.int32)

        @pl.when(nc < n_c)
        def _():
            for cp in copies(te_ref[nr], nc):
                cp.start()


def _moe_up_kernel(te_ref, tv_ref, tf_ref, nx_ref, x_ref, w_hbm, h_ref, wst_ref, wbf_ref, sem_ref, *, i_moe):
    f = pl.program_id(0)
    r = pl.program_id(1)

    def copies(e, ft):
        out = []
        for part in range(2):
            col = pl.multiple_of((part * UP_TILES + ft) * UP_TILE, 128)
            out.append(pltpu.make_async_copy(w_hbm.at[i_moe, e, :, pl.ds(col, UP_TILE)],
                                             wst_ref.at[part], sem_ref.at[part]))
        return out

    def cast():
        wbf_ref[...] = wst_ref[...].astype(BF16)

    _expert_weight_ring(f, r, UP_TILES, te_ref, tf_ref, nx_ref, copies, cast)

    @pl.when(tv_ref[r] == 1)
    def _():
        x = x_ref[...]
        gt = jnp.dot(x, wbf_ref[0], preferred_element_type=F32)
        up = jnp.dot(x, wbf_ref[1], preferred_element_type=F32)
        h_ref[...] = (gt * _sigmoid(gt) * up).astype(BF16)

    @pl.when(tv_ref[r] == 0)
    def _():
        h_ref[...] = jnp.zeros_like(h_ref)


def _moe_up(i_moe, tile_e, tile_valid, tile_first, tile_next, xs, w13):
    return pl.pallas_call(
        functools.partial(_moe_up_kernel, i_moe=i_moe),
        out_shape=jax.ShapeDtypeStruct((SORT_ROWS, D_FF_EXPERT), BF16),
        grid_spec=pltpu.PrefetchScalarGridSpec(
            num_scalar_prefetch=4,
            grid=(UP_TILES, SORT_TILES),
            in_specs=[pl.BlockSpec((ROW_TILE, D_MODEL), lambda f, r, *_: (r, 0)),
                      pl.BlockSpec(memory_space=pl.ANY)],
            out_specs=pl.BlockSpec((ROW_TILE, UP_TILE), lambda f, r, *_: (r, f)),
            scratch_shapes=[pltpu.VMEM((2, D_MODEL, UP_TILE), F32),
                            pltpu.VMEM((2, D_MODEL, UP_TILE), BF16),
                            pltpu.SemaphoreType.DMA((2,))]),
        compiler_params=_cparams(2),
        name="moe_up",
    )(tile_e, tile_valid, tile_first, tile_next, xs, w13)


def _moe_down_kernel(te_ref, tv_ref, tf_ref, nx_ref, h_ref, w_hbm, y_ref, wst_ref, wbf_ref, sem_ref, *, i_moe):
    n = pl.program_id(0)
    r = pl.program_id(1)

    def copies(e, nt):
        col = pl.multiple_of(nt * DOWN_TILE, 128)
        return [pltpu.make_async_copy(w_hbm.at[i_moe, e, :, pl.ds(col, DOWN_TILE)], wst_ref, sem_ref.at[0])]

    def cast():
        wbf_ref[...] = wst_ref[...].astype(BF16)

    _expert_weight_ring(n, r, D_MODEL // DOWN_TILE, te_ref, tf_ref, nx_ref, copies, cast)

    @pl.when(tv_ref[r] == 1)
    def _():
        y_ref[...] = jnp.dot(h_ref[...], wbf_ref[...], preferred_element_type=F32).astype(BF16)

    @pl.when(tv_ref[r] == 0)
    def _():
        y_ref[...] = jnp.zeros_like(y_ref)


def _moe_down(i_moe, tile_e, tile_valid, tile_first, tile_next, hs, w2):
    return pl.pallas_call(
        functools.partial(_moe_down_kernel, i_moe=i_moe),
        out_shape=jax.ShapeDtypeStruct((SORT_ROWS, D_MODEL), BF16),
        grid_spec=pltpu.PrefetchScalarGridSpec(
            num_scalar_prefetch=4,
            grid=(D_MODEL // DOWN_TILE, SORT_TILES),
            in_specs=[pl.BlockSpec((ROW_TILE, D_FF_EXPERT), lambda n, r, *_: (r, 0)),
                      pl.BlockSpec(memory_space=pl.ANY)],
            out_specs=pl.BlockSpec((ROW_TILE, DOWN_TILE), lambda n, r, *_: (r, n)),
            scratch_shapes=[pltpu.VMEM((D_FF_EXPERT, DOWN_TILE), F32),
                            pltpu.VMEM((D_FF_EXPERT, DOWN_TILE), BF16),
                            pltpu.SemaphoreType.DMA((1,))]),
        compiler_params=_cparams(2),
        name="moe_down",
    )(tile_e, tile_valid, tile_first, tile_next, hs, w2)


def _window_copy(y_hbm, ybuf_ref, sem_ref, src, buf, slot):
    return pltpu.make_async_copy(y_hbm.at[pl.ds(src, WIN), :],
                                 ybuf_ref.at[buf, pl.ds(slot * WIN, WIN), :], sem_ref.at[buf, slot])


def _start_windows(src_ref, y_hbm, ybuf_ref, sem_ref, tile, buf):
    for s in range(N_WIN):
        src = pl.multiple_of(src_ref[tile * N_WIN + s], 16)
        _window_copy(y_hbm, ybuf_ref, sem_ref, src, buf, s).start()


def _combine_kernel(src_ref, lo_ref, hi_ref, base_ref, info_ref, x_ref, mod_ref, g_ref, y_hbm,
                    yp_ref, ys_ref, ybuf_ref, sem_ref):
    j = pl.program_id(0)
    buf = j % 2

    @pl.when(j == 0)
    def _():
        _start_windows(src_ref, y_hbm, ybuf_ref, sem_ref, 0, 0)

    @pl.when(j + 1 < N_TILES)
    def _():
        _start_windows(src_ref, y_hbm, ybuf_ref, sem_ref, j + 1, 1 - buf)

    info = info_ref[...]
    pos1 = _sorted_pos(info[:, 0:1], info[:, 2:3], base_ref)
    pos2 = _sorted_pos(info[:, 1:2], info[:, 3:4], base_ref)
    g1, g2 = info[:, 4:5], info[:, 5:6]
    iota = lax.broadcasted_iota(jnp.int32, (1, WIN), 1)
    ids = []
    for s in range(N_WIN):
        row = src_ref[j * N_WIN + s] + iota
        ok = jnp.logical_and(row >= lo_ref[j * N_WIN + s], row < hi_ref[j * N_WIN + s])
        ids.append(jnp.where(ok, row, -1))
    row_id = jnp.concatenate(ids, axis=1).astype(F32)
    sel = (jnp.where(pos1 == row_id, g1, 0.0) + jnp.where(pos2 == row_id, g2, 0.0)).astype(BF16)

    for s in range(N_WIN):
        _window_copy(y_hbm, ybuf_ref, sem_ref, 0, buf, s).wait()
    ffn = jnp.dot(sel, ybuf_ref[buf], preferred_element_type=F32)
    out = x_ref[...] + mod_ref[5:6, :] * _rms(ffn, g_ref[3:4, :])

    @pl.when(j < P_TILES)
    def _():
        yp_ref[...] = out

    @pl.when(j >= P_TILES)
    def _():
        ys_ref[...] = out


def _combine(l, win_src, win_lo, win_hi, base, info, x, mod, norm_g, ys):
    return pl.pallas_call(
        _combine_kernel,
        out_shape=(jax.ShapeDtypeStruct((N_PROMPT, D_MODEL), F32),
                   jax.ShapeDtypeStruct((N_SAMPLE, D_MODEL), F32)),
        grid_spec=pltpu.PrefetchScalarGridSpec(
            num_scalar_prefetch=4,
            grid=(N_TILES,),
            in_specs=[pl.BlockSpec((TOK_TILE, 128), lambda i, *_: (i, 0)),
                      pl.BlockSpec((TOK_TILE, D_MODEL), lambda i, *_: (i, 0)),
                      pl.BlockSpec((None, None, 6, D_MODEL), lambda i, *_: (l, _cond_id(i), 0, 0)),
                      pl.BlockSpec((None, 4, D_MODEL), lambda i, *_: (l, 0, 0)),
                      pl.BlockSpec(memory_space=pl.ANY)],
            out_specs=(pl.BlockSpec((TOK_TILE, D_MODEL), lambda i, *_: (jnp.minimum(i, P_TILES - 1), 0)),
                       pl.BlockSpec((TOK_TILE, D_MODEL), lambda i, *_: (jnp.maximum(i - P_TILES, 0), 0))),
            scratch_shapes=[pltpu.VMEM((2, N_WIN * WIN, D_MODEL), BF16),
                            pltpu.SemaphoreType.DMA((2, N_WIN))]),
        compiler_params=_cparams(1),
        name="moe_combine",
    )(win_src, win_lo, win_hi, base, info, x, mod, norm_g, ys)


def _moe_ffn(l, i_moe, h2, x, mod, norm_g, router_w, moe_w13, moe_w2):
    rw_pad = jnp.zeros((D_MODEL, 128), BF16).at[:, :N_EXPERTS].set(router_w[i_moe].astype(BF16))
    info, info_t, cb, tot = _router(h2, rw_pad)

    counts = tot[0, :N_EXPERTS].astype(jnp.int32)
    padded = (counts + ROW_TILE - 1) // ROW_TILE * ROW_TILE
    seg_end = jnp.cumsum(padded)
    base = (seg_end - padded).astype(jnp.int32)
    cbx = cb[:, 0, :N_EXPERTS].astype(jnp.int32)
    cb_end = jnp.concatenate([cbx[1:], counts[None, :]], axis=0)

    row0 = jnp.arange(SORT_TILES, dtype=jnp.int32) * ROW_TILE
    tile_e = jnp.minimum(jnp.sum(seg_end[None, :] <= row0[:, None], axis=1), N_EXPERTS - 1).astype(jnp.int32)
    k0 = row0 - base[tile_e]
    tile_valid = jnp.logical_and(k0 >= 0, k0 < counts[tile_e])
    tile_first = jnp.logical_and(tile_valid, k0 == 0)
    k_end = jnp.minimum(k0 + ROW_TILE, counts[tile_e])
    cb_t = cbx.T[tile_e]
    c_lo = jnp.sum(cb_t <= k0[:, None], axis=1) - 1
    c_hi = jnp.sum(cb_t < k_end[:, None], axis=1) - 1
    c_n = jnp.where(tile_valid, c_hi - c_lo + 1, 0).astype(jnp.int32)
    c_lo = jnp.where(tile_valid, c_lo, 0).astype(jnp.int32)
    tile_idx = jnp.arange(SORT_TILES, dtype=jnp.int32)
    first_idx = jnp.where(tile_first, tile_idx, SORT_TILES)
    later = jnp.where(tile_idx[None, :] > tile_idx[:, None], first_idx[None, :], SORT_TILES)
    tile_next = jnp.min(later, axis=1)
    tile_next = jnp.where(tile_next >= SORT_TILES, 0, tile_next).astype(jnp.int32)
    tile_valid = tile_valid.astype(jnp.int32)
    tile_first = tile_first.astype(jnp.int32)

    seg_lo = base[None, :] + cbx
    seg_hi = base[None, :] + cb_end
    seg_n = seg_hi - seg_lo
    w0 = seg_lo // 16 * 16
    nw = jnp.where(seg_n > 0, (seg_lo - w0 + seg_n + WIN - 1) // WIN, 0)
    nw_end = jnp.cumsum(nw, axis=1)
    slot = jnp.arange(N_WIN, dtype=jnp.int32)
    slot_e = jnp.minimum(jnp.sum(nw_end[:, None, :] <= slot[None, :, None], axis=2), N_EXPERTS - 1)
    take = lambda a: jnp.take_along_axis(a, slot_e, axis=1)
    slot_k = slot[None, :] - (take(nw_end) - take(nw))
    slot_ok = slot[None, :] < nw_end[:, -1:]
    slot_src = take(w0) + WIN * slot_k
    win_src = jnp.where(slot_ok, slot_src, 0).astype(jnp.int32).reshape(-1)
    win_lo = jnp.where(slot_ok, take(seg_lo), 0).astype(jnp.int32).reshape(-1)
    win_hi = jnp.where(slot_ok, take(seg_hi), 0).astype(jnp.int32).reshape(-1)

    xs = _dispatch(c_lo, c_n, base, info_t, h2)
    hs = _moe_up(i_moe, tile_e, tile_valid, tile_first, tile_next, xs, moe_w13)
    ys = _moe_down(i_moe, tile_e, tile_valid, tile_first, tile_next, hs, moe_w2)
    return _combine(l, win_src, win_lo, win_hi, base, info, x, mod, norm_g, ys)


def kernel(x_prompt, x_sample, cache_k, cache_v, c, c_ctx, w_ada, b_ada, norm_g, w_in, w_out,
           lam_params, subln_g, dw_weight, dw_bias, conv_ln_g, conv_ln_b, dense_w13, dense_w2,
           router_w, moe_w13, moe_w2):
    x = (x_prompt.reshape(N_PROMPT, D_MODEL), x_sample.reshape(N_SAMPLE, D_MODEL))
    cond8 = jnp.zeros((8, D_MODEL), F32).at[0].set(c_ctx).at[1:1 + DEC_BATCH].set(c)
    mod = _ada_modulation(cond8, w_ada, b_ada).reshape(DEPTH, 8, 6, D_MODEL)
    rope = _rope_tables()
    ck = cache_k.reshape(DEC_BATCH, DEPTH, PAST_LEN * ATT_HEADS, V_DIM)
    cv = cache_v.reshape(DEC_BATCH, DEPTH, PAST_LEN * ATT_HEADS, V_DIM)

    caches = None
    for l in range(DEPTH):
        lam_init = 0.8 - 0.6 * math.exp(-0.3 * l)
        qkvg, *caches = _inproj(l, x, mod, norm_g, w_in, rope, caches)
        o_ctx, o_lat = _attention(l, lam_init, lam_params, subln_g, qkvg, ck, cv)
        cvo = _conformer_conv(l, qkvg, dw_weight, dw_bias, conv_ln_g, conv_ln_b)
        x, h2 = _outproj(l, o_ctx, o_lat, cvo, x, mod, norm_g, w_out)
        i = l // 2
        if l % 2 == 0:
            x = _dense_ffn(l, h2, x, mod, norm_g, dense_w13[i].astype(BF16), dense_w2[i].astype(BF16))
        else:
            x = _moe_ffn(l, i, h2, x, mod, norm_g, router_w, moe_w13, moe_w2)

    xp, xs = x if isinstance(x, tuple) else (x[:N_PROMPT], x[N_PROMPT:])
    new_k, new_v = (a.reshape(BATCH, DEPTH, SEQ, ATT_HEADS, V_DIM) for a in caches)
    return (xp.reshape(BATCH, SEQ, D_MODEL), xs.reshape(DEC_BATCH, DEC_SEQ, D_MODEL), new_k, new_v)
```

```python
import functools
import math

import jax
import jax.numpy as jnp
import numpy as np
from jax import lax
from jax.experimental import pallas as pl
from jax.experimental.pallas import tpu as pltpu

F32 = jnp.float32
BF16 = jnp.bfloat16

D_MODEL = 1024
BATCH = 32
SEQ = 256
DEPTH = 2
DEC_BATCH = 2
DEC_SEQ = 2048
PAST_LEN = 512
GRID_W = 64
ATT_HEADS = 4
QK_DIM = 64
V_DIM = 128
ATT_W = ATT_HEADS * V_DIM
IN_W = 5 * ATT_W
CONV_CH = 512
CONV_WIDTH = 31
D_FF = 2816
N_EXPERTS = 8
D_FF_EXPERT = 3584
ROPE_THETA = 10000.0
NORM_EPS = 1e-6
LN_EPS = 1e-5

N_PROMPT = BATCH * SEQ
N_SAMPLE = DEC_BATCH * DEC_SEQ
N_TOK = N_PROMPT + N_SAMPLE

TOK_TILE = 512
N_TILES = N_TOK // TOK_TILE
P_TILES = N_PROMPT // TOK_TILE
S_TILES_PER_BATCH = DEC_SEQ // TOK_TILE

ROW_TILE = 512
SORT_TILES = (2 * N_TOK + N_EXPERTS * ROW_TILE) // ROW_TILE + 1
SORT_ROWS = SORT_TILES * ROW_TILE
DISPATCH_TILE = 256
UP_TILE = 896
UP_TILES = D_FF_EXPERT // UP_TILE
DOWN_TILE = 512
WIN = 128
N_WIN = 16

VMEM_LIMIT = 56 * 1024 * 1024


def _cparams(n_axes):
    return pltpu.CompilerParams(dimension_semantics=("arbitrary",) * n_axes,
                                vmem_limit_bytes=VMEM_LIMIT)


def _cond_id(i):
    return jnp.where(i < P_TILES, 0, 1 + (i - P_TILES) // S_TILES_PER_BATCH)


def _sigmoid(x):
    return 1.0 / (1.0 + jnp.exp(-x))


def _rms(x, g):
    return x * lax.rsqrt(jnp.mean(x * x, axis=-1, keepdims=True) + NORM_EPS) * g


def _ada_kernel(c_ref, w_ref, b_ref, o_ref):
    c = c_ref[...]
    s = (c * _sigmoid(c)).astype(BF16)
    o_ref[...] = jnp.dot(s, w_ref[...].astype(BF16), preferred_element_type=F32) + b_ref[...]


def _ada_modulation(cond8, w_ada, b_ada):
    tn = 1536
    n = 6 * D_MODEL
    return pl.pallas_call(
        _ada_kernel,
        out_shape=jax.ShapeDtypeStruct((DEPTH, 8, n), F32),
        grid=(DEPTH, n // tn),
        in_specs=[pl.BlockSpec((8, D_MODEL), lambda l, j: (0, 0)),
                  pl.BlockSpec((None, D_MODEL, tn), lambda l, j: (l, 0, j)),
                  pl.BlockSpec((None, 1, tn), lambda l, j: (l, 0, j))],
        out_specs=pl.BlockSpec((None, 8, tn), lambda l, j: (l, 0, j)),
        compiler_params=_cparams(2),
        name="ada_modulation",
    )(cond8, w_ada, b_ada.reshape(DEPTH, 1, n))


def _tile_x(xa_ref, xb_ref):
    return jnp.where(pl.program_id(0) < P_TILES, xa_ref[...], xb_ref[...])


def _x_specs(x):
    if isinstance(x, tuple):
        xa, xb = x
        b_idx = lambda i, *_: (jnp.maximum(i - P_TILES, 0), 0)
    else:
        xa = xb = x
        b_idx = lambda i, *_: (jnp.maximum(i, P_TILES), 0)
    a_idx = lambda i, *_: (jnp.minimum(i, P_TILES - 1), 0)
    return [pl.BlockSpec((TOK_TILE, D_MODEL), a_idx), pl.BlockSpec((TOK_TILE, D_MODEL), b_idx)], (xa, xb)


SEQ_PER_TILE = TOK_TILE // SEQ
CACHE_ROWS = SEQ * ATT_HEADS


def _inproj_kernel(*refs, layer):
    if layer == 0:
        (xa_ref, xb_ref, mod_ref, g_ref, w_ref, cos_ref, sina_ref, sinb_ref,
         qkvg_ref, kc_ref, vc_ref, wbf_ref) = refs
    else:
        (xa_ref, xb_ref, mod_ref, g_ref, w_ref, cos_ref, sina_ref, sinb_ref, _, _,
         qkvg_ref, kc_ref, vc_ref, wbf_ref) = refs
    i = pl.program_id(0)

    @pl.when(i == 0)
    def _():
        wbf_ref[...] = w_ref[...].astype(BF16)

    h = _rms(_tile_x(xa_ref, xb_ref), g_ref[0:1, :]) * (1.0 + mod_ref[1:2, :]) + mod_ref[0:1, :]
    proj = jnp.dot(h.astype(BF16), wbf_ref[...], preferred_element_type=F32)
    qkvg_ref[:, 2 * ATT_W:] = proj[:, 2 * ATT_W:].astype(BF16)

    @pl.when(i < P_TILES)
    def _():
        qkvg_ref[:, :2 * ATT_W] = proj[:, :2 * ATT_W].astype(BF16)
        for ref, col0 in ((kc_ref, ATT_W), (vc_ref, 2 * ATT_W)):
            for s in range(SEQ_PER_TILE):
                for hd in range(ATT_HEADS):
                    val = proj[SEQ * s:SEQ * (s + 1), col0 + V_DIM * hd:col0 + V_DIM * (hd + 1)]
                    rows = pl.ds(hd, SEQ, stride=ATT_HEADS)
                    if layer == 0:
                        ref[s, 0, rows, :] = val
                    else:
                        ref[s, rows, :] = val
            if layer == 0:
                ref[:, 1:] = jnp.zeros((SEQ_PER_TILE, DEPTH - 1, CACHE_ROWS, V_DIM), F32)

    @pl.when(i >= P_TILES)
    def _():
        cos = cos_ref[...]
        sina = sina_ref[...]
        sinb = sinb_ref[...]
        for c in range(2 * ATT_W // 128):
            xg = proj[:, 128 * c:128 * (c + 1)]
            fwd = pltpu.roll(xg, 128 - 16, 1)
            bwd = pltpu.roll(xg, 16, 1)
            qkvg_ref[:, 128 * c:128 * (c + 1)] = (xg * cos + fwd * sina + bwd * sinb).astype(BF16)


def _inproj(l, x, mod, norm_g, w_in, rope, caches):
    cos, sina, sinb = rope
    x_specs, x_args = _x_specs(x)

    def rope_idx(i):
        return (jnp.maximum(i - P_TILES, 0) % S_TILES_PER_BATCH, 0)

    cache_shape = jax.ShapeDtypeStruct((BATCH, DEPTH, CACHE_ROWS, V_DIM), F32)
    if l == 0:
        cache_spec = pl.BlockSpec((SEQ_PER_TILE, DEPTH, CACHE_ROWS, V_DIM),
                                  lambda i: (jnp.minimum(i, P_TILES - 1), 0, 0, 0))
        extra_specs, extra_args, aliases = [], (), {}
    else:
        cache_spec = pl.BlockSpec((SEQ_PER_TILE, None, CACHE_ROWS, V_DIM),
                                  lambda i: (jnp.minimum(i, P_TILES - 1), l, 0, 0))
        extra_specs = [pl.BlockSpec(memory_space=pl.ANY)] * 2
        extra_args = tuple(caches)
        aliases = {8: 1, 9: 2}

    return pl.pallas_call(
        functools.partial(_inproj_kernel, layer=l),
        out_shape=(jax.ShapeDtypeStruct((N_TOK, IN_W), BF16), cache_shape, cache_shape),
        grid=(N_TILES,),
        in_specs=x_specs + [
            pl.BlockSpec((None, None, 6, D_MODEL), lambda i: (l, _cond_id(i), 0, 0)),
            pl.BlockSpec((None, 4, D_MODEL), lambda i: (l, 0, 0)),
            pl.BlockSpec((None, D_MODEL, IN_W), lambda i: (l, 0, 0), pipeline_mode=pl.Buffered(1)),
            pl.BlockSpec((TOK_TILE, 128), rope_idx),
            pl.BlockSpec((TOK_TILE, 128), rope_idx),
            pl.BlockSpec((TOK_TILE, 128), rope_idx)] + extra_specs,
        out_specs=(pl.BlockSpec((TOK_TILE, IN_W), lambda i: (i, 0)), cache_spec, cache_spec),
        scratch_shapes=[pltpu.VMEM((D_MODEL, IN_W), BF16)],
        input_output_aliases=aliases,
        compiler_params=_cparams(1),
        name=f"inproj_{l}",
    )(*x_args, mod, norm_g, w_in, cos, sina, sinb, *extra_args)


def _rope_tables():
    rows = DEC_SEQ // GRID_W
    row_pos = np.repeat(np.arange(rows, dtype=np.float64), GRID_W)
    col_pos = np.tile(np.arange(GRID_W, dtype=np.float64), rows)
    half = QK_DIM // 2
    inv_freq = 1.0 / (ROPE_THETA ** (np.arange(0, half, 2, dtype=np.float64) / half))
    ang_r = row_pos[:, None] * inv_freq
    ang_c = col_pos[:, None] * inv_freq
    ang = np.concatenate([ang_r, ang_r, ang_c, ang_c], axis=-1)
    cos = np.tile(np.cos(ang), (1, 2)).astype(np.float32)
    sin = np.tile(np.sin(ang), (1, 2)).astype(np.float32)
    first = (np.arange(128) % 32) < 16
    sina = np.where(first[None, :], -sin, 0.0).astype(np.float32)
    sinb = np.where(first[None, :], 0.0, sin).astype(np.float32)
    return jnp.asarray(cos), jnp.asarray(sina), jnp.asarray(sinb)


def _attn_kernel(*refs, lam_init, has_ext):
    if has_ext:
        lamp_ref, sub_ref, q_ref, k_ref, v_ref, ke_ref, ve_ref, o_ref = refs
    else:
        lamp_ref, sub_ref, q_ref, k_ref, v_ref, o_ref = refs
    lp = lamp_ref[...]
    lam = (jnp.exp(jnp.sum(lp[0:1] * lp[1:2], axis=-1, keepdims=True))
           - jnp.exp(jnp.sum(lp[2:3] * lp[3:4], axis=-1, keepdims=True)) + lam_init)
    lane = lax.broadcasted_iota(jnp.int32, (1, V_DIM), 1)
    nt = (((1,), (1,)), ((), ()))
    scale = QK_DIM ** -0.5
    map_scale = [jnp.where(lane < QK_DIM, scale, 0.0).astype(BF16),
                 jnp.where(lane < QK_DIM, 0.0, scale).astype(BF16)]

    tq = q_ref.shape[0]
    head_cols = [slice(V_DIM * hd, V_DIM * (hd + 1)) for hd in range(ATT_HEADS)]

    def q_map(hd, m):
        return q_ref[:, head_cols[hd]] * map_scale[m]

    def with_ones(v):
        return jnp.concatenate([v, jnp.ones_like(v)], axis=1)

    outs = []
    if has_ext:
        for hd in range(ATT_HEADS):
            kh = k_ref[:, head_cols[hd]]
            v_aug = with_ones(v_ref[:, head_cols[hd]])
            head_rows = pl.ds(hd, PAST_LEN, stride=ATT_HEADS)
            keh = ke_ref[head_rows, :].astype(BF16)
            ve_aug = with_ones(ve_ref[head_rows, :].astype(BF16))
            ratio = []
            for m in range(2):
                qm = q_map(hd, m)
                s = lax.dot_general(qm, kh, nt, preferred_element_type=F32)
                se = lax.dot_general(qm, keh, nt, preferred_element_type=F32)
                mx = jnp.maximum(jnp.max(s, axis=-1, keepdims=True), jnp.max(se, axis=-1, keepdims=True))
                pv = (jnp.dot(jnp.exp((s - mx).astype(BF16)), v_aug, preferred_element_type=F32)
                      + jnp.dot(jnp.exp((se - mx).astype(BF16)), ve_aug, preferred_element_type=F32))
                ratio.append(pv[:, :V_DIM] / pv[:, V_DIM:V_DIM + 1])
            outs.append(ratio[0] - lam * ratio[1])
    else:
        pairs = [(hd, m) for hd in range(ATT_HEADS) for m in range(2)]
        s = jnp.concatenate([lax.dot_general(q_map(hd, m), k_ref[:, head_cols[hd]], nt,
                                             preferred_element_type=F32) for hd, m in pairs], axis=0)
        p = jnp.exp((s - jnp.max(s, axis=-1, keepdims=True)).astype(BF16))
        for hd in range(ATT_HEADS):
            pv = jnp.dot(p[2 * tq * hd:2 * tq * (hd + 1)], with_ones(v_ref[:, head_cols[hd]]),
                         preferred_element_type=F32)
            ratio = pv[:, :V_DIM] / pv[:, V_DIM:V_DIM + 1]
            outs.append(ratio[:tq] - lam * ratio[tq:])

    heads = [(_rms(o, sub_ref[...]) * (1.0 - lam_init)).astype(BF16) for o in outs]
    o_ref[...] = jnp.concatenate(heads, axis=1)


def _attention(l, lam_init, lam_params, subln_g, qkvg, cache_k, cache_v):
    small = [pl.BlockSpec((None, 4, QK_DIM), lambda *_: (l, 0, 0)),
             pl.BlockSpec((None, 1, V_DIM), lambda *_: (l, 0, 0))]
    sub3 = subln_g.reshape(DEPTH, 1, V_DIM)

    o_ctx = pl.pallas_call(
        functools.partial(_attn_kernel, lam_init=lam_init, has_ext=False),
        out_shape=jax.ShapeDtypeStruct((N_PROMPT, ATT_W), BF16),
        grid=(BATCH,),
        in_specs=small + [pl.BlockSpec((SEQ, ATT_W), lambda b: (b, 0)),
                          pl.BlockSpec((SEQ, ATT_W), lambda b: (b, 1)),
                          pl.BlockSpec((SEQ, ATT_W), lambda b: (b, 2))],
        out_specs=pl.BlockSpec((SEQ, ATT_W), lambda b: (b, 0)),
        compiler_params=_cparams(1),
        name=f"attn_ctx_{l}",
    )(lam_params, sub3, qkvg, qkvg, qkvg)

    tq = 256
    q_tiles = DEC_SEQ // tq
    q_base = N_PROMPT // tq
    kv_base = N_PROMPT // DEC_SEQ

    def q_idx(b, i):
        return (q_base + b * q_tiles + i, 0)

    o_lat = pl.pallas_call(
        functools.partial(_attn_kernel, lam_init=lam_init, has_ext=True),
        out_shape=jax.ShapeDtypeStruct((N_SAMPLE, ATT_W), BF16),
        grid=(DEC_BATCH, q_tiles),
        in_specs=small + [pl.BlockSpec((tq, ATT_W), q_idx),
                          pl.BlockSpec((DEC_SEQ, ATT_W), lambda b, i: (kv_base + b, 1)),
                          pl.BlockSpec((DEC_SEQ, ATT_W), lambda b, i: (kv_base + b, 2)),
                          pl.BlockSpec((None, None, PAST_LEN * ATT_HEADS, V_DIM), lambda b, i: (b, l, 0, 0)),
                          pl.BlockSpec((None, None, PAST_LEN * ATT_HEADS, V_DIM), lambda b, i: (b, l, 0, 0))],
        out_specs=pl.BlockSpec((tq, ATT_W), lambda b, i: (b * q_tiles + i, 0)),
        compiler_params=_cparams(2),
        name=f"attn_lat_{l}",
    )(lam_params, sub3, qkvg, qkvg, qkvg, cache_k, cache_v)
    return o_ctx, o_lat


CONV_TILE = 256
CONV_HALO = 16
CONV_SUB = 64
CONV_TILES = N_TOK // CONV_TILE
CONV_P_TILES = N_PROMPT // CONV_TILE
CONV_S_TILES = DEC_SEQ // CONV_TILE


def _glu(a_ref, b_ref):
    return a_ref[...].astype(F32) * _sigmoid(b_ref[...].astype(F32))


def _conv_kernel(a_ref, b_ref, ap_ref, bp_ref, an_ref, bn_ref, w_ref, bias_ref, lg_ref, lb_ref,
                 o_ref, upad_ref):
    i = pl.program_id(0)
    q = (i - CONV_P_TILES) % CONV_S_TILES
    latent = i >= CONV_P_TILES
    has_prev = jnp.logical_and(latent, q != 0)
    has_next = jnp.logical_and(latent, q != CONV_S_TILES - 1)
    upad_ref[0, 0:CONV_HALO, :] = jnp.where(has_prev, _glu(ap_ref, bp_ref), 0.0)
    upad_ref[0, CONV_HALO:CONV_HALO + CONV_TILE, :] = _glu(a_ref, b_ref)
    upad_ref[0, CONV_HALO + CONV_TILE:, :] = jnp.where(has_next, _glu(an_ref, bn_ref), 0.0)
    rows = CONV_TILE + 2 * CONV_HALO - 8
    for s in range(1, 8):
        upad_ref[s, 0:rows, :] = upad_ref[0, s:s + rows, :]
    first_tap = CONV_HALO - CONV_WIDTH // 2

    for t in range(CONV_TILE // CONV_SUB):
        base = t * CONV_SUB
        acc = jnp.zeros((CONV_SUB, CONV_CH), F32)
        for j in range(CONV_WIDTH):
            lo = base + (first_tap + j) // 8 * 8
            acc = acc + upad_ref[(first_tap + j) % 8, lo:lo + CONV_SUB, :] * w_ref[j:j + 1, :]
        y = acc + bias_ref[...]
        mu = jnp.mean(y, axis=-1, keepdims=True)
        yc = y - mu
        var = jnp.mean(yc * yc, axis=-1, keepdims=True)
        z = yc * lax.rsqrt(var + LN_EPS) * lg_ref[...] + lb_ref[...]
        o_ref[base:base + CONV_SUB, :] = (z * _sigmoid(z)).astype(BF16)


def _conformer_conv(l, qkvg, dw_w, dw_b, ln_g, ln_b):
    def vec(a):
        return a.reshape(DEPTH, 1, CONV_CH)

    per = CONV_TILE // CONV_HALO
    last = N_TOK // CONV_HALO - 1

    def prev_idx(col):
        return lambda i: (jnp.maximum(i * per - 1, 0), col)

    def next_idx(col):
        return lambda i: (jnp.minimum((i + 1) * per, last), col)

    return pl.pallas_call(
        _conv_kernel,
        out_shape=jax.ShapeDtypeStruct((N_TOK, CONV_CH), BF16),
        grid=(CONV_TILES,),
        in_specs=[pl.BlockSpec((CONV_TILE, ATT_W), lambda i: (i, 3)),
                  pl.BlockSpec((CONV_TILE, ATT_W), lambda i: (i, 4)),
                  pl.BlockSpec((CONV_HALO, ATT_W), prev_idx(3)),
                  pl.BlockSpec((CONV_HALO, ATT_W), prev_idx(4)),
                  pl.BlockSpec((CONV_HALO, ATT_W), next_idx(3)),
                  pl.BlockSpec((CONV_HALO, ATT_W), next_idx(4)),
                  pl.BlockSpec((None, CONV_WIDTH, CONV_CH), lambda i: (l, 0, 0)),
                  pl.BlockSpec((None, 1, CONV_CH), lambda i: (l, 0, 0)),
                  pl.BlockSpec((None, 1, CONV_CH), lambda i: (l, 0, 0)),
                  pl.BlockSpec((None, 1, CONV_CH), lambda i: (l, 0, 0))],
        out_specs=pl.BlockSpec((CONV_TILE, CONV_CH), lambda i: (i, 0)),
        scratch_shapes=[pltpu.VMEM((8, CONV_TILE + 2 * CONV_HALO, CONV_CH), F32)],
        compiler_params=_cparams(1),
        name=f"conv_{l}",
    )(qkvg, qkvg, qkvg, qkvg, qkvg, qkvg, dw_w, vec(dw_b), vec(ln_g), vec(ln_b))


def _outproj_kernel(oc_ref, ol_ref, cv_ref, xa_ref, xb_ref, mod_ref, g_ref, w_ref, xo_ref, h2_ref, wbf_ref):
    i = pl.program_id(0)

    @pl.when(i == 0)
    def _():
        wbf_ref[...] = w_ref[...].astype(BF16)

    half = TOK_TILE // 2
    for rows in (slice(0, half), slice(half, TOK_TILE)):
        o = jnp.where(i < P_TILES, oc_ref[rows, :], ol_ref[rows, :])
        m = (jnp.dot(o, wbf_ref[0:ATT_W, :], preferred_element_type=F32)
             + jnp.dot(cv_ref[rows, :], wbf_ref[ATT_W:, :], preferred_element_type=F32))
        x = jnp.where(i < P_TILES, xa_ref[rows, :], xb_ref[rows, :])
        xn = x + mod_ref[2:3, :] * _rms(m, g_ref[1:2, :])
        xo_ref[rows, :] = xn
        h2_ref[rows, :] = (_rms(xn, g_ref[2:3, :]) * (1.0 + mod_ref[4:5, :]) + mod_ref[3:4, :]).astype(BF16)


def _outproj(l, o_ctx, o_lat, cv, x, mod, norm_g, w_out):
    x_specs, x_args = _x_specs(x)
    return pl.pallas_call(
        _outproj_kernel,
        out_shape=(jax.ShapeDtypeStruct((N_TOK, D_MODEL), F32),
                   jax.ShapeDtypeStruct((N_TOK, D_MODEL), BF16)),
        grid=(N_TILES,),
        in_specs=[pl.BlockSpec((TOK_TILE, ATT_W), lambda i: (jnp.minimum(i, P_TILES - 1), 0)),
                  pl.BlockSpec((TOK_TILE, ATT_W), lambda i: (jnp.maximum(i - P_TILES, 0), 0)),
                  pl.BlockSpec((TOK_TILE, CONV_CH), lambda i: (i, 0))] + x_specs + [
                  pl.BlockSpec((None, None, 6, D_MODEL), lambda i: (l, _cond_id(i), 0, 0)),
                  pl.BlockSpec((None, 4, D_MODEL), lambda i: (l, 0, 0)),
                  pl.BlockSpec((None, D_MODEL, D_MODEL), lambda i: (l, 0, 0),
                               pipeline_mode=pl.Buffered(1))],
        out_specs=(pl.BlockSpec((TOK_TILE, D_MODEL), lambda i: (i, 0)),
                   pl.BlockSpec((TOK_TILE, D_MODEL), lambda i: (i, 0))),
        scratch_shapes=[pltpu.VMEM((D_MODEL, D_MODEL), BF16)],
        compiler_params=_cparams(1),
        name=f"outproj_{l}",
    )(o_ctx, o_lat, cv, *x_args, mod, norm_g, w_out)


DENSE_CHUNK = D_FF // 2


def _dense_ffn_kernel(h_ref, x_ref, mod_ref, g_ref, w13_ref, w2_ref, xo_ref):
    h = h_ref[...]
    acc = None
    for c in range(D_FF // DENSE_CHUNK):
        lo = c * DENSE_CHUNK
        gt = jnp.dot(h, w13_ref[:, lo:lo + DENSE_CHUNK], preferred_element_type=F32)
        up = jnp.dot(h, w13_ref[:, D_FF + lo:D_FF + lo + DENSE_CHUNK], preferred_element_type=F32)
        a = (gt * _sigmoid(gt) * up).astype(BF16)
        part = jnp.dot(a, w2_ref[lo:lo + DENSE_CHUNK, :], preferred_element_type=F32)
        acc = part if acc is None else acc + part
    xo_ref[...] = x_ref[...] + mod_ref[5:6, :] * _rms(acc, g_ref[3:4, :])


def _dense_ffn(l, h2, x, mod, norm_g, w13, w2):
    return pl.pallas_call(
        _dense_ffn_kernel,
        out_shape=jax.ShapeDtypeStruct((N_TOK, D_MODEL), F32),
        grid=(N_TILES,),
        in_specs=[pl.BlockSpec((TOK_TILE, D_MODEL), lambda i: (i, 0)),
                  pl.BlockSpec((TOK_TILE, D_MODEL), lambda i: (i, 0)),
                  pl.BlockSpec((None, None, 6, D_MODEL), lambda i: (l, _cond_id(i), 0, 0)),
                  pl.BlockSpec((None, 4, D_MODEL), lambda i: (l, 0, 0)),
                  pl.BlockSpec((D_MODEL, 2 * D_FF), lambda i: (0, 0), pipeline_mode=pl.Buffered(1)),
                  pl.BlockSpec((D_FF, D_MODEL), lambda i: (0, 0), pipeline_mode=pl.Buffered(1))],
        out_specs=pl.BlockSpec((TOK_TILE, D_MODEL), lambda i: (i, 0)),
        compiler_params=_cparams(1),
        name=f"dense_ffn_{l}",
    )(h2, x, mod, norm_g, w13, w2)


def _router_kernel(h_ref, rw_ref, info_ref, info_t_ref, cb_ref, tot_ref, carry_ref):
    i = pl.program_id(0)

    @pl.when(i == 0)
    def _():
        carry_ref[...] = jnp.zeros_like(carry_ref)

    lane = lax.broadcasted_iota(jnp.int32, (TOK_TILE, 128), 1)
    lanef = lane.astype(F32)
    logits = jnp.dot(h_ref[...], rw_ref[...], preferred_element_type=F32)
    logits = jnp.where(lane < N_EXPERTS, logits, -jnp.inf)
    big = jnp.asarray(128.0, F32)
    m1 = jnp.max(logits, axis=-1, keepdims=True)
    e1 = jnp.min(jnp.where(logits == m1, lanef, big), axis=-1, keepdims=True)
    oh1 = lanef == e1
    rest = jnp.where(oh1, -jnp.inf, logits)
    m2 = jnp.max(rest, axis=-1, keepdims=True)
    e2 = jnp.min(jnp.where(rest == m2, lanef, big), axis=-1, keepdims=True)
    oh2 = lanef == e2
    ex = jnp.exp(m2 - m1)
    g1 = 1.0 / (1.0 + ex)
    g2 = ex / (1.0 + ex)

    oh = jnp.where(jnp.logical_or(oh1, oh2), 1.0, 0.0)
    r = lax.broadcasted_iota(jnp.int32, (TOK_TILE, TOK_TILE), 0)
    c = lax.broadcasted_iota(jnp.int32, (TOK_TILE, TOK_TILE), 1)
    tri = jnp.where(c <= r, 1.0, 0.0).astype(BF16)
    incl = jnp.dot(tri, oh.astype(BF16), preferred_element_type=F32)
    carry = carry_ref[0:1, :]
    excl = incl - oh + carry
    rank1 = jnp.sum(jnp.where(oh1, excl, 0.0), axis=-1, keepdims=True)
    rank2 = jnp.sum(jnp.where(oh2, excl, 0.0), axis=-1, keepdims=True)

    info = jnp.where(lane == 0, e1, 0.0)
    for k, col in enumerate((e2, rank1, rank2, g1, g2), start=1):
        info = jnp.where(lane == k, col, info)
    info_ref[...] = info
    info_t_ref[...] = info.T[0:8, :]

    cb_ref[...] = carry_ref[...]
    new_carry = carry + incl[TOK_TILE - 1:TOK_TILE, :]
    carry_ref[...] = jnp.broadcast_to(new_carry, carry_ref.shape)
    tot_ref[...] = jnp.broadcast_to(new_carry, tot_ref.shape)


def _router(h2, rw_pad):
    return pl.pallas_call(
        _router_kernel,
        out_shape=(jax.ShapeDtypeStruct((N_TOK, 128), F32),
                   jax.ShapeDtypeStruct((N_TILES, 8, TOK_TILE), F32),
                   jax.ShapeDtypeStruct((N_TILES, 8, 128), F32),
                   jax.ShapeDtypeStruct((8, 128), F32)),
        grid=(N_TILES,),
        in_specs=[pl.BlockSpec((TOK_TILE, D_MODEL), lambda i: (i, 0)),
                  pl.BlockSpec((D_MODEL, 128), lambda i: (0, 0))],
        out_specs=(pl.BlockSpec((TOK_TILE, 128), lambda i: (i, 0)),
                   pl.BlockSpec((None, 8, TOK_TILE), lambda i: (i, 0, 0)),
                   pl.BlockSpec((None, 8, 128), lambda i: (i, 0, 0)),
                   pl.BlockSpec((8, 128), lambda i: (0, 0))),
        scratch_shapes=[pltpu.VMEM((8, 128), F32)],
        compiler_params=_cparams(1),
        name="moe_router",
    )(h2, rw_pad)


def _sorted_pos(expert, rank, base_ref):
    start = jnp.zeros_like(rank)
    for e in range(N_EXPERTS):
        start = jnp.where(expert == float(e), base_ref[e].astype(F32), start)
    return start + rank


def _dispatch_kernel(clo_ref, cn_ref, base_ref, info_t_ref, x_ref, o_ref):
    r = pl.program_id(0)
    o_ref[...] = jnp.zeros_like(o_ref)
    rows = (r * DISPATCH_TILE + lax.broadcasted_iota(jnp.int32, (DISPATCH_TILE, TOK_TILE), 0)).astype(F32)

    def body(k, carry):
        c = clo_ref[r] + k
        it = info_t_ref[c]
        pos1 = _sorted_pos(it[0:1, :], it[2:3, :], base_ref)
        pos2 = _sorted_pos(it[1:2, :], it[3:4, :], base_ref)
        hit = jnp.logical_or(rows == pos1, rows == pos2)
        sel = jnp.where(hit, 1.0, 0.0).astype(BF16)
        off = pl.multiple_of(c * TOK_TILE, TOK_TILE)
        o_ref[...] += jnp.dot(sel, x_ref[pl.ds(off, TOK_TILE), :], preferred_element_type=F32).astype(BF16)
        return carry

    lax.fori_loop(0, cn_ref[r], body, 0)


def _dispatch(c_lo, c_n, base, info_t, h2):
    return pl.pallas_call(
        _dispatch_kernel,
        out_shape=jax.ShapeDtypeStruct((SORT_ROWS, D_MODEL), BF16),
        grid_spec=pltpu.PrefetchScalarGridSpec(
            num_scalar_prefetch=3,
            grid=(SORT_ROWS // DISPATCH_TILE,),
            in_specs=[pl.BlockSpec((N_TILES, 8, TOK_TILE), lambda r, *_: (0, 0, 0)),
                      pl.BlockSpec((N_TOK, D_MODEL), lambda r, *_: (0, 0),
                                   pipeline_mode=pl.Buffered(1))],
            out_specs=pl.BlockSpec((DISPATCH_TILE, D_MODEL), lambda r, *_: (r, 0))),
        compiler_params=_cparams(1),
        name="moe_dispatch",
    )(c_lo, c_n, base, info_t, h2)


def _expert_weight_ring(c, r, n_c, te_ref, tf_ref, nx_ref, copies, cast):
    @pl.when(jnp.logical_and(c == 0, r == 0))
    def _():
        for cp in copies(te_ref[0], 0):
            cp.start()

    @pl.when(tf_ref[r] == 1)
    def _():
        for cp in copies(te_ref[r], c):
            cp.wait()
        cast()
        nr = nx_ref[r]
        nc = c + (nr <= r).astype(jnp.int32)

        @pl.when(nc < n_c)
        def _():
            for cp in copies(te_ref[nr], nc):
                cp.start()


def _moe_up_kernel(te_ref, tv_ref, tf_ref, nx_ref, x_ref, w_hbm, h_ref, wst_ref, wbf_ref, sem_ref, *, i_moe):
    f = pl.program_id(0)
    r = pl.program_id(1)

    def copies(e, ft):
        out = []
        for part in range(2):
            col = pl.multiple_of((part * UP_TILES + ft) * UP_TILE, 128)
            out.append(pltpu.make_async_copy(w_hbm.at[i_moe, e, :, pl.ds(col, UP_TILE)],
                                             wst_ref.at[part], sem_ref.at[part]))
        return out

    def cast():
        wbf_ref[...] = wst_ref[...].astype(BF16)

    _expert_weight_ring(f, r, UP_TILES, te_ref, tf_ref, nx_ref, copies, cast)

    @pl.when(tv_ref[r] == 1)
    def _():
        x = x_ref[...]
        gt = jnp.dot(x, wbf_ref[0], preferred_element_type=F32)
        up = jnp.dot(x, wbf_ref[1], preferred_element_type=F32)
        h_ref[...] = (gt * _sigmoid(gt) * up).astype(BF16)

    @pl.when(tv_ref[r] == 0)
    def _():
        h_ref[...] = jnp.zeros_like(h_ref)


def _moe_up(i_moe, tile_e, tile_valid, tile_first, tile_next, xs, w13):
    return pl.pallas_call(
        functools.partial(_moe_up_kernel, i_moe=i_moe),
        out_shape=jax.ShapeDtypeStruct((SORT_ROWS, D_FF_EXPERT), BF16),
        grid_spec=pltpu.PrefetchScalarGridSpec(
            num_scalar_prefetch=4,
            grid=(UP_TILES, SORT_TILES),
            in_specs=[pl.BlockSpec((ROW_TILE, D_MODEL), lambda f, r, *_: (r, 0)),
                      pl.BlockSpec(memory_space=pl.ANY)],
            out_specs=pl.BlockSpec((ROW_TILE, UP_TILE), lambda f, r, *_: (r, f)),
            scratch_shapes=[pltpu.VMEM((2, D_MODEL, UP_TILE), F32),
                            pltpu.VMEM((2, D_MODEL, UP_TILE), BF16),
                            pltpu.SemaphoreType.DMA((2,))]),
        compiler_params=_cparams(2),
        name="moe_up",
    )(tile_e, tile_valid, tile_first, tile_next, xs, w13)


def _moe_down_kernel(te_ref, tv_ref, tf_ref, nx_ref, h_ref, w_hbm, y_ref, wst_ref, wbf_ref, sem_ref, *, i_moe):
    n = pl.program_id(0)
    r = pl.program_id(1)

    def copies(e, nt):
        col = pl.multiple_of(nt * DOWN_TILE, 128)
        return [pltpu.make_async_copy(w_hbm.at[i_moe, e, :, pl.ds(col, DOWN_TILE)], wst_ref, sem_ref.at[0])]

    def cast():
        wbf_ref[...] = wst_ref[...].astype(BF16)

    _expert_weight_ring(n, r, D_MODEL // DOWN_TILE, te_ref, tf_ref, nx_ref, copies, cast)

    @pl.when(tv_ref[r] == 1)
    def _():
        y_ref[...] = jnp.dot(h_ref[...], wbf_ref[...], preferred_element_type=F32).astype(BF16)

    @pl.when(tv_ref[r] == 0)
    def _():
        y_ref[...] = jnp.zeros_like(y_ref)


def _moe_down(i_moe, tile_e, tile_valid, tile_first, tile_next, hs, w2):
    return pl.pallas_call(
        functools.partial(_moe_down_kernel, i_moe=i_moe),
        out_shape=jax.ShapeDtypeStruct((SORT_ROWS, D_MODEL), BF16),
        grid_spec=pltpu.PrefetchScalarGridSpec(
            num_scalar_prefetch=4,
            grid=(D_MODEL // DOWN_TILE, SORT_TILES),
            in_specs=[pl.BlockSpec((ROW_TILE, D_FF_EXPERT), lambda n, r, *_: (r, 0)),
                      pl.BlockSpec(memory_space=pl.ANY)],
            out_specs=pl.BlockSpec((ROW_TILE, DOWN_TILE), lambda n, r, *_: (r, n)),
            scratch_shapes=[pltpu.VMEM((D_FF_EXPERT, DOWN_TILE), F32),
                            pltpu.VMEM((D_FF_EXPERT, DOWN_TILE), BF16),
                            pltpu.SemaphoreType.DMA((1,))]),
        compiler_params=_cparams(2),
        name="moe_down",
    )(tile_e, tile_valid, tile_first, tile_next, hs, w2)


def _window_copy(y_hbm, ybuf_ref, sem_ref, src, buf, slot):
    return pltpu.make_async_copy(y_hbm.at[pl.ds(src, WIN), :],
                                 ybuf_ref.at[buf, pl.ds(slot * WIN, WIN), :], sem_ref.at[buf, slot])


def _start_windows(src_ref, y_hbm, ybuf_ref, sem_ref, tile, buf):
    for s in range(N_WIN):
        src = pl.multiple_of(src_ref[tile * N_WIN + s], 16)
        _window_copy(y_hbm, ybuf_ref, sem_ref, src, buf, s).start()


def _combine_kernel(src_ref, lo_ref, hi_ref, base_ref, info_ref, x_ref, mod_ref, g_ref, y_hbm,
                    yp_ref, ys_ref, ybuf_ref, sem_ref):
    j = pl.program_id(0)
    buf = j % 2

    @pl.when(j == 0)
    def _():
        _start_windows(src_ref, y_hbm, ybuf_ref, sem_ref, 0, 0)

    @pl.when(j + 1 < N_TILES)
    def _():
        _start_windows(src_ref, y_hbm, ybuf_ref, sem_ref, j + 1, 1 - buf)

    info = info_ref[...]
    pos1 = _sorted_pos(info[:, 0:1], info[:, 2:3], base_ref)
    pos2 = _sorted_pos(info[:, 1:2], info[:, 3:4], base_ref)
    g1, g2 = info[:, 4:5], info[:, 5:6]
    iota = lax.broadcasted_iota(jnp.int32, (1, WIN), 1)
    ids = []
    for s in range(N_WIN):
        row = src_ref[j * N_WIN + s] + iota
        ok = jnp.logical_and(row >= lo_ref[j * N_WIN + s], row < hi_ref[j * N_WIN + s])
        ids.append(jnp.where(ok, row, -1))
    row_id = jnp.concatenate(ids, axis=1).astype(F32)
    sel = (jnp.where(pos1 == row_id, g1, 0.0) + jnp.where(pos2 == row_id, g2, 0.0)).astype(BF16)

    for s in range(N_WIN):
        _window_copy(y_hbm, ybuf_ref, sem_ref, 0, buf, s).wait()
    ffn = jnp.dot(sel, ybuf_ref[buf], preferred_element_type=F32)
    out = x_ref[...] + mod_ref[5:6, :] * _rms(ffn, g_ref[3:4, :])

    @pl.when(j < P_TILES)
    def _():
        yp_ref[...] = out

    @pl.when(j >= P_TILES)
    def _():
        ys_ref[...] = out


def _combine(l, win_src, win_lo, win_hi, base, info, x, mod, norm_g, ys):
    return pl.pallas_call(
        _combine_kernel,
        out_shape=(jax.ShapeDtypeStruct((N_PROMPT, D_MODEL), F32),
                   jax.ShapeDtypeStruct((N_SAMPLE, D_MODEL), F32)),
        grid_spec=pltpu.PrefetchScalarGridSpec(
            num_scalar_prefetch=4,
            grid=(N_TILES,),
            in_specs=[pl.BlockSpec((TOK_TILE, 128), lambda i, *_: (i, 0)),
                      pl.BlockSpec((TOK_TILE, D_MODEL), lambda i, *_: (i, 0)),
                      pl.BlockSpec((None, None, 6, D_MODEL), lambda i, *_: (l, _cond_id(i), 0, 0)),
                      pl.BlockSpec((None, 4, D_MODEL), lambda i, *_: (l, 0, 0)),
                      pl.BlockSpec(memory_space=pl.ANY)],
            out_specs=(pl.BlockSpec((TOK_TILE, D_MODEL), lambda i, *_: (jnp.minimum(i, P_TILES - 1), 0)),
                       pl.BlockSpec((TOK_TILE, D_MODEL), lambda i, *_: (jnp.maximum(i - P_TILES, 0), 0))),
            scratch_shapes=[pltpu.VMEM((2, N_WIN * WIN, D_MODEL), BF16),
                            pltpu.SemaphoreType.DMA((2, N_WIN))]),
        compiler_params=_cparams(1),
        name="moe_combine",
    )(win_src, win_lo, win_hi, base, info, x, mod, norm_g, ys)


def _moe_ffn(l, i_moe, h2, x, mod, norm_g, router_w, moe_w13, moe_w2):
    rw_pad = jnp.zeros((D_MODEL, 128), BF16).at[:, :N_EXPERTS].set(router_w[i_moe].astype(BF16))
    info, info_t, cb, tot = _router(h2, rw_pad)

    counts = tot[0, :N_EXPERTS].astype(jnp.int32)
    padded = (counts + ROW_TILE - 1) // ROW_TILE * ROW_TILE
    seg_end = jnp.cumsum(padded)
    base = (seg_end - padded).astype(jnp.int32)
    cbx = cb[:, 0, :N_EXPERTS].astype(jnp.int32)
    cb_end = jnp.concatenate([cbx[1:], counts[None, :]], axis=0)

    row0 = jnp.arange(SORT_TILES, dtype=jnp.int32) * ROW_TILE
    tile_e = jnp.minimum(jnp.sum(seg_end[None, :] <= row0[:, None], axis=1), N_EXPERTS - 1).astype(jnp.int32)
    k0 = row0 - base[tile_e]
    tile_valid = jnp.logical_and(k0 >= 0, k0 < counts[tile_e])
    tile_first = jnp.logical_and(tile_valid, k0 == 0)
    per = ROW_TILE // DISPATCH_TILE
    d_e = jnp.repeat(tile_e, per)
    d_k0 = jnp.repeat(k0, per) + jnp.tile(jnp.arange(per, dtype=jnp.int32) * DISPATCH_TILE, SORT_TILES)
    d_valid = jnp.logical_and(d_k0 >= 0, d_k0 < counts[d_e])
    d_kend = jnp.minimum(d_k0 + DISPATCH_TILE, counts[d_e])
    cb_t = cbx.T[d_e]
    c_lo = jnp.sum(cb_t <= d_k0[:, None], axis=1) - 1
    c_hi = jnp.sum(cb_t < d_kend[:, None], axis=1) - 1
    c_n = jnp.where(d_valid, c_hi - c_lo + 1, 0).astype(jnp.int32)
    c_lo = jnp.where(d_valid, c_lo, 0).astype(jnp.int32)
    tile_idx = jnp.arange(SORT_TILES, dtype=jnp.int32)
    first_idx = jnp.where(tile_first, tile_idx, SORT_TILES)
    later = jnp.where(tile_idx[None, :] > tile_idx[:, None], first_idx[None, :], SORT_TILES)
    tile_next = jnp.min(later, axis=1)
    tile_next = jnp.where(tile_next >= SORT_TILES, 0, tile_next).astype(jnp.int32)
    tile_valid = tile_valid.astype(jnp.int32)
    tile_first = tile_first.astype(jnp.int32)

    seg_lo = base[None, :] + cbx
    seg_hi = base[None, :] + cb_end
    seg_n = seg_hi - seg_lo
    w0 = seg_lo // 16 * 16
    nw = jnp.where(seg_n > 0, (seg_lo - w0 + seg_n + WIN - 1) // WIN, 0)
    nw_end = jnp.cumsum(nw, axis=1)
    slot = jnp.arange(N_WIN, dtype=jnp.int32)
    slot_e = jnp.minimum(jnp.sum(nw_end[:, None, :] <= slot[None, :, None], axis=2), N_EXPERTS - 1)
    pick = slot_e[:, :, None] == jnp.arange(N_EXPERTS, dtype=jnp.int32)[None, None, :]
    take = lambda a: jnp.sum(jnp.where(pick, a[:, None, :], 0), axis=2)
    slot_k = slot[None, :] - (take(nw_end) - take(nw))
    slot_ok = slot[None, :] < nw_end[:, -1:]
    slot_src = take(w0) + WIN * slot_k
    win_src = jnp.where(slot_ok, slot_src, 0).astype(jnp.int32).reshape(-1)
    win_lo = jnp.where(slot_ok, take(seg_lo), 0).astype(jnp.int32).reshape(-1)
    win_hi = jnp.where(slot_ok, take(seg_hi), 0).astype(jnp.int32).reshape(-1)

    xs = _dispatch(c_lo, c_n, base, info_t, h2)
    hs = _moe_up(i_moe, tile_e, tile_valid, tile_first, tile_next, xs, moe_w13)
    ys = _moe_down(i_moe, tile_e, tile_valid, tile_first, tile_next, hs, moe_w2)
    return _combine(l, win_src, win_lo, win_hi, base, info, x, mod, norm_g, ys)


def kernel(x_prompt, x_sample, cache_k, cache_v, c, c_ctx, w_ada, b_ada, norm_g, w_in, w_out,
           lam_params, subln_g, dw_weight, dw_bias, conv_ln_g, conv_ln_b, dense_w13, dense_w2,
           router_w, moe_w13, moe_w2):
    x = (x_prompt.reshape(N_PROMPT, D_MODEL), x_sample.reshape(N_SAMPLE, D_MODEL))
    cond8 = jnp.zeros((8, D_MODEL), F32).at[0].set(c_ctx).at[1:1 + DEC_BATCH].set(c)
    mod = _ada_modulation(cond8, w_ada, b_ada).reshape(DEPTH, 8, 6, D_MODEL)
    rope = _rope_tables()
    ck = cache_k.reshape(DEC_BATCH, DEPTH, PAST_LEN * ATT_HEADS, V_DIM)
    cv = cache_v.reshape(DEC_BATCH, DEPTH, PAST_LEN * ATT_HEADS, V_DIM)

    caches = None
    for l in range(DEPTH):
        lam_init = 0.8 - 0.6 * math.exp(-0.3 * l)
        qkvg, *caches = _inproj(l, x, mod, norm_g, w_in, rope, caches)
        o_ctx, o_lat = _attention(l, lam_init, lam_params, subln_g, qkvg, ck, cv)
        cvo = _conformer_conv(l, qkvg, dw_weight, dw_bias, conv_ln_g, conv_ln_b)
        x, h2 = _outproj(l, o_ctx, o_lat, cvo, x, mod, norm_g, w_out)
        i = l // 2
        if l % 2 == 0:
            x = _dense_ffn(l, h2, x, mod, norm_g, dense_w13[i].astype(BF16), dense_w2[i].astype(BF16))
        else:
            x = _moe_ffn(l, i, h2, x, mod, norm_g, router_w, moe_w13, moe_w2)

    xp, xs = x if isinstance(x, tuple) else (x[:N_PROMPT], x[N_PROMPT:])
    new_k, new_v = (a.reshape(BATCH, DEPTH, SEQ, ATT_HEADS, V_DIM) for a in caches)
    return (xp.reshape(BATCH, SEQ, D_MODEL), xs.reshape(DEC_BATCH, DEC_SEQ, D_MODEL), new_k, new_v)
```

```python
import functools
import math

import jax
import jax.numpy as jnp
import numpy as np
from jax import lax
from jax.experimental import pallas as pl
from jax.experimental.pallas import tpu as pltpu

F32 = jnp.float32
BF16 = jnp.bfloat16

D_MODEL = 1024
BATCH = 32
SEQ = 256
DEPTH = 2
DEC_BATCH = 2
DEC_SEQ = 2048
PAST_LEN = 512
GRID_W = 64
ATT_HEADS = 4
QK_DIM = 64
V_DIM = 128
ATT_W = ATT_HEADS * V_DIM
IN_W = 5 * ATT_W
CONV_CH = 512
CONV_WIDTH = 31
D_FF = 2816
N_EXPERTS = 8
D_FF_EXPERT = 3584
ROPE_THETA = 10000.0
NORM_EPS = 1e-6
LN_EPS = 1e-5

N_PROMPT = BATCH * SEQ
N_SAMPLE = DEC_BATCH * DEC_SEQ
N_TOK = N_PROMPT + N_SAMPLE

TOK_TILE = 512
N_TILES = N_TOK // TOK_TILE
P_TILES = N_PROMPT // TOK_TILE
S_TILES_PER_BATCH = DEC_SEQ // TOK_TILE

ROW_TILE = 512
SORT_TILES = (2 * N_TOK + N_EXPERTS * ROW_TILE) // ROW_TILE + 1
SORT_ROWS = SORT_TILES * ROW_TILE
DISPATCH_TILE = 256
UP_TILE = 896
UP_TILES = D_FF_EXPERT // UP_TILE
DOWN_TILE = 512
WIN = 128
N_WIN = 16

VMEM_LIMIT = 56 * 1024 * 1024


def _cparams(n_axes):
    return pltpu.CompilerParams(dimension_semantics=("arbitrary",) * n_axes,
                                vmem_limit_bytes=VMEM_LIMIT)


def _cond_id(i):
    return jnp.where(i < P_TILES, 0, 1 + (i - P_TILES) // S_TILES_PER_BATCH)


def _sigmoid(x):
    return 1.0 / (1.0 + jnp.exp(-x))


def _rms(x, g):
    return x * lax.rsqrt(jnp.mean(x * x, axis=-1, keepdims=True) + NORM_EPS) * g


def _ada_kernel(c_ref, w_ref, b_ref, o_ref):
    c = c_ref[...]
    s = (c * _sigmoid(c)).astype(BF16)
    o_ref[...] = jnp.dot(s, w_ref[...].astype(BF16), preferred_element_type=F32) + b_ref[...]


def _ada_modulation(cond8, w_ada, b_ada):
    tn = 1536
    n = 6 * D_MODEL
    return pl.pallas_call(
        _ada_kernel,
        out_shape=jax.ShapeDtypeStruct((DEPTH, 8, n), F32),
        grid=(DEPTH, n // tn),
        in_specs=[pl.BlockSpec((8, D_MODEL), lambda l, j: (0, 0)),
                  pl.BlockSpec((None, D_MODEL, tn), lambda l, j: (l, 0, j)),
                  pl.BlockSpec((None, 1, tn), lambda l, j: (l, 0, j))],
        out_specs=pl.BlockSpec((None, 8, tn), lambda l, j: (l, 0, j)),
        compiler_params=_cparams(2),
        name="ada_modulation",
    )(cond8, w_ada, b_ada.reshape(DEPTH, 1, n))


def _tile_x(xa_ref, xb_ref):
    return jnp.where(pl.program_id(0) < P_TILES, xa_ref[...], xb_ref[...])


def _x_specs(x):
    last = N_TILES - 1
    if isinstance(x, tuple):
        xa, xb = x
        b_idx = lambda i, *_: (jnp.maximum(jnp.minimum(i, last) - P_TILES, 0), 0)
    else:
        xa = xb = x
        b_idx = lambda i, *_: (jnp.maximum(jnp.minimum(i, last), P_TILES), 0)
    a_idx = lambda i, *_: (jnp.minimum(i, P_TILES - 1), 0)
    return [pl.BlockSpec((TOK_TILE, D_MODEL), a_idx), pl.BlockSpec((TOK_TILE, D_MODEL), b_idx)], (xa, xb)


SEQ_PER_TILE = TOK_TILE // SEQ
CACHE_ROWS = SEQ * ATT_HEADS


CONV_LAG = 2
CONV_RING = 4
CONV_TILE = 256
CONV_HALO = 16
CONV_SUB = 32


def _conv_pass(upad_ref, w_ref, bias_ref, lg_ref, lb_ref, o_ref, row0):
    rows = CONV_TILE + 2 * CONV_HALO - 8
    for s in range(1, 8):
        upad_ref[s, 0:rows, :] = upad_ref[0, s:s + rows, :]
    first_tap = CONV_HALO - CONV_WIDTH // 2
    for t in range(CONV_TILE // CONV_SUB):
        base = t * CONV_SUB
        acc = jnp.zeros((CONV_SUB, CONV_CH), F32)
        for j in range(CONV_WIDTH):
            lo = base + (first_tap + j) // 8 * 8
            acc = acc + upad_ref[(first_tap + j) % 8, lo:lo + CONV_SUB, :] * w_ref[j:j + 1, :]
        y = acc + bias_ref[...]
        mu = jnp.mean(y, axis=-1, keepdims=True)
        yc = y - mu
        var = jnp.mean(yc * yc, axis=-1, keepdims=True)
        z = yc * lax.rsqrt(var + LN_EPS) * lg_ref[...] + lb_ref[...]
        o_ref[row0 + base:row0 + base + CONV_SUB, :] = (z * _sigmoid(z)).astype(BF16)


def _mixer_in_kernel(*refs, layer):
    n_in = 12 + (2 if layer else 0)
    (xa_ref, xb_ref, mod_ref, g_ref, w_ref, cos_ref, sina_ref, sinb_ref,
     dw_ref, db_ref, lg_ref, lb_ref) = refs[:12]
    qkv_ref, cv_ref, kc_ref, vc_ref, wbf_ref, ring_ref, glu_ref, pj_ref, upad_ref = refs[n_in:]
    i = pl.program_id(0)
    t = jnp.minimum(i, N_TILES - 1)

    @pl.when(i == 0)
    def _():
        wbf_ref[...] = w_ref[...].astype(BF16)
        ring_ref[...] = jnp.zeros_like(ring_ref)
        glu_ref[...] = jnp.zeros_like(glu_ref)

    ring_ref[(i - 1) % CONV_RING] = glu_ref[...]

    h = _rms(_tile_x(xa_ref, xb_ref), g_ref[0:1, :]) * (1.0 + mod_ref[1:2, :]) + mod_ref[0:1, :]
    hb = h.astype(BF16)

    def proj(part):
        return jnp.dot(hb, wbf_ref[:, part * ATT_W:(part + 1) * ATT_W], preferred_element_type=F32)

    for part in range(3):
        pj_ref[:, part * ATT_W:(part + 1) * ATT_W] = proj(part)
    glu_ref[...] = proj(3) * _sigmoid(proj(4))

    j = i - CONV_LAG
    latent = j >= P_TILES
    q = (j - P_TILES) % S_TILES_PER_BATCH
    has_prev = jnp.logical_and(latent, q != 0)
    has_next = jnp.logical_and(latent, q != S_TILES_PER_BATCH - 1)
    cur = j % CONV_RING
    prev_tail = ring_ref[(j - 1) % CONV_RING, TOK_TILE - CONV_HALO:, :]
    next_head = ring_ref[(j + 1) % CONV_RING, 0:CONV_HALO, :]
    for s in range(TOK_TILE // CONV_TILE):
        lo = s * CONV_TILE
        if s == 0:
            before = jnp.where(has_prev, prev_tail, 0.0)
        else:
            before = jnp.where(latent, ring_ref[cur, lo - CONV_HALO:lo, :], 0.0)
        if s == TOK_TILE // CONV_TILE - 1:
            after = jnp.where(has_next, next_head, 0.0)
        else:
            after = jnp.where(latent, ring_ref[cur, lo + CONV_TILE:lo + CONV_TILE + CONV_HALO, :], 0.0)
        pad = upad_ref
        pad[0, 0:CONV_HALO, :] = before
        pad[0, CONV_HALO:CONV_HALO + CONV_TILE, :] = ring_ref[cur, lo:lo + CONV_TILE, :]
        pad[0, CONV_HALO + CONV_TILE:, :] = after
        _conv_pass(pad, dw_ref, db_ref, lg_ref, lb_ref, cv_ref, lo)

    qkv_ref[:, 2 * ATT_W:] = pj_ref[:, 2 * ATT_W:].astype(BF16)

    @pl.when(t < P_TILES)
    def _():
        qkv_ref[:, :2 * ATT_W] = pj_ref[:, :2 * ATT_W].astype(BF16)
        for ref, col0 in ((kc_ref, ATT_W), (vc_ref, 2 * ATT_W)):
            for s in range(SEQ_PER_TILE):
                for hd in range(ATT_HEADS):
                    val = pj_ref[SEQ * s:SEQ * (s + 1), col0 + V_DIM * hd:col0 + V_DIM * (hd + 1)]
                    rows = pl.ds(hd, SEQ, stride=ATT_HEADS)
                    if layer == 0:
                        ref[s, 0, rows, :] = val
                    else:
                        ref[s, rows, :] = val
            if layer == 0:
                ref[:, 1:] = jnp.zeros((SEQ_PER_TILE, DEPTH - 1, CACHE_ROWS, V_DIM), F32)

    @pl.when(t >= P_TILES)
    def _():
        cos = cos_ref[...]
        sina = sina_ref[...]
        sinb = sinb_ref[...]
        for c in range(2 * ATT_W // 128):
            xg = pj_ref[:, 128 * c:128 * (c + 1)]
            fwd = pltpu.roll(xg, 128 - 16, 1)
            bwd = pltpu.roll(xg, 16, 1)
            qkv_ref[:, 128 * c:128 * (c + 1)] = (xg * cos + fwd * sina + bwd * sinb).astype(BF16)


def _mixer_in(l, x, mod, norm_g, w_in, rope, conv_params, caches):
    cos, sina, sinb = rope
    dw_w, dw_b, ln_g, ln_b = conv_params
    x_specs, x_args = _x_specs(x)

    def tile(i):
        return jnp.minimum(i, N_TILES - 1)

    def rope_idx(i):
        return (jnp.maximum(tile(i) - P_TILES, 0) % S_TILES_PER_BATCH, 0)

    def vec(a):
        return a.reshape(DEPTH, 1, CONV_CH)

    cache_shape = jax.ShapeDtypeStruct((BATCH, DEPTH, CACHE_ROWS, V_DIM), F32)
    if l == 0:
        cache_spec = pl.BlockSpec((SEQ_PER_TILE, DEPTH, CACHE_ROWS, V_DIM),
                                  lambda i: (jnp.minimum(i, P_TILES - 1), 0, 0, 0))
        extra_specs, extra_args, aliases = [], (), {}
    else:
        cache_spec = pl.BlockSpec((SEQ_PER_TILE, None, CACHE_ROWS, V_DIM),
                                  lambda i: (jnp.minimum(i, P_TILES - 1), l, 0, 0))
        extra_specs = [pl.BlockSpec(memory_space=pl.ANY)] * 2
        extra_args = tuple(caches)
        aliases = {12: 2, 13: 3}

    return pl.pallas_call(
        functools.partial(_mixer_in_kernel, layer=l),
        out_shape=(jax.ShapeDtypeStruct((N_TOK, 3 * ATT_W), BF16),
                   jax.ShapeDtypeStruct((N_TOK, CONV_CH), BF16), cache_shape, cache_shape),
        grid=(N_TILES + CONV_LAG,),
        in_specs=x_specs + [
            pl.BlockSpec((None, None, 6, D_MODEL), lambda i: (l, _cond_id(tile(i)), 0, 0)),
            pl.BlockSpec((None, 4, D_MODEL), lambda i: (l, 0, 0)),
            pl.BlockSpec((None, D_MODEL, IN_W), lambda i: (l, 0, 0), pipeline_mode=pl.Buffered(1)),
            pl.BlockSpec((TOK_TILE, 128), rope_idx),
            pl.BlockSpec((TOK_TILE, 128), rope_idx),
            pl.BlockSpec((TOK_TILE, 128), rope_idx),
            pl.BlockSpec((None, CONV_WIDTH, CONV_CH), lambda i: (l, 0, 0)),
            pl.BlockSpec((None, 1, CONV_CH), lambda i: (l, 0, 0)),
            pl.BlockSpec((None, 1, CONV_CH), lambda i: (l, 0, 0)),
            pl.BlockSpec((None, 1, CONV_CH), lambda i: (l, 0, 0))] + extra_specs,
        out_specs=(pl.BlockSpec((TOK_TILE, 3 * ATT_W), lambda i: (tile(i), 0)),
                   pl.BlockSpec((TOK_TILE, CONV_CH), lambda i: (jnp.maximum(i - CONV_LAG, 0), 0)),
                   cache_spec, cache_spec),
        scratch_shapes=[pltpu.VMEM((D_MODEL, IN_W), BF16),
                        pltpu.VMEM((CONV_RING, TOK_TILE, CONV_CH), F32),
                        pltpu.VMEM((TOK_TILE, CONV_CH), F32),
                        pltpu.VMEM((TOK_TILE, 3 * ATT_W), F32),
                        pltpu.VMEM((8, CONV_TILE + 2 * CONV_HALO, CONV_CH), F32)],
        input_output_aliases=aliases,
        compiler_params=_cparams(1),
        name=f"mixer_in_{l}",
    )(*x_args, mod, norm_g, w_in, cos, sina, sinb, dw_w, vec(dw_b), vec(ln_g), vec(ln_b), *extra_args)


def _rope_tables():
    rows = DEC_SEQ // GRID_W
    row_pos = np.repeat(np.arange(rows, dtype=np.float64), GRID_W)
    col_pos = np.tile(np.arange(GRID_W, dtype=np.float64), rows)
    half = QK_DIM // 2
    inv_freq = 1.0 / (ROPE_THETA ** (np.arange(0, half, 2, dtype=np.float64) / half))
    ang_r = row_pos[:, None] * inv_freq
    ang_c = col_pos[:, None] * inv_freq
    ang = np.concatenate([ang_r, ang_r, ang_c, ang_c], axis=-1)
    cos = np.tile(np.cos(ang), (1, 2)).astype(np.float32)
    sin = np.tile(np.sin(ang), (1, 2)).astype(np.float32)
    first = (np.arange(128) % 32) < 16
    sina = np.where(first[None, :], -sin, 0.0).astype(np.float32)
    sinb = np.where(first[None, :], 0.0, sin).astype(np.float32)
    return jnp.asarray(cos), jnp.asarray(sina), jnp.asarray(sinb)


def _attn_kernel(*refs, lam_init, has_ext):
    if has_ext:
        lamp_ref, sub_ref, q_ref, k_ref, v_ref, ke_ref, ve_ref, o_ref = refs
    else:
        lamp_ref, sub_ref, q_ref, k_ref, v_ref, o_ref = refs
    lp = lamp_ref[...]
    lam = (jnp.exp(jnp.sum(lp[0:1] * lp[1:2], axis=-1, keepdims=True))
           - jnp.exp(jnp.sum(lp[2:3] * lp[3:4], axis=-1, keepdims=True)) + lam_init)
    lane = lax.broadcasted_iota(jnp.int32, (1, V_DIM), 1)
    nt = (((1,), (1,)), ((), ()))
    scale = QK_DIM ** -0.5
    map_scale = [jnp.where(lane < QK_DIM, scale, 0.0).astype(BF16),
                 jnp.where(lane < QK_DIM, 0.0, scale).astype(BF16)]

    tq = q_ref.shape[0]
    head_cols = [slice(V_DIM * hd, V_DIM * (hd + 1)) for hd in range(ATT_HEADS)]

    def q_map(hd, m):
        return q_ref[:, head_cols[hd]] * map_scale[m]

    def with_ones(v):
        return jnp.concatenate([v, jnp.ones_like(v)], axis=1)

    outs = []
    if has_ext:
        for hd in range(ATT_HEADS):
            kh = k_ref[:, head_cols[hd]]
            v_aug = with_ones(v_ref[:, head_cols[hd]])
            head_rows = pl.ds(hd, PAST_LEN, stride=ATT_HEADS)
            keh = ke_ref[head_rows, :].astype(BF16)
            ve_aug = with_ones(ve_ref[head_rows, :].astype(BF16))
            ratio = []
            for m in range(2):
                qm = q_map(hd, m)
                s = lax.dot_general(qm, kh, nt, preferred_element_type=F32)
                se = lax.dot_general(qm, keh, nt, preferred_element_type=F32)
                mx = jnp.maximum(jnp.max(s, axis=-1, keepdims=True), jnp.max(se, axis=-1, keepdims=True))
                pv = (jnp.dot(jnp.exp((s - mx).astype(BF16)), v_aug, preferred_element_type=F32)
                      + jnp.dot(jnp.exp((se - mx).astype(BF16)), ve_aug, preferred_element_type=F32))
                ratio.append(pv[:, :V_DIM] / pv[:, V_DIM:V_DIM + 1])
            outs.append(ratio[0] - lam * ratio[1])
    else:
        pairs = [(hd, m) for hd in range(ATT_HEADS) for m in range(2)]
        s = jnp.concatenate([lax.dot_general(q_map(hd, m), k_ref[:, head_cols[hd]], nt,
                                             preferred_element_type=F32) for hd, m in pairs], axis=0)
        p = jnp.exp((s - jnp.max(s, axis=-1, keepdims=True)).astype(BF16))
        for hd in range(ATT_HEADS):
            pv = jnp.dot(p[2 * tq * hd:2 * tq * (hd + 1)], with_ones(v_ref[:, head_cols[hd]]),
                         preferred_element_type=F32)
            ratio = pv[:, :V_DIM] / pv[:, V_DIM:V_DIM + 1]
            outs.append(ratio[:tq] - lam * ratio[tq:])

    heads = [(_rms(o, sub_ref[...]) * (1.0 - lam_init)).astype(BF16) for o in outs]
    o_ref[...] = jnp.concatenate(heads, axis=1)


def _attention(l, lam_init, lam_params, subln_g, qkvg, cache_k, cache_v):
    small = [pl.BlockSpec((None, 4, QK_DIM), lambda *_: (l, 0, 0)),
             pl.BlockSpec((None, 1, V_DIM), lambda *_: (l, 0, 0))]
    sub3 = subln_g.reshape(DEPTH, 1, V_DIM)

    o_ctx = pl.pallas_call(
        functools.partial(_attn_kernel, lam_init=lam_init, has_ext=False),
        out_shape=jax.ShapeDtypeStruct((N_PROMPT, ATT_W), BF16),
        grid=(BATCH,),
        in_specs=small + [pl.BlockSpec((SEQ, ATT_W), lambda b: (b, 0)),
                          pl.BlockSpec((SEQ, ATT_W), lambda b: (b, 1)),
                          pl.BlockSpec((SEQ, ATT_W), lambda b: (b, 2))],
        out_specs=pl.BlockSpec((SEQ, ATT_W), lambda b: (b, 0)),
        compiler_params=_cparams(1),
        name=f"attn_ctx_{l}",
    )(lam_params, sub3, qkvg, qkvg, qkvg)

    tq = 256
    q_tiles = DEC_SEQ // tq
    q_base = N_PROMPT // tq
    kv_base = N_PROMPT // DEC_SEQ

    def q_idx(b, i):
        return (q_base + b * q_tiles + i, 0)

    o_lat = pl.pallas_call(
        functools.partial(_attn_kernel, lam_init=lam_init, has_ext=True),
        out_shape=jax.ShapeDtypeStruct((N_SAMPLE, ATT_W), BF16),
        grid=(DEC_BATCH, q_tiles),
        in_specs=small + [pl.BlockSpec((tq, ATT_W), q_idx),
                          pl.BlockSpec((DEC_SEQ, ATT_W), lambda b, i: (kv_base + b, 1)),
                          pl.BlockSpec((DEC_SEQ, ATT_W), lambda b, i: (kv_base + b, 2)),
                          pl.BlockSpec((None, None, PAST_LEN * ATT_HEADS, V_DIM), lambda b, i: (b, l, 0, 0)),
                          pl.BlockSpec((None, None, PAST_LEN * ATT_HEADS, V_DIM), lambda b, i: (b, l, 0, 0))],
        out_specs=pl.BlockSpec((tq, ATT_W), lambda b, i: (b * q_tiles + i, 0)),
        compiler_params=_cparams(2),
        name=f"attn_lat_{l}",
    )(lam_params, sub3, qkvg, qkvg, qkvg, cache_k, cache_v)
    return o_ctx, o_lat


def _outproj_kernel(*refs, with_router):
    if with_router:
        (oc_ref, ol_ref, cv_ref, xa_ref, xb_ref, mod_ref, g_ref, w_ref, rw_ref,
         xo_ref, h2_ref, info_ref, info_t_ref, cb_ref, tot_ref, wbf_ref, carry_ref) = refs
    else:
        oc_ref, ol_ref, cv_ref, xa_ref, xb_ref, mod_ref, g_ref, w_ref, xo_ref, h2_ref, wbf_ref = refs
    i = pl.program_id(0)

    @pl.when(i == 0)
    def _():
        wbf_ref[...] = w_ref[...].astype(BF16)

    half = TOK_TILE // 2
    for rows in (slice(0, half), slice(half, TOK_TILE)):
        o = jnp.where(i < P_TILES, oc_ref[rows, :], ol_ref[rows, :])
        m = (jnp.dot(o, wbf_ref[0:ATT_W, :], preferred_element_type=F32)
             + jnp.dot(cv_ref[rows, :], wbf_ref[ATT_W:, :], preferred_element_type=F32))
        x = jnp.where(i < P_TILES, xa_ref[rows, :], xb_ref[rows, :])
        xn = x + mod_ref[2:3, :] * _rms(m, g_ref[1:2, :])
        xo_ref[rows, :] = xn
        h2_ref[rows, :] = (_rms(xn, g_ref[2:3, :]) * (1.0 + mod_ref[4:5, :]) + mod_ref[3:4, :]).astype(BF16)

    if with_router:
        @pl.when(i == 0)
        def _():
            carry_ref[...] = jnp.zeros_like(carry_ref)

        _route_tile(h2_ref[...], rw_ref, info_ref, info_t_ref, cb_ref, tot_ref, carry_ref)


def _outproj(l, o_ctx, o_lat, cv, x, mod, norm_g, w_out, router_w=None):
    x_specs, x_args = _x_specs(x)
    out_shape = [jax.ShapeDtypeStruct((N_TOK, D_MODEL), F32), jax.ShapeDtypeStruct((N_TOK, D_MODEL), BF16)]
    out_specs = [pl.BlockSpec((TOK_TILE, D_MODEL), lambda i: (i, 0)),
                 pl.BlockSpec((TOK_TILE, D_MODEL), lambda i: (i, 0))]
    scratch = [pltpu.VMEM((D_MODEL, D_MODEL), BF16)]
    extra_specs, extra_args = [], ()
    if router_w is not None:
        rw_pad = jnp.zeros((D_MODEL, 128), BF16).at[:, :N_EXPERTS].set(router_w.astype(BF16))
        extra_specs, extra_args = [pl.BlockSpec((D_MODEL, 128), lambda i: (0, 0))], (rw_pad,)
        out_shape += [jax.ShapeDtypeStruct((N_TOK, 128), F32),
                      jax.ShapeDtypeStruct((N_TILES, 8, TOK_TILE), F32),
                      jax.ShapeDtypeStruct((N_TILES, 8, 128), F32),
                      jax.ShapeDtypeStruct((8, 128), F32)]
        out_specs += [pl.BlockSpec((TOK_TILE, 128), lambda i: (i, 0)),
                      pl.BlockSpec((None, 8, TOK_TILE), lambda i: (i, 0, 0)),
                      pl.BlockSpec((None, 8, 128), lambda i: (i, 0, 0)),
                      pl.BlockSpec((8, 128), lambda i: (0, 0))]
        scratch += [pltpu.VMEM((8, 128), F32)]
    return pl.pallas_call(
        functools.partial(_outproj_kernel, with_router=router_w is not None),
        out_shape=tuple(out_shape),
        grid=(N_TILES,),
        in_specs=[pl.BlockSpec((TOK_TILE, ATT_W), lambda i: (jnp.minimum(i, P_TILES - 1), 0)),
                  pl.BlockSpec((TOK_TILE, ATT_W), lambda i: (jnp.maximum(i - P_TILES, 0), 0)),
                  pl.BlockSpec((TOK_TILE, CONV_CH), lambda i: (i, 0))] + x_specs + [
                  pl.BlockSpec((None, None, 6, D_MODEL), lambda i: (l, _cond_id(i), 0, 0)),
                  pl.BlockSpec((None, 4, D_MODEL), lambda i: (l, 0, 0)),
                  pl.BlockSpec((None, D_MODEL, D_MODEL), lambda i: (l, 0, 0),
                               pipeline_mode=pl.Buffered(1))] + extra_specs,
        out_specs=tuple(out_specs),
        scratch_shapes=scratch,
        compiler_params=_cparams(1),
        name=f"outproj_{l}",
    )(o_ctx, o_lat, cv, *x_args, mod, norm_g, w_out, *extra_args)


DENSE_CHUNK = D_FF // 2


def _dense_ffn_kernel(h_ref, x_ref, mod_ref, g_ref, w13_ref, w2_ref, xo_ref):
    h = h_ref[...]
    acc = None
    for c in range(D_FF // DENSE_CHUNK):
        lo = c * DENSE_CHUNK
        gt = jnp.dot(h, w13_ref[:, lo:lo + DENSE_CHUNK], preferred_element_type=F32)
        up = jnp.dot(h, w13_ref[:, D_FF + lo:D_FF + lo + DENSE_CHUNK], preferred_element_type=F32)
        a = (gt * _sigmoid(gt) * up).astype(BF16)
        part = jnp.dot(a, w2_ref[lo:lo + DENSE_CHUNK, :], preferred_element_type=F32)
        acc = part if acc is None else acc + part
    xo_ref[...] = x_ref[...] + mod_ref[5:6, :] * _rms(acc, g_ref[3:4, :])


def _dense_ffn(l, h2, x, mod, norm_g, w13, w2):
    return pl.pallas_call(
        _dense_ffn_kernel,
        out_shape=jax.ShapeDtypeStruct((N_TOK, D_MODEL), F32),
        grid=(N_TILES,),
        in_specs=[pl.BlockSpec((TOK_TILE, D_MODEL), lambda i: (i, 0)),
                  pl.BlockSpec((TOK_TILE, D_MODEL), lambda i: (i, 0)),
                  pl.BlockSpec((None, None, 6, D_MODEL), lambda i: (l, _cond_id(i), 0, 0)),
                  pl.BlockSpec((None, 4, D_MODEL), lambda i: (l, 0, 0)),
                  pl.BlockSpec((D_MODEL, 2 * D_FF), lambda i: (0, 0), pipeline_mode=pl.Buffered(1)),
                  pl.BlockSpec((D_FF, D_MODEL), lambda i: (0, 0), pipeline_mode=pl.Buffered(1))],
        out_specs=pl.BlockSpec((TOK_TILE, D_MODEL), lambda i: (i, 0)),
        compiler_params=_cparams(1),
        name=f"dense_ffn_{l}",
    )(h2, x, mod, norm_g, w13, w2)


def _route_tile(h, rw_ref, info_ref, info_t_ref, cb_ref, tot_ref, carry_ref):
    lane = lax.broadcasted_iota(jnp.int32, (TOK_TILE, 128), 1)
    lanef = lane.astype(F32)
    logits = jnp.dot(h, rw_ref[...], preferred_element_type=F32)
    logits = jnp.where(lane < N_EXPERTS, logits, -jnp.inf)
    big = jnp.asarray(128.0, F32)
    m1 = jnp.max(logits, axis=-1, keepdims=True)
    e1 = jnp.min(jnp.where(logits == m1, lanef, big), axis=-1, keepdims=True)
    oh1 = lanef == e1
    rest = jnp.where(oh1, -jnp.inf, logits)
    m2 = jnp.max(rest, axis=-1, keepdims=True)
    e2 = jnp.min(jnp.where(rest == m2, lanef, big), axis=-1, keepdims=True)
    oh2 = lanef == e2
    ex = jnp.exp(m2 - m1)
    g1 = 1.0 / (1.0 + ex)
    g2 = ex / (1.0 + ex)

    oh = jnp.where(jnp.logical_or(oh1, oh2), 1.0, 0.0)
    r = lax.broadcasted_iota(jnp.int32, (TOK_TILE, TOK_TILE), 0)
    c = lax.broadcasted_iota(jnp.int32, (TOK_TILE, TOK_TILE), 1)
    tri = jnp.where(c <= r, 1.0, 0.0).astype(BF16)
    incl = jnp.dot(tri, oh.astype(BF16), preferred_element_type=F32)
    carry = carry_ref[0:1, :]
    excl = incl - oh + carry
    rank1 = jnp.sum(jnp.where(oh1, excl, 0.0), axis=-1, keepdims=True)
    rank2 = jnp.sum(jnp.where(oh2, excl, 0.0), axis=-1, keepdims=True)

    info = jnp.where(lane == 0, e1, 0.0)
    for k, col in enumerate((e2, rank1, rank2, g1, g2), start=1):
        info = jnp.where(lane == k, col, info)
    info_ref[...] = info
    info_t_ref[...] = info.T[0:8, :]

    cb_ref[...] = carry_ref[...]
    new_carry = carry + incl[TOK_TILE - 1:TOK_TILE, :]
    carry_ref[...] = jnp.broadcast_to(new_carry, carry_ref.shape)
    tot_ref[...] = jnp.broadcast_to(new_carry, tot_ref.shape)


def _sorted_pos(expert, rank, base_ref):
    start = jnp.zeros_like(rank)
    for e in range(N_EXPERTS):
        start = jnp.where(expert == float(e), base_ref[e].astype(F32), start)
    return start + rank


def _dispatch_kernel(clo_ref, cn_ref, base_ref, info_t_ref, x_ref, o_ref):
    r = pl.program_id(0)
    o_ref[...] = jnp.zeros_like(o_ref)
    rows = (r * DISPATCH_TILE + lax.broadcasted_iota(jnp.int32, (DISPATCH_TILE, TOK_TILE), 0)).astype(F32)

    def body(k, carry):
        c = clo_ref[r] + k
        it = info_t_ref[c]
        pos1 = _sorted_pos(it[0:1, :], it[2:3, :], base_ref)
        pos2 = _sorted_pos(it[1:2, :], it[3:4, :], base_ref)
        hit = jnp.logical_or(rows == pos1, rows == pos2)
        sel = jnp.where(hit, 1.0, 0.0).astype(BF16)
        off = pl.multiple_of(c * TOK_TILE, TOK_TILE)
        o_ref[...] += jnp.dot(sel, x_ref[pl.ds(off, TOK_TILE), :], preferred_element_type=F32).astype(BF16)
        return carry

    lax.fori_loop(0, cn_ref[r], body, 0)


def _dispatch(c_lo, c_n, base, info_t, h2):
    return pl.pallas_call(
        _dispatch_kernel,
        out_shape=jax.ShapeDtypeStruct((SORT_ROWS, D_MODEL), BF16),
        grid_spec=pltpu.PrefetchScalarGridSpec(
            num_scalar_prefetch=3,
            grid=(SORT_ROWS // DISPATCH_TILE,),
            in_specs=[pl.BlockSpec((N_TILES, 8, TOK_TILE), lambda r, *_: (0, 0, 0)),
                      pl.BlockSpec((N_TOK, D_MODEL), lambda r, *_: (0, 0),
                                   pipeline_mode=pl.Buffered(1))],
            out_specs=pl.BlockSpec((DISPATCH_TILE, D_MODEL), lambda r, *_: (r, 0))),
        compiler_params=_cparams(1),
        name="moe_dispatch",
    )(c_lo, c_n, base, info_t, h2)


def _expert_weight_ring(c, r, n_c, te_ref, tf_ref, nx_ref, copies, cast):
    @pl.when(jnp.logical_and(c == 0, r == 0))
    def _():
        for cp in copies(te_ref[0], 0):
            cp.start()

    @pl.when(tf_ref[r] == 1)
    def _():
        for cp in copies(te_ref[r], c):
            cp.wait()
        cast()
        nr = nx_ref[r]
        nc = c + (nr <= r).astype(jnp.int32)

        @pl.when(nc < n_c)
        def _():
            for cp in copies(te_ref[nr], nc):
                cp.start()


def _moe_up_kernel(te_ref, tv_ref, tf_ref, nx_ref, x_ref, w_hbm, h_ref, wst_ref, wbf_ref, sem_ref, *, i_moe):
    f = pl.program_id(0)
    r = pl.program_id(1)

    def copies(e, ft):
        out = []
        for part in range(2):
            col = pl.multiple_of((part * UP_TILES + ft) * UP_TILE, 128)
            out.append(pltpu.make_async_copy(w_hbm.at[i_moe, e, :, pl.ds(col, UP_TILE)],
                                             wst_ref.at[part], sem_ref.at[part]))
        return out

    def cast():
        wbf_ref[...] = wst_ref[...].astype(BF16)

    _expert_weight_ring(f, r, UP_TILES, te_ref, tf_ref, nx_ref, copies, cast)

    @pl.when(tv_ref[r] == 1)
    def _():
        x = x_ref[...]
        gt = jnp.dot(x, wbf_ref[0], preferred_element_type=F32)
        up = jnp.dot(x, wbf_ref[1], preferred_element_type=F32)
        h_ref[...] = (gt * _sigmoid(gt) * up).astype(BF16)

    @pl.when(tv_ref[r] == 0)
    def _():
        h_ref[...] = jnp.zeros_like(h_ref)


def _moe_up(i_moe, tile_e, tile_valid, tile_first, tile_next, xs, w13):
    return pl.pallas_call(
        functools.partial(_moe_up_kernel, i_moe=i_moe),
        out_shape=jax.ShapeDtypeStruct((SORT_ROWS, D_FF_EXPERT), BF16),
        grid_spec=pltpu.PrefetchScalarGridSpec(
            num_scalar_prefetch=4,
            grid=(UP_TILES, SORT_TILES),
            in_specs=[pl.BlockSpec((ROW_TILE, D_MODEL), lambda f, r, *_: (r, 0)),
                      pl.BlockSpec(memory_space=pl.ANY)],
            out_specs=pl.BlockSpec((ROW_TILE, UP_TILE), lambda f, r, *_: (r, f)),
            scratch_shapes=[pltpu.VMEM((2, D_MODEL, UP_TILE), F32),
                            pltpu.VMEM((2, D_MODEL, UP_TILE), BF16),
                            pltpu.SemaphoreType.DMA((2,))]),
        compiler_params=_cparams(2),
        name="moe_up",
    )(tile_e, tile_valid, tile_first, tile_next, xs, w13)


def _moe_down_kernel(te_ref, tv_ref, tf_ref, nx_ref, h_ref, w_hbm, y_ref, wst_ref, wbf_ref, sem_ref, *, i_moe):
    n = pl.program_id(0)
    r = pl.program_id(1)

    def copies(e, nt):
        col = pl.multiple_of(nt * DOWN_TILE, 128)
        return [pltpu.make_async_copy(w_hbm.at[i_moe, e, :, pl.ds(col, DOWN_TILE)], wst_ref, sem_ref.at[0])]

    def cast():
        wbf_ref[...] = wst_ref[...].astype(BF16)

    _expert_weight_ring(n, r, D_MODEL // DOWN_TILE, te_ref, tf_ref, nx_ref, copies, cast)

    @pl.when(tv_ref[r] == 1)
    def _():
        y_ref[...] = jnp.dot(h_ref[...], wbf_ref[...], preferred_element_type=F32).astype(BF16)

    @pl.when(tv_ref[r] == 0)
    def _():
        y_ref[...] = jnp.zeros_like(y_ref)


def _moe_down(i_moe, tile_e, tile_valid, tile_first, tile_next, hs, w2):
    return pl.pallas_call(
        functools.partial(_moe_down_kernel, i_moe=i_moe),
        out_shape=jax.ShapeDtypeStruct((SORT_ROWS, D_MODEL), BF16),
        grid_spec=pltpu.PrefetchScalarGridSpec(
            num_scalar_prefetch=4,
            grid=(D_MODEL // DOWN_TILE, SORT_TILES),
            in_specs=[pl.BlockSpec((ROW_TILE, D_FF_EXPERT), lambda n, r, *_: (r, 0)),
                      pl.BlockSpec(memory_space=pl.ANY)],
            out_specs=pl.BlockSpec((ROW_TILE, DOWN_TILE), lambda n, r, *_: (r, n)),
            scratch_shapes=[pltpu.VMEM((D_FF_EXPERT, DOWN_TILE), F32),
                            pltpu.VMEM((D_FF_EXPERT, DOWN_TILE), BF16),
                            pltpu.SemaphoreType.DMA((1,))]),
        compiler_params=_cparams(2),
        name="moe_down",
    )(tile_e, tile_valid, tile_first, tile_next, hs, w2)


def _window_copy(y_hbm, ybuf_ref, sem_ref, src, buf, slot):
    return pltpu.make_async_copy(y_hbm.at[pl.ds(src, WIN), :],
                                 ybuf_ref.at[buf, pl.ds(slot * WIN, WIN), :], sem_ref.at[buf, slot])


def _start_windows(src_ref, y_hbm, ybuf_ref, sem_ref, tile, buf):
    for s in range(N_WIN):
        src = pl.multiple_of(src_ref[tile * N_WIN + s], 16)
        _window_copy(y_hbm, ybuf_ref, sem_ref, src, buf, s).start()


def _combine_kernel(src_ref, lo_ref, hi_ref, base_ref, info_ref, x_ref, mod_ref, g_ref, y_hbm,
                    yp_ref, ys_ref, ybuf_ref, sem_ref):
    j = pl.program_id(0)
    buf = j % 2

    @pl.when(j == 0)
    def _():
        _start_windows(src_ref, y_hbm, ybuf_ref, sem_ref, 0, 0)

    @pl.when(j + 1 < N_TILES)
    def _():
        _start_windows(src_ref, y_hbm, ybuf_ref, sem_ref, j + 1, 1 - buf)

    info = info_ref[...]
    pos1 = _sorted_pos(info[:, 0:1], info[:, 2:3], base_ref)
    pos2 = _sorted_pos(info[:, 1:2], info[:, 3:4], base_ref)
    g1, g2 = info[:, 4:5], info[:, 5:6]
    iota = lax.broadcasted_iota(jnp.int32, (1, WIN), 1)
    ids = []
    for s in range(N_WIN):
        row = src_ref[j * N_WIN + s] + iota
        ok = jnp.logical_and(row >= lo_ref[j * N_WIN + s], row < hi_ref[j * N_WIN + s])
        ids.append(jnp.where(ok, row, -1))
    row_id = jnp.concatenate(ids, axis=1).astype(F32)
    sel = (jnp.where(pos1 == row_id, g1, 0.0) + jnp.where(pos2 == row_id, g2, 0.0)).astype(BF16)

    for s in range(N_WIN):
        _window_copy(y_hbm, ybuf_ref, sem_ref, 0, buf, s).wait()
    ffn = jnp.dot(sel, ybuf_ref[buf], preferred_element_type=F32)
    out = x_ref[...] + mod_ref[5:6, :] * _rms(ffn, g_ref[3:4, :])

    @pl.when(j < P_TILES)
    def _():
        yp_ref[...] = out

    @pl.when(j >= P_TILES)
    def _():
        ys_ref[...] = out


def _combine(l, win_src, win_lo, win_hi, base, info, x, mod, norm_g, ys):
    return pl.pallas_call(
        _combine_kernel,
        out_shape=(jax.ShapeDtypeStruct((N_PROMPT, D_MODEL), F32),
                   jax.ShapeDtypeStruct((N_SAMPLE, D_MODEL), F32)),
        grid_spec=pltpu.PrefetchScalarGridSpec(
            num_scalar_prefetch=4,
            grid=(N_TILES,),
            in_specs=[pl.BlockSpec((TOK_TILE, 128), lambda i, *_: (i, 0)),
                      pl.BlockSpec((TOK_TILE, D_MODEL), lambda i, *_: (i, 0)),
                      pl.BlockSpec((None, None, 6, D_MODEL), lambda i, *_: (l, _cond_id(i), 0, 0)),
                      pl.BlockSpec((None, 4, D_MODEL), lambda i, *_: (l, 0, 0)),
                      pl.BlockSpec(memory_space=pl.ANY)],
            out_specs=(pl.BlockSpec((TOK_TILE, D_MODEL), lambda i, *_: (jnp.minimum(i, P_TILES - 1), 0)),
                       pl.BlockSpec((TOK_TILE, D_MODEL), lambda i, *_: (jnp.maximum(i - P_TILES, 0), 0))),
            scratch_shapes=[pltpu.VMEM((2, N_WIN * WIN, D_MODEL), BF16),
                            pltpu.SemaphoreType.DMA((2, N_WIN))]),
        compiler_params=_cparams(1),
        name="moe_combine",
    )(win_src, win_lo, win_hi, base, info, x, mod, norm_g, ys)


def _moe_ffn(l, i_moe, h2, x, mod, norm_g, routing, moe_w13, moe_w2):
    info, info_t, cb, tot = routing

    counts = tot[0, :N_EXPERTS].astype(jnp.int32)
    padded = (counts + ROW_TILE - 1) // ROW_TILE * ROW_TILE
    seg_end = jnp.cumsum(padded)
    base = (seg_end - padded).astype(jnp.int32)
    cbx = cb[:, 0, :N_EXPERTS].astype(jnp.int32)
    cb_end = jnp.concatenate([cbx[1:], counts[None, :]], axis=0)

    row0 = jnp.arange(SORT_TILES, dtype=jnp.int32) * ROW_TILE
    tile_e = jnp.minimum(jnp.sum(seg_end[None, :] <= row0[:, None], axis=1), N_EXPERTS - 1).astype(jnp.int32)
    k0 = row0 - base[tile_e]
    tile_valid = jnp.logical_and(k0 >= 0, k0 < counts[tile_e])
    tile_first = jnp.logical_and(tile_valid, k0 == 0)
    per = ROW_TILE // DISPATCH_TILE
    d_e = jnp.repeat(tile_e, per)
    d_k0 = jnp.repeat(k0, per) + jnp.tile(jnp.arange(per, dtype=jnp.int32) * DISPATCH_TILE, SORT_TILES)
    d_valid = jnp.logical_and(d_k0 >= 0, d_k0 < counts[d_e])
    d_kend = jnp.minimum(d_k0 + DISPATCH_TILE, counts[d_e])
    cb_t = cbx.T[d_e]
    c_lo = jnp.sum(cb_t <= d_k0[:, None], axis=1) - 1
    c_hi = jnp.sum(cb_t < d_kend[:, None], axis=1) - 1
    c_n = jnp.where(d_valid, c_hi - c_lo + 1, 0).astype(jnp.int32)
    c_lo = jnp.where(d_valid, c_lo, 0).astype(jnp.int32)
    tile_idx = jnp.arange(SORT_TILES, dtype=jnp.int32)
    first_idx = jnp.where(tile_first, tile_idx, SORT_TILES)
    later = jnp.where(tile_idx[None, :] > tile_idx[:, None], first_idx[None, :], SORT_TILES)
    tile_next = jnp.min(later, axis=1)
    tile_next = jnp.where(tile_next >= SORT_TILES, 0, tile_next).astype(jnp.int32)
    tile_valid = tile_valid.astype(jnp.int32)
    tile_first = tile_first.astype(jnp.int32)

    seg_lo = base[None, :] + cbx
    seg_hi = base[None, :] + cb_end
    seg_n = seg_hi - seg_lo
    w0 = seg_lo // 16 * 16
    nw = jnp.where(seg_n > 0, (seg_lo - w0 + seg_n + WIN - 1) // WIN, 0)
    nw_end = jnp.cumsum(nw, axis=1)
    slot = jnp.arange(N_WIN, dtype=jnp.int32)
    slot_e = jnp.minimum(jnp.sum(nw_end[:, None, :] <= slot[None, :, None], axis=2), N_EXPERTS - 1)
    pick = slot_e[:, :, None] == jnp.arange(N_EXPERTS, dtype=jnp.int32)[None, None, :]
    take = lambda a: jnp.sum(jnp.where(pick, a[:, None, :], 0), axis=2)
    slot_k = slot[None, :] - (take(nw_end) - take(nw))
    slot_ok = slot[None, :] < nw_end[:, -1:]
    slot_src = take(w0) + WIN * slot_k
    win_src = jnp.where(slot_ok, slot_src, 0).astype(jnp.int32).reshape(-1)
    win_lo = jnp.where(slot_ok, take(seg_lo), 0).astype(jnp.int32).reshape(-1)
    win_hi = jnp.where(slot_ok, take(seg_hi), 0).astype(jnp.int32).reshape(-1)

    xs = _dispatch(c_lo, c_n, base, info_t, h2)
    hs = _moe_up(i_moe, tile_e, tile_valid, tile_first, tile_next, xs, moe_w13)
    ys = _moe_down(i_moe, tile_e, tile_valid, tile_first, tile_next, hs, moe_w2)
    return _combine(l, win_src, win_lo, win_hi, base, info, x, mod, norm_g, ys)


def kernel(x_prompt, x_sample, cache_k, cache_v, c, c_ctx, w_ada, b_ada, norm_g, w_in, w_out,
           lam_params, subln_g, dw_weight, dw_bias, conv_ln_g, conv_ln_b, dense_w13, dense_w2,
           router_w, moe_w13, moe_w2):
    x = (x_prompt.reshape(N_PROMPT, D_MODEL), x_sample.reshape(N_SAMPLE, D_MODEL))
    cond8 = jnp.zeros((8, D_MODEL), F32).at[0].set(c_ctx).at[1:1 + DEC_BATCH].set(c)
    mod = _ada_modulation(cond8, w_ada, b_ada).reshape(DEPTH, 8, 6, D_MODEL)
    rope = _rope_tables()
    ck = cache_k.reshape(DEC_BATCH, DEPTH, PAST_LEN * ATT_HEADS, V_DIM)
    cv = cache_v.reshape(DEC_BATCH, DEPTH, PAST_LEN * ATT_HEADS, V_DIM)

    caches = None
    for l in range(DEPTH):
        lam_init = 0.8 - 0.6 * math.exp(-0.3 * l)
        qkv, cvo, *caches = _mixer_in(l, x, mod, norm_g, w_in, rope,
                                      (dw_weight, dw_bias, conv_ln_g, conv_ln_b), caches)
        o_ctx, o_lat = _attention(l, lam_init, lam_params, subln_g, qkv, ck, cv)
        i = l // 2
        if l % 2 == 0:
            x, h2 = _outproj(l, o_ctx, o_lat, cvo, x, mod, norm_g, w_out)
            x = _dense_ffn(l, h2, x, mod, norm_g, dense_w13[i].astype(BF16), dense_w2[i].astype(BF16))
        else:
            x, h2, *routing = _outproj(l, o_ctx, o_lat, cvo, x, mod, norm_g, w_out, router_w[i])
            x = _moe_ffn(l, i, h2, x, mod, norm_g, routing, moe_w13, moe_w2)

    xp, xs = x if isinstance(x, tuple) else (x[:N_PROMPT], x[N_PROMPT:])
    new_k, new_v = (a.reshape(BATCH, DEPTH, SEQ, ATT_HEADS, V_DIM) for a in caches)
    return (xp.reshape(BATCH, SEQ, D_MODEL), xs.reshape(DEC_BATCH, DEC_SEQ, D_MODEL), new_k, new_v)
```

```python
import functools
import math

import jax
import jax.numpy as jnp
import numpy as np
from jax import lax
from jax.experimental import pallas as pl
from jax.experimental.pallas import tpu as pltpu

F32 = jnp.float32
BF16 = jnp.bfloat16

D_MODEL = 1024
BATCH = 32
SEQ = 256
DEPTH = 2
DEC_BATCH = 2
DEC_SEQ = 2048
PAST_LEN = 512
GRID_W = 64
ATT_HEADS = 4
QK_DIM = 64
V_DIM = 128
ATT_W = ATT_HEADS * V_DIM
IN_W = 5 * ATT_W
CONV_CH = 512
CONV_WIDTH = 31
D_FF = 2816
N_EXPERTS = 8
D_FF_EXPERT = 3584
ROPE_THETA = 10000.0
NORM_EPS = 1e-6
LN_EPS = 1e-5

N_PROMPT = BATCH * SEQ
N_SAMPLE = DEC_BATCH * DEC_SEQ
N_TOK = N_PROMPT + N_SAMPLE

TOK_TILE = 512
N_TILES = N_TOK // TOK_TILE
P_TILES = N_PROMPT // TOK_TILE
S_TILES_PER_BATCH = DEC_SEQ // TOK_TILE

ROW_TILE = 512
SORT_TILES = (2 * N_TOK + N_EXPERTS * ROW_TILE) // ROW_TILE + 1
SORT_ROWS = SORT_TILES * ROW_TILE
DISPATCH_TILE = 256
UP_TILE = 1792
UP_TILES = D_FF_EXPERT // UP_TILE
DOWN_TILE = 1024
WIN = 128
N_WIN = 16

VMEM_LIMIT = 56 * 1024 * 1024


def _cparams(n_axes):
    return pltpu.CompilerParams(dimension_semantics=("arbitrary",) * n_axes,
                                vmem_limit_bytes=VMEM_LIMIT)


def _cond_id(i):
    return jnp.where(i < P_TILES, 0, 1 + (i - P_TILES) // S_TILES_PER_BATCH)


def _sigmoid(x):
    return 1.0 / (1.0 + jnp.exp(-x))


def _rms(x, g):
    return x * lax.rsqrt(jnp.mean(x * x, axis=-1, keepdims=True) + NORM_EPS) * g


def _ada_kernel(c_ref, w_ref, b_ref, o_ref):
    c = c_ref[...]
    s = (c * _sigmoid(c)).astype(BF16)
    o_ref[...] = jnp.dot(s, w_ref[...].astype(BF16), preferred_element_type=F32) + b_ref[...]


def _ada_modulation(cond8, w_ada, b_ada):
    tn = 1536
    n = 6 * D_MODEL
    return pl.pallas_call(
        _ada_kernel,
        out_shape=jax.ShapeDtypeStruct((DEPTH, 8, n), F32),
        grid=(DEPTH, n // tn),
        in_specs=[pl.BlockSpec((8, D_MODEL), lambda l, j: (0, 0)),
                  pl.BlockSpec((None, D_MODEL, tn), lambda l, j: (l, 0, j)),
                  pl.BlockSpec((None, 1, tn), lambda l, j: (l, 0, j))],
        out_specs=pl.BlockSpec((None, 8, tn), lambda l, j: (l, 0, j)),
        compiler_params=_cparams(2),
        name="ada_modulation",
    )(cond8, w_ada, b_ada.reshape(DEPTH, 1, n))


def _tile_x(xa_ref, xb_ref):
    return jnp.where(pl.program_id(0) < P_TILES, xa_ref[...], xb_ref[...])


def _x_specs(x):
    last = N_TILES - 1
    if isinstance(x, tuple):
        xa, xb = x
        b_idx = lambda i, *_: (jnp.maximum(jnp.minimum(i, last) - P_TILES, 0), 0)
    else:
        xa = xb = x
        b_idx = lambda i, *_: (jnp.maximum(jnp.minimum(i, last), P_TILES), 0)
    a_idx = lambda i, *_: (jnp.minimum(i, P_TILES - 1), 0)
    return [pl.BlockSpec((TOK_TILE, D_MODEL), a_idx), pl.BlockSpec((TOK_TILE, D_MODEL), b_idx)], (xa, xb)


SEQ_PER_TILE = TOK_TILE // SEQ
CACHE_ROWS = SEQ * ATT_HEADS


CONV_LAG = 2
CONV_RING = 4
CONV_TILE = 256
CONV_HALO = 16
CONV_SUB = 32


def _conv_pass(upad_ref, w_ref, bias_ref, lg_ref, lb_ref, o_ref, row0, between):
    rows = CONV_TILE + 2 * CONV_HALO - 8
    for s in range(1, 8):
        upad_ref[s, 0:rows, :] = upad_ref[0, s:s + rows, :]
    first_tap = CONV_HALO - CONV_WIDTH // 2
    for t in range(CONV_TILE // CONV_SUB):
        between()
        base = t * CONV_SUB
        acc = jnp.zeros((CONV_SUB, CONV_CH), F32)
        for j in range(CONV_WIDTH):
            lo = base + (first_tap + j) // 8 * 8
            acc = acc + upad_ref[(first_tap + j) % 8, lo:lo + CONV_SUB, :] * w_ref[j:j + 1, :]
        y = acc + bias_ref[...]
        mu = jnp.mean(y, axis=-1, keepdims=True)
        yc = y - mu
        var = jnp.mean(yc * yc, axis=-1, keepdims=True)
        z = yc * lax.rsqrt(var + LN_EPS) * lg_ref[...] + lb_ref[...]
        o_ref[row0 + base:row0 + base + CONV_SUB, :] = (z * _sigmoid(z)).astype(BF16)


def _mixer_in_kernel(*refs, layer):
    n_in = 12 + (2 if layer else 0)
    (xa_ref, xb_ref, mod_ref, g_ref, w_ref, cos_ref, sina_ref, sinb_ref,
     dw_ref, db_ref, lg_ref, lb_ref) = refs[:12]
    qkv_ref, cv_ref, kc_ref, vc_ref, wbf_ref, ring_ref, glu_ref, pj_ref, upad_ref = refs[n_in:]
    i = pl.program_id(0)
    t = jnp.minimum(i, N_TILES - 1)

    @pl.when(i == 0)
    def _():
        wbf_ref[...] = w_ref[...].astype(BF16)
        ring_ref[...] = jnp.zeros_like(ring_ref)
        glu_ref[...] = jnp.zeros_like(glu_ref)

    ring_ref[(i - 1) % CONV_RING] = glu_ref[...]

    h = _rms(_tile_x(xa_ref, xb_ref), g_ref[0:1, :]) * (1.0 + mod_ref[1:2, :]) + mod_ref[0:1, :]
    hb = h.astype(BF16)

    def proj(part):
        return jnp.dot(hb, wbf_ref[:, part * ATT_W:(part + 1) * ATT_W], preferred_element_type=F32)

    def proj_part(part):
        if part < 3:
            pj_ref[:, part * ATT_W:(part + 1) * ATT_W] = proj(part)
        elif part == 3:
            glu_ref[...] = proj(3)
        else:
            glu_ref[...] = glu_ref[...] * _sigmoid(proj(4))

    conv_units = 2 * (CONV_TILE // CONV_SUB)
    issue_at = {k * conv_units // 5: k for k in range(5)}
    unit = [0]

    def between():
        if unit[0] in issue_at:
            proj_part(issue_at[unit[0]])
        unit[0] += 1

    j = i - CONV_LAG
    latent = j >= P_TILES
    q = (j - P_TILES) % S_TILES_PER_BATCH
    has_prev = jnp.logical_and(latent, q != 0)
    has_next = jnp.logical_and(latent, q != S_TILES_PER_BATCH - 1)
    cur = j % CONV_RING
    prev_tail = ring_ref[(j - 1) % CONV_RING, TOK_TILE - CONV_HALO:, :]
    next_head = ring_ref[(j + 1) % CONV_RING, 0:CONV_HALO, :]
    for s in range(TOK_TILE // CONV_TILE):
        lo = s * CONV_TILE
        if s == 0:
            before = jnp.where(has_prev, prev_tail, 0.0)
        else:
            before = jnp.where(latent, ring_ref[cur, lo - CONV_HALO:lo, :], 0.0)
        if s == TOK_TILE // CONV_TILE - 1:
            after = jnp.where(has_next, next_head, 0.0)
        else:
            after = jnp.where(latent, ring_ref[cur, lo + CONV_TILE:lo + CONV_TILE + CONV_HALO, :], 0.0)
        pad = upad_ref
        pad[0, 0:CONV_HALO, :] = before
        pad[0, CONV_HALO:CONV_HALO + CONV_TILE, :] = ring_ref[cur, lo:lo + CONV_TILE, :]
        pad[0, CONV_HALO + CONV_TILE:, :] = after
        _conv_pass(pad, dw_ref, db_ref, lg_ref, lb_ref, cv_ref, lo, between)

    qkv_ref[:, 2 * ATT_W:] = pj_ref[:, 2 * ATT_W:].astype(BF16)

    @pl.when(t < P_TILES)
    def _():
        qkv_ref[:, :2 * ATT_W] = pj_ref[:, :2 * ATT_W].astype(BF16)
        for ref, col0 in ((kc_ref, ATT_W), (vc_ref, 2 * ATT_W)):
            for s in range(SEQ_PER_TILE):
                for hd in range(ATT_HEADS):
                    val = pj_ref[SEQ * s:SEQ * (s + 1), col0 + V_DIM * hd:col0 + V_DIM * (hd + 1)]
                    rows = pl.ds(hd, SEQ, stride=ATT_HEADS)
                    if layer == 0:
                        ref[s, 0, rows, :] = val
                    else:
                        ref[s, rows, :] = val
            if layer == 0:
                ref[:, 1:] = jnp.zeros((SEQ_PER_TILE, DEPTH - 1, CACHE_ROWS, V_DIM), F32)

    @pl.when(t >= P_TILES)
    def _():
        cos = cos_ref[...]
        sina = sina_ref[...]
        sinb = sinb_ref[...]
        for c in range(2 * ATT_W // 128):
            xg = pj_ref[:, 128 * c:128 * (c + 1)]
            fwd = pltpu.roll(xg, 128 - 16, 1)
            bwd = pltpu.roll(xg, 16, 1)
            qkv_ref[:, 128 * c:128 * (c + 1)] = (xg * cos + fwd * sina + bwd * sinb).astype(BF16)


def _mixer_in(l, x, mod, norm_g, w_in, rope, conv_params, caches):
    cos, sina, sinb = rope
    dw_w, dw_b, ln_g, ln_b = conv_params
    x_specs, x_args = _x_specs(x)

    def tile(i):
        return jnp.minimum(i, N_TILES - 1)

    def rope_idx(i):
        return (jnp.maximum(tile(i) - P_TILES, 0) % S_TILES_PER_BATCH, 0)

    def vec(a):
        return a.reshape(DEPTH, 1, CONV_CH)

    cache_shape = jax.ShapeDtypeStruct((BATCH, DEPTH, CACHE_ROWS, V_DIM), F32)
    if l == 0:
        cache_spec = pl.BlockSpec((SEQ_PER_TILE, DEPTH, CACHE_ROWS, V_DIM),
                                  lambda i: (jnp.minimum(i, P_TILES - 1), 0, 0, 0))
        extra_specs, extra_args, aliases = [], (), {}
    else:
        cache_spec = pl.BlockSpec((SEQ_PER_TILE, None, CACHE_ROWS, V_DIM),
                                  lambda i: (jnp.minimum(i, P_TILES - 1), l, 0, 0))
        extra_specs = [pl.BlockSpec(memory_space=pl.ANY)] * 2
        extra_args = tuple(caches)
        aliases = {12: 2, 13: 3}

    return pl.pallas_call(
        functools.partial(_mixer_in_kernel, layer=l),
        out_shape=(jax.ShapeDtypeStruct((N_TOK, 3 * ATT_W), BF16),
                   jax.ShapeDtypeStruct((N_TOK, CONV_CH), BF16), cache_shape, cache_shape),
        grid=(N_TILES + CONV_LAG,),
        in_specs=x_specs + [
            pl.BlockSpec((None, None, 6, D_MODEL), lambda i: (l, _cond_id(tile(i)), 0, 0)),
            pl.BlockSpec((None, 4, D_MODEL), lambda i: (l, 0, 0)),
            pl.BlockSpec((None, D_MODEL, IN_W), lambda i: (l, 0, 0), pipeline_mode=pl.Buffered(1)),
            pl.BlockSpec((TOK_TILE, 128), rope_idx),
            pl.BlockSpec((TOK_TILE, 128), rope_idx),
            pl.BlockSpec((TOK_TILE, 128), rope_idx),
            pl.BlockSpec((None, CONV_WIDTH, CONV_CH), lambda i: (l, 0, 0)),
            pl.BlockSpec((None, 1, CONV_CH), lambda i: (l, 0, 0)),
            pl.BlockSpec((None, 1, CONV_CH), lambda i: (l, 0, 0)),
            pl.BlockSpec((None, 1, CONV_CH), lambda i: (l, 0, 0))] + extra_specs,
        out_specs=(pl.BlockSpec((TOK_TILE, 3 * ATT_W), lambda i: (tile(i), 0)),
                   pl.BlockSpec((TOK_TILE, CONV_CH), lambda i: (jnp.maximum(i - CONV_LAG, 0), 0)),
                   cache_spec, cache_spec),
        scratch_shapes=[pltpu.VMEM((D_MODEL, IN_W), BF16),
                        pltpu.VMEM((CONV_RING, TOK_TILE, CONV_CH), F32),
                        pltpu.VMEM((TOK_TILE, CONV_CH), F32),
                        pltpu.VMEM((TOK_TILE, 3 * ATT_W), F32),
                        pltpu.VMEM((8, CONV_TILE + 2 * CONV_HALO, CONV_CH), F32)],
        input_output_aliases=aliases,
        compiler_params=_cparams(1),
        name=f"mixer_in_{l}",
    )(*x_args, mod, norm_g, w_in, cos, sina, sinb, dw_w, vec(dw_b), vec(ln_g), vec(ln_b), *extra_args)


def _rope_tables():
    rows = DEC_SEQ // GRID_W
    row_pos = np.repeat(np.arange(rows, dtype=np.float64), GRID_W)
    col_pos = np.tile(np.arange(GRID_W, dtype=np.float64), rows)
    half = QK_DIM // 2
    inv_freq = 1.0 / (ROPE_THETA ** (np.arange(0, half, 2, dtype=np.float64) / half))
    ang_r = row_pos[:, None] * inv_freq
    ang_c = col_pos[:, None] * inv_freq
    ang = np.concatenate([ang_r, ang_r, ang_c, ang_c], axis=-1)
    cos = np.tile(np.cos(ang), (1, 2)).astype(np.float32)
    sin = np.tile(np.sin(ang), (1, 2)).astype(np.float32)
    first = (np.arange(128) % 32) < 16
    sina = np.where(first[None, :], -sin, 0.0).astype(np.float32)
    sinb = np.where(first[None, :], 0.0, sin).astype(np.float32)
    return jnp.asarray(cos), jnp.asarray(sina), jnp.asarray(sinb)


def _attn_kernel(*refs, lam_init, has_ext):
    if has_ext:
        lamp_ref, sub_ref, q_ref, k_ref, v_ref, ke_ref, ve_ref, o_ref = refs
    else:
        lamp_ref, sub_ref, q_ref, k_ref, v_ref, o_ref = refs
    lp = lamp_ref[...]
    lam = (jnp.exp(jnp.sum(lp[0:1] * lp[1:2], axis=-1, keepdims=True))
           - jnp.exp(jnp.sum(lp[2:3] * lp[3:4], axis=-1, keepdims=True)) + lam_init)
    lane = lax.broadcasted_iota(jnp.int32, (1, V_DIM), 1)
    nt = (((1,), (1,)), ((), ()))
    scale = QK_DIM ** -0.5
    map_scale = [jnp.where(lane < QK_DIM, scale, 0.0).astype(BF16),
                 jnp.where(lane < QK_DIM, 0.0, scale).astype(BF16)]

    tq = q_ref.shape[0]
    head_cols = [slice(V_DIM * hd, V_DIM * (hd + 1)) for hd in range(ATT_HEADS)]

    def q_map(hd, m):
        return q_ref[:, head_cols[hd]] * map_scale[m]

    def with_ones(v):
        return jnp.concatenate([v, jnp.ones_like(v)], axis=1)

    outs = []
    if has_ext:
        for hd in range(ATT_HEADS):
            kh = k_ref[:, head_cols[hd]]
            v_aug = with_ones(v_ref[:, head_cols[hd]])
            head_rows = pl.ds(hd, PAST_LEN, stride=ATT_HEADS)
            keh = ke_ref[head_rows, :].astype(BF16)
            ve_aug = with_ones(ve_ref[head_rows, :].astype(BF16))
            ratio = []
            for m in range(2):
                qm = q_map(hd, m)
                s = lax.dot_general(qm, kh, nt, preferred_element_type=F32)
                se = lax.dot_general(qm, keh, nt, preferred_element_type=F32)
                mx = jnp.maximum(jnp.max(s, axis=-1, keepdims=True), jnp.max(se, axis=-1, keepdims=True))
                pv = (jnp.dot(jnp.exp((s - mx).astype(BF16)), v_aug, preferred_element_type=F32)
                      + jnp.dot(jnp.exp((se - mx).astype(BF16)), ve_aug, preferred_element_type=F32))
                ratio.append(pv[:, :V_DIM] / pv[:, V_DIM:V_DIM + 1])
            outs.append(ratio[0] - lam * ratio[1])
    else:
        pairs = [(hd, m) for hd in range(ATT_HEADS) for m in range(2)]
        s = jnp.concatenate([lax.dot_general(q_map(hd, m), k_ref[:, head_cols[hd]], nt,
                                             preferred_element_type=F32) for hd, m in pairs], axis=0)
        p = jnp.exp((s - jnp.max(s, axis=-1, keepdims=True)).astype(BF16))
        for hd in range(ATT_HEADS):
            pv = jnp.dot(p[2 * tq * hd:2 * tq * (hd + 1)], with_ones(v_ref[:, head_cols[hd]]),
                         preferred_element_type=F32)
            ratio = pv[:, :V_DIM] / pv[:, V_DIM:V_DIM + 1]
            outs.append(ratio[:tq] - lam * ratio[tq:])

    heads = [(_rms(o, sub_ref[...]) * (1.0 - lam_init)).astype(BF16) for o in outs]
    o_ref[...] = jnp.concatenate(heads, axis=1)


def _attention(l, lam_init, lam_params, subln_g, qkvg, cache_k, cache_v):
    small = [pl.BlockSpec((None, 4, QK_DIM), lambda *_: (l, 0, 0)),
             pl.BlockSpec((None, 1, V_DIM), lambda *_: (l, 0, 0))]
    sub3 = subln_g.reshape(DEPTH, 1, V_DIM)

    o_ctx = pl.pallas_call(
        functools.partial(_attn_kernel, lam_init=lam_init, has_ext=False),
        out_shape=jax.ShapeDtypeStruct((N_PROMPT, ATT_W), BF16),
        grid=(BATCH,),
        in_specs=small + [pl.BlockSpec((SEQ, ATT_W), lambda b: (b, 0)),
                          pl.BlockSpec((SEQ, ATT_W), lambda b: (b, 1)),
                          pl.BlockSpec((SEQ, ATT_W), lambda b: (b, 2))],
        out_specs=pl.BlockSpec((SEQ, ATT_W), lambda b: (b, 0)),
        compiler_params=_cparams(1),
        name=f"attn_ctx_{l}",
    )(lam_params, sub3, qkvg, qkvg, qkvg)

    tq = 256
    q_tiles = DEC_SEQ // tq
    q_base = N_PROMPT // tq
    kv_base = N_PROMPT // DEC_SEQ

    def q_idx(b, i):
        return (q_base + b * q_tiles + i, 0)

    o_lat = pl.pallas_call(
        functools.partial(_attn_kernel, lam_init=lam_init, has_ext=True),
        out_shape=jax.ShapeDtypeStruct((N_SAMPLE, ATT_W), BF16),
        grid=(DEC_BATCH, q_tiles),
        in_specs=small + [pl.BlockSpec((tq, ATT_W), q_idx),
                          pl.BlockSpec((DEC_SEQ, ATT_W), lambda b, i: (kv_base + b, 1)),
                          pl.BlockSpec((DEC_SEQ, ATT_W), lambda b, i: (kv_base + b, 2)),
                          pl.BlockSpec((None, None, PAST_LEN * ATT_HEADS, V_DIM), lambda b, i: (b, l, 0, 0)),
                          pl.BlockSpec((None, None, PAST_LEN * ATT_HEADS, V_DIM), lambda b, i: (b, l, 0, 0))],
        out_specs=pl.BlockSpec((tq, ATT_W), lambda b, i: (b * q_tiles + i, 0)),
        compiler_params=_cparams(2),
        name=f"attn_lat_{l}",
    )(lam_params, sub3, qkvg, qkvg, qkvg, cache_k, cache_v)
    return o_ctx, o_lat


def _outproj_kernel(*refs, with_router):
    if with_router:
        (oc_ref, ol_ref, cv_ref, xa_ref, xb_ref, mod_ref, g_ref, w_ref, rw_ref,
         xo_ref, h2_ref, info_ref, info_t_ref, cb_ref, tot_ref, wbf_ref, carry_ref) = refs
    else:
        oc_ref, ol_ref, cv_ref, xa_ref, xb_ref, mod_ref, g_ref, w_ref, xo_ref, h2_ref, wbf_ref = refs
    i = pl.program_id(0)

    @pl.when(i == 0)
    def _():
        wbf_ref[...] = w_ref[...].astype(BF16)

    half = TOK_TILE // 2
    for rows in (slice(0, half), slice(half, TOK_TILE)):
        o = jnp.where(i < P_TILES, oc_ref[rows, :], ol_ref[rows, :])
        m = (jnp.dot(o, wbf_ref[0:ATT_W, :], preferred_element_type=F32)
             + jnp.dot(cv_ref[rows, :], wbf_ref[ATT_W:, :], preferred_element_type=F32))
        x = jnp.where(i < P_TILES, xa_ref[rows, :], xb_ref[rows, :])
        xn = x + mod_ref[2:3, :] * _rms(m, g_ref[1:2, :])
        xo_ref[rows, :] = xn
        h2_ref[rows, :] = (_rms(xn, g_ref[2:3, :]) * (1.0 + mod_ref[4:5, :]) + mod_ref[3:4, :]).astype(BF16)

    if with_router:
        @pl.when(i == 0)
        def _():
            carry_ref[...] = jnp.zeros_like(carry_ref)

        _route_tile(h2_ref[...], rw_ref, info_ref, info_t_ref, cb_ref, tot_ref, carry_ref)


def _outproj(l, o_ctx, o_lat, cv, x, mod, norm_g, w_out, router_w=None):
    x_specs, x_args = _x_specs(x)
    out_shape = [jax.ShapeDtypeStruct((N_TOK, D_MODEL), F32), jax.ShapeDtypeStruct((N_TOK, D_MODEL), BF16)]
    out_specs = [pl.BlockSpec((TOK_TILE, D_MODEL), lambda i: (i, 0)),
                 pl.BlockSpec((TOK_TILE, D_MODEL), lambda i: (i, 0))]
    scratch = [pltpu.VMEM((D_MODEL, D_MODEL), BF16)]
    extra_specs, extra_args = [], ()
    if router_w is not None:
        rw_pad = jnp.zeros((D_MODEL, 128), BF16).at[:, :N_EXPERTS].set(router_w.astype(BF16))
        extra_specs, extra_args = [pl.BlockSpec((D_MODEL, 128), lambda i: (0, 0))], (rw_pad,)
        out_shape += [jax.ShapeDtypeStruct((N_TOK, 128), F32),
                      jax.ShapeDtypeStruct((N_TILES, 8, TOK_TILE), F32),
                      jax.ShapeDtypeStruct((N_TILES, 8, 128), F32),
                      jax.ShapeDtypeStruct((8, 128), F32)]
        out_specs += [pl.BlockSpec((TOK_TILE, 128), lambda i: (i, 0)),
                      pl.BlockSpec((None, 8, TOK_TILE), lambda i: (i, 0, 0)),
                      pl.BlockSpec((None, 8, 128), lambda i: (i, 0, 0)),
                      pl.BlockSpec((8, 128), lambda i: (0, 0))]
        scratch += [pltpu.VMEM((8, 128), F32)]
    return pl.pallas_call(
        functools.partial(_outproj_kernel, with_router=router_w is not None),
        out_shape=tuple(out_shape),
        grid=(N_TILES,),
        in_specs=[pl.BlockSpec((TOK_TILE, ATT_W), lambda i: (jnp.minimum(i, P_TILES - 1), 0)),
                  pl.BlockSpec((TOK_TILE, ATT_W), lambda i: (jnp.maximum(i - P_TILES, 0), 0)),
                  pl.BlockSpec((TOK_TILE, CONV_CH), lambda i: (i, 0))] + x_specs + [
                  pl.BlockSpec((None, None, 6, D_MODEL), lambda i: (l, _cond_id(i), 0, 0)),
                  pl.BlockSpec((None, 4, D_MODEL), lambda i: (l, 0, 0)),
                  pl.BlockSpec((None, D_MODEL, D_MODEL), lambda i: (l, 0, 0),
                               pipeline_mode=pl.Buffered(1))] + extra_specs,
        out_specs=tuple(out_specs),
        scratch_shapes=scratch,
        compiler_params=_cparams(1),
        name=f"outproj_{l}",
    )(o_ctx, o_lat, cv, *x_args, mod, norm_g, w_out, *extra_args)


DENSE_CHUNK = D_FF


def _dense_ffn_kernel(h_ref, x_ref, mod_ref, g_ref, w13_ref, w2_ref, xo_ref):
    h = h_ref[...]
    acc = None
    for c in range(D_FF // DENSE_CHUNK):
        lo = c * DENSE_CHUNK
        gt = jnp.dot(h, w13_ref[:, lo:lo + DENSE_CHUNK], preferred_element_type=F32)
        up = jnp.dot(h, w13_ref[:, D_FF + lo:D_FF + lo + DENSE_CHUNK], preferred_element_type=F32)
        a = (gt * _sigmoid(gt) * up).astype(BF16)
        part = jnp.dot(a, w2_ref[lo:lo + DENSE_CHUNK, :], preferred_element_type=F32)
        acc = part if acc is None else acc + part
    xo_ref[...] = x_ref[...] + mod_ref[5:6, :] * _rms(acc, g_ref[3:4, :])


def _dense_ffn(l, h2, x, mod, norm_g, w13, w2):
    return pl.pallas_call(
        _dense_ffn_kernel,
        out_shape=jax.ShapeDtypeStruct((N_TOK, D_MODEL), F32),
        grid=(N_TILES,),
        in_specs=[pl.BlockSpec((TOK_TILE, D_MODEL), lambda i: (i, 0)),
                  pl.BlockSpec((TOK_TILE, D_MODEL), lambda i: (i, 0)),
                  pl.BlockSpec((None, None, 6, D_MODEL), lambda i: (l, _cond_id(i), 0, 0)),
                  pl.BlockSpec((None, 4, D_MODEL), lambda i: (l, 0, 0)),
                  pl.BlockSpec((D_MODEL, 2 * D_FF), lambda i: (0, 0), pipeline_mode=pl.Buffered(1)),
                  pl.BlockSpec((D_FF, D_MODEL), lambda i: (0, 0), pipeline_mode=pl.Buffered(1))],
        out_specs=pl.BlockSpec((TOK_TILE, D_MODEL), lambda i: (i, 0)),
        compiler_params=_cparams(1),
        name=f"dense_ffn_{l}",
    )(h2, x, mod, norm_g, w13, w2)


def _route_tile(h, rw_ref, info_ref, info_t_ref, cb_ref, tot_ref, carry_ref):
    lane = lax.broadcasted_iota(jnp.int32, (TOK_TILE, 128), 1)
    lanef = lane.astype(F32)
    logits = jnp.dot(h, rw_ref[...], preferred_element_type=F32)
    logits = jnp.where(lane < N_EXPERTS, logits, -jnp.inf)
    big = jnp.asarray(128.0, F32)
    m1 = jnp.max(logits, axis=-1, keepdims=True)
    e1 = jnp.min(jnp.where(logits == m1, lanef, big), axis=-1, keepdims=True)
    oh1 = lanef == e1
    rest = jnp.where(oh1, -jnp.inf, logits)
    m2 = jnp.max(rest, axis=-1, keepdims=True)
    e2 = jnp.min(jnp.where(rest == m2, lanef, big), axis=-1, keepdims=True)
    oh2 = lanef == e2
    ex = jnp.exp(m2 - m1)
    g1 = 1.0 / (1.0 + ex)
    g2 = ex / (1.0 + ex)

    oh = jnp.where(jnp.logical_or(oh1, oh2), 1.0, 0.0)
    r = lax.broadcasted_iota(jnp.int32, (TOK_TILE, TOK_TILE), 0)
    c = lax.broadcasted_iota(jnp.int32, (TOK_TILE, TOK_TILE), 1)
    tri = jnp.where(c <= r, 1.0, 0.0).astype(BF16)
    incl = jnp.dot(tri, oh.astype(BF16), preferred_element_type=F32)
    carry = carry_ref[0:1, :]
    excl = incl - oh + carry
    rank1 = jnp.sum(jnp.where(oh1, excl, 0.0), axis=-1, keepdims=True)
    rank2 = jnp.sum(jnp.where(oh2, excl, 0.0), axis=-1, keepdims=True)

    info = jnp.where(lane == 0, e1, 0.0)
    for k, col in enumerate((e2, rank1, rank2, g1, g2), start=1):
        info = jnp.where(lane == k, col, info)
    info_ref[...] = info
    info_t_ref[...] = info.T[0:8, :]

    cb_ref[...] = carry_ref[...]
    new_carry = carry + incl[TOK_TILE - 1:TOK_TILE, :]
    carry_ref[...] = jnp.broadcast_to(new_carry, carry_ref.shape)
    tot_ref[...] = jnp.broadcast_to(new_carry, tot_ref.shape)


def _sorted_pos(expert, rank, base_ref):
    start = jnp.zeros_like(rank)
    for e in range(N_EXPERTS):
        start = jnp.where(expert == float(e), base_ref[e].astype(F32), start)
    return start + rank


def _dispatch_kernel(clo_ref, cn_ref, base_ref, info_t_ref, x_ref, o_ref):
    r = pl.program_id(0)
    o_ref[...] = jnp.zeros_like(o_ref)
    rows = (r * DISPATCH_TILE + lax.broadcasted_iota(jnp.int32, (DISPATCH_TILE, TOK_TILE), 0)).astype(F32)

    def body(k, carry):
        c = clo_ref[r] + k
        it = info_t_ref[c]
        pos1 = _sorted_pos(it[0:1, :], it[2:3, :], base_ref)
        pos2 = _sorted_pos(it[1:2, :], it[3:4, :], base_ref)
        hit = jnp.logical_or(rows == pos1, rows == pos2)
        sel = jnp.where(hit, 1.0, 0.0).astype(BF16)
        off = pl.multiple_of(c * TOK_TILE, TOK_TILE)
        o_ref[...] += jnp.dot(sel, x_ref[pl.ds(off, TOK_TILE), :], preferred_element_type=F32).astype(BF16)
        return carry

    lax.fori_loop(0, cn_ref[r], body, 0)


def _dispatch(c_lo, c_n, base, info_t, h2):
    return pl.pallas_call(
        _dispatch_kernel,
        out_shape=jax.ShapeDtypeStruct((SORT_ROWS, D_MODEL), BF16),
        grid_spec=pltpu.PrefetchScalarGridSpec(
            num_scalar_prefetch=3,
            grid=(SORT_ROWS // DISPATCH_TILE,),
            in_specs=[pl.BlockSpec((N_TILES, 8, TOK_TILE), lambda r, *_: (0, 0, 0)),
                      pl.BlockSpec((N_TOK, D_MODEL), lambda r, *_: (0, 0),
                                   pipeline_mode=pl.Buffered(1))],
            out_specs=pl.BlockSpec((DISPATCH_TILE, D_MODEL), lambda r, *_: (r, 0))),
        compiler_params=_cparams(1),
        name="moe_dispatch",
    )(c_lo, c_n, base, info_t, h2)


TILE_UNUSED, TILE_HALF, TILE_FULL = 0, 1, 2


def _expert_weight_ring(c, r, n_c, te_ref, tf_ref, nx_ref, copies, cast):
    @pl.when(jnp.logical_and(c == 0, r == 0))
    def _():
        for cp in copies(te_ref[0], 0):
            cp.start()

    @pl.when(tf_ref[r] == 1)
    def _():
        for cp in copies(te_ref[r], c):
            cp.wait()
        cast()
        nr = nx_ref[r]
        nc = c + (nr <= r).astype(jnp.int32)

        @pl.when(nc < n_c)
        def _():
            for cp in copies(te_ref[nr], nc):
                cp.start()


def _moe_up_kernel(te_ref, tv_ref, tf_ref, nx_ref, x_ref, w_hbm, h_ref, wst_ref, wbf_ref, sem_ref, *, i_moe):
    f = pl.program_id(0)
    r = pl.program_id(1)

    def copies(e, ft):
        out = []
        for part in range(2):
            col = pl.multiple_of((part * UP_TILES + ft) * UP_TILE, 128)
            out.append(pltpu.make_async_copy(w_hbm.at[i_moe, e, :, pl.ds(col, UP_TILE)],
                                             wst_ref.at[part], sem_ref.at[part]))
        return out

    def cast():
        wbf_ref[...] = wst_ref[...].astype(BF16)

    _expert_weight_ring(f, r, UP_TILES, te_ref, tf_ref, nx_ref, copies, cast)

    def hidden(x):
        gt = jnp.dot(x, wbf_ref[0], preferred_element_type=F32)
        up = jnp.dot(x, wbf_ref[1], preferred_element_type=F32)
        return (gt * _sigmoid(gt) * up).astype(BF16)

    half = ROW_TILE // 2

    @pl.when(tv_ref[r] == TILE_FULL)
    def _():
        h_ref[...] = hidden(x_ref[...])

    @pl.when(tv_ref[r] == TILE_HALF)
    def _():
        h_ref[0:half, :] = hidden(x_ref[0:half, :])
        h_ref[half:, :] = jnp.zeros((half, UP_TILE), BF16)

    @pl.when(tv_ref[r] == TILE_UNUSED)
    def _():
        h_ref[...] = jnp.zeros_like(h_ref)


def _moe_up(i_moe, tile_e, tile_valid, tile_first, tile_next, xs, w13):
    return pl.pallas_call(
        functools.partial(_moe_up_kernel, i_moe=i_moe),
        out_shape=jax.ShapeDtypeStruct((SORT_ROWS, D_FF_EXPERT), BF16),
        grid_spec=pltpu.PrefetchScalarGridSpec(
            num_scalar_prefetch=4,
            grid=(UP_TILES, SORT_TILES),
            in_specs=[pl.BlockSpec((ROW_TILE, D_MODEL), lambda f, r, *_: (r, 0)),
                      pl.BlockSpec(memory_space=pl.ANY)],
            out_specs=pl.BlockSpec((ROW_TILE, UP_TILE), lambda f, r, *_: (r, f)),
            scratch_shapes=[pltpu.VMEM((2, D_MODEL, UP_TILE), F32),
                            pltpu.VMEM((2, D_MODEL, UP_TILE), BF16),
                            pltpu.SemaphoreType.DMA((2,))]),
        compiler_params=_cparams(2),
        name="moe_up",
    )(tile_e, tile_valid, tile_first, tile_next, xs, w13)


def _moe_down_kernel(te_ref, tv_ref, tf_ref, nx_ref, h_ref, w_hbm, y_ref, wst_ref, wbf_ref, sem_ref, *, i_moe):
    n = pl.program_id(0)
    r = pl.program_id(1)

    def copies(e, nt):
        col = pl.multiple_of(nt * DOWN_TILE, 128)
        return [pltpu.make_async_copy(w_hbm.at[i_moe, e, :, pl.ds(col, DOWN_TILE)], wst_ref, sem_ref.at[0])]

    def cast():
        wbf_ref[...] = wst_ref[...].astype(BF16)

    _expert_weight_ring(n, r, D_MODEL // DOWN_TILE, te_ref, tf_ref, nx_ref, copies, cast)

    half = ROW_TILE // 2

    @pl.when(tv_ref[r] == TILE_FULL)
    def _():
        y_ref[...] = jnp.dot(h_ref[...], wbf_ref[...], preferred_element_type=F32).astype(BF16)

    @pl.when(tv_ref[r] == TILE_HALF)
    def _():
        y_ref[0:half, :] = jnp.dot(h_ref[0:half, :], wbf_ref[...], preferred_element_type=F32).astype(BF16)
        y_ref[half:, :] = jnp.zeros((half, DOWN_TILE), BF16)

    @pl.when(tv_ref[r] == TILE_UNUSED)
    def _():
        y_ref[...] = jnp.zeros_like(y_ref)


def _moe_down(i_moe, tile_e, tile_valid, tile_first, tile_next, hs, w2):
    return pl.pallas_call(
        functools.partial(_moe_down_kernel, i_moe=i_moe),
        out_shape=jax.ShapeDtypeStruct((SORT_ROWS, D_MODEL), BF16),
        grid_spec=pltpu.PrefetchScalarGridSpec(
            num_scalar_prefetch=4,
            grid=(D_MODEL // DOWN_TILE, SORT_TILES),
            in_specs=[pl.BlockSpec((ROW_TILE, D_FF_EXPERT), lambda n, r, *_: (r, 0)),
                      pl.BlockSpec(memory_space=pl.ANY)],
            out_specs=pl.BlockSpec((ROW_TILE, DOWN_TILE), lambda n, r, *_: (r, n)),
            scratch_shapes=[pltpu.VMEM((D_FF_EXPERT, DOWN_TILE), F32),
                            pltpu.VMEM((D_FF_EXPERT, DOWN_TILE), BF16),
                            pltpu.SemaphoreType.DMA((1,))]),
        compiler_params=_cparams(2),
        name="moe_down",
    )(tile_e, tile_valid, tile_first, tile_next, hs, w2)


def _window_copy(y_hbm, ybuf_ref, sem_ref, src, buf, slot):
    return pltpu.make_async_copy(y_hbm.at[pl.ds(src, WIN), :],
                                 ybuf_ref.at[buf, pl.ds(slot * WIN, WIN), :], sem_ref.at[buf, slot])


def _start_windows(src_ref, y_hbm, ybuf_ref, sem_ref, tile, buf):
    for s in range(N_WIN):
        src = pl.multiple_of(src_ref[tile * N_WIN + s], 16)
        _window_copy(y_hbm, ybuf_ref, sem_ref, src, buf, s).start()


def _combine_kernel(src_ref, lo_ref, hi_ref, base_ref, info_ref, x_ref, mod_ref, g_ref, y_hbm,
                    yp_ref, ys_ref, ybuf_ref, sem_ref):
    j = pl.program_id(0)
    buf = j % 2

    @pl.when(j == 0)
    def _():
        _start_windows(src_ref, y_hbm, ybuf_ref, sem_ref, 0, 0)

    @pl.when(j + 1 < N_TILES)
    def _():
        _start_windows(src_ref, y_hbm, ybuf_ref, sem_ref, j + 1, 1 - buf)

    info = info_ref[...]
    pos1 = _sorted_pos(info[:, 0:1], info[:, 2:3], base_ref)
    pos2 = _sorted_pos(info[:, 1:2], info[:, 3:4], base_ref)
    g1, g2 = info[:, 4:5], info[:, 5:6]
    iota = lax.broadcasted_iota(jnp.int32, (1, WIN), 1)
    ids = []
    for s in range(N_WIN):
        row = src_ref[j * N_WIN + s] + iota
        ok = jnp.logical_and(row >= lo_ref[j * N_WIN + s], row < hi_ref[j * N_WIN + s])
        ids.append(jnp.where(ok, row, -1))
    row_id = jnp.concatenate(ids, axis=1).astype(F32)
    sel = (jnp.where(pos1 == row_id, g1, 0.0) + jnp.where(pos2 == row_id, g2, 0.0)).astype(BF16)

    for s in range(N_WIN):
        _window_copy(y_hbm, ybuf_ref, sem_ref, 0, buf, s).wait()
    ffn = jnp.dot(sel, ybuf_ref[buf], preferred_element_type=F32)
    out = x_ref[...] + mod_ref[5:6, :] * _rms(ffn, g_ref[3:4, :])

    @pl.when(j < P_TILES)
    def _():
        yp_ref[...] = out

    @pl.when(j >= P_TILES)
    def _():
        ys_ref[...] = out


def _combine(l, win_src, win_lo, win_hi, base, info, x, mod, norm_g, ys):
    return pl.pallas_call(
        _combine_kernel,
        out_shape=(jax.ShapeDtypeStruct((N_PROMPT, D_MODEL), F32),
                   jax.ShapeDtypeStruct((N_SAMPLE, D_MODEL), F32)),
        grid_spec=pltpu.PrefetchScalarGridSpec(
            num_scalar_prefetch=4,
            grid=(N_TILES,),
            in_specs=[pl.BlockSpec((TOK_TILE, 128), lambda i, *_: (i, 0)),
                      pl.BlockSpec((TOK_TILE, D_MODEL), lambda i, *_: (i, 0)),
                      pl.BlockSpec((None, None, 6, D_MODEL), lambda i, *_: (l, _cond_id(i), 0, 0)),
                      pl.BlockSpec((None, 4, D_MODEL), lambda i, *_: (l, 0, 0)),
                      pl.BlockSpec(memory_space=pl.ANY)],
            out_specs=(pl.BlockSpec((TOK_TILE, D_MODEL), lambda i, *_: (jnp.minimum(i, P_TILES - 1), 0)),
                       pl.BlockSpec((TOK_TILE, D_MODEL), lambda i, *_: (jnp.maximum(i - P_TILES, 0), 0))),
            scratch_shapes=[pltpu.VMEM((2, N_WIN * WIN, D_MODEL), BF16),
                            pltpu.SemaphoreType.DMA((2, N_WIN))]),
        compiler_params=_cparams(1),
        name="moe_combine",
    )(win_src, win_lo, win_hi, base, info, x, mod, norm_g, ys)


def _moe_ffn(l, i_moe, h2, x, mod, norm_g, routing, moe_w13, moe_w2):
    info, info_t, cb, tot = routing

    counts = tot[0, :N_EXPERTS].astype(jnp.int32)
    padded = (counts + ROW_TILE - 1) // ROW_TILE * ROW_TILE
    seg_end = jnp.cumsum(padded)
    base = (seg_end - padded).astype(jnp.int32)
    cbx = cb[:, 0, :N_EXPERTS].astype(jnp.int32)
    cb_end = jnp.concatenate([cbx[1:], counts[None, :]], axis=0)

    row0 = jnp.arange(SORT_TILES, dtype=jnp.int32) * ROW_TILE
    tile_e = jnp.minimum(jnp.sum(seg_end[None, :] <= row0[:, None], axis=1), N_EXPERTS - 1).astype(jnp.int32)
    k0 = row0 - base[tile_e]
    tile_valid = jnp.logical_and(k0 >= 0, k0 < counts[tile_e])
    tile_first = jnp.logical_and(tile_valid, k0 == 0)
    per = ROW_TILE // DISPATCH_TILE
    d_e = jnp.repeat(tile_e, per)
    d_k0 = jnp.repeat(k0, per) + jnp.tile(jnp.arange(per, dtype=jnp.int32) * DISPATCH_TILE, SORT_TILES)
    d_valid = jnp.logical_and(d_k0 >= 0, d_k0 < counts[d_e])
    d_kend = jnp.minimum(d_k0 + DISPATCH_TILE, counts[d_e])
    cb_t = cbx.T[d_e]
    c_lo = jnp.sum(cb_t <= d_k0[:, None], axis=1) - 1
    c_hi = jnp.sum(cb_t < d_kend[:, None], axis=1) - 1
    c_n = jnp.where(d_valid, c_hi - c_lo + 1, 0).astype(jnp.int32)
    c_lo = jnp.where(d_valid, c_lo, 0).astype(jnp.int32)
    tile_idx = jnp.arange(SORT_TILES, dtype=jnp.int32)
    first_idx = jnp.where(tile_first, tile_idx, SORT_TILES)
    later = jnp.where(tile_idx[None, :] > tile_idx[:, None], first_idx[None, :], SORT_TILES)
    tile_next = jnp.min(later, axis=1)
    tile_next = jnp.where(tile_next >= SORT_TILES, 0, tile_next).astype(jnp.int32)
    tile_rows = counts[tile_e] - k0
    tile_valid = jnp.where(tile_valid, jnp.where(tile_rows <= ROW_TILE // 2, TILE_HALF, TILE_FULL),
                           TILE_UNUSED).astype(jnp.int32)
    tile_first = tile_first.astype(jnp.int32)

    seg_lo = base[None, :] + cbx
    seg_hi = base[None, :] + cb_end
    seg_n = seg_hi - seg_lo
    w0 = seg_lo // 16 * 16
    nw = jnp.where(seg_n > 0, (seg_lo - w0 + seg_n + WIN - 1) // WIN, 0)
    nw_end = jnp.cumsum(nw, axis=1)
    slot = jnp.arange(N_WIN, dtype=jnp.int32)
    slot_e = jnp.minimum(jnp.sum(nw_end[:, None, :] <= slot[None, :, None], axis=2), N_EXPERTS - 1)
    pick = slot_e[:, :, None] == jnp.arange(N_EXPERTS, dtype=jnp.int32)[None, None, :]
    take = lambda a: jnp.sum(jnp.where(pick, a[:, None, :], 0), axis=2)
    slot_k = slot[None, :] - (take(nw_end) - take(nw))
    slot_ok = slot[None, :] < nw_end[:, -1:]
    slot_src = take(w0) + WIN * slot_k
    win_src = jnp.where(slot_ok, slot_src, 0).astype(jnp.int32).reshape(-1)
    win_lo = jnp.where(slot_ok, take(seg_lo), 0).astype(jnp.int32).reshape(-1)
    win_hi = jnp.where(slot_ok, take(seg_hi), 0).astype(jnp.int32).reshape(-1)

    xs = _dispatch(c_lo, c_n, base, info_t, h2)
    hs = _moe_up(i_moe, tile_e, tile_valid, tile_first, tile_next, xs, moe_w13)
    ys = _moe_down(i_moe, tile_e, tile_valid, tile_first, tile_next, hs, moe_w2)
    return _combine(l, win_src, win_lo, win_hi, base, info, x, mod, norm_g, ys)


def kernel(x_prompt, x_sample, cache_k, cache_v, c, c_ctx, w_ada, b_ada, norm_g, w_in, w_out,
           lam_params, subln_g, dw_weight, dw_bias, conv_ln_g, conv_ln_b, dense_w13, dense_w2,
           router_w, moe_w13, moe_w2):
    x = (x_prompt.reshape(N_PROMPT, D_MODEL), x_sample.reshape(N_SAMPLE, D_MODEL))
    cond8 = jnp.zeros((8, D_MODEL), F32).at[0].set(c_ctx).at[1:1 + DEC_BATCH].set(c)
    mod = _ada_modulation(cond8, w_ada, b_ada).reshape(DEPTH, 8, 6, D_MODEL)
    rope = _rope_tables()
    ck = cache_k.reshape(DEC_BATCH, DEPTH, PAST_LEN * ATT_HEADS, V_DIM)
    cv = cache_v.reshape(DEC_BATCH, DEPTH, PAST_LEN * ATT_HEADS, V_DIM)

    caches = None
    for l in range(DEPTH):
        lam_init = 0.8 - 0.6 * math.exp(-0.3 * l)
        qkv, cvo, *caches = _mixer_in(l, x, mod, norm_g, w_in, rope,
                                      (dw_weight, dw_bias, conv_ln_g, conv_ln_b), caches)
        o_ctx, o_lat = _attention(l, lam_init, lam_params, subln_g, qkv, ck, cv)
        i = l // 2
        if l % 2 == 0:
            x, h2 = _outproj(l, o_ctx, o_lat, cvo, x, mod, norm_g, w_out)
            x = _dense_ffn(l, h2, x, mod, norm_g, dense_w13[i].astype(BF16), dense_w2[i].astype(BF16))
        else:
            x, h2, *routing = _outproj(l, o_ctx, o_lat, cvo, x, mod, norm_g, w_out, router_w[i])
            x = _moe_ffn(l, i, h2, x, mod, norm_g, routing, moe_w13, moe_w2)

    xp, xs = x if isinstance(x, tuple) else (x[:N_PROMPT], x[N_PROMPT:])
    new_k, new_v = (a.reshape(BATCH, DEPTH, SEQ, ATT_HEADS, V_DIM) for a in caches)
    return (xp.reshape(BATCH, SEQ, D_MODEL), xs.reshape(DEC_BATCH, DEC_SEQ, D_MODEL), new_k, new_v)
```

```python
import functools
import math

import jax
import jax.numpy as jnp
import numpy as np
from jax import lax
from jax.experimental import pallas as pl
from jax.experimental.pallas import tpu as pltpu

F32 = jnp.float32
BF16 = jnp.bfloat16

D_MODEL = 1024
BATCH = 32
SEQ = 256
DEPTH = 2
DEC_BATCH = 2
DEC_SEQ = 2048
PAST_LEN = 512
GRID_W = 64
ATT_HEADS = 4
QK_DIM = 64
V_DIM = 128
ATT_W = ATT_HEADS * V_DIM
IN_W = 5 * ATT_W
CONV_CH = 512
CONV_WIDTH = 31
D_FF = 2816
N_EXPERTS = 8
D_FF_EXPERT = 3584
ROPE_THETA = 10000.0
NORM_EPS = 1e-6
LN_EPS = 1e-5

N_PROMPT = BATCH * SEQ
N_SAMPLE = DEC_BATCH * DEC_SEQ
N_TOK = N_PROMPT + N_SAMPLE

TOK_TILE = 512
N_TILES = N_TOK // TOK_TILE
P_TILES = N_PROMPT // TOK_TILE
S_TILES_PER_BATCH = DEC_SEQ // TOK_TILE

ROW_TILE = 512
SORT_TILES = (2 * N_TOK + N_EXPERTS * ROW_TILE) // ROW_TILE + 1
SORT_ROWS = SORT_TILES * ROW_TILE
DISPATCH_TILE = 256
UP_TILE = 1792
UP_TILES = D_FF_EXPERT // UP_TILE
DOWN_TILE = 1024
WIN = 128
N_WIN = 16

VMEM_LIMIT = 56 * 1024 * 1024


def _cparams(n_axes):
    return pltpu.CompilerParams(dimension_semantics=("arbitrary",) * n_axes,
                                vmem_limit_bytes=VMEM_LIMIT)


def _cond_id(i):
    return jnp.where(i < P_TILES, 0, 1 + (i - P_TILES) // S_TILES_PER_BATCH)


def _sigmoid(x):
    return 1.0 / (1.0 + jnp.exp(-x))


def _rms(x, g):
    return x * lax.rsqrt(jnp.mean(x * x, axis=-1, keepdims=True) + NORM_EPS) * g


def _ada_kernel(c_ref, w_ref, b_ref, o_ref):
    c = c_ref[...]
    s = (c * _sigmoid(c)).astype(BF16)
    o_ref[...] = jnp.dot(s, w_ref[...].astype(BF16), preferred_element_type=F32) + b_ref[...]


def _ada_modulation(cond8, w_ada, b_ada):
    tn = 1536
    n = 6 * D_MODEL
    return pl.pallas_call(
        _ada_kernel,
        out_shape=jax.ShapeDtypeStruct((DEPTH, 8, n), F32),
        grid=(DEPTH, n // tn),
        in_specs=[pl.BlockSpec((8, D_MODEL), lambda l, j: (0, 0)),
                  pl.BlockSpec((None, D_MODEL, tn), lambda l, j: (l, 0, j)),
                  pl.BlockSpec((None, 1, tn), lambda l, j: (l, 0, j))],
        out_specs=pl.BlockSpec((None, 8, tn), lambda l, j: (l, 0, j)),
        compiler_params=_cparams(2),
        name="ada_modulation",
    )(cond8, w_ada, b_ada.reshape(DEPTH, 1, n))


def _tile_x(xa_ref, xb_ref):
    return jnp.where(pl.program_id(0) < P_TILES, xa_ref[...], xb_ref[...])


def _x_specs(x):
    last = N_TILES - 1
    if isinstance(x, tuple):
        xa, xb = x
        b_idx = lambda i, *_: (jnp.maximum(jnp.minimum(i, last) - P_TILES, 0), 0)
    else:
        xa = xb = x
        b_idx = lambda i, *_: (jnp.maximum(jnp.minimum(i, last), P_TILES), 0)
    a_idx = lambda i, *_: (jnp.minimum(i, P_TILES - 1), 0)
    return [pl.BlockSpec((TOK_TILE, D_MODEL), a_idx), pl.BlockSpec((TOK_TILE, D_MODEL), b_idx)], (xa, xb)


SEQ_PER_TILE = TOK_TILE // SEQ
CACHE_ROWS = SEQ * ATT_HEADS


CONV_LAG = 2
CONV_RING = 4
CONV_TILE = 256
CONV_HALO = 16
CONV_SUB = 32


def _conv_pass(upad_ref, w_ref, bias_ref, lg_ref, lb_ref, o_ref, row0, between):
    rows = CONV_TILE + 2 * CONV_HALO - 8
    for s in range(1, 8):
        upad_ref[s, 0:rows, :] = upad_ref[0, s:s + rows, :]
    first_tap = CONV_HALO - CONV_WIDTH // 2
    for t in range(CONV_TILE // CONV_SUB):
        between()
        base = t * CONV_SUB
        acc = jnp.zeros((CONV_SUB, CONV_CH), F32)
        for j in range(CONV_WIDTH):
            lo = base + (first_tap + j) // 8 * 8
            acc = acc + upad_ref[(first_tap + j) % 8, lo:lo + CONV_SUB, :] * w_ref[j:j + 1, :]
        y = acc + bias_ref[...]
        mu = jnp.mean(y, axis=-1, keepdims=True)
        yc = y - mu
        var = jnp.mean(yc * yc, axis=-1, keepdims=True)
        z = yc * lax.rsqrt(var + LN_EPS) * lg_ref[...] + lb_ref[...]
        o_ref[row0 + base:row0 + base + CONV_SUB, :] = (z * _sigmoid(z)).astype(BF16)


def _mixer_in_kernel(*refs, layer):
    n_in = 12 + (2 if layer else 0)
    (xa_ref, xb_ref, mod_ref, g_ref, w_ref, cos_ref, sina_ref, sinb_ref,
     dw_ref, db_ref, lg_ref, lb_ref) = refs[:12]
    qkv_ref, cv_ref, kc_ref, vc_ref, wbf_ref, ring_ref, glu_ref, pj_ref, upad_ref = refs[n_in:]
    i = pl.program_id(0)
    t = jnp.minimum(i, N_TILES - 1)

    @pl.when(i == 0)
    def _():
        wbf_ref[...] = w_ref[...].astype(BF16)
        ring_ref[...] = jnp.zeros_like(ring_ref)
        glu_ref[...] = jnp.zeros_like(glu_ref)

    ring_ref[(i - 1) % CONV_RING] = glu_ref[...]

    h = _rms(_tile_x(xa_ref, xb_ref), g_ref[0:1, :]) * (1.0 + mod_ref[1:2, :]) + mod_ref[0:1, :]
    hb = h.astype(BF16)

    def proj(part):
        return jnp.dot(hb, wbf_ref[:, part * ATT_W:(part + 1) * ATT_W], preferred_element_type=F32)

    def proj_part(part):
        if part < 3:
            pj_ref[:, part * ATT_W:(part + 1) * ATT_W] = proj(part)
        elif part == 3:
            glu_ref[...] = proj(3)
        else:
            glu_ref[...] = glu_ref[...] * _sigmoid(proj(4))

    conv_units = 2 * (CONV_TILE // CONV_SUB)
    issue_at = {k * conv_units // 5: k for k in range(5)}
    unit = [0]

    def between():
        if unit[0] in issue_at:
            proj_part(issue_at[unit[0]])
        unit[0] += 1

    j = i - CONV_LAG
    latent = j >= P_TILES
    q = (j - P_TILES) % S_TILES_PER_BATCH
    has_prev = jnp.logical_and(latent, q != 0)
    has_next = jnp.logical_and(latent, q != S_TILES_PER_BATCH - 1)
    cur = j % CONV_RING
    prev_tail = ring_ref[(j - 1) % CONV_RING, TOK_TILE - CONV_HALO:, :]
    next_head = ring_ref[(j + 1) % CONV_RING, 0:CONV_HALO, :]
    for s in range(TOK_TILE // CONV_TILE):
        lo = s * CONV_TILE
        if s == 0:
            before = jnp.where(has_prev, prev_tail, 0.0)
        else:
            before = jnp.where(latent, ring_ref[cur, lo - CONV_HALO:lo, :], 0.0)
        if s == TOK_TILE // CONV_TILE - 1:
            after = jnp.where(has_next, next_head, 0.0)
        else:
            after = jnp.where(latent, ring_ref[cur, lo + CONV_TILE:lo + CONV_TILE + CONV_HALO, :], 0.0)
        pad = upad_ref
        pad[0, 0:CONV_HALO, :] = before
        pad[0, CONV_HALO:CONV_HALO + CONV_TILE, :] = ring_ref[cur, lo:lo + CONV_TILE, :]
        pad[0, CONV_HALO + CONV_TILE:, :] = after
        _conv_pass(pad, dw_ref, db_ref, lg_ref, lb_ref, cv_ref, lo, between)

    qkv_ref[:, 2 * ATT_W:] = pj_ref[:, 2 * ATT_W:].astype(BF16)

    @pl.when(t < P_TILES)
    def _():
        qkv_ref[:, :2 * ATT_W] = pj_ref[:, :2 * ATT_W].astype(BF16)
        for ref, col0 in ((kc_ref, ATT_W), (vc_ref, 2 * ATT_W)):
            for s in range(SEQ_PER_TILE):
                for hd in range(ATT_HEADS):
                    val = pj_ref[SEQ * s:SEQ * (s + 1), col0 + V_DIM * hd:col0 + V_DIM * (hd + 1)]
                    rows = pl.ds(hd, SEQ, stride=ATT_HEADS)
                    if layer == 0:
                        ref[s, 0, rows, :] = val
                    else:
                        ref[s, rows, :] = val
            if layer == 0:
                ref[:, 1:] = jnp.zeros((SEQ_PER_TILE, DEPTH - 1, CACHE_ROWS, V_DIM), F32)

    @pl.when(t >= P_TILES)
    def _():
        cos = cos_ref[...]
        sina = sina_ref[...]
        sinb = sinb_ref[...]
        for c in range(2 * ATT_W // 128):
            xg = pj_ref[:, 128 * c:128 * (c + 1)]
            fwd = pltpu.roll(xg, 128 - 16, 1)
            bwd = pltpu.roll(xg, 16, 1)
            qkv_ref[:, 128 * c:128 * (c + 1)] = (xg * cos + fwd * sina + bwd * sinb).astype(BF16)


def _mixer_in(l, x, mod, norm_g, w_in, rope, conv_params, caches):
    cos, sina, sinb = rope
    dw_w, dw_b, ln_g, ln_b = conv_params
    x_specs, x_args = _x_specs(x)

    def tile(i):
        return jnp.minimum(i, N_TILES - 1)

    def rope_idx(i):
        return (jnp.maximum(tile(i) - P_TILES, 0) % S_TILES_PER_BATCH, 0)

    def vec(a):
        return a.reshape(DEPTH, 1, CONV_CH)

    cache_shape = jax.ShapeDtypeStruct((BATCH, DEPTH, CACHE_ROWS, V_DIM), F32)
    if l == 0:
        cache_spec = pl.BlockSpec((SEQ_PER_TILE, DEPTH, CACHE_ROWS, V_DIM),
                                  lambda i: (jnp.minimum(i, P_TILES - 1), 0, 0, 0))
        extra_specs, extra_args, aliases = [], (), {}
    else:
        cache_spec = pl.BlockSpec((SEQ_PER_TILE, None, CACHE_ROWS, V_DIM),
                                  lambda i: (jnp.minimum(i, P_TILES - 1), l, 0, 0))
        extra_specs = [pl.BlockSpec(memory_space=pl.ANY)] * 2
        extra_args = tuple(caches)
        aliases = {12: 2, 13: 3}

    return pl.pallas_call(
        functools.partial(_mixer_in_kernel, layer=l),
        out_shape=(jax.ShapeDtypeStruct((N_TOK, 3 * ATT_W), BF16),
                   jax.ShapeDtypeStruct((N_TOK, CONV_CH), BF16), cache_shape, cache_shape),
        grid=(N_TILES + CONV_LAG,),
        in_specs=x_specs + [
            pl.BlockSpec((None, None, 6, D_MODEL), lambda i: (l, _cond_id(tile(i)), 0, 0)),
            pl.BlockSpec((None, 4, D_MODEL), lambda i: (l, 0, 0)),
            pl.BlockSpec((None, D_MODEL, IN_W), lambda i: (l, 0, 0), pipeline_mode=pl.Buffered(1)),
            pl.BlockSpec((TOK_TILE, 128), rope_idx),
            pl.BlockSpec((TOK_TILE, 128), rope_idx),
            pl.BlockSpec((TOK_TILE, 128), rope_idx),
            pl.BlockSpec((None, CONV_WIDTH, CONV_CH), lambda i: (l, 0, 0)),
            pl.BlockSpec((None, 1, CONV_CH), lambda i: (l, 0, 0)),
            pl.BlockSpec((None, 1, CONV_CH), lambda i: (l, 0, 0)),
            pl.BlockSpec((None, 1, CONV_CH), lambda i: (l, 0, 0))] + extra_specs,
        out_specs=(pl.BlockSpec((TOK_TILE, 3 * ATT_W), lambda i: (tile(i), 0)),
                   pl.BlockSpec((TOK_TILE, CONV_CH), lambda i: (jnp.maximum(i - CONV_LAG, 0), 0)),
                   cache_spec, cache_spec),
        scratch_shapes=[pltpu.VMEM((D_MODEL, IN_W), BF16),
                        pltpu.VMEM((CONV_RING, TOK_TILE, CONV_CH), F32),
                        pltpu.VMEM((TOK_TILE, CONV_CH), F32),
                        pltpu.VMEM((TOK_TILE, 3 * ATT_W), F32),
                        pltpu.VMEM((8, CONV_TILE + 2 * CONV_HALO, CONV_CH), F32)],
        input_output_aliases=aliases,
        compiler_params=_cparams(1),
        name=f"mixer_in_{l}",
    )(*x_args, mod, norm_g, w_in, cos, sina, sinb, dw_w, vec(dw_b), vec(ln_g), vec(ln_b), *extra_args)


def _rope_tables():
    rows = DEC_SEQ // GRID_W
    row_pos = np.repeat(np.arange(rows, dtype=np.float64), GRID_W)
    col_pos = np.tile(np.arange(GRID_W, dtype=np.float64), rows)
    half = QK_DIM // 2
    inv_freq = 1.0 / (ROPE_THETA ** (np.arange(0, half, 2, dtype=np.float64) / half))
    ang_r = row_pos[:, None] * inv_freq
    ang_c = col_pos[:, None] * inv_freq
    ang = np.concatenate([ang_r, ang_r, ang_c, ang_c], axis=-1)
    cos = np.tile(np.cos(ang), (1, 2)).astype(np.float32)
    sin = np.tile(np.sin(ang), (1, 2)).astype(np.float32)
    first = (np.arange(128) % 32) < 16
    sina = np.where(first[None, :], -sin, 0.0).astype(np.float32)
    sinb = np.where(first[None, :], 0.0, sin).astype(np.float32)
    return jnp.asarray(cos), jnp.asarray(sina), jnp.asarray(sinb)


def _attn_kernel(*refs, lam_init, has_ext):
    if has_ext:
        lamp_ref, sub_ref, q_ref, k_ref, v_ref, ke_ref, ve_ref, o_ref = refs
    else:
        lamp_ref, sub_ref, q_ref, k_ref, v_ref, o_ref = refs
    lp = lamp_ref[...]
    lam = (jnp.exp(jnp.sum(lp[0:1] * lp[1:2], axis=-1, keepdims=True))
           - jnp.exp(jnp.sum(lp[2:3] * lp[3:4], axis=-1, keepdims=True)) + lam_init)
    lane = lax.broadcasted_iota(jnp.int32, (1, V_DIM), 1)
    nt = (((1,), (1,)), ((), ()))
    scale = QK_DIM ** -0.5
    map_scale = [jnp.where(lane < QK_DIM, scale, 0.0).astype(BF16),
                 jnp.where(lane < QK_DIM, 0.0, scale).astype(BF16)]

    tq = q_ref.shape[0]
    head_cols = [slice(V_DIM * hd, V_DIM * (hd + 1)) for hd in range(ATT_HEADS)]

    def q_map(hd, m):
        return q_ref[:, head_cols[hd]] * map_scale[m]

    def with_ones(v):
        return jnp.concatenate([v, jnp.ones_like(v)], axis=1)

    outs = []
    if has_ext:
        for hd in range(ATT_HEADS):
            kh = k_ref[:, head_cols[hd]]
            v_aug = with_ones(v_ref[:, head_cols[hd]])
            head_rows = pl.ds(hd, PAST_LEN, stride=ATT_HEADS)
            keh = ke_ref[head_rows, :].astype(BF16)
            ve_aug = with_ones(ve_ref[head_rows, :].astype(BF16))
            ratio = []
            for m in range(2):
                qm = q_map(hd, m)
                s = lax.dot_general(qm, kh, nt, preferred_element_type=F32)
                se = lax.dot_general(qm, keh, nt, preferred_element_type=F32)
                mx = jnp.maximum(jnp.max(s, axis=-1, keepdims=True), jnp.max(se, axis=-1, keepdims=True))
                pv = (jnp.dot(jnp.exp((s - mx).astype(BF16)), v_aug, preferred_element_type=F32)
                      + jnp.dot(jnp.exp((se - mx).astype(BF16)), ve_aug, preferred_element_type=F32))
                ratio.append(pv[:, :V_DIM] / pv[:, V_DIM:V_DIM + 1])
            outs.append(ratio[0] - lam * ratio[1])
    else:
        pairs = [(hd, m) for hd in range(ATT_HEADS) for m in range(2)]
        s = jnp.concatenate([lax.dot_general(q_map(hd, m), k_ref[:, head_cols[hd]], nt,
                                             preferred_element_type=F32) for hd, m in pairs], axis=0)
        p = jnp.exp((s - jnp.max(s, axis=-1, keepdims=True)).astype(BF16))
        for hd in range(ATT_HEADS):
            pv = jnp.dot(p[2 * tq * hd:2 * tq * (hd + 1)], with_ones(v_ref[:, head_cols[hd]]),
                         preferred_element_type=F32)
            ratio = pv[:, :V_DIM] / pv[:, V_DIM:V_DIM + 1]
            outs.append(ratio[:tq] - lam * ratio[tq:])

    heads = [(_rms(o, sub_ref[...]) * (1.0 - lam_init)).astype(BF16) for o in outs]
    o_ref[...] = jnp.concatenate(heads, axis=1)


def _attention(l, lam_init, lam_params, subln_g, qkvg, cache_k, cache_v):
    small = [pl.BlockSpec((None, 4, QK_DIM), lambda *_: (l, 0, 0)),
             pl.BlockSpec((None, 1, V_DIM), lambda *_: (l, 0, 0))]
    sub3 = subln_g.reshape(DEPTH, 1, V_DIM)

    o_ctx = pl.pallas_call(
        functools.partial(_attn_kernel, lam_init=lam_init, has_ext=False),
        out_shape=jax.ShapeDtypeStruct((N_PROMPT, ATT_W), BF16),
        grid=(BATCH,),
        in_specs=small + [pl.BlockSpec((SEQ, ATT_W), lambda b: (b, 0)),
                          pl.BlockSpec((SEQ, ATT_W), lambda b: (b, 1)),
                          pl.BlockSpec((SEQ, ATT_W), lambda b: (b, 2))],
        out_specs=pl.BlockSpec((SEQ, ATT_W), lambda b: (b, 0)),
        compiler_params=_cparams(1),
        name=f"attn_ctx_{l}",
    )(lam_params, sub3, qkvg, qkvg, qkvg)

    tq = 256
    q_tiles = DEC_SEQ // tq
    q_base = N_PROMPT // tq
    kv_base = N_PROMPT // DEC_SEQ

    def q_idx(b, i):
        return (q_base + b * q_tiles + i, 0)

    o_lat = pl.pallas_call(
        functools.partial(_attn_kernel, lam_init=lam_init, has_ext=True),
        out_shape=jax.ShapeDtypeStruct((N_SAMPLE, ATT_W), BF16),
        grid=(DEC_BATCH, q_tiles),
        in_specs=small + [pl.BlockSpec((tq, ATT_W), q_idx),
                          pl.BlockSpec((DEC_SEQ, ATT_W), lambda b, i: (kv_base + b, 1)),
                          pl.BlockSpec((DEC_SEQ, ATT_W), lambda b, i: (kv_base + b, 2)),
                          pl.BlockSpec((None, None, PAST_LEN * ATT_HEADS, V_DIM), lambda b, i: (b, l, 0, 0)),
                          pl.BlockSpec((None, None, PAST_LEN * ATT_HEADS, V_DIM), lambda b, i: (b, l, 0, 0))],
        out_specs=pl.BlockSpec((tq, ATT_W), lambda b, i: (b * q_tiles + i, 0)),
        compiler_params=_cparams(2),
        name=f"attn_lat_{l}",
    )(lam_params, sub3, qkvg, qkvg, qkvg, cache_k, cache_v)
    return o_ctx, o_lat


def _outproj_kernel(*refs, ffn):
    if ffn == "routed":
        (oc_ref, ol_ref, cv_ref, xa_ref, xb_ref, mod_ref, g_ref, w_ref, rw_ref,
         xo_ref, h2_ref, info_ref, info_t_ref, cb_ref, tot_ref, wbf_ref, carry_ref) = refs
    else:
        (oc_ref, ol_ref, cv_ref, xa_ref, xb_ref, mod_ref, g_ref, w_ref, w13_ref, w2_ref,
         xo_ref, wbf_ref, h2_ref) = refs
    i = pl.program_id(0)

    @pl.when(i == 0)
    def _():
        wbf_ref[...] = w_ref[...].astype(BF16)

    half = TOK_TILE // 2
    for rows in (slice(0, half), slice(half, TOK_TILE)):
        o = jnp.where(i < P_TILES, oc_ref[rows, :], ol_ref[rows, :])
        m = (jnp.dot(o, wbf_ref[0:ATT_W, :], preferred_element_type=F32)
             + jnp.dot(cv_ref[rows, :], wbf_ref[ATT_W:, :], preferred_element_type=F32))
        x = jnp.where(i < P_TILES, xa_ref[rows, :], xb_ref[rows, :])
        xn = x + mod_ref[2:3, :] * _rms(m, g_ref[1:2, :])
        xo_ref[rows, :] = xn
        h2_ref[rows, :] = (_rms(xn, g_ref[2:3, :]) * (1.0 + mod_ref[4:5, :]) + mod_ref[3:4, :]).astype(BF16)

    if ffn == "routed":
        @pl.when(i == 0)
        def _():
            carry_ref[...] = jnp.zeros_like(carry_ref)

        _route_tile(h2_ref[...], rw_ref, info_ref, info_t_ref, cb_ref, tot_ref, carry_ref)
    else:
        h = h2_ref[...]
        gt = jnp.dot(h, w13_ref[:, :D_FF], preferred_element_type=F32)
        up = jnp.dot(h, w13_ref[:, D_FF:], preferred_element_type=F32)
        a = (gt * _sigmoid(gt) * up).astype(BF16)
        y = jnp.dot(a, w2_ref[...], preferred_element_type=F32)
        xo_ref[...] = xo_ref[...] + mod_ref[5:6, :] * _rms(y, g_ref[3:4, :])


def _outproj(l, o_ctx, o_lat, cv, x, mod, norm_g, w_out, router_w=None, dense_w=None):
    x_specs, x_args = _x_specs(x)
    out_shape = [jax.ShapeDtypeStruct((N_TOK, D_MODEL), F32)]
    out_specs = [pl.BlockSpec((TOK_TILE, D_MODEL), lambda i: (i, 0))]
    scratch = [pltpu.VMEM((D_MODEL, D_MODEL), BF16)]
    if dense_w is not None:
        extra_specs = [pl.BlockSpec((D_MODEL, 2 * D_FF), lambda i: (0, 0), pipeline_mode=pl.Buffered(1)),
                       pl.BlockSpec((D_FF, D_MODEL), lambda i: (0, 0), pipeline_mode=pl.Buffered(1))]
        extra_args = tuple(dense_w)
        scratch += [pltpu.VMEM((TOK_TILE, D_MODEL), BF16)]
    else:
        rw_pad = jnp.zeros((D_MODEL, 128), BF16).at[:, :N_EXPERTS].set(router_w.astype(BF16))
        extra_specs, extra_args = [pl.BlockSpec((D_MODEL, 128), lambda i: (0, 0))], (rw_pad,)
        out_shape += [jax.ShapeDtypeStruct((N_TOK, D_MODEL), BF16)]
        out_specs += [pl.BlockSpec((TOK_TILE, D_MODEL), lambda i: (i, 0))]
        out_shape += [jax.ShapeDtypeStruct((N_TOK, 128), F32),
                      jax.ShapeDtypeStruct((N_TILES, 8, TOK_TILE), F32),
                      jax.ShapeDtypeStruct((N_TILES, 8, 128), F32),
                      jax.ShapeDtypeStruct((8, 128), F32)]
        out_specs += [pl.BlockSpec((TOK_TILE, 128), lambda i: (i, 0)),
                      pl.BlockSpec((None, 8, TOK_TILE), lambda i: (i, 0, 0)),
                      pl.BlockSpec((None, 8, 128), lambda i: (i, 0, 0)),
                      pl.BlockSpec((8, 128), lambda i: (0, 0))]
        scratch += [pltpu.VMEM((8, 128), F32)]
    return pl.pallas_call(
        functools.partial(_outproj_kernel, ffn="dense" if dense_w is not None else "routed"),
        out_shape=tuple(out_shape),
        grid=(N_TILES,),
        in_specs=[pl.BlockSpec((TOK_TILE, ATT_W), lambda i: (jnp.minimum(i, P_TILES - 1), 0)),
                  pl.BlockSpec((TOK_TILE, ATT_W), lambda i: (jnp.maximum(i - P_TILES, 0), 0)),
                  pl.BlockSpec((TOK_TILE, CONV_CH), lambda i: (i, 0))] + x_specs + [
                  pl.BlockSpec((None, None, 6, D_MODEL), lambda i: (l, _cond_id(i), 0, 0)),
                  pl.BlockSpec((None, 4, D_MODEL), lambda i: (l, 0, 0)),
                  pl.BlockSpec((None, D_MODEL, D_MODEL), lambda i: (l, 0, 0),
                               pipeline_mode=pl.Buffered(1))] + extra_specs,
        out_specs=tuple(out_specs),
        scratch_shapes=scratch,
        compiler_params=_cparams(1),
        name=f"outproj_{l}",
    )(o_ctx, o_lat, cv, *x_args, mod, norm_g, w_out, *extra_args)


def _route_tile(h, rw_ref, info_ref, info_t_ref, cb_ref, tot_ref, carry_ref):
    lane = lax.broadcasted_iota(jnp.int32, (TOK_TILE, 128), 1)
    lanef = lane.astype(F32)
    logits = jnp.dot(h, rw_ref[...], preferred_element_type=F32)
    logits = jnp.where(lane < N_EXPERTS, logits, -jnp.inf)
    big = jnp.asarray(128.0, F32)
    m1 = jnp.max(logits, axis=-1, keepdims=True)
    e1 = jnp.min(jnp.where(logits == m1, lanef, big), axis=-1, keepdims=True)
    oh1 = lanef == e1
    rest = jnp.where(oh1, -jnp.inf, logits)
    m2 = jnp.max(rest, axis=-1, keepdims=True)
    e2 = jnp.min(jnp.where(rest == m2, lanef, big), axis=-1, keepdims=True)
    oh2 = lanef == e2
    ex = jnp.exp(m2 - m1)
    g1 = 1.0 / (1.0 + ex)
    g2 = ex / (1.0 + ex)

    oh = jnp.where(jnp.logical_or(oh1, oh2), 1.0, 0.0)
    r = lax.broadcasted_iota(jnp.int32, (TOK_TILE, TOK_TILE), 0)
    c = lax.broadcasted_iota(jnp.int32, (TOK_TILE, TOK_TILE), 1)
    tri = jnp.where(c <= r, 1.0, 0.0).astype(BF16)
    incl = jnp.dot(tri, oh.astype(BF16), preferred_element_type=F32)
    carry = carry_ref[0:1, :]
    excl = incl - oh + carry
    rank1 = jnp.sum(jnp.where(oh1, excl, 0.0), axis=-1, keepdims=True)
    rank2 = jnp.sum(jnp.where(oh2, excl, 0.0), axis=-1, keepdims=True)

    info = jnp.where(lane == 0, e1, 0.0)
    for k, col in enumerate((e2, rank1, rank2, g1, g2), start=1):
        info = jnp.where(lane == k, col, info)
    info_ref[...] = info
    info_t_ref[...] = info.T[0:8, :]

    cb_ref[...] = carry_ref[...]
    new_carry = carry + incl[TOK_TILE - 1:TOK_TILE, :]
    carry_ref[...] = jnp.broadcast_to(new_carry, carry_ref.shape)
    tot_ref[...] = jnp.broadcast_to(new_carry, tot_ref.shape)


def _sorted_pos(expert, rank, base_ref):
    start = jnp.zeros_like(rank)
    for e in range(N_EXPERTS):
        start = jnp.where(expert == float(e), base_ref[e].astype(F32), start)
    return start + rank


def _dispatch_kernel(clo_ref, cn_ref, base_ref, info_t_ref, x_ref, o_ref):
    r = pl.program_id(0)
    o_ref[...] = jnp.zeros_like(o_ref)
    rows = (r * DISPATCH_TILE + lax.broadcasted_iota(jnp.int32, (DISPATCH_TILE, TOK_TILE), 0)).astype(F32)

    def body(k, carry):
        c = clo_ref[r] + k
        it = info_t_ref[c]
        pos1 = _sorted_pos(it[0:1, :], it[2:3, :], base_ref)
        pos2 = _sorted_pos(it[1:2, :], it[3:4, :], base_ref)
        hit = jnp.logical_or(rows == pos1, rows == pos2)
        sel = jnp.where(hit, 1.0, 0.0).astype(BF16)
        off = pl.multiple_of(c * TOK_TILE, TOK_TILE)
        o_ref[...] += jnp.dot(sel, x_ref[pl.ds(off, TOK_TILE), :], preferred_element_type=F32).astype(BF16)
        return carry

    lax.fori_loop(0, cn_ref[r], body, 0)


def _dispatch(c_lo, c_n, base, info_t, h2):
    return pl.pallas_call(
        _dispatch_kernel,
        out_shape=jax.ShapeDtypeStruct((SORT_ROWS, D_MODEL), BF16),
        grid_spec=pltpu.PrefetchScalarGridSpec(
            num_scalar_prefetch=3,
            grid=(SORT_ROWS // DISPATCH_TILE,),
            in_specs=[pl.BlockSpec((N_TILES, 8, TOK_TILE), lambda r, *_: (0, 0, 0)),
                      pl.BlockSpec((N_TOK, D_MODEL), lambda r, *_: (0, 0),
                                   pipeline_mode=pl.Buffered(1))],
            out_specs=pl.BlockSpec((DISPATCH_TILE, D_MODEL), lambda r, *_: (r, 0))),
        compiler_params=_cparams(1),
        name="moe_dispatch",
    )(c_lo, c_n, base, info_t, h2)


TILE_UNUSED, TILE_HALF, TILE_FULL = 0, 1, 2


def _expert_weight_ring(c, r, n_c, te_ref, tf_ref, nx_ref, copies, cast):
    @pl.when(jnp.logical_and(c == 0, r == 0))
    def _():
        for cp in copies(te_ref[0], 0):
            cp.start()

    @pl.when(tf_ref[r] == 1)
    def _():
        for cp in copies(te_ref[r], c):
            cp.wait()
        cast()
        nr = nx_ref[r]
        nc = c + (nr <= r).astype(jnp.int32)

        @pl.when(nc < n_c)
        def _():
            for cp in copies(te_ref[nr], nc):
                cp.start()


def _moe_up_kernel(te_ref, tv_ref, tf_ref, nx_ref, x_ref, w_hbm, h_ref, wst_ref, wbf_ref, sem_ref, *, i_moe):
    f = pl.program_id(0)
    r = pl.program_id(1)

    def copies(e, ft):
        out = []
        for part in range(2):
            col = pl.multiple_of((part * UP_TILES + ft) * UP_TILE, 128)
            out.append(pltpu.make_async_copy(w_hbm.at[i_moe, e, :, pl.ds(col, UP_TILE)],
                                             wst_ref.at[part], sem_ref.at[part]))
        return out

    def cast():
        wbf_ref[...] = wst_ref[...].astype(BF16)

    _expert_weight_ring(f, r, UP_TILES, te_ref, tf_ref, nx_ref, copies, cast)

    def hidden(x):
        gt = jnp.dot(x, wbf_ref[0], preferred_element_type=F32)
        up = jnp.dot(x, wbf_ref[1], preferred_element_type=F32)
        return (gt * _sigmoid(gt) * up).astype(BF16)

    half = ROW_TILE // 2

    @pl.when(tv_ref[r] == TILE_FULL)
    def _():
        h_ref[...] = hidden(x_ref[...])

    @pl.when(tv_ref[r] == TILE_HALF)
    def _():
        h_ref[0:half, :] = hidden(x_ref[0:half, :])
        h_ref[half:, :] = jnp.zeros((half, UP_TILE), BF16)

    @pl.when(tv_ref[r] == TILE_UNUSED)
    def _():
        h_ref[...] = jnp.zeros_like(h_ref)


def _moe_up(i_moe, tile_e, tile_valid, tile_first, tile_next, xs, w13):
    return pl.pallas_call(
        functools.partial(_moe_up_kernel, i_moe=i_moe),
        out_shape=jax.ShapeDtypeStruct((SORT_ROWS, D_FF_EXPERT), BF16),
        grid_spec=pltpu.PrefetchScalarGridSpec(
            num_scalar_prefetch=4,
            grid=(UP_TILES, SORT_TILES),
            in_specs=[pl.BlockSpec((ROW_TILE, D_MODEL), lambda f, r, *_: (r, 0)),
                      pl.BlockSpec(memory_space=pl.ANY)],
            out_specs=pl.BlockSpec((ROW_TILE, UP_TILE), lambda f, r, *_: (r, f)),
            scratch_shapes=[pltpu.VMEM((2, D_MODEL, UP_TILE), F32),
                            pltpu.VMEM((2, D_MODEL, UP_TILE), BF16),
                            pltpu.SemaphoreType.DMA((2,))]),
        compiler_params=_cparams(2),
        name="moe_up",
    )(tile_e, tile_valid, tile_first, tile_next, xs, w13)


def _moe_down_kernel(te_ref, tv_ref, tf_ref, nx_ref, h_ref, w_hbm, y_ref, wst_ref, wbf_ref, sem_ref, *, i_moe):
    n = pl.program_id(0)
    r = pl.program_id(1)

    def copies(e, nt):
        col = pl.multiple_of(nt * DOWN_TILE, 128)
        return [pltpu.make_async_copy(w_hbm.at[i_moe, e, :, pl.ds(col, DOWN_TILE)], wst_ref, sem_ref.at[0])]

    def cast():
        wbf_ref[...] = wst_ref[...].astype(BF16)

    _expert_weight_ring(n, r, D_MODEL // DOWN_TILE, te_ref, tf_ref, nx_ref, copies, cast)

    half = ROW_TILE // 2

    @pl.when(tv_ref[r] == TILE_FULL)
    def _():
        y_ref[...] = jnp.dot(h_ref[...], wbf_ref[...], preferred_element_type=F32).astype(BF16)

    @pl.when(tv_ref[r] == TILE_HALF)
    def _():
        y_ref[0:half, :] = jnp.dot(h_ref[0:half, :], wbf_ref[...], preferred_element_type=F32).astype(BF16)
        y_ref[half:, :] = jnp.zeros((half, DOWN_TILE), BF16)

    @pl.when(tv_ref[r] == TILE_UNUSED)
    def _():
        y_ref[...] = jnp.zeros_like(y_ref)


def _moe_down(i_moe, tile_e, tile_valid, tile_first, tile_next, hs, w2):
    return pl.pallas_call(
        functools.partial(_moe_down_kernel, i_moe=i_moe),
        out_shape=jax.ShapeDtypeStruct((SORT_ROWS, D_MODEL), BF16),
        grid_spec=pltpu.PrefetchScalarGridSpec(
            num_scalar_prefetch=4,
            grid=(D_MODEL // DOWN_TILE, SORT_TILES),
            in_specs=[pl.BlockSpec((ROW_TILE, D_FF_EXPERT), lambda n, r, *_: (r, 0)),
                      pl.BlockSpec(memory_space=pl.ANY)],
            out_specs=pl.BlockSpec((ROW_TILE, DOWN_TILE), lambda n, r, *_: (r, n)),
            scratch_shapes=[pltpu.VMEM((D_FF_EXPERT, DOWN_TILE), F32),
                            pltpu.VMEM((D_FF_EXPERT, DOWN_TILE), BF16),
                            pltpu.SemaphoreType.DMA((1,))]),
        compiler_params=_cparams(2),
        name="moe_down",
    )(tile_e, tile_valid, tile_first, tile_next, hs, w2)


def _window_copy(y_hbm, ybuf_ref, sem_ref, src, buf, slot):
    return pltpu.make_async_copy(y_hbm.at[pl.ds(src, WIN), :],
                                 ybuf_ref.at[buf, pl.ds(slot * WIN, WIN), :], sem_ref.at[buf, slot])


def _start_windows(src_ref, y_hbm, ybuf_ref, sem_ref, tile, buf):
    for s in range(N_WIN):
        src = pl.multiple_of(src_ref[tile * N_WIN + s], 16)
        _window_copy(y_hbm, ybuf_ref, sem_ref, src, buf, s).start()


def _combine_kernel(src_ref, lo_ref, hi_ref, base_ref, info_ref, x_ref, mod_ref, g_ref, y_hbm,
                    yp_ref, ys_ref, ybuf_ref, sem_ref):
    j = pl.program_id(0)
    buf = j % 2

    @pl.when(j == 0)
    def _():
        _start_windows(src_ref, y_hbm, ybuf_ref, sem_ref, 0, 0)

    @pl.when(j + 1 < N_TILES)
    def _():
        _start_windows(src_ref, y_hbm, ybuf_ref, sem_ref, j + 1, 1 - buf)

    info = info_ref[...]
    pos1 = _sorted_pos(info[:, 0:1], info[:, 2:3], base_ref)
    pos2 = _sorted_pos(info[:, 1:2], info[:, 3:4], base_ref)
    g1, g2 = info[:, 4:5], info[:, 5:6]
    iota = lax.broadcasted_iota(jnp.int32, (1, WIN), 1)
    ids = []
    for s in range(N_WIN):
        row = src_ref[j * N_WIN + s] + iota
        ok = jnp.logical_and(row >= lo_ref[j * N_WIN + s], row < hi_ref[j * N_WIN + s])
        ids.append(jnp.where(ok, row, -1))
    row_id = jnp.concatenate(ids, axis=1).astype(F32)
    sel = (jnp.where(pos1 == row_id, g1, 0.0) + jnp.where(pos2 == row_id, g2, 0.0)).astype(BF16)

    for s in range(N_WIN):
        _window_copy(y_hbm, ybuf_ref, sem_ref, 0, buf, s).wait()
    ffn = jnp.dot(sel, ybuf_ref[buf], preferred_element_type=F32)
    out = x_ref[...] + mod_ref[5:6, :] * _rms(ffn, g_ref[3:4, :])

    @pl.when(j < P_TILES)
    def _():
        yp_ref[...] = out

    @pl.when(j >= P_TILES)
    def _():
        ys_ref[...] = out


def _combine(l, win_src, win_lo, win_hi, base, info, x, mod, norm_g, ys):
    return pl.pallas_call(
        _combine_kernel,
        out_shape=(jax.ShapeDtypeStruct((N_PROMPT, D_MODEL), F32),
                   jax.ShapeDtypeStruct((N_SAMPLE, D_MODEL), F32)),
        grid_spec=pltpu.PrefetchScalarGridSpec(
            num_scalar_prefetch=4,
            grid=(N_TILES,),
            in_specs=[pl.BlockSpec((TOK_TILE, 128), lambda i, *_: (i, 0)),
                      pl.BlockSpec((TOK_TILE, D_MODEL), lambda i, *_: (i, 0)),
                      pl.BlockSpec((None, None, 6, D_MODEL), lambda i, *_: (l, _cond_id(i), 0, 0)),
                      pl.BlockSpec((None, 4, D_MODEL), lambda i, *_: (l, 0, 0)),
                      pl.BlockSpec(memory_space=pl.ANY)],
            out_specs=(pl.BlockSpec((TOK_TILE, D_MODEL), lambda i, *_: (jnp.minimum(i, P_TILES - 1), 0)),
                       pl.BlockSpec((TOK_TILE, D_MODEL), lambda i, *_: (jnp.maximum(i - P_TILES, 0), 0))),
            scratch_shapes=[pltpu.VMEM((2, N_WIN * WIN, D_MODEL), BF16),
                            pltpu.SemaphoreType.DMA((2, N_WIN))]),
        compiler_params=_cparams(1),
        name="moe_combine",
    )(win_src, win_lo, win_hi, base, info, x, mod, norm_g, ys)


def _moe_ffn(l, i_moe, h2, x, mod, norm_g, routing, moe_w13, moe_w2):
    info, info_t, cb, tot = routing

    ar8 = jnp.arange(N_EXPERTS, dtype=jnp.int32)

    def at8(vec, idx):
        return jnp.sum(jnp.where(idx[..., None] == ar8, vec, 0), axis=-1)

    def cumsum8(a):
        return jnp.sum(jnp.where(ar8[None, :] <= ar8[:, None], a[..., None, :], 0), axis=-1)

    counts = tot[0, :N_EXPERTS].astype(jnp.int32)
    padded = (counts + ROW_TILE - 1) // ROW_TILE * ROW_TILE
    seg_end = cumsum8(padded)
    base = (seg_end - padded).astype(jnp.int32)
    cbx = cb[:, 0, :N_EXPERTS].astype(jnp.int32)
    cb_end = jnp.concatenate([cbx[1:], counts[None, :]], axis=0)

    def tile_segments(n_tiles, rows):
        row0 = jnp.arange(n_tiles, dtype=jnp.int32) * rows
        e = jnp.minimum(jnp.sum(seg_end[None, :] <= row0[:, None], axis=1), N_EXPERTS - 1).astype(jnp.int32)
        k0 = row0 - at8(base, e)
        return e, k0, jnp.logical_and(k0 >= 0, k0 < at8(counts, e))

    tile_e, k0, tile_valid = tile_segments(SORT_TILES, ROW_TILE)
    tile_first = jnp.logical_and(tile_valid, k0 == 0)
    d_e, d_k0, d_valid = tile_segments(SORT_ROWS // DISPATCH_TILE, DISPATCH_TILE)
    d_kend = jnp.minimum(d_k0 + DISPATCH_TILE, at8(counts, d_e))
    cb_t = jnp.sum(jnp.where(d_e[:, None, None] == ar8[None, :, None], cbx.T[None, :, :], 0), axis=1)
    c_lo = jnp.sum(cb_t <= d_k0[:, None], axis=1) - 1
    c_hi = jnp.sum(cb_t < d_kend[:, None], axis=1) - 1
    c_n = jnp.where(d_valid, c_hi - c_lo + 1, 0).astype(jnp.int32)
    c_lo = jnp.where(d_valid, c_lo, 0).astype(jnp.int32)
    tile_idx = jnp.arange(SORT_TILES, dtype=jnp.int32)
    first_idx = jnp.where(tile_first, tile_idx, SORT_TILES)
    later = jnp.where(tile_idx[None, :] > tile_idx[:, None], first_idx[None, :], SORT_TILES)
    tile_next = jnp.min(later, axis=1)
    tile_next = jnp.where(tile_next >= SORT_TILES, 0, tile_next).astype(jnp.int32)
    tile_rows = at8(counts, tile_e) - k0
    tile_valid = jnp.where(tile_valid, jnp.where(tile_rows <= ROW_TILE // 2, TILE_HALF, TILE_FULL),
                           TILE_UNUSED).astype(jnp.int32)
    tile_first = tile_first.astype(jnp.int32)

    seg_lo = base[None, :] + cbx
    seg_hi = base[None, :] + cb_end
    seg_n = seg_hi - seg_lo
    w0 = seg_lo // 16 * 16
    nw = jnp.where(seg_n > 0, (seg_lo - w0 + seg_n + WIN - 1) // WIN, 0)
    nw_end = cumsum8(nw)
    slot = jnp.arange(N_WIN, dtype=jnp.int32)
    slot_e = jnp.minimum(jnp.sum(nw_end[:, None, :] <= slot[None, :, None], axis=2), N_EXPERTS - 1)
    pick = slot_e[:, :, None] == jnp.arange(N_EXPERTS, dtype=jnp.int32)[None, None, :]
    take = lambda a: jnp.sum(jnp.where(pick, a[:, None, :], 0), axis=2)
    slot_k = slot[None, :] - (take(nw_end) - take(nw))
    slot_ok = slot[None, :] < nw_end[:, -1:]
    slot_src = take(w0) + WIN * slot_k
    win_src = jnp.where(slot_ok, slot_src, 0).astype(jnp.int32).reshape(-1)
    win_lo = jnp.where(slot_ok, take(seg_lo), 0).astype(jnp.int32).reshape(-1)
    win_hi = jnp.where(slot_ok, take(seg_hi), 0).astype(jnp.int32).reshape(-1)

    xs = _dispatch(c_lo, c_n, base, info_t, h2)
    hs = _moe_up(i_moe, tile_e, tile_valid, tile_first, tile_next, xs, moe_w13)
    ys = _moe_down(i_moe, tile_e, tile_valid, tile_first, tile_next, hs, moe_w2)
    return _combine(l, win_src, win_lo, win_hi, base, info, x, mod, norm_g, ys)


def kernel(x_prompt, x_sample, cache_k, cache_v, c, c_ctx, w_ada, b_ada, norm_g, w_in, w_out,
           lam_params, subln_g, dw_weight, dw_bias, conv_ln_g, conv_ln_b, dense_w13, dense_w2,
           router_w, moe_w13, moe_w2):
    x = (x_prompt.reshape(N_PROMPT, D_MODEL), x_sample.reshape(N_SAMPLE, D_MODEL))
    cond8 = jnp.zeros((8, D_MODEL), F32).at[0].set(c_ctx).at[1:1 + DEC_BATCH].set(c)
    mod = _ada_modulation(cond8, w_ada, b_ada).reshape(DEPTH, 8, 6, D_MODEL)
    rope = _rope_tables()
    ck = cache_k.reshape(DEC_BATCH, DEPTH, PAST_LEN * ATT_HEADS, V_DIM)
    cv = cache_v.reshape(DEC_BATCH, DEPTH, PAST_LEN * ATT_HEADS, V_DIM)

    caches = None
    for l in range(DEPTH):
        lam_init = 0.8 - 0.6 * math.exp(-0.3 * l)
        qkv, cvo, *caches = _mixer_in(l, x, mod, norm_g, w_in, rope,
                                      (dw_weight, dw_bias, conv_ln_g, conv_ln_b), caches)
        o_ctx, o_lat = _attention(l, lam_init, lam_params, subln_g, qkv, ck, cv)
        i = l // 2
        if l % 2 == 0:
            (x,) = _outproj(l, o_ctx, o_lat, cvo, x, mod, norm_g, w_out,
                            dense_w=(dense_w13[i].astype(BF16), dense_w2[i].astype(BF16)))
        else:
            x, h2, *routing = _outproj(l, o_ctx, o_lat, cvo, x, mod, norm_g, w_out, router_w=router_w[i])
            x = _moe_ffn(l, i, h2, x, mod, norm_g, routing, moe_w13, moe_w2)

    xp, xs = x if isinstance(x, tuple) else (x[:N_PROMPT], x[N_PROMPT:])
    new_k, new_v = (a.reshape(BATCH, DEPTH, SEQ, ATT_HEADS, V_DIM) for a in caches)
    return (xp.reshape(BATCH, SEQ, D_MODEL), xs.reshape(DEC_BATCH, DEC_SEQ, D_MODEL), new_k, new_v)
```

```python
import functools
import math

import jax
import jax.numpy as jnp
import numpy as np
from jax import lax
from jax.experimental import pallas as pl
from jax.experimental.pallas import tpu as pltpu

F32 = jnp.float32
BF16 = jnp.bfloat16

D_MODEL = 1024
BATCH = 32
SEQ = 256
DEPTH = 2
DEC_BATCH = 2
DEC_SEQ = 2048
PAST_LEN = 512
GRID_W = 64
ATT_HEADS = 4
QK_DIM = 64
V_DIM = 128
ATT_W = ATT_HEADS * V_DIM
IN_W = 5 * ATT_W
CONV_CH = 512
CONV_WIDTH = 31
D_FF = 2816
N_EXPERTS = 8
D_FF_EXPERT = 3584
ROPE_THETA = 10000.0
NORM_EPS = 1e-6
LN_EPS = 1e-5

N_PROMPT = BATCH * SEQ
N_SAMPLE = DEC_BATCH * DEC_SEQ
N_TOK = N_PROMPT + N_SAMPLE

TOK_TILE = 512
N_TILES = N_TOK // TOK_TILE
P_TILES = N_PROMPT // TOK_TILE
S_TILES_PER_BATCH = DEC_SEQ // TOK_TILE

ROW_TILE = 512
SORT_TILES = (2 * N_TOK + N_EXPERTS * ROW_TILE) // ROW_TILE + 1
SORT_ROWS = SORT_TILES * ROW_TILE
DISPATCH_TILE = 256
UP_TILE = 1792
UP_TILES = D_FF_EXPERT // UP_TILE
DOWN_TILE = 1024
WIN = 128
N_WIN = 16

VMEM_LIMIT = 56 * 1024 * 1024


def _cparams(n_axes):
    return pltpu.CompilerParams(dimension_semantics=("arbitrary",) * n_axes,
                                vmem_limit_bytes=VMEM_LIMIT)


def _cond_id(i):
    return jnp.where(i < P_TILES, 0, 1 + (i - P_TILES) // S_TILES_PER_BATCH)


def _sigmoid(x):
    return 1.0 / (1.0 + jnp.exp(-x))


def _rms(x, g):
    return x * lax.rsqrt(jnp.mean(x * x, axis=-1, keepdims=True) + NORM_EPS) * g


def _ada_kernel(c_ref, w_ref, b_ref, o_ref):
    c = c_ref[...]
    s = (c * _sigmoid(c)).astype(BF16)
    o_ref[...] = jnp.dot(s, w_ref[...].astype(BF16), preferred_element_type=F32) + b_ref[...]


def _ada_modulation(cond8, w_ada, b_ada):
    tn = 1536
    n = 6 * D_MODEL
    return pl.pallas_call(
        _ada_kernel,
        out_shape=jax.ShapeDtypeStruct((DEPTH, 8, n), F32),
        grid=(DEPTH, n // tn),
        in_specs=[pl.BlockSpec((8, D_MODEL), lambda l, j: (0, 0)),
                  pl.BlockSpec((None, D_MODEL, tn), lambda l, j: (l, 0, j)),
                  pl.BlockSpec((None, 1, tn), lambda l, j: (l, 0, j))],
        out_specs=pl.BlockSpec((None, 8, tn), lambda l, j: (l, 0, j)),
        compiler_params=_cparams(2),
        name="ada_modulation",
    )(cond8, w_ada, b_ada.reshape(DEPTH, 1, n))


def _tile_x(xa_ref, xb_ref):
    return jnp.where(pl.program_id(0) < P_TILES, xa_ref[...], xb_ref[...])


def _x_specs(x):
    last = N_TILES - 1
    if isinstance(x, tuple):
        xa, xb = x
        b_idx = lambda i, *_: (jnp.maximum(jnp.minimum(i, last) - P_TILES, 0), 0)
    else:
        xa = xb = x
        b_idx = lambda i, *_: (jnp.maximum(jnp.minimum(i, last), P_TILES), 0)
    a_idx = lambda i, *_: (jnp.minimum(i, P_TILES - 1), 0)
    return [pl.BlockSpec((TOK_TILE, D_MODEL), a_idx), pl.BlockSpec((TOK_TILE, D_MODEL), b_idx)], (xa, xb)


SEQ_PER_TILE = TOK_TILE // SEQ
CACHE_ROWS = SEQ * ATT_HEADS


CONV_LAG = 2
CONV_RING = 4
CONV_TILE = 256
CONV_HALO = 16
CONV_SUB = 32


def _conv_pass(upad_ref, w_ref, bias_ref, lg_ref, lb_ref, o_ref, row0, between):
    rows = CONV_TILE + 2 * CONV_HALO - 8
    for s in range(1, 8):
        upad_ref[s, 0:rows, :] = upad_ref[0, s:s + rows, :]
    first_tap = CONV_HALO - CONV_WIDTH // 2
    groups = CONV_SUB // 8
    for t in range(CONV_TILE // CONV_SUB):
        between()
        base = t * CONV_SUB
        acc = jnp.zeros((groups, 8, CONV_CH), F32)
        for j in range(CONV_WIDTH):
            lo = base + (first_tap + j) // 8 * 8
            taps = upad_ref[(first_tap + j) % 8, lo:lo + CONV_SUB, :].reshape(groups, 8, CONV_CH)
            acc = acc + taps * w_ref[j][None]
        y = acc.reshape(CONV_SUB, CONV_CH) + bias_ref[...]
        mu = jnp.mean(y, axis=-1, keepdims=True)
        yc = y - mu
        var = jnp.mean(yc * yc, axis=-1, keepdims=True)
        z = yc * lax.rsqrt(var + LN_EPS) * lg_ref[...] + lb_ref[...]
        o_ref[row0 + base:row0 + base + CONV_SUB, :] = (z * _sigmoid(z)).astype(BF16)


def _mixer_in_kernel(*refs, layer):
    n_in = 12 + (2 if layer else 0)
    (xa_ref, xb_ref, mod_ref, g_ref, w_ref, cos_ref, sina_ref, sinb_ref,
     dw_ref, db_ref, lg_ref, lb_ref) = refs[:12]
    qkv_ref, cv_ref, kc_ref, vc_ref, wbf_ref, ring_ref, glu_ref, pj_ref, upad_ref, taps_ref = refs[n_in:]
    i = pl.program_id(0)
    t = jnp.minimum(i, N_TILES - 1)

    @pl.when(i == 0)
    def _():
        wbf_ref[...] = w_ref[...].astype(BF16)
        ring_ref[...] = jnp.zeros_like(ring_ref)
        glu_ref[...] = jnp.zeros_like(glu_ref)
        for j in range(CONV_WIDTH):
            taps_ref[j] = jnp.broadcast_to(dw_ref[j:j + 1, :], (8, CONV_CH))

    ring_ref[(i - 1) % CONV_RING] = glu_ref[...]

    h = _rms(_tile_x(xa_ref, xb_ref), g_ref[0:1, :]) * (1.0 + mod_ref[1:2, :]) + mod_ref[0:1, :]
    hb = h.astype(BF16)

    def proj(part):
        return jnp.dot(hb, wbf_ref[:, part * ATT_W:(part + 1) * ATT_W], preferred_element_type=F32)

    def proj_part(part):
        if part < 3:
            pj_ref[:, part * ATT_W:(part + 1) * ATT_W] = proj(part)
        elif part == 3:
            glu_ref[...] = proj(3)
        else:
            glu_ref[...] = glu_ref[...] * _sigmoid(proj(4))

    conv_units = 2 * (CONV_TILE // CONV_SUB)
    issue_at = {k * conv_units // 5: k for k in range(5)}
    unit = [0]

    def between():
        if unit[0] in issue_at:
            proj_part(issue_at[unit[0]])
        unit[0] += 1

    j = i - CONV_LAG
    latent = j >= P_TILES
    q = (j - P_TILES) % S_TILES_PER_BATCH
    has_prev = jnp.logical_and(latent, q != 0)
    has_next = jnp.logical_and(latent, q != S_TILES_PER_BATCH - 1)
    cur = j % CONV_RING
    prev_tail = ring_ref[(j - 1) % CONV_RING, TOK_TILE - CONV_HALO:, :]
    next_head = ring_ref[(j + 1) % CONV_RING, 0:CONV_HALO, :]
    for s in range(TOK_TILE // CONV_TILE):
        lo = s * CONV_TILE
        if s == 0:
            before = jnp.where(has_prev, prev_tail, 0.0)
        else:
            before = jnp.where(latent, ring_ref[cur, lo - CONV_HALO:lo, :], 0.0)
        if s == TOK_TILE // CONV_TILE - 1:
            after = jnp.where(has_next, next_head, 0.0)
        else:
            after = jnp.where(latent, ring_ref[cur, lo + CONV_TILE:lo + CONV_TILE + CONV_HALO, :], 0.0)
        pad = upad_ref
        pad[0, 0:CONV_HALO, :] = before
        pad[0, CONV_HALO:CONV_HALO + CONV_TILE, :] = ring_ref[cur, lo:lo + CONV_TILE, :]
        pad[0, CONV_HALO + CONV_TILE:, :] = after
        _conv_pass(pad, taps_ref, db_ref, lg_ref, lb_ref, cv_ref, lo, between)

    qkv_ref[:, 2 * ATT_W:] = pj_ref[:, 2 * ATT_W:].astype(BF16)

    @pl.when(t < P_TILES)
    def _():
        qkv_ref[:, :2 * ATT_W] = pj_ref[:, :2 * ATT_W].astype(BF16)
        for ref, col0 in ((kc_ref, ATT_W), (vc_ref, 2 * ATT_W)):
            for s in range(SEQ_PER_TILE):
                for hd in range(ATT_HEADS):
                    val = pj_ref[SEQ * s:SEQ * (s + 1), col0 + V_DIM * hd:col0 + V_DIM * (hd + 1)]
                    rows = pl.ds(hd, SEQ, stride=ATT_HEADS)
                    if layer == 0:
                        ref[s, 0, rows, :] = val
                    else:
                        ref[s, rows, :] = val
            if layer == 0:
                ref[:, 1:] = jnp.zeros((SEQ_PER_TILE, DEPTH - 1, CACHE_ROWS, V_DIM), F32)

    @pl.when(t >= P_TILES)
    def _():
        cos = cos_ref[...]
        sina = sina_ref[...]
        sinb = sinb_ref[...]
        for c in range(2 * ATT_W // 128):
            xg = pj_ref[:, 128 * c:128 * (c + 1)]
            fwd = pltpu.roll(xg, 128 - 16, 1)
            bwd = pltpu.roll(xg, 16, 1)
            qkv_ref[:, 128 * c:128 * (c + 1)] = (xg * cos + fwd * sina + bwd * sinb).astype(BF16)


def _mixer_in(l, x, mod, norm_g, w_in, rope, conv_params, caches):
    cos, sina, sinb = rope
    dw_w, dw_b, ln_g, ln_b = conv_params
    x_specs, x_args = _x_specs(x)

    def tile(i):
        return jnp.minimum(i, N_TILES - 1)

    def rope_idx(i):
        return (jnp.maximum(tile(i) - P_TILES, 0) % S_TILES_PER_BATCH, 0)

    def vec(a):
        return a.reshape(DEPTH, 1, CONV_CH)

    cache_shape = jax.ShapeDtypeStruct((BATCH, DEPTH, CACHE_ROWS, V_DIM), F32)
    if l == 0:
        cache_spec = pl.BlockSpec((SEQ_PER_TILE, DEPTH, CACHE_ROWS, V_DIM),
                                  lambda i: (jnp.minimum(i, P_TILES - 1), 0, 0, 0))
        extra_specs, extra_args, aliases = [], (), {}
    else:
        cache_spec = pl.BlockSpec((SEQ_PER_TILE, None, CACHE_ROWS, V_DIM),
                                  lambda i: (jnp.minimum(i, P_TILES - 1), l, 0, 0))
        extra_specs = [pl.BlockSpec(memory_space=pl.ANY)] * 2
        extra_args = tuple(caches)
        aliases = {12: 2, 13: 3}

    return pl.pallas_call(
        functools.partial(_mixer_in_kernel, layer=l),
        out_shape=(jax.ShapeDtypeStruct((N_TOK, 3 * ATT_W), BF16),
                   jax.ShapeDtypeStruct((N_TOK, CONV_CH), BF16), cache_shape, cache_shape),
        grid=(N_TILES + CONV_LAG,),
        in_specs=x_specs + [
            pl.BlockSpec((None, None, 6, D_MODEL), lambda i: (l, _cond_id(tile(i)), 0, 0)),
            pl.BlockSpec((None, 4, D_MODEL), lambda i: (l, 0, 0)),
            pl.BlockSpec((None, D_MODEL, IN_W), lambda i: (l, 0, 0), pipeline_mode=pl.Buffered(1)),
            pl.BlockSpec((TOK_TILE, 128), rope_idx),
            pl.BlockSpec((TOK_TILE, 128), rope_idx),
            pl.BlockSpec((TOK_TILE, 128), rope_idx),
            pl.BlockSpec((None, CONV_WIDTH, CONV_CH), lambda i: (l, 0, 0)),
            pl.BlockSpec((None, 1, CONV_CH), lambda i: (l, 0, 0)),
            pl.BlockSpec((None, 1, CONV_CH), lambda i: (l, 0, 0)),
            pl.BlockSpec((None, 1, CONV_CH), lambda i: (l, 0, 0))] + extra_specs,
        out_specs=(pl.BlockSpec((TOK_TILE, 3 * ATT_W), lambda i: (tile(i), 0)),
                   pl.BlockSpec((TOK_TILE, CONV_CH), lambda i: (jnp.maximum(i - CONV_LAG, 0), 0)),
                   cache_spec, cache_spec),
        scratch_shapes=[pltpu.VMEM((D_MODEL, IN_W), BF16),
                        pltpu.VMEM((CONV_RING, TOK_TILE, CONV_CH), F32),
                        pltpu.VMEM((TOK_TILE, CONV_CH), F32),
                        pltpu.VMEM((TOK_TILE, 3 * ATT_W), F32),
                        pltpu.VMEM((8, CONV_TILE + 2 * CONV_HALO, CONV_CH), F32),
                        pltpu.VMEM((CONV_WIDTH, 8, CONV_CH), F32)],
        input_output_aliases=aliases,
        compiler_params=_cparams(1),
        name=f"mixer_in_{l}",
    )(*x_args, mod, norm_g, w_in, cos, sina, sinb, dw_w, vec(dw_b), vec(ln_g), vec(ln_b), *extra_args)


def _rope_tables():
    rows = DEC_SEQ // GRID_W
    row_pos = np.repeat(np.arange(rows, dtype=np.float64), GRID_W)
    col_pos = np.tile(np.arange(GRID_W, dtype=np.float64), rows)
    half = QK_DIM // 2
    inv_freq = 1.0 / (ROPE_THETA ** (np.arange(0, half, 2, dtype=np.float64) / half))
    ang_r = row_pos[:, None] * inv_freq
    ang_c = col_pos[:, None] * inv_freq
    ang = np.concatenate([ang_r, ang_r, ang_c, ang_c], axis=-1)
    cos = np.tile(np.cos(ang), (1, 2)).astype(np.float32)
    sin = np.tile(np.sin(ang), (1, 2)).astype(np.float32)
    first = (np.arange(128) % 32) < 16
    sina = np.where(first[None, :], -sin, 0.0).astype(np.float32)
    sinb = np.where(first[None, :], 0.0, sin).astype(np.float32)
    return jnp.asarray(cos), jnp.asarray(sina), jnp.asarray(sinb)


def _attn_kernel(*refs, lam_init, has_ext):
    if has_ext:
        lamp_ref, sub_ref, q_ref, k_ref, v_ref, ke_ref, ve_ref, o_ref = refs
    else:
        lamp_ref, sub_ref, q_ref, k_ref, v_ref, o_ref = refs
    lp = lamp_ref[...]
    lam = (jnp.exp(jnp.sum(lp[0:1] * lp[1:2], axis=-1, keepdims=True))
           - jnp.exp(jnp.sum(lp[2:3] * lp[3:4], axis=-1, keepdims=True)) + lam_init)
    lane = lax.broadcasted_iota(jnp.int32, (1, V_DIM), 1)
    nt = (((1,), (1,)), ((), ()))
    scale = QK_DIM ** -0.5
    map_scale = [jnp.where(lane < QK_DIM, scale, 0.0).astype(BF16),
                 jnp.where(lane < QK_DIM, 0.0, scale).astype(BF16)]

    tq = q_ref.shape[0]
    head_cols = [slice(V_DIM * hd, V_DIM * (hd + 1)) for hd in range(ATT_HEADS)]

    def q_map(hd, m):
        return q_ref[:, head_cols[hd]] * map_scale[m]

    def with_ones(v):
        return jnp.concatenate([v, jnp.ones_like(v)], axis=1)

    outs = []
    if has_ext:
        for hd in range(ATT_HEADS):
            kh = k_ref[:, head_cols[hd]]
            v_aug = with_ones(v_ref[:, head_cols[hd]])
            head_rows = pl.ds(hd, PAST_LEN, stride=ATT_HEADS)
            keh = ke_ref[head_rows, :].astype(BF16)
            ve_aug = with_ones(ve_ref[head_rows, :].astype(BF16))
            ratio = []
            for m in range(2):
                qm = q_map(hd, m)
                s = lax.dot_general(qm, kh, nt, preferred_element_type=F32)
                se = lax.dot_general(qm, keh, nt, preferred_element_type=F32)
                mx = jnp.maximum(jnp.max(s, axis=-1, keepdims=True), jnp.max(se, axis=-1, keepdims=True))
                pv = (jnp.dot(jnp.exp((s - mx).astype(BF16)), v_aug, preferred_element_type=F32)
                      + jnp.dot(jnp.exp((se - mx).astype(BF16)), ve_aug, preferred_element_type=F32))
                ratio.append(pv[:, :V_DIM] / pv[:, V_DIM:])
            outs.append(ratio[0] - lam * ratio[1])
    else:
        pairs = [(hd, m) for hd in range(ATT_HEADS) for m in range(2)]
        s = jnp.concatenate([lax.dot_general(q_map(hd, m), k_ref[:, head_cols[hd]], nt,
                                             preferred_element_type=F32) for hd, m in pairs], axis=0)
        p = jnp.exp((s - jnp.max(s, axis=-1, keepdims=True)).astype(BF16))
        for hd in range(ATT_HEADS):
            pv = jnp.dot(p[2 * tq * hd:2 * tq * (hd + 1)], with_ones(v_ref[:, head_cols[hd]]),
                         preferred_element_type=F32)
            ratio = pv[:, :V_DIM] / pv[:, V_DIM:]
            outs.append(ratio[:tq] - lam * ratio[tq:])

    heads = [(_rms(o, sub_ref[...]) * (1.0 - lam_init)).astype(BF16) for o in outs]
    o_ref[...] = jnp.concatenate(heads, axis=1)


def _attention(l, lam_init, lam_params, subln_g, qkvg, cache_k, cache_v):
    small = [pl.BlockSpec((None, 4, QK_DIM), lambda *_: (l, 0, 0)),
             pl.BlockSpec((None, 1, V_DIM), lambda *_: (l, 0, 0))]
    sub3 = subln_g.reshape(DEPTH, 1, V_DIM)

    o_ctx = pl.pallas_call(
        functools.partial(_attn_kernel, lam_init=lam_init, has_ext=False),
        out_shape=jax.ShapeDtypeStruct((N_PROMPT, ATT_W), BF16),
        grid=(BATCH,),
        in_specs=small + [pl.BlockSpec((SEQ, ATT_W), lambda b: (b, 0)),
                          pl.BlockSpec((SEQ, ATT_W), lambda b: (b, 1)),
                          pl.BlockSpec((SEQ, ATT_W), lambda b: (b, 2))],
        out_specs=pl.BlockSpec((SEQ, ATT_W), lambda b: (b, 0)),
        compiler_params=_cparams(1),
        name=f"attn_ctx_{l}",
    )(lam_params, sub3, qkvg, qkvg, qkvg)

    tq = 256
    q_tiles = DEC_SEQ // tq
    q_base = N_PROMPT // tq
    kv_base = N_PROMPT // DEC_SEQ

    def q_idx(b, i):
        return (q_base + b * q_tiles + i, 0)

    o_lat = pl.pallas_call(
        functools.partial(_attn_kernel, lam_init=lam_init, has_ext=True),
        out_shape=jax.ShapeDtypeStruct((N_SAMPLE, ATT_W), BF16),
        grid=(DEC_BATCH, q_tiles),
        in_specs=small + [pl.BlockSpec((tq, ATT_W), q_idx),
                          pl.BlockSpec((DEC_SEQ, ATT_W), lambda b, i: (kv_base + b, 1)),
                          pl.BlockSpec((DEC_SEQ, ATT_W), lambda b, i: (kv_base + b, 2)),
                          pl.BlockSpec((None, None, PAST_LEN * ATT_HEADS, V_DIM), lambda b, i: (b, l, 0, 0)),
                          pl.BlockSpec((None, None, PAST_LEN * ATT_HEADS, V_DIM), lambda b, i: (b, l, 0, 0))],
        out_specs=pl.BlockSpec((tq, ATT_W), lambda b, i: (b * q_tiles + i, 0)),
        compiler_params=_cparams(2),
        name=f"attn_lat_{l}",
    )(lam_params, sub3, qkvg, qkvg, qkvg, cache_k, cache_v)
    return o_ctx, o_lat


def _outproj_kernel(*refs, ffn):
    if ffn == "routed":
        (oc_ref, ol_ref, cv_ref, xa_ref, xb_ref, mod_ref, g_ref, w_ref, rw_ref,
         xo_ref, h2_ref, info_ref, info_t_ref, cb_ref, tot_ref, wbf_ref, carry_ref) = refs
    else:
        (oc_ref, ol_ref, cv_ref, xa_ref, xb_ref, mod_ref, g_ref, w_ref, w13_ref, w2_ref,
         xo_ref, wbf_ref, h2_ref) = refs
    i = pl.program_id(0)

    @pl.when(i == 0)
    def _():
        wbf_ref[...] = w_ref[...].astype(BF16)

    half = TOK_TILE // 2
    for rows in (slice(0, half), slice(half, TOK_TILE)):
        o = jnp.where(i < P_TILES, oc_ref[rows, :], ol_ref[rows, :])
        m = (jnp.dot(o, wbf_ref[0:ATT_W, :], preferred_element_type=F32)
             + jnp.dot(cv_ref[rows, :], wbf_ref[ATT_W:, :], preferred_element_type=F32))
        x = jnp.where(i < P_TILES, xa_ref[rows, :], xb_ref[rows, :])
        xn = x + mod_ref[2:3, :] * _rms(m, g_ref[1:2, :])
        xo_ref[rows, :] = xn
        h2_ref[rows, :] = (_rms(xn, g_ref[2:3, :]) * (1.0 + mod_ref[4:5, :]) + mod_ref[3:4, :]).astype(BF16)

    if ffn == "routed":
        @pl.when(i == 0)
        def _():
            carry_ref[...] = jnp.zeros_like(carry_ref)

        _route_tile(h2_ref[...], rw_ref, info_ref, info_t_ref, cb_ref, tot_ref, carry_ref)
    else:
        h = h2_ref[...]
        gt = jnp.dot(h, w13_ref[:, :D_FF], preferred_element_type=F32)
        up = jnp.dot(h, w13_ref[:, D_FF:], preferred_element_type=F32)
        a = (gt * _sigmoid(gt) * up).astype(BF16)
        y = jnp.dot(a, w2_ref[...], preferred_element_type=F32)
        xo_ref[...] = xo_ref[...] + mod_ref[5:6, :] * _rms(y, g_ref[3:4, :])


def _outproj(l, o_ctx, o_lat, cv, x, mod, norm_g, w_out, router_w=None, dense_w=None):
    x_specs, x_args = _x_specs(x)
    out_shape = [jax.ShapeDtypeStruct((N_TOK, D_MODEL), F32)]
    out_specs = [pl.BlockSpec((TOK_TILE, D_MODEL), lambda i: (i, 0))]
    scratch = [pltpu.VMEM((D_MODEL, D_MODEL), BF16)]
    if dense_w is not None:
        extra_specs = [pl.BlockSpec((D_MODEL, 2 * D_FF), lambda i: (0, 0), pipeline_mode=pl.Buffered(1)),
                       pl.BlockSpec((D_FF, D_MODEL), lambda i: (0, 0), pipeline_mode=pl.Buffered(1))]
        extra_args = tuple(dense_w)
        scratch += [pltpu.VMEM((TOK_TILE, D_MODEL), BF16)]
    else:
        rw_pad = jnp.zeros((D_MODEL, 128), BF16).at[:, :N_EXPERTS].set(router_w.astype(BF16))
        extra_specs, extra_args = [pl.BlockSpec((D_MODEL, 128), lambda i: (0, 0))], (rw_pad,)
        out_shape += [jax.ShapeDtypeStruct((N_TOK, D_MODEL), BF16)]
        out_specs += [pl.BlockSpec((TOK_TILE, D_MODEL), lambda i: (i, 0))]
        out_shape += [jax.ShapeDtypeStruct((N_TOK, 128), F32),
                      jax.ShapeDtypeStruct((N_TILES, 8, TOK_TILE), F32),
                      jax.ShapeDtypeStruct((N_TILES, 8, 128), F32),
                      jax.ShapeDtypeStruct((8, 128), F32)]
        out_specs += [pl.BlockSpec((TOK_TILE, 128), lambda i: (i, 0)),
                      pl.BlockSpec((None, 8, TOK_TILE), lambda i: (i, 0, 0)),
                      pl.BlockSpec((None, 8, 128), lambda i: (i, 0, 0)),
                      pl.BlockSpec((8, 128), lambda i: (0, 0))]
        scratch += [pltpu.VMEM((8, 128), F32)]
    return pl.pallas_call(
        functools.partial(_outproj_kernel, ffn="dense" if dense_w is not None else "routed"),
        out_shape=tuple(out_shape),
        grid=(N_TILES,),
        in_specs=[pl.BlockSpec((TOK_TILE, ATT_W), lambda i: (jnp.minimum(i, P_TILES - 1), 0)),
                  pl.BlockSpec((TOK_TILE, ATT_W), lambda i: (jnp.maximum(i - P_TILES, 0), 0)),
                  pl.BlockSpec((TOK_TILE, CONV_CH), lambda i: (i, 0))] + x_specs + [
                  pl.BlockSpec((None, None, 6, D_MODEL), lambda i: (l, _cond_id(i), 0, 0)),
                  pl.BlockSpec((None, 4, D_MODEL), lambda i: (l, 0, 0)),
                  pl.BlockSpec((None, D_MODEL, D_MODEL), lambda i: (l, 0, 0),
                               pipeline_mode=pl.Buffered(1))] + extra_specs,
        out_specs=tuple(out_specs),
        scratch_shapes=scratch,
        compiler_params=_cparams(1),
        name=f"outproj_{l}",
    )(o_ctx, o_lat, cv, *x_args, mod, norm_g, w_out, *extra_args)


def _route_tile(h, rw_ref, info_ref, info_t_ref, cb_ref, tot_ref, carry_ref):
    lane = lax.broadcasted_iota(jnp.int32, (TOK_TILE, 128), 1)
    lanef = lane.astype(F32)
    logits = jnp.dot(h, rw_ref[...], preferred_element_type=F32)
    logits = jnp.where(lane < N_EXPERTS, logits, -jnp.inf)
    big = jnp.asarray(128.0, F32)
    m1 = jnp.max(logits, axis=-1, keepdims=True)
    e1 = jnp.min(jnp.where(logits == m1, lanef, big), axis=-1, keepdims=True)
    oh1 = lanef == e1
    rest = jnp.where(oh1, -jnp.inf, logits)
    m2 = jnp.max(rest, axis=-1, keepdims=True)
    e2 = jnp.min(jnp.where(rest == m2, lanef, big), axis=-1, keepdims=True)
    oh2 = lanef == e2
    ex = jnp.exp(m2 - m1)
    g1 = 1.0 / (1.0 + ex)
    g2 = ex / (1.0 + ex)

    oh = jnp.where(jnp.logical_or(oh1, oh2), 1.0, 0.0)
    r = lax.broadcasted_iota(jnp.int32, (TOK_TILE, TOK_TILE), 0)
    c = lax.broadcasted_iota(jnp.int32, (TOK_TILE, TOK_TILE), 1)
    tri = jnp.where(c <= r, 1.0, 0.0).astype(BF16)
    incl = jnp.dot(tri, oh.astype(BF16), preferred_element_type=F32)
    carry = carry_ref[0:1, :]
    excl = incl - oh + carry
    rank1 = jnp.sum(jnp.where(oh1, excl, 0.0), axis=-1, keepdims=True)
    rank2 = jnp.sum(jnp.where(oh2, excl, 0.0), axis=-1, keepdims=True)

    info = jnp.where(lane == 0, e1, 0.0)
    for k, col in enumerate((e2, rank1, rank2, g1, g2), start=1):
        info = jnp.where(lane == k, col, info)
    info_ref[...] = info
    info_t_ref[...] = info.T[0:8, :]

    cb_ref[...] = carry_ref[...]
    new_carry = carry + incl[TOK_TILE - 1:TOK_TILE, :]
    carry_ref[...] = jnp.broadcast_to(new_carry, carry_ref.shape)
    tot_ref[...] = jnp.broadcast_to(new_carry, tot_ref.shape)


def _sorted_pos(expert, rank, base_ref):
    start = jnp.zeros_like(rank)
    for e in range(N_EXPERTS):
        start = jnp.where(expert == float(e), base_ref[e].astype(F32), start)
    return start + rank


def _dispatch_kernel(clo_ref, cn_ref, base_ref, info_t_ref, x_ref, o_ref):
    r = pl.program_id(0)
    o_ref[...] = jnp.zeros_like(o_ref)
    rows = (r * DISPATCH_TILE + lax.broadcasted_iota(jnp.int32, (DISPATCH_TILE, TOK_TILE), 0)).astype(F32)

    def body(k, carry):
        c = clo_ref[r] + k
        it = info_t_ref[c]
        pos1 = _sorted_pos(it[0:1, :], it[2:3, :], base_ref)
        pos2 = _sorted_pos(it[1:2, :], it[3:4, :], base_ref)
        hit = jnp.logical_or(rows == pos1, rows == pos2)
        sel = jnp.where(hit, 1.0, 0.0).astype(BF16)
        off = pl.multiple_of(c * TOK_TILE, TOK_TILE)
        o_ref[...] += jnp.dot(sel, x_ref[pl.ds(off, TOK_TILE), :], preferred_element_type=F32).astype(BF16)
        return carry

    lax.fori_loop(0, cn_ref[r], body, 0)


def _dispatch(c_lo, c_n, base, info_t, h2):
    return pl.pallas_call(
        _dispatch_kernel,
        out_shape=jax.ShapeDtypeStruct((SORT_ROWS, D_MODEL), BF16),
        grid_spec=pltpu.PrefetchScalarGridSpec(
            num_scalar_prefetch=3,
            grid=(SORT_ROWS // DISPATCH_TILE,),
            in_specs=[pl.BlockSpec((N_TILES, 8, TOK_TILE), lambda r, *_: (0, 0, 0)),
                      pl.BlockSpec((N_TOK, D_MODEL), lambda r, *_: (0, 0),
                                   pipeline_mode=pl.Buffered(1))],
            out_specs=pl.BlockSpec((DISPATCH_TILE, D_MODEL), lambda r, *_: (r, 0))),
        compiler_params=_cparams(1),
        name="moe_dispatch",
    )(c_lo, c_n, base, info_t, h2)


TILE_UNUSED, TILE_HALF, TILE_FULL = 0, 1, 2


def _expert_weight_ring(c, r, n_c, te_ref, tf_ref, nx_ref, copies, cast):
    @pl.when(jnp.logical_and(c == 0, r == 0))
    def _():
        for cp in copies(te_ref[0], 0):
            cp.start()

    @pl.when(tf_ref[r] == 1)
    def _():
        for cp in copies(te_ref[r], c):
            cp.wait()
        cast()
        nr = nx_ref[r]
        nc = c + (nr <= r).astype(jnp.int32)

        @pl.when(nc < n_c)
        def _():
            for cp in copies(te_ref[nr], nc):
                cp.start()


def _moe_up_kernel(te_ref, tv_ref, tf_ref, nx_ref, x_ref, w_hbm, h_ref, wst_ref, wbf_ref, sem_ref, *, i_moe):
    f = pl.program_id(0)
    r = pl.program_id(1)

    def copies(e, ft):
        out = []
        for part in range(2):
            col = pl.multiple_of((part * UP_TILES + ft) * UP_TILE, 128)
            out.append(pltpu.make_async_copy(w_hbm.at[i_moe, e, :, pl.ds(col, UP_TILE)],
                                             wst_ref.at[part], sem_ref.at[part]))
        return out

    def cast():
        wbf_ref[...] = wst_ref[...].astype(BF16)

    _expert_weight_ring(f, r, UP_TILES, te_ref, tf_ref, nx_ref, copies, cast)

    def hidden(x):
        gt = jnp.dot(x, wbf_ref[0], preferred_element_type=F32)
        up = jnp.dot(x, wbf_ref[1], preferred_element_type=F32)
        return (gt * _sigmoid(gt) * up).astype(BF16)

    half = ROW_TILE // 2

    @pl.when(tv_ref[r] == TILE_FULL)
    def _():
        h_ref[...] = hidden(x_ref[...])

    @pl.when(tv_ref[r] == TILE_HALF)
    def _():
        h_ref[0:half, :] = hidden(x_ref[0:half, :])
        h_ref[half:, :] = jnp.zeros((half, UP_TILE), BF16)

    @pl.when(tv_ref[r] == TILE_UNUSED)
    def _():
        h_ref[...] = jnp.zeros_like(h_ref)


def _moe_up(i_moe, tile_e, tile_valid, tile_first, tile_next, xs, w13):
    return pl.pallas_call(
        functools.partial(_moe_up_kernel, i_moe=i_moe),
        out_shape=jax.ShapeDtypeStruct((SORT_ROWS, D_FF_EXPERT), BF16),
        grid_spec=pltpu.PrefetchScalarGridSpec(
            num_scalar_prefetch=4,
            grid=(UP_TILES, SORT_TILES),
            in_specs=[pl.BlockSpec((ROW_TILE, D_MODEL), lambda f, r, *_: (r, 0)),
                      pl.BlockSpec(memory_space=pl.ANY)],
            out_specs=pl.BlockSpec((ROW_TILE, UP_TILE), lambda f, r, *_: (r, f)),
            scratch_shapes=[pltpu.VMEM((2, D_MODEL, UP_TILE), F32),
                            pltpu.VMEM((2, D_MODEL, UP_TILE), BF16),
                            pltpu.SemaphoreType.DMA((2,))]),
        compiler_params=_cparams(2),
        name="moe_up",
    )(tile_e, tile_valid, tile_first, tile_next, xs, w13)


def _moe_down_kernel(te_ref, tv_ref, tf_ref, nx_ref, h_ref, w_hbm, y_ref, wst_ref, wbf_ref, sem_ref, *, i_moe):
    n = pl.program_id(0)
    r = pl.program_id(1)

    def copies(e, nt):
        col = pl.multiple_of(nt * DOWN_TILE, 128)
        return [pltpu.make_async_copy(w_hbm.at[i_moe, e, :, pl.ds(col, DOWN_TILE)], wst_ref, sem_ref.at[0])]

    def cast():
        wbf_ref[...] = wst_ref[...].astype(BF16)

    _expert_weight_ring(n, r, D_MODEL // DOWN_TILE, te_ref, tf_ref, nx_ref, copies, cast)

    half = ROW_TILE // 2

    @pl.when(tv_ref[r] == TILE_FULL)
    def _():
        y_ref[...] = jnp.dot(h_ref[...], wbf_ref[...], preferred_element_type=F32).astype(BF16)

    @pl.when(tv_ref[r] == TILE_HALF)
    def _():
        y_ref[0:half, :] = jnp.dot(h_ref[0:half, :], wbf_ref[...], preferred_element_type=F32).astype(BF16)
        y_ref[half:, :] = jnp.zeros((half, DOWN_TILE), BF16)

    @pl.when(tv_ref[r] == TILE_UNUSED)
    def _():
        y_ref[...] = jnp.zeros_like(y_ref)


def _moe_down(i_moe, tile_e, tile_valid, tile_first, tile_next, hs, w2):
    return pl.pallas_call(
        functools.partial(_moe_down_kernel, i_moe=i_moe),
        out_shape=jax.ShapeDtypeStruct((SORT_ROWS, D_MODEL), BF16),
        grid_spec=pltpu.PrefetchScalarGridSpec(
            num_scalar_prefetch=4,
            grid=(D_MODEL // DOWN_TILE, SORT_TILES),
            in_specs=[pl.BlockSpec((ROW_TILE, D_FF_EXPERT), lambda n, r, *_: (r, 0)),
                      pl.BlockSpec(memory_space=pl.ANY)],
            out_specs=pl.BlockSpec((ROW_TILE, DOWN_TILE), lambda n, r, *_: (r, n)),
            scratch_shapes=[pltpu.VMEM((D_FF_EXPERT, DOWN_TILE), F32),
                            pltpu.VMEM((D_FF_EXPERT, DOWN_TILE), BF16),
                            pltpu.SemaphoreType.DMA((1,))]),
        compiler_params=_cparams(2),
        name="moe_down",
    )(tile_e, tile_valid, tile_first, tile_next, hs, w2)


def _window_copy(y_hbm, ybuf_ref, sem_ref, src, buf, slot):
    return pltpu.make_async_copy(y_hbm.at[pl.ds(src, WIN), :],
                                 ybuf_ref.at[buf, pl.ds(slot * WIN, WIN), :], sem_ref.at[buf, slot])


def _start_windows(src_ref, y_hbm, ybuf_ref, sem_ref, tile, buf):
    for s in range(N_WIN):
        src = pl.multiple_of(src_ref[tile * N_WIN + s], 16)
        _window_copy(y_hbm, ybuf_ref, sem_ref, src, buf, s).start()


def _combine_kernel(src_ref, lo_ref, hi_ref, base_ref, info_ref, x_ref, mod_ref, g_ref, y_hbm,
                    yp_ref, ys_ref, ybuf_ref, sem_ref):
    j = pl.program_id(0)
    buf = j % 2

    @pl.when(j == 0)
    def _():
        _start_windows(src_ref, y_hbm, ybuf_ref, sem_ref, 0, 0)

    @pl.when(j + 1 < N_TILES)
    def _():
        _start_windows(src_ref, y_hbm, ybuf_ref, sem_ref, j + 1, 1 - buf)

    info = info_ref[...]
    pos1 = _sorted_pos(info[:, 0:1], info[:, 2:3], base_ref)
    pos2 = _sorted_pos(info[:, 1:2], info[:, 3:4], base_ref)
    g1, g2 = info[:, 4:5], info[:, 5:6]
    iota = lax.broadcasted_iota(jnp.int32, (1, WIN), 1)
    ids = []
    for s in range(N_WIN):
        row = src_ref[j * N_WIN + s] + iota
        ok = jnp.logical_and(row >= lo_ref[j * N_WIN + s], row < hi_ref[j * N_WIN + s])
        ids.append(jnp.where(ok, row, -1))
    row_id = jnp.concatenate(ids, axis=1).astype(F32)
    sel = (jnp.where(pos1 == row_id, g1, 0.0) + jnp.where(pos2 == row_id, g2, 0.0)).astype(BF16)

    for s in range(N_WIN):
        _window_copy(y_hbm, ybuf_ref, sem_ref, 0, buf, s).wait()
    ffn = jnp.dot(sel, ybuf_ref[buf], preferred_element_type=F32)
    out = x_ref[...] + mod_ref[5:6, :] * _rms(ffn, g_ref[3:4, :])

    @pl.when(j < P_TILES)
    def _():
        yp_ref[...] = out

    @pl.when(j >= P_TILES)
    def _():
        ys_ref[...] = out


def _combine(l, win_src, win_lo, win_hi, base, info, x, mod, norm_g, ys):
    return pl.pallas_call(
        _combine_kernel,
        out_shape=(jax.ShapeDtypeStruct((N_PROMPT, D_MODEL), F32),
                   jax.ShapeDtypeStruct((N_SAMPLE, D_MODEL), F32)),
        grid_spec=pltpu.PrefetchScalarGridSpec(
            num_scalar_prefetch=4,
            grid=(N_TILES,),
            in_specs=[pl.BlockSpec((TOK_TILE, 128), lambda i, *_: (i, 0)),
                      pl.BlockSpec((TOK_TILE, D_MODEL), lambda i, *_: (i, 0)),
                      pl.BlockSpec((None, None, 6, D_MODEL), lambda i, *_: (l, _cond_id(i), 0, 0)),
                      pl.BlockSpec((None, 4, D_MODEL), lambda i, *_: (l, 0, 0)),
                      pl.BlockSpec(memory_space=pl.ANY)],
            out_specs=(pl.BlockSpec((TOK_TILE, D_MODEL), lambda i, *_: (jnp.minimum(i, P_TILES - 1), 0)),
                       pl.BlockSpec((TOK_TILE, D_MODEL), lambda i, *_: (jnp.maximum(i - P_TILES, 0), 0))),
            scratch_shapes=[pltpu.VMEM((2, N_WIN * WIN, D_MODEL), BF16),
                            pltpu.SemaphoreType.DMA((2, N_WIN))]),
        compiler_params=_cparams(1),
        name="moe_combine",
    )(win_src, win_lo, win_hi, base, info, x, mod, norm_g, ys)


def _moe_ffn(l, i_moe, h2, x, mod, norm_g, routing, moe_w13, moe_w2):
    info, info_t, cb, tot = routing

    ar8 = jnp.arange(N_EXPERTS, dtype=jnp.int32)

    def at8(vec, idx):
        return jnp.sum(jnp.where(idx[..., None] == ar8, vec, 0), axis=-1)

    def cumsum8(a):
        return jnp.sum(jnp.where(ar8[None, :] <= ar8[:, None], a[..., None, :], 0), axis=-1)

    counts = tot[0, :N_EXPERTS].astype(jnp.int32)
    padded = (counts + ROW_TILE - 1) // ROW_TILE * ROW_TILE
    seg_end = cumsum8(padded)
    base = (seg_end - padded).astype(jnp.int32)
    cbx = cb[:, 0, :N_EXPERTS].astype(jnp.int32)
    cb_end = jnp.concatenate([cbx[1:], counts[None, :]], axis=0)

    def tile_segments(n_tiles, rows):
        row0 = jnp.arange(n_tiles, dtype=jnp.int32) * rows
        e = jnp.minimum(jnp.sum(seg_end[None, :] <= row0[:, None], axis=1), N_EXPERTS - 1).astype(jnp.int32)
        k0 = row0 - at8(base, e)
        return e, k0, jnp.logical_and(k0 >= 0, k0 < at8(counts, e))

    tile_e, k0, tile_valid = tile_segments(SORT_TILES, ROW_TILE)
    tile_first = jnp.logical_and(tile_valid, k0 == 0)
    d_e, d_k0, d_valid = tile_segments(SORT_ROWS // DISPATCH_TILE, DISPATCH_TILE)
    d_kend = jnp.minimum(d_k0 + DISPATCH_TILE, at8(counts, d_e))
    cb_t = jnp.sum(jnp.where(d_e[:, None, None] == ar8[None, :, None], cbx.T[None, :, :], 0), axis=1)
    c_lo = jnp.sum(cb_t <= d_k0[:, None], axis=1) - 1
    c_hi = jnp.sum(cb_t < d_kend[:, None], axis=1) - 1
    c_n = jnp.where(d_valid, c_hi - c_lo + 1, 0).astype(jnp.int32)
    c_lo = jnp.where(d_valid, c_lo, 0).astype(jnp.int32)
    tile_idx = jnp.arange(SORT_TILES, dtype=jnp.int32)
    first_idx = jnp.where(tile_first, tile_idx, SORT_TILES)
    later = jnp.where(tile_idx[None, :] > tile_idx[:, None], first_idx[None, :], SORT_TILES)
    tile_next = jnp.min(later, axis=1)
    tile_next = jnp.where(tile_next >= SORT_TILES, 0, tile_next).astype(jnp.int32)
    tile_rows = at8(counts, tile_e) - k0
    tile_valid = jnp.where(tile_valid, jnp.where(tile_rows <= ROW_TILE // 2, TILE_HALF, TILE_FULL),
                           TILE_UNUSED).astype(jnp.int32)
    tile_first = tile_first.astype(jnp.int32)

    seg_lo = base[None, :] + cbx
    seg_hi = base[None, :] + cb_end
    seg_n = seg_hi - seg_lo
    w0 = seg_lo // 16 * 16
    nw = jnp.where(seg_n > 0, (seg_lo - w0 + seg_n + WIN - 1) // WIN, 0)
    nw_end = cumsum8(nw)
    slot = jnp.arange(N_WIN, dtype=jnp.int32)
    slot_e = jnp.minimum(jnp.sum(nw_end[:, None, :] <= slot[None, :, None], axis=2), N_EXPERTS - 1)
    pick = slot_e[:, :, None] == jnp.arange(N_EXPERTS, dtype=jnp.int32)[None, None, :]
    take = lambda a: jnp.sum(jnp.where(pick, a[:, None, :], 0), axis=2)
    slot_k = slot[None, :] - (take(nw_end) - take(nw))
    slot_ok = slot[None, :] < nw_end[:, -1:]
    slot_src = take(w0) + WIN * slot_k
    win_src = jnp.where(slot_ok, slot_src, 0).astype(jnp.int32).reshape(-1)
    win_lo = jnp.where(slot_ok, take(seg_lo), 0).astype(jnp.int32).reshape(-1)
    win_hi = jnp.where(slot_ok, take(seg_hi), 0).astype(jnp.int32).reshape(-1)

    xs = _dispatch(c_lo, c_n, base, info_t, h2)
    hs = _moe_up(i_moe, tile_e, tile_valid, tile_first, tile_next, xs, moe_w13)
    ys = _moe_down(i_moe, tile_e, tile_valid, tile_first, tile_next, hs, moe_w2)
    return _combine(l, win_src, win_lo, win_hi, base, info, x, mod, norm_g, ys)


def kernel(x_prompt, x_sample, cache_k, cache_v, c, c_ctx, w_ada, b_ada, norm_g, w_in, w_out,
           lam_params, subln_g, dw_weight, dw_bias, conv_ln_g, conv_ln_b, dense_w13, dense_w2,
           router_w, moe_w13, moe_w2):
    x = (x_prompt.reshape(N_PROMPT, D_MODEL), x_sample.reshape(N_SAMPLE, D_MODEL))
    cond8 = jnp.zeros((8, D_MODEL), F32).at[0].set(c_ctx).at[1:1 + DEC_BATCH].set(c)
    mod = _ada_modulation(cond8, w_ada, b_ada).reshape(DEPTH, 8, 6, D_MODEL)
    rope = _rope_tables()
    ck = cache_k.reshape(DEC_BATCH, DEPTH, PAST_LEN * ATT_HEADS, V_DIM)
    cv = cache_v.reshape(DEC_BATCH, DEPTH, PAST_LEN * ATT_HEADS, V_DIM)

    caches = None
    for l in range(DEPTH):
        lam_init = 0.8 - 0.6 * math.exp(-0.3 * l)
        qkv, cvo, *caches = _mixer_in(l, x, mod, norm_g, w_in, rope,
                                      (dw_weight, dw_bias, conv_ln_g, conv_ln_b), caches)
        o_ctx, o_lat = _attention(l, lam_init, lam_params, subln_g, qkv, ck, cv)
        i = l // 2
        if l % 2 == 0:
            (x,) = _outproj(l, o_ctx, o_lat, cvo, x, mod, norm_g, w_out,
                            dense_w=(dense_w13[i].astype(BF16), dense_w2[i].astype(BF16)))
        else:
            x, h2, *routing = _outproj(l, o_ctx, o_lat, cvo, x, mod, norm_g, w_out, router_w=router_w[i])
            x = _moe_ffn(l, i, h2, x, mod, norm_g, routing, moe_w13, moe_w2)

    xp, xs = x if isinstance(x, tuple) else (x[:N_PROMPT], x[N_PROMPT:])
    new_k, new_v = (a.reshape(BATCH, DEPTH, SEQ, ATT_HEADS, V_DIM) for a in caches)
    return (xp.reshape(BATCH, SEQ, D_MODEL), xs.reshape(DEC_BATCH, DEC_SEQ, D_MODEL), new_k, new_v)
```

```python
import functools
import math

import jax
import jax.numpy as jnp
import numpy as np
from jax import lax
from jax.experimental import pallas as pl
from jax.experimental.pallas import tpu as pltpu

F32 = jnp.float32
BF16 = jnp.bfloat16

D_MODEL = 1024
BATCH = 32
SEQ = 256
DEPTH = 2
DEC_BATCH = 2
DEC_SEQ = 2048
PAST_LEN = 512
GRID_W = 64
ATT_HEADS = 4
QK_DIM = 64
V_DIM = 128
ATT_W = ATT_HEADS * V_DIM
IN_W = 5 * ATT_W
CONV_CH = 512
CONV_WIDTH = 31
D_FF = 2816
N_EXPERTS = 8
D_FF_EXPERT = 3584
ROPE_THETA = 10000.0
NORM_EPS = 1e-6
LN_EPS = 1e-5

N_PROMPT = BATCH * SEQ
N_SAMPLE = DEC_BATCH * DEC_SEQ
N_TOK = N_PROMPT + N_SAMPLE

TOK_TILE = 512
N_TILES = N_TOK // TOK_TILE
P_TILES = N_PROMPT // TOK_TILE
S_TILES_PER_BATCH = DEC_SEQ // TOK_TILE

ROW_TILE = 512
SORT_TILES = (2 * N_TOK + N_EXPERTS * ROW_TILE) // ROW_TILE + 1
SORT_ROWS = SORT_TILES * ROW_TILE
DISPATCH_TILE = 256
UP_TILE = 1792
UP_TILES = D_FF_EXPERT // UP_TILE
DOWN_TILE = 1024
WIN = 128
N_WIN = 16

VMEM_LIMIT = 56 * 1024 * 1024


def _cparams(n_axes):
    return pltpu.CompilerParams(dimension_semantics=("arbitrary",) * n_axes,
                                vmem_limit_bytes=VMEM_LIMIT)


def _cond_id(i):
    return jnp.where(i < P_TILES, 0, 1 + (i - P_TILES) // S_TILES_PER_BATCH)


def _sigmoid(x):
    return 1.0 / (1.0 + jnp.exp(-x))


def _rms(x, g):
    return x * lax.rsqrt(jnp.mean(x * x, axis=-1, keepdims=True) + NORM_EPS) * g


def _ada_kernel(c_ref, w_ref, b_ref, o_ref):
    c = c_ref[...]
    s = (c * _sigmoid(c)).astype(BF16)
    o_ref[...] = jnp.dot(s, w_ref[...].astype(BF16), preferred_element_type=F32) + b_ref[...]


def _ada_modulation(cond8, w_ada, b_ada):
    tn = 1536
    n = 6 * D_MODEL
    return pl.pallas_call(
        _ada_kernel,
        out_shape=jax.ShapeDtypeStruct((DEPTH, 8, n), F32),
        grid=(DEPTH, n // tn),
        in_specs=[pl.BlockSpec((8, D_MODEL), lambda l, j: (0, 0)),
                  pl.BlockSpec((None, D_MODEL, tn), lambda l, j: (l, 0, j)),
                  pl.BlockSpec((None, 1, tn), lambda l, j: (l, 0, j))],
        out_specs=pl.BlockSpec((None, 8, tn), lambda l, j: (l, 0, j)),
        compiler_params=_cparams(2),
        name="ada_modulation",
    )(cond8, w_ada, b_ada.reshape(DEPTH, 1, n))


def _tile_x(xa_ref, xb_ref):
    return jnp.where(pl.program_id(0) < P_TILES, xa_ref[...], xb_ref[...])


def _x_specs(x):
    last = N_TILES - 1
    if isinstance(x, tuple):
        xa, xb = x
        b_idx = lambda i, *_: (jnp.maximum(jnp.minimum(i, last) - P_TILES, 0), 0)
    else:
        xa = xb = x
        b_idx = lambda i, *_: (jnp.maximum(jnp.minimum(i, last), P_TILES), 0)
    a_idx = lambda i, *_: (jnp.minimum(i, P_TILES - 1), 0)
    return [pl.BlockSpec((TOK_TILE, D_MODEL), a_idx), pl.BlockSpec((TOK_TILE, D_MODEL), b_idx)], (xa, xb)


SEQ_PER_TILE = TOK_TILE // SEQ
CACHE_ROWS = SEQ * ATT_HEADS


CONV_LAG = 2
CONV_RING = 4
CONV_TILE = 256
CONV_HALO = 16
CONV_SUB = 32


def _conv_pass(upad_ref, w_ref, bias_ref, lg_ref, lb_ref, o_ref, row0, between):
    rows = CONV_TILE + 2 * CONV_HALO - 8
    for s in range(1, 8):
        upad_ref[s, 0:rows, :] = upad_ref[0, s:s + rows, :]
    first_tap = CONV_HALO - CONV_WIDTH // 2
    groups = CONV_SUB // 8
    for t in range(CONV_TILE // CONV_SUB):
        between()
        base = t * CONV_SUB
        acc = jnp.zeros((groups, 8, CONV_CH), F32)
        for j in range(CONV_WIDTH):
            lo = base + (first_tap + j) // 8 * 8
            taps = upad_ref[(first_tap + j) % 8, lo:lo + CONV_SUB, :].reshape(groups, 8, CONV_CH)
            acc = acc + taps * w_ref[j][None]
        y = acc.reshape(CONV_SUB, CONV_CH) + bias_ref[...]
        mu = jnp.mean(y, axis=-1, keepdims=True)
        yc = y - mu
        var = jnp.mean(yc * yc, axis=-1, keepdims=True)
        z = yc * lax.rsqrt(var + LN_EPS) * lg_ref[...] + lb_ref[...]
        o_ref[row0 + base:row0 + base + CONV_SUB, :] = (z * _sigmoid(z)).astype(BF16)


def _mixer_in_kernel(*refs, layer):
    n_in = 12 + (2 if layer else 0)
    (xa_ref, xb_ref, mod_ref, g_ref, w_ref, cos_ref, sina_ref, sinb_ref,
     dw_ref, db_ref, lg_ref, lb_ref) = refs[:12]
    qkv_ref, cv_ref, kc_ref, vc_ref, wbf_ref, ring_ref, glu_ref, pj_ref, upad_ref, taps_ref = refs[n_in:]
    i = pl.program_id(0)
    t = jnp.minimum(i, N_TILES - 1)

    @pl.when(i == 0)
    def _():
        wbf_ref[...] = w_ref[...].astype(BF16)
        ring_ref[...] = jnp.zeros_like(ring_ref)
        glu_ref[...] = jnp.zeros_like(glu_ref)
        for j in range(CONV_WIDTH):
            taps_ref[j] = jnp.broadcast_to(dw_ref[j:j + 1, :], (8, CONV_CH))

    ring_ref[(i - 1) % CONV_RING] = glu_ref[...]

    h = _rms(_tile_x(xa_ref, xb_ref), g_ref[0:1, :]) * (1.0 + mod_ref[1:2, :]) + mod_ref[0:1, :]
    hb = h.astype(BF16)

    def proj(part):
        return jnp.dot(hb, wbf_ref[:, part * ATT_W:(part + 1) * ATT_W], preferred_element_type=F32)

    def proj_part(part):
        if part < 3:
            pj_ref[:, part * ATT_W:(part + 1) * ATT_W] = proj(part)
        elif part == 3:
            glu_ref[...] = proj(3)
        else:
            glu_ref[...] = glu_ref[...] * _sigmoid(proj(4))

    conv_units = 2 * (CONV_TILE // CONV_SUB)
    issue_at = {k * conv_units // 5: k for k in range(5)}
    unit = [0]

    def between():
        if unit[0] in issue_at:
            proj_part(issue_at[unit[0]])
        unit[0] += 1

    j = i - CONV_LAG
    latent = j >= P_TILES
    q = (j - P_TILES) % S_TILES_PER_BATCH
    has_prev = jnp.logical_and(latent, q != 0)
    has_next = jnp.logical_and(latent, q != S_TILES_PER_BATCH - 1)
    cur = j % CONV_RING
    prev_tail = ring_ref[(j - 1) % CONV_RING, TOK_TILE - CONV_HALO:, :]
    next_head = ring_ref[(j + 1) % CONV_RING, 0:CONV_HALO, :]
    for s in range(TOK_TILE // CONV_TILE):
        lo = s * CONV_TILE
        if s == 0:
            before = jnp.where(has_prev, prev_tail, 0.0)
        else:
            before = jnp.where(latent, ring_ref[cur, lo - CONV_HALO:lo, :], 0.0)
        if s == TOK_TILE // CONV_TILE - 1:
            after = jnp.where(has_next, next_head, 0.0)
        else:
            after = jnp.where(latent, ring_ref[cur, lo + CONV_TILE:lo + CONV_TILE + CONV_HALO, :], 0.0)
        pad = upad_ref
        pad[0, 0:CONV_HALO, :] = before
        pad[0, CONV_HALO:CONV_HALO + CONV_TILE, :] = ring_ref[cur, lo:lo + CONV_TILE, :]
        pad[0, CONV_HALO + CONV_TILE:, :] = after
        _conv_pass(pad, taps_ref, db_ref, lg_ref, lb_ref, cv_ref, lo, between)

    qkv_ref[:, 2 * ATT_W:] = pj_ref[:, 2 * ATT_W:].astype(BF16)

    @pl.when(t < P_TILES)
    def _():
        qkv_ref[:, :2 * ATT_W] = pj_ref[:, :2 * ATT_W].astype(BF16)
        for ref, col0 in ((kc_ref, ATT_W), (vc_ref, 2 * ATT_W)):
            for s in range(SEQ_PER_TILE):
                for hd in range(ATT_HEADS):
                    val = pj_ref[SEQ * s:SEQ * (s + 1), col0 + V_DIM * hd:col0 + V_DIM * (hd + 1)]
                    rows = pl.ds(hd, SEQ, stride=ATT_HEADS)
                    if layer == 0:
                        ref[s, 0, rows, :] = val
                    else:
                        ref[s, rows, :] = val
            if layer == 0:
                ref[:, 1:] = jnp.zeros((SEQ_PER_TILE, DEPTH - 1, CACHE_ROWS, V_DIM), F32)

    @pl.when(t >= P_TILES)
    def _():
        cos = cos_ref[...]
        sina = sina_ref[...]
        sinb = sinb_ref[...]
        for c in range(2 * ATT_W // 128):
            xg = pj_ref[:, 128 * c:128 * (c + 1)]
            fwd = pltpu.roll(xg, 128 - 16, 1)
            bwd = pltpu.roll(xg, 16, 1)
            qkv_ref[:, 128 * c:128 * (c + 1)] = (xg * cos + fwd * sina + bwd * sinb).astype(BF16)


def _mixer_in(l, x, mod, norm_g, w_in, rope, conv_params, caches):
    cos, sina, sinb = rope
    dw_w, dw_b, ln_g, ln_b = conv_params
    x_specs, x_args = _x_specs(x)

    def tile(i):
        return jnp.minimum(i, N_TILES - 1)

    def rope_idx(i):
        return (jnp.maximum(tile(i) - P_TILES, 0) % S_TILES_PER_BATCH, 0)

    def vec(a):
        return a.reshape(DEPTH, 1, CONV_CH)

    cache_shape = jax.ShapeDtypeStruct((BATCH, DEPTH, CACHE_ROWS, V_DIM), F32)
    if l == 0:
        cache_spec = pl.BlockSpec((SEQ_PER_TILE, DEPTH, CACHE_ROWS, V_DIM),
                                  lambda i: (jnp.minimum(i, P_TILES - 1), 0, 0, 0))
        extra_specs, extra_args, aliases = [], (), {}
    else:
        cache_spec = pl.BlockSpec((SEQ_PER_TILE, None, CACHE_ROWS, V_DIM),
                                  lambda i: (jnp.minimum(i, P_TILES - 1), l, 0, 0))
        extra_specs = [pl.BlockSpec(memory_space=pl.ANY)] * 2
        extra_args = tuple(caches)
        aliases = {12: 2, 13: 3}

    return pl.pallas_call(
        functools.partial(_mixer_in_kernel, layer=l),
        out_shape=(jax.ShapeDtypeStruct((N_TOK, 3 * ATT_W), BF16),
                   jax.ShapeDtypeStruct((N_TOK, CONV_CH), BF16), cache_shape, cache_shape),
        grid=(N_TILES + CONV_LAG,),
        in_specs=x_specs + [
            pl.BlockSpec((None, None, 6, D_MODEL), lambda i: (l, _cond_id(tile(i)), 0, 0)),
            pl.BlockSpec((None, 4, D_MODEL), lambda i: (l, 0, 0)),
            pl.BlockSpec((None, D_MODEL, IN_W), lambda i: (l, 0, 0), pipeline_mode=pl.Buffered(1)),
            pl.BlockSpec((TOK_TILE, 128), rope_idx),
            pl.BlockSpec((TOK_TILE, 128), rope_idx),
            pl.BlockSpec((TOK_TILE, 128), rope_idx),
            pl.BlockSpec((None, CONV_WIDTH, CONV_CH), lambda i: (l, 0, 0)),
            pl.BlockSpec((None, 1, CONV_CH), lambda i: (l, 0, 0)),
            pl.BlockSpec((None, 1, CONV_CH), lambda i: (l, 0, 0)),
            pl.BlockSpec((None, 1, CONV_CH), lambda i: (l, 0, 0))] + extra_specs,
        out_specs=(pl.BlockSpec((TOK_TILE, 3 * ATT_W), lambda i: (tile(i), 0)),
                   pl.BlockSpec((TOK_TILE, CONV_CH), lambda i: (jnp.maximum(i - CONV_LAG, 0), 0)),
                   cache_spec, cache_spec),
        scratch_shapes=[pltpu.VMEM((D_MODEL, IN_W), BF16),
                        pltpu.VMEM((CONV_RING, TOK_TILE, CONV_CH), F32),
                        pltpu.VMEM((TOK_TILE, CONV_CH), F32),
                        pltpu.VMEM((TOK_TILE, 3 * ATT_W), F32),
                        pltpu.VMEM((8, CONV_TILE + 2 * CONV_HALO, CONV_CH), F32),
                        pltpu.VMEM((CONV_WIDTH, 8, CONV_CH), F32)],
        input_output_aliases=aliases,
        compiler_params=_cparams(1),
        name=f"mixer_in_{l}",
    )(*x_args, mod, norm_g, w_in, cos, sina, sinb, dw_w, vec(dw_b), vec(ln_g), vec(ln_b), *extra_args)


def _rope_tables():
    rows = DEC_SEQ // GRID_W
    row_pos = np.repeat(np.arange(rows, dtype=np.float64), GRID_W)
    col_pos = np.tile(np.arange(GRID_W, dtype=np.float64), rows)
    half = QK_DIM // 2
    inv_freq = 1.0 / (ROPE_THETA ** (np.arange(0, half, 2, dtype=np.float64) / half))
    ang_r = row_pos[:, None] * inv_freq
    ang_c = col_pos[:, None] * inv_freq
    ang = np.concatenate([ang_r, ang_r, ang_c, ang_c], axis=-1)
    cos = np.tile(np.cos(ang), (1, 2)).astype(np.float32)
    sin = np.tile(np.sin(ang), (1, 2)).astype(np.float32)
    first = (np.arange(128) % 32) < 16
    sina = np.where(first[None, :], -sin, 0.0).astype(np.float32)
    sinb = np.where(first[None, :], 0.0, sin).astype(np.float32)
    return jnp.asarray(cos), jnp.asarray(sina), jnp.asarray(sinb)


def _attn_kernel(*refs, lam_init, has_ext):
    if has_ext:
        lamp_ref, sub_ref, q_ref, k_ref, v_ref, ke_ref, ve_ref, o_ref = refs
    else:
        lamp_ref, sub_ref, q_ref, k_ref, v_ref, o_ref = refs
    lp = lamp_ref[...]
    lam = (jnp.exp(jnp.sum(lp[0:1] * lp[1:2], axis=-1, keepdims=True))
           - jnp.exp(jnp.sum(lp[2:3] * lp[3:4], axis=-1, keepdims=True)) + lam_init)
    lane = lax.broadcasted_iota(jnp.int32, (1, V_DIM), 1)
    nt = (((1,), (1,)), ((), ()))
    scale = QK_DIM ** -0.5
    map_scale = [jnp.where(lane < QK_DIM, scale, 0.0).astype(BF16),
                 jnp.where(lane < QK_DIM, 0.0, scale).astype(BF16)]

    tq = q_ref.shape[0]
    head_cols = [slice(V_DIM * hd, V_DIM * (hd + 1)) for hd in range(ATT_HEADS)]

    def q_map(hd, m):
        return q_ref[:, head_cols[hd]] * map_scale[m]

    def with_ones(v):
        return jnp.concatenate([v, jnp.ones_like(v)], axis=1)

    outs = []
    if has_ext:
        for hd in range(ATT_HEADS):
            kh = k_ref[:, head_cols[hd]]
            v_aug = with_ones(v_ref[:, head_cols[hd]])
            head_rows = pl.ds(hd, PAST_LEN, stride=ATT_HEADS)
            keh = ke_ref[head_rows, :].astype(BF16)
            ve_aug = with_ones(ve_ref[head_rows, :].astype(BF16))
            ratio = []
            for m in range(2):
                qm = q_map(hd, m)
                s = lax.dot_general(qm, kh, nt, preferred_element_type=F32)
                se = lax.dot_general(qm, keh, nt, preferred_element_type=F32)
                mx = jnp.maximum(jnp.max(s, axis=-1, keepdims=True), jnp.max(se, axis=-1, keepdims=True))
                pv = (jnp.dot(jnp.exp((s - mx).astype(BF16)), v_aug, preferred_element_type=F32)
                      + jnp.dot(jnp.exp((se - mx).astype(BF16)), ve_aug, preferred_element_type=F32))
                ratio.append(pv[:, :V_DIM] / pv[:, V_DIM:])
            outs.append(ratio[0] - lam * ratio[1])
    else:
        pairs = [(hd, m) for hd in range(ATT_HEADS) for m in range(2)]
        s = jnp.concatenate([lax.dot_general(q_map(hd, m), k_ref[:, head_cols[hd]], nt,
                                             preferred_element_type=F32) for hd, m in pairs], axis=0)
        p = jnp.exp((s - jnp.max(s, axis=-1, keepdims=True)).astype(BF16))
        for hd in range(ATT_HEADS):
            pv = jnp.dot(p[2 * tq * hd:2 * tq * (hd + 1)], with_ones(v_ref[:, head_cols[hd]]),
                         preferred_element_type=F32)
            ratio = pv[:, :V_DIM] / pv[:, V_DIM:]
            outs.append(ratio[:tq] - lam * ratio[tq:])

    heads = [(_rms(o, sub_ref[...]) * (1.0 - lam_init)).astype(BF16) for o in outs]
    o_ref[...] = jnp.concatenate(heads, axis=1)


def _attention(l, lam_init, lam_params, subln_g, qkvg, cache_k, cache_v):
    small = [pl.BlockSpec((None, 4, QK_DIM), lambda *_: (l, 0, 0)),
             pl.BlockSpec((None, 1, V_DIM), lambda *_: (l, 0, 0))]
    sub3 = subln_g.reshape(DEPTH, 1, V_DIM)

    o_ctx = pl.pallas_call(
        functools.partial(_attn_kernel, lam_init=lam_init, has_ext=False),
        out_shape=jax.ShapeDtypeStruct((N_PROMPT, ATT_W), BF16),
        grid=(BATCH,),
        in_specs=small + [pl.BlockSpec((SEQ, ATT_W), lambda b: (b, 0)),
                          pl.BlockSpec((SEQ, ATT_W), lambda b: (b, 1)),
                          pl.BlockSpec((SEQ, ATT_W), lambda b: (b, 2))],
        out_specs=pl.BlockSpec((SEQ, ATT_W), lambda b: (b, 0)),
        compiler_params=_cparams(1),
        name=f"attn_ctx_{l}",
    )(lam_params, sub3, qkvg, qkvg, qkvg)

    tq = 256
    q_tiles = DEC_SEQ // tq
    q_base = N_PROMPT // tq
    kv_base = N_PROMPT // DEC_SEQ

    def q_idx(b, i):
        return (q_base + b * q_tiles + i, 0)

    o_lat = pl.pallas_call(
        functools.partial(_attn_kernel, lam_init=lam_init, has_ext=True),
        out_shape=jax.ShapeDtypeStruct((N_SAMPLE, ATT_W), BF16),
        grid=(DEC_BATCH, q_tiles),
        in_specs=small + [pl.BlockSpec((tq, ATT_W), q_idx),
                          pl.BlockSpec((DEC_SEQ, ATT_W), lambda b, i: (kv_base + b, 1)),
                          pl.BlockSpec((DEC_SEQ, ATT_W), lambda b, i: (kv_base + b, 2)),
                          pl.BlockSpec((None, None, PAST_LEN * ATT_HEADS, V_DIM), lambda b, i: (b, l, 0, 0)),
                          pl.BlockSpec((None, None, PAST_LEN * ATT_HEADS, V_DIM), lambda b, i: (b, l, 0, 0))],
        out_specs=pl.BlockSpec((tq, ATT_W), lambda b, i: (b * q_tiles + i, 0)),
        compiler_params=_cparams(2),
        name=f"attn_lat_{l}",
    )(lam_params, sub3, qkvg, qkvg, qkvg, cache_k, cache_v)
    return o_ctx, o_lat


def _outproj_kernel(*refs, ffn):
    if ffn == "routed":
        (oc_ref, ol_ref, cv_ref, xa_ref, xb_ref, mod_ref, g_ref, w_ref, rw_ref,
         xo_ref, h2_ref, info_ref, info_t_ref, cb_ref, tot_ref, wbf_ref, carry_ref) = refs
    else:
        (oc_ref, ol_ref, cv_ref, xa_ref, xb_ref, mod_ref, g_ref, w_ref, w13_ref, w2_ref,
         xo_ref, wbf_ref, h2_ref) = refs
    i = pl.program_id(0)

    @pl.when(i == 0)
    def _():
        wbf_ref[...] = w_ref[...].astype(BF16)

    half = TOK_TILE // 2
    for rows in (slice(0, half), slice(half, TOK_TILE)):
        o = jnp.where(i < P_TILES, oc_ref[rows, :], ol_ref[rows, :])
        m = (jnp.dot(o, wbf_ref[0:ATT_W, :], preferred_element_type=F32)
             + jnp.dot(cv_ref[rows, :], wbf_ref[ATT_W:, :], preferred_element_type=F32))
        x = jnp.where(i < P_TILES, xa_ref[rows, :], xb_ref[rows, :])
        xn = x + mod_ref[2:3, :] * _rms(m, g_ref[1:2, :])
        xo_ref[rows, :] = xn
        h2_ref[rows, :] = (_rms(xn, g_ref[2:3, :]) * (1.0 + mod_ref[4:5, :]) + mod_ref[3:4, :]).astype(BF16)

    if ffn == "routed":
        @pl.when(i == 0)
        def _():
            carry_ref[...] = jnp.zeros_like(carry_ref)

        _route_tile(h2_ref[...], rw_ref, info_ref, info_t_ref, cb_ref, tot_ref, carry_ref)
    else:
        h = h2_ref[...]
        gt = jnp.dot(h, w13_ref[:, :D_FF], preferred_element_type=F32)
        up = jnp.dot(h, w13_ref[:, D_FF:], preferred_element_type=F32)
        a = (gt * _sigmoid(gt) * up).astype(BF16)
        y = jnp.dot(a, w2_ref[...], preferred_element_type=F32)
        xo_ref[...] = xo_ref[...] + mod_ref[5:6, :] * _rms(y, g_ref[3:4, :])


def _outproj(l, o_ctx, o_lat, cv, x, mod, norm_g, w_out, router_w=None, dense_w=None):
    x_specs, x_args = _x_specs(x)
    out_shape = [jax.ShapeDtypeStruct((N_TOK, D_MODEL), F32)]
    out_specs = [pl.BlockSpec((TOK_TILE, D_MODEL), lambda i: (i, 0))]
    scratch = [pltpu.VMEM((D_MODEL, D_MODEL), BF16)]
    if dense_w is not None:
        extra_specs = [pl.BlockSpec((D_MODEL, 2 * D_FF), lambda i: (0, 0), pipeline_mode=pl.Buffered(1)),
                       pl.BlockSpec((D_FF, D_MODEL), lambda i: (0, 0), pipeline_mode=pl.Buffered(1))]
        extra_args = tuple(dense_w)
        scratch += [pltpu.VMEM((TOK_TILE, D_MODEL), BF16)]
    else:
        rw_pad = jnp.zeros((D_MODEL, 128), BF16).at[:, :N_EXPERTS].set(router_w.astype(BF16))
        extra_specs, extra_args = [pl.BlockSpec((D_MODEL, 128), lambda i: (0, 0))], (rw_pad,)
        out_shape += [jax.ShapeDtypeStruct((N_TOK, D_MODEL), BF16)]
        out_specs += [pl.BlockSpec((TOK_TILE, D_MODEL), lambda i: (i, 0))]
        out_shape += [jax.ShapeDtypeStruct((N_TOK, 128), F32),
                      jax.ShapeDtypeStruct((N_TILES, 8, TOK_TILE), F32),
                      jax.ShapeDtypeStruct((N_TILES, 8, 128), F32),
                      jax.ShapeDtypeStruct((8, 128), F32)]
        out_specs += [pl.BlockSpec((TOK_TILE, 128), lambda i: (i, 0)),
                      pl.BlockSpec((None, 8, TOK_TILE), lambda i: (i, 0, 0)),
                      pl.BlockSpec((None, 8, 128), lambda i: (i, 0, 0)),
                      pl.BlockSpec((8, 128), lambda i: (0, 0))]
        scratch += [pltpu.VMEM((8, 128), F32)]
    return pl.pallas_call(
        functools.partial(_outproj_kernel, ffn="dense" if dense_w is not None else "routed"),
        out_shape=tuple(out_shape),
        grid=(N_TILES,),
        in_specs=[pl.BlockSpec((TOK_TILE, ATT_W), lambda i: (jnp.minimum(i, P_TILES - 1), 0)),
                  pl.BlockSpec((TOK_TILE, ATT_W), lambda i: (jnp.maximum(i - P_TILES, 0), 0)),
                  pl.BlockSpec((TOK_TILE, CONV_CH), lambda i: (i, 0))] + x_specs + [
                  pl.BlockSpec((None, None, 6, D_MODEL), lambda i: (l, _cond_id(i), 0, 0)),
                  pl.BlockSpec((None, 4, D_MODEL), lambda i: (l, 0, 0)),
                  pl.BlockSpec((None, D_MODEL, D_MODEL), lambda i: (l, 0, 0),
                               pipeline_mode=pl.Buffered(1))] + extra_specs,
        out_specs=tuple(out_specs),
        scratch_shapes=scratch,
        compiler_params=_cparams(1),
        name=f"outproj_{l}",
    )(o_ctx, o_lat, cv, *x_args, mod, norm_g, w_out, *extra_args)


def _route_tile(h, rw_ref, info_ref, info_t_ref, cb_ref, tot_ref, carry_ref):
    lane = lax.broadcasted_iota(jnp.int32, (TOK_TILE, 128), 1)
    lanef = lane.astype(F32)
    logits = jnp.dot(h, rw_ref[...], preferred_element_type=F32)
    logits = jnp.where(lane < N_EXPERTS, logits, -jnp.inf)
    big = jnp.asarray(128.0, F32)
    m1 = jnp.max(logits, axis=-1, keepdims=True)
    e1 = jnp.min(jnp.where(logits == m1, lanef, big), axis=-1, keepdims=True)
    oh1 = lanef == e1
    rest = jnp.where(oh1, -jnp.inf, logits)
    m2 = jnp.max(rest, axis=-1, keepdims=True)
    e2 = jnp.min(jnp.where(rest == m2, lanef, big), axis=-1, keepdims=True)
    oh2 = lanef == e2
    ex = jnp.exp(m2 - m1)
    g1 = 1.0 / (1.0 + ex)
    g2 = ex / (1.0 + ex)

    oh = jnp.where(jnp.logical_or(oh1, oh2), 1.0, 0.0)
    r = lax.broadcasted_iota(jnp.int32, (TOK_TILE, TOK_TILE), 0)
    c = lax.broadcasted_iota(jnp.int32, (TOK_TILE, TOK_TILE), 1)
    tri = jnp.where(c <= r, 1.0, 0.0).astype(BF16)
    incl = jnp.dot(tri, oh.astype(BF16), preferred_element_type=F32)
    carry = carry_ref[0:1, :]
    excl = incl - oh + carry
    rank1 = jnp.sum(jnp.where(oh1, excl, 0.0), axis=-1, keepdims=True)
    rank2 = jnp.sum(jnp.where(oh2, excl, 0.0), axis=-1, keepdims=True)

    info = jnp.where(lane == 0, e1, 0.0)
    for k, col in enumerate((e2, rank1, rank2, g1, g2), start=1):
        info = jnp.where(lane == k, col, info)
    info_ref[...] = info
    info_t_ref[...] = info.T[0:8, :]

    cb_ref[...] = carry_ref[...]
    new_carry = carry + incl[TOK_TILE - 1:TOK_TILE, :]
    carry_ref[...] = jnp.broadcast_to(new_carry, carry_ref.shape)
    tot_ref[...] = jnp.broadcast_to(new_carry, tot_ref.shape)


def _sorted_pos(expert, rank, base_ref):
    start = jnp.zeros_like(rank)
    for e in range(N_EXPERTS):
        start = jnp.where(expert == float(e), base_ref[e].astype(F32), start)
    return start + rank


def _dispatch_kernel(clo_ref, cn_ref, base_ref, info_t_ref, x_ref, o_ref):
    r = pl.program_id(0)
    o_ref[...] = jnp.zeros_like(o_ref)
    rows = (r * DISPATCH_TILE + lax.broadcasted_iota(jnp.int32, (DISPATCH_TILE, TOK_TILE), 0)).astype(F32)

    def body(k, carry):
        c = clo_ref[r] + k
        it = info_t_ref[c]
        pos1 = _sorted_pos(it[0:1, :], it[2:3, :], base_ref)
        pos2 = _sorted_pos(it[1:2, :], it[3:4, :], base_ref)
        hit = jnp.logical_or(rows == pos1, rows == pos2)
        sel = jnp.where(hit, 1.0, 0.0).astype(BF16)
        off = pl.multiple_of(c * TOK_TILE, TOK_TILE)
        o_ref[...] += jnp.dot(sel, x_ref[pl.ds(off, TOK_TILE), :], preferred_element_type=F32).astype(BF16)
        return carry

    lax.fori_loop(0, cn_ref[r], body, 0)


def _dispatch(c_lo, c_n, base, info_t, h2):
    return pl.pallas_call(
        _dispatch_kernel,
        out_shape=jax.ShapeDtypeStruct((SORT_ROWS, D_MODEL), BF16),
        grid_spec=pltpu.PrefetchScalarGridSpec(
            num_scalar_prefetch=3,
            grid=(SORT_ROWS // DISPATCH_TILE,),
            in_specs=[pl.BlockSpec((N_TILES, 8, TOK_TILE), lambda r, *_: (0, 0, 0)),
                      pl.BlockSpec((N_TOK, D_MODEL), lambda r, *_: (0, 0),
                                   pipeline_mode=pl.Buffered(1))],
            out_specs=pl.BlockSpec((DISPATCH_TILE, D_MODEL), lambda r, *_: (r, 0))),
        compiler_params=_cparams(1),
        name="moe_dispatch",
    )(c_lo, c_n, base, info_t, h2)


TILE_UNUSED, TILE_HALF, TILE_FULL = 0, 1, 2
UP_FIRST_BLOCKS = (slice(0, 1024), slice(1024, UP_TILE))
DOWN_FIRST_BLOCKS = (slice(0, DOWN_TILE // 2), slice(DOWN_TILE // 2, DOWN_TILE))


def _expert_weight_ring(c, r, n_c, te_ref, tf_ref, nx_ref, copies, cast):
    @pl.when(jnp.logical_and(c == 0, r == 0))
    def _():
        for cp in copies(te_ref[0], 0):
            cp.start()

    @pl.when(tf_ref[r] == 1)
    def _():
        for cp in copies(te_ref[r], c):
            cp.wait()
        cast()
        nr = nx_ref[r]
        nc = c + (nr <= r).astype(jnp.int32)

        @pl.when(nc < n_c)
        def _():
            for cp in copies(te_ref[nr], nc):
                cp.start()


def _moe_up_kernel(te_ref, tv_ref, tf_ref, nx_ref, x_ref, w_hbm, h_ref, wst_ref, wbf_ref, sem_ref, *, i_moe):
    f = pl.program_id(0)
    r = pl.program_id(1)

    def copies(e, ft):
        out = []
        for part in range(2):
            col = pl.multiple_of((part * UP_TILES + ft) * UP_TILE, 128)
            out.append(pltpu.make_async_copy(w_hbm.at[i_moe, e, :, pl.ds(col, UP_TILE)],
                                             wst_ref.at[part], sem_ref.at[part]))
        return out

    def hidden(x, cols=slice(None)):
        gt = jnp.dot(x, wbf_ref[0, :, cols], preferred_element_type=F32)
        up = jnp.dot(x, wbf_ref[1, :, cols], preferred_element_type=F32)
        return (gt * _sigmoid(gt) * up).astype(BF16)

    def first_tile():
        x = x_ref[...]
        for cols in UP_FIRST_BLOCKS:
            wbf_ref[:, :, cols] = wst_ref[:, :, cols].astype(BF16)
            h_ref[:, cols] = hidden(x, cols)

    _expert_weight_ring(f, r, UP_TILES, te_ref, tf_ref, nx_ref, copies, first_tile)

    half = ROW_TILE // 2
    later_tile = tf_ref[r] == 0

    @pl.when(jnp.logical_and(later_tile, tv_ref[r] == TILE_FULL))
    def _():
        h_ref[...] = hidden(x_ref[...])

    @pl.when(jnp.logical_and(later_tile, tv_ref[r] == TILE_HALF))
    def _():
        h_ref[0:half, :] = hidden(x_ref[0:half, :])
        h_ref[half:, :] = jnp.zeros((half, UP_TILE), BF16)

    @pl.when(tv_ref[r] == TILE_UNUSED)
    def _():
        h_ref[...] = jnp.zeros_like(h_ref)


def _moe_up(i_moe, tile_e, tile_valid, tile_first, tile_next, xs, w13):
    return pl.pallas_call(
        functools.partial(_moe_up_kernel, i_moe=i_moe),
        out_shape=jax.ShapeDtypeStruct((SORT_ROWS, D_FF_EXPERT), BF16),
        grid_spec=pltpu.PrefetchScalarGridSpec(
            num_scalar_prefetch=4,
            grid=(UP_TILES, SORT_TILES),
            in_specs=[pl.BlockSpec((ROW_TILE, D_MODEL), lambda f, r, *_: (r, 0)),
                      pl.BlockSpec(memory_space=pl.ANY)],
            out_specs=pl.BlockSpec((ROW_TILE, UP_TILE), lambda f, r, *_: (r, f)),
            scratch_shapes=[pltpu.VMEM((2, D_MODEL, UP_TILE), F32),
                            pltpu.VMEM((2, D_MODEL, UP_TILE), BF16),
                            pltpu.SemaphoreType.DMA((2,))]),
        compiler_params=_cparams(2),
        name="moe_up",
    )(tile_e, tile_valid, tile_first, tile_next, xs, w13)


def _moe_down_kernel(te_ref, tv_ref, tf_ref, nx_ref, h_ref, w_hbm, y_ref, wst_ref, wbf_ref, sem_ref, *, i_moe):
    n = pl.program_id(0)
    r = pl.program_id(1)

    def copies(e, nt):
        col = pl.multiple_of(nt * DOWN_TILE, 128)
        return [pltpu.make_async_copy(w_hbm.at[i_moe, e, :, pl.ds(col, DOWN_TILE)], wst_ref, sem_ref.at[0])]

    def first_tile():
        h = h_ref[...]
        for cols in DOWN_FIRST_BLOCKS:
            wbf_ref[:, cols] = wst_ref[:, cols].astype(BF16)
            y_ref[:, cols] = jnp.dot(h, wbf_ref[:, cols], preferred_element_type=F32).astype(BF16)

    _expert_weight_ring(n, r, D_MODEL // DOWN_TILE, te_ref, tf_ref, nx_ref, copies, first_tile)

    half = ROW_TILE // 2
    later_tile = tf_ref[r] == 0

    @pl.when(jnp.logical_and(later_tile, tv_ref[r] == TILE_FULL))
    def _():
        y_ref[...] = jnp.dot(h_ref[...], wbf_ref[...], preferred_element_type=F32).astype(BF16)

    @pl.when(jnp.logical_and(later_tile, tv_ref[r] == TILE_HALF))
    def _():
        y_ref[0:half, :] = jnp.dot(h_ref[0:half, :], wbf_ref[...], preferred_element_type=F32).astype(BF16)
        y_ref[half:, :] = jnp.zeros((half, DOWN_TILE), BF16)

    @pl.when(tv_ref[r] == TILE_UNUSED)
    def _():
        y_ref[...] = jnp.zeros_like(y_ref)


def _moe_down(i_moe, tile_e, tile_valid, tile_first, tile_next, hs, w2):
    return pl.pallas_call(
        functools.partial(_moe_down_kernel, i_moe=i_moe),
        out_shape=jax.ShapeDtypeStruct((SORT_ROWS, D_MODEL), BF16),
        grid_spec=pltpu.PrefetchScalarGridSpec(
            num_scalar_prefetch=4,
            grid=(D_MODEL // DOWN_TILE, SORT_TILES),
            in_specs=[pl.BlockSpec((ROW_TILE, D_FF_EXPERT), lambda n, r, *_: (r, 0)),
                      pl.BlockSpec(memory_space=pl.ANY)],
            out_specs=pl.BlockSpec((ROW_TILE, DOWN_TILE), lambda n, r, *_: (r, n)),
            scratch_shapes=[pltpu.VMEM((D_FF_EXPERT, DOWN_TILE), F32),
                            pltpu.VMEM((D_FF_EXPERT, DOWN_TILE), BF16),
                            pltpu.SemaphoreType.DMA((1,))]),
        compiler_params=_cparams(2),
        name="moe_down",
    )(tile_e, tile_valid, tile_first, tile_next, hs, w2)


def _window_copy(y_hbm, ybuf_ref, sem_ref, src, buf, slot):
    return pltpu.make_async_copy(y_hbm.at[pl.ds(src, WIN), :],
                                 ybuf_ref.at[buf, pl.ds(slot * WIN, WIN), :], sem_ref.at[buf, slot])


def _start_windows(src_ref, y_hbm, ybuf_ref, sem_ref, tile, buf):
    for s in range(N_WIN):
        src = pl.multiple_of(src_ref[tile * N_WIN + s], 16)
        _window_copy(y_hbm, ybuf_ref, sem_ref, src, buf, s).start()


def _combine_kernel(src_ref, lo_ref, hi_ref, base_ref, info_ref, x_ref, mod_ref, g_ref, y_hbm,
                    yp_ref, ys_ref, ybuf_ref, sem_ref):
    j = pl.program_id(0)
    buf = j % 2

    @pl.when(j == 0)
    def _():
        _start_windows(src_ref, y_hbm, ybuf_ref, sem_ref, 0, 0)

    @pl.when(j + 1 < N_TILES)
    def _():
        _start_windows(src_ref, y_hbm, ybuf_ref, sem_ref, j + 1, 1 - buf)

    info = info_ref[...]
    pos1 = _sorted_pos(info[:, 0:1], info[:, 2:3], base_ref)
    pos2 = _sorted_pos(info[:, 1:2], info[:, 3:4], base_ref)
    g1, g2 = info[:, 4:5], info[:, 5:6]
    iota = lax.broadcasted_iota(jnp.int32, (1, WIN), 1)
    ids = []
    for s in range(N_WIN):
        row = src_ref[j * N_WIN + s] + iota
        ok = jnp.logical_and(row >= lo_ref[j * N_WIN + s], row < hi_ref[j * N_WIN + s])
        ids.append(jnp.where(ok, row, -1))
    row_id = jnp.concatenate(ids, axis=1).astype(F32)
    sel = (jnp.where(pos1 == row_id, g1, 0.0) + jnp.where(pos2 == row_id, g2, 0.0)).astype(BF16)

    for s in range(N_WIN):
        _window_copy(y_hbm, ybuf_ref, sem_ref, 0, buf, s).wait()
    ffn = jnp.dot(sel, ybuf_ref[buf], preferred_element_type=F32)
    out = x_ref[...] + mod_ref[5:6, :] * _rms(ffn, g_ref[3:4, :])

    @pl.when(j < P_TILES)
    def _():
        yp_ref[...] = out

    @pl.when(j >= P_TILES)
    def _():
        ys_ref[...] = out


def _combine(l, win_src, win_lo, win_hi, base, info, x, mod, norm_g, ys):
    return pl.pallas_call(
        _combine_kernel,
        out_shape=(jax.ShapeDtypeStruct((N_PROMPT, D_MODEL), F32),
                   jax.ShapeDtypeStruct((N_SAMPLE, D_MODEL), F32)),
        grid_spec=pltpu.PrefetchScalarGridSpec(
            num_scalar_prefetch=4,
            grid=(N_TILES,),
            in_specs=[pl.BlockSpec((TOK_TILE, 128), lambda i, *_: (i, 0)),
                      pl.BlockSpec((TOK_TILE, D_MODEL), lambda i, *_: (i, 0)),
                      pl.BlockSpec((None, None, 6, D_MODEL), lambda i, *_: (l, _cond_id(i), 0, 0)),
                      pl.BlockSpec((None, 4, D_MODEL), lambda i, *_: (l, 0, 0)),
                      pl.BlockSpec(memory_space=pl.ANY)],
            out_specs=(pl.BlockSpec((TOK_TILE, D_MODEL), lambda i, *_: (jnp.minimum(i, P_TILES - 1), 0)),
                       pl.BlockSpec((TOK_TILE, D_MODEL), lambda i, *_: (jnp.maximum(i - P_TILES, 0), 0))),
            scratch_shapes=[pltpu.VMEM((2, N_WIN * WIN, D_MODEL), BF16),
                            pltpu.SemaphoreType.DMA((2, N_WIN))]),
        compiler_params=_cparams(1),
        name="moe_combine",
    )(win_src, win_lo, win_hi, base, info, x, mod, norm_g, ys)


def _moe_ffn(l, i_moe, h2, x, mod, norm_g, routing, moe_w13, moe_w2):
    info, info_t, cb, tot = routing

    ar8 = jnp.arange(N_EXPERTS, dtype=jnp.int32)

    def at8(vec, idx):
        return jnp.sum(jnp.where(idx[..., None] == ar8, vec, 0), axis=-1)

    def cumsum8(a):
        return jnp.sum(jnp.where(ar8[None, :] <= ar8[:, None], a[..., None, :], 0), axis=-1)

    counts = tot[0, :N_EXPERTS].astype(jnp.int32)
    padded = (counts + ROW_TILE - 1) // ROW_TILE * ROW_TILE
    seg_end = cumsum8(padded)
    base = (seg_end - padded).astype(jnp.int32)
    cbx = cb[:, 0, :N_EXPERTS].astype(jnp.int32)
    cb_end = jnp.concatenate([cbx[1:], counts[None, :]], axis=0)

    def tile_segments(n_tiles, rows):
        row0 = jnp.arange(n_tiles, dtype=jnp.int32) * rows
        e = jnp.minimum(jnp.sum(seg_end[None, :] <= row0[:, None], axis=1), N_EXPERTS - 1).astype(jnp.int32)
        k0 = row0 - at8(base, e)
        return e, k0, jnp.logical_and(k0 >= 0, k0 < at8(counts, e))

    tile_e, k0, tile_valid = tile_segments(SORT_TILES, ROW_TILE)
    tile_first = jnp.logical_and(tile_valid, k0 == 0)
    d_e, d_k0, d_valid = tile_segments(SORT_ROWS // DISPATCH_TILE, DISPATCH_TILE)
    d_kend = jnp.minimum(d_k0 + DISPATCH_TILE, at8(counts, d_e))
    cb_t = jnp.sum(jnp.where(d_e[:, None, None] == ar8[None, :, None], cbx.T[None, :, :], 0), axis=1)
    c_lo = jnp.sum(cb_t <= d_k0[:, None], axis=1) - 1
    c_hi = jnp.sum(cb_t < d_kend[:, None], axis=1) - 1
    c_n = jnp.where(d_valid, c_hi - c_lo + 1, 0).astype(jnp.int32)
    c_lo = jnp.where(d_valid, c_lo, 0).astype(jnp.int32)
    tile_idx = jnp.arange(SORT_TILES, dtype=jnp.int32)
    first_idx = jnp.where(tile_first, tile_idx, SORT_TILES)
    later = jnp.where(tile_idx[None, :] > tile_idx[:, None], first_idx[None, :], SORT_TILES)
    tile_next = jnp.min(later, axis=1)
    tile_next = jnp.where(tile_next >= SORT_TILES, 0, tile_next).astype(jnp.int32)
    tile_rows = at8(counts, tile_e) - k0
    tile_valid = jnp.where(tile_valid, jnp.where(tile_rows <= ROW_TILE // 2, TILE_HALF, TILE_FULL),
                           TILE_UNUSED).astype(jnp.int32)
    tile_first = tile_first.astype(jnp.int32)

    seg_lo = base[None, :] + cbx
    seg_hi = base[None, :] + cb_end
    seg_n = seg_hi - seg_lo
    w0 = seg_lo // 16 * 16
    nw = jnp.where(seg_n > 0, (seg_lo - w0 + seg_n + WIN - 1) // WIN, 0)
    nw_end = cumsum8(nw)
    slot = jnp.arange(N_WIN, dtype=jnp.int32)
    slot_e = jnp.minimum(jnp.sum(nw_end[:, None, :] <= slot[None, :, None], axis=2), N_EXPERTS - 1)
    pick = slot_e[:, :, None] == jnp.arange(N_EXPERTS, dtype=jnp.int32)[None, None, :]
    take = lambda a: jnp.sum(jnp.where(pick, a[:, None, :], 0), axis=2)
    slot_k = slot[None, :] - (take(nw_end) - take(nw))
    slot_ok = slot[None, :] < nw_end[:, -1:]
    slot_src = take(w0) + WIN * slot_k
    win_src = jnp.where(slot_ok, slot_src, 0).astype(jnp.int32).reshape(-1)
    win_lo = jnp.where(slot_ok, take(seg_lo), 0).astype(jnp.int32).reshape(-1)
    win_hi = jnp.where(slot_ok, take(seg_hi), 0).astype(jnp.int32).reshape(-1)

    xs = _dispatch(c_lo, c_n, base, info_t, h2)
    hs = _moe_up(i_moe, tile_e, tile_valid, tile_first, tile_next, xs, moe_w13)
    ys = _moe_down(i_moe, tile_e, tile_valid, tile_first, tile_next, hs, moe_w2)
    return _combine(l, win_src, win_lo, win_hi, base, info, x, mod, norm_g, ys)


def kernel(x_prompt, x_sample, cache_k, cache_v, c, c_ctx, w_ada, b_ada, norm_g, w_in, w_out,
           lam_params, subln_g, dw_weight, dw_bias, conv_ln_g, conv_ln_b, dense_w13, dense_w2,
           router_w, moe_w13, moe_w2):
    x = (x_prompt.reshape(N_PROMPT, D_MODEL), x_sample.reshape(N_SAMPLE, D_MODEL))
    cond8 = jnp.zeros((8, D_MODEL), F32).at[0].set(c_ctx).at[1:1 + DEC_BATCH].set(c)
    mod = _ada_modulation(cond8, w_ada, b_ada).reshape(DEPTH, 8, 6, D_MODEL)
    rope = _rope_tables()
    ck = cache_k.reshape(DEC_BATCH, DEPTH, PAST_LEN * ATT_HEADS, V_DIM)
    cv = cache_v.reshape(DEC_BATCH, DEPTH, PAST_LEN * ATT_HEADS, V_DIM)

    caches = None
    for l in range(DEPTH):
        lam_init = 0.8 - 0.6 * math.exp(-0.3 * l)
        qkv, cvo, *caches = _mixer_in(l, x, mod, norm_g, w_in, rope,
                                      (dw_weight, dw_bias, conv_ln_g, conv_ln_b), caches)
        o_ctx, o_lat = _attention(l, lam_init, lam_params, subln_g, qkv, ck, cv)
        i = l // 2
        if l % 2 == 0:
            (x,) = _outproj(l, o_ctx, o_lat, cvo, x, mod, norm_g, w_out,
                            dense_w=(dense_w13[i].astype(BF16), dense_w2[i].astype(BF16)))
        else:
            x, h2, *routing = _outproj(l, o_ctx, o_lat, cvo, x, mod, norm_g, w_out, router_w=router_w[i])
            x = _moe_ffn(l, i, h2, x, mod, norm_g, routing, moe_w13, moe_w2)

    xp, xs = x if isinstance(x, tuple) else (x[:N_PROMPT], x[N_PROMPT:])
    new_k, new_v = (a.reshape(BATCH, DEPTH, SEQ, ATT_HEADS, V_DIM) for a in caches)
    return (xp.reshape(BATCH, SEQ, D_MODEL), xs.reshape(DEC_BATCH, DEC_SEQ, D_MODEL), new_k, new_v)
```

```python
import functools
import math

import jax
import jax.numpy as jnp
import numpy as np
from jax import lax
from jax.experimental import pallas as pl
from jax.experimental.pallas import tpu as pltpu

F32 = jnp.float32
BF16 = jnp.bfloat16

D_MODEL = 1024
BATCH = 32
SEQ = 256
DEPTH = 2
DEC_BATCH = 2
DEC_SEQ = 2048
PAST_LEN = 512
GRID_W = 64
ATT_HEADS = 4
QK_DIM = 64
V_DIM = 128
ATT_W = ATT_HEADS * V_DIM
IN_W = 5 * ATT_W
CONV_CH = 512
CONV_WIDTH = 31
D_FF = 2816
N_EXPERTS = 8
D_FF_EXPERT = 3584
ROPE_THETA = 10000.0
NORM_EPS = 1e-6
LN_EPS = 1e-5

N_PROMPT = BATCH * SEQ
N_SAMPLE = DEC_BATCH * DEC_SEQ
N_TOK = N_PROMPT + N_SAMPLE

LANES = 128
BF16_SUBLANES = 16
MXU_TILE = 256
VMEM_BYTES = 64 * 1024 * 1024
VMEM_LIMIT = VMEM_BYTES - 8 * 1024 * 1024

TOK_TILE = 512
N_TILES = N_TOK // TOK_TILE
P_TILES = N_PROMPT // TOK_TILE
S_TILES_PER_BATCH = DEC_SEQ // TOK_TILE

ROW_TILE = 512
SORT_TILES = (2 * N_TOK + N_EXPERTS * ROW_TILE) // ROW_TILE + 1
SORT_ROWS = SORT_TILES * ROW_TILE
DISPATCH_TILE = 256
UP_TILE = 7 * MXU_TILE
UP_TILES = D_FF_EXPERT // UP_TILE
DOWN_TILE = 4 * MXU_TILE
WIN = 128
N_WIN = (2 * TOK_TILE + N_EXPERTS * (BF16_SUBLANES - 1 + WIN - 1)) // WIN

assert D_FF_EXPERT % UP_TILE == 0 and D_MODEL % DOWN_TILE == 0 and D_FF % MXU_TILE == 0
assert ROW_TILE % DISPATCH_TILE == 0 and N_TOK % TOK_TILE == 0 and N_PROMPT % TOK_TILE == 0
assert DEPTH == 2, "layer 0 zero-fills exactly one later cache slab; the last layer must be the routed one"


def _cparams(n_axes):
    return pltpu.CompilerParams(dimension_semantics=("arbitrary",) * n_axes,
                                vmem_limit_bytes=VMEM_LIMIT)


def _cond_id(i):
    return jnp.where(i < P_TILES, 0, 1 + (i - P_TILES) // S_TILES_PER_BATCH)


def _sigmoid(x):
    return 1.0 / (1.0 + jnp.exp(-x))


def _rms(x, g):
    return x * lax.rsqrt(jnp.mean(x * x, axis=-1, keepdims=True) + NORM_EPS) * g


def _ada_kernel(c_ref, w_ref, b_ref, o_ref):
    c = c_ref[...]
    s = (c * _sigmoid(c)).astype(BF16)
    o_ref[...] = jnp.dot(s, w_ref[...].astype(BF16), preferred_element_type=F32) + b_ref[...]


def _ada_modulation(cond8, w_ada, b_ada):
    tn = 1536
    n = 6 * D_MODEL
    return pl.pallas_call(
        _ada_kernel,
        out_shape=jax.ShapeDtypeStruct((DEPTH, 8, n), F32),
        grid=(DEPTH, n // tn),
        in_specs=[pl.BlockSpec((8, D_MODEL), lambda l, j: (0, 0)),
                  pl.BlockSpec((None, D_MODEL, tn), lambda l, j: (l, 0, j)),
                  pl.BlockSpec((None, 1, tn), lambda l, j: (l, 0, j))],
        out_specs=pl.BlockSpec((None, 8, tn), lambda l, j: (l, 0, j)),
        compiler_params=_cparams(2),
        name="ada_modulation",
    )(cond8, w_ada, b_ada.reshape(DEPTH, 1, n))


def _tile_x(xa_ref, xb_ref):
    return jnp.where(pl.program_id(0) < P_TILES, xa_ref[...], xb_ref[...])


def _x_specs(x):
    last = N_TILES - 1
    if isinstance(x, tuple):
        xa, xb = x
        b_idx = lambda i, *_: (jnp.maximum(jnp.minimum(i, last) - P_TILES, 0), 0)
    else:
        xa = xb = x
        b_idx = lambda i, *_: (jnp.maximum(jnp.minimum(i, last), P_TILES), 0)
    a_idx = lambda i, *_: (jnp.minimum(i, P_TILES - 1), 0)
    return [pl.BlockSpec((TOK_TILE, D_MODEL), a_idx), pl.BlockSpec((TOK_TILE, D_MODEL), b_idx)], (xa, xb)


ROT_SPAN = QK_DIM // 4
SEQ_PER_TILE = TOK_TILE // SEQ
CACHE_ROWS = SEQ * ATT_HEADS


CONV_LAG = 2
CONV_RING = 3
CONV_TILE = 256
CONV_HALO = 16
CONV_SUB = 32


def _conv_pass(upad_ref, w_ref, bias_ref, lg_ref, lb_ref, o_ref, row0):
    rows = CONV_TILE + 2 * CONV_HALO - 8
    for s in range(1, 8):
        upad_ref[s, 0:rows, :] = upad_ref[0, s:s + rows, :]
    first_tap = CONV_HALO - CONV_WIDTH // 2
    groups = CONV_SUB // 8
    for t in range(CONV_TILE // CONV_SUB):
        base = t * CONV_SUB
        acc = jnp.zeros((groups, 8, CONV_CH), F32)
        for j in range(CONV_WIDTH):
            lo = base + (first_tap + j) // 8 * 8
            taps = upad_ref[(first_tap + j) % 8, lo:lo + CONV_SUB, :].reshape(groups, 8, CONV_CH)
            acc = acc + taps * w_ref[j][None]
        y = acc.reshape(CONV_SUB, CONV_CH) + bias_ref[...]
        mu = jnp.mean(y, axis=-1, keepdims=True)
        yc = y - mu
        var = jnp.mean(yc * yc, axis=-1, keepdims=True)
        z = yc * lax.rsqrt(var + LN_EPS) * lg_ref[...] + lb_ref[...]
        o_ref[row0 + base:row0 + base + CONV_SUB, :] = (z * _sigmoid(z)).astype(BF16)


def _mixer_in_kernel(*refs, layer):
    n_in = 12 + (2 if layer else 0)
    (xa_ref, xb_ref, mod_ref, g_ref, w_ref, cos_ref, sina_ref, sinb_ref,
     dw_ref, db_ref, lg_ref, lb_ref) = refs[:12]
    qkv_ref, cv_ref, kc_ref, vc_ref, wbf_ref, ring_ref, glu_ref, pj_ref, upad_ref, taps_ref = refs[n_in:]
    i = pl.program_id(0)
    t = jnp.minimum(i, N_TILES - 1)

    @pl.when(i == 0)
    def _():
        wbf_ref[...] = w_ref[...].astype(BF16)
        ring_ref[...] = jnp.zeros_like(ring_ref)
        glu_ref[...] = jnp.zeros_like(glu_ref)
        for j in range(CONV_WIDTH):
            taps_ref[j] = jnp.broadcast_to(dw_ref[j:j + 1, :], (8, CONV_CH))

    ring_ref[(i - 1) % CONV_RING] = glu_ref[...]

    h = _rms(_tile_x(xa_ref, xb_ref), g_ref[0:1, :]) * (1.0 + mod_ref[1:2, :]) + mod_ref[0:1, :]
    hb = h.astype(BF16)

    def proj(part):
        return jnp.dot(hb, wbf_ref[:, part * ATT_W:(part + 1) * ATT_W], preferred_element_type=F32)

    for part in range(3):
        pj_ref[:, part * ATT_W:(part + 1) * ATT_W] = proj(part)
    glu_ref[...] = proj(3) * _sigmoid(proj(4))

    j = i - CONV_LAG
    latent = j >= P_TILES
    q = (j - P_TILES) % S_TILES_PER_BATCH
    has_prev = jnp.logical_and(latent, q != 0)
    has_next = jnp.logical_and(latent, q != S_TILES_PER_BATCH - 1)
    cur = j % CONV_RING
    prev_tail = ring_ref[(j - 1) % CONV_RING, TOK_TILE - CONV_HALO:, :]
    next_head = ring_ref[(j + 1) % CONV_RING, 0:CONV_HALO, :]
    for s in range(TOK_TILE // CONV_TILE):
        lo = s * CONV_TILE
        if s == 0:
            before = jnp.where(has_prev, prev_tail, 0.0)
        else:
            before = jnp.where(latent, ring_ref[cur, lo - CONV_HALO:lo, :], 0.0)
        if s == TOK_TILE // CONV_TILE - 1:
            after = jnp.where(has_next, next_head, 0.0)
        else:
            after = jnp.where(latent, ring_ref[cur, lo + CONV_TILE:lo + CONV_TILE + CONV_HALO, :], 0.0)
        upad_ref[0, 0:CONV_HALO, :] = before
        upad_ref[0, CONV_HALO:CONV_HALO + CONV_TILE, :] = ring_ref[cur, lo:lo + CONV_TILE, :]
        upad_ref[0, CONV_HALO + CONV_TILE:, :] = after
        _conv_pass(upad_ref, taps_ref, db_ref, lg_ref, lb_ref, cv_ref, lo)

    qkv_ref[:, 2 * ATT_W:] = pj_ref[:, 2 * ATT_W:].astype(BF16)

    @pl.when(t < P_TILES)
    def _():
        qkv_ref[:, :2 * ATT_W] = pj_ref[:, :2 * ATT_W].astype(BF16)
        for ref, col0 in ((kc_ref, ATT_W), (vc_ref, 2 * ATT_W)):
            for s in range(SEQ_PER_TILE):
                for hd in range(ATT_HEADS):
                    val = pj_ref[SEQ * s:SEQ * (s + 1), col0 + V_DIM * hd:col0 + V_DIM * (hd + 1)]
                    rows = pl.ds(hd, SEQ, stride=ATT_HEADS)
                    if layer == 0:
                        ref[s, 0, rows, :] = val
                    else:
                        ref[s, rows, :] = val
            if layer == 0:
                ref[:, 1:] = jnp.zeros((SEQ_PER_TILE, DEPTH - 1, CACHE_ROWS, V_DIM), F32)

    @pl.when(t >= P_TILES)
    def _():
        cos = cos_ref[...]
        sina = sina_ref[...]
        sinb = sinb_ref[...]
        for c in range(2 * ATT_W // V_DIM):
            xg = pj_ref[:, V_DIM * c:V_DIM * (c + 1)]
            fwd = pltpu.roll(xg, V_DIM - ROT_SPAN, 1)
            bwd = pltpu.roll(xg, ROT_SPAN, 1)
            qkv_ref[:, V_DIM * c:V_DIM * (c + 1)] = (xg * cos + fwd * sina + bwd * sinb).astype(BF16)


def _mixer_in(l, x, mod, norm_g, w_in, rope, conv_params, caches):
    cos, sina, sinb = rope
    dw_w, dw_b, ln_g, ln_b = conv_params
    x_specs, x_args = _x_specs(x)

    def tile(i):
        return jnp.minimum(i, N_TILES - 1)

    def rope_idx(i):
        return (jnp.maximum(tile(i) - P_TILES, 0) % S_TILES_PER_BATCH, 0)

    def vec(a):
        return a.reshape(DEPTH, 1, CONV_CH)

    cache_shape = jax.ShapeDtypeStruct((BATCH, DEPTH, CACHE_ROWS, V_DIM), F32)
    if l == 0:
        cache_spec = pl.BlockSpec((SEQ_PER_TILE, DEPTH, CACHE_ROWS, V_DIM),
                                  lambda i: (jnp.minimum(i, P_TILES - 1), 0, 0, 0))
        extra_specs, extra_args, aliases = [], (), {}
    else:
        cache_spec = pl.BlockSpec((SEQ_PER_TILE, None, CACHE_ROWS, V_DIM),
                                  lambda i: (jnp.minimum(i, P_TILES - 1), l, 0, 0))
        extra_specs = [pl.BlockSpec(memory_space=pl.ANY)] * 2
        extra_args = tuple(caches)
        aliases = {12: 2, 13: 3}

    return pl.pallas_call(
        functools.partial(_mixer_in_kernel, layer=l),
        out_shape=(jax.ShapeDtypeStruct((N_TOK, 3 * ATT_W), BF16),
                   jax.ShapeDtypeStruct((N_TOK, CONV_CH), BF16), cache_shape, cache_shape),
        grid=(N_TILES + CONV_LAG,),
        in_specs=x_specs + [
            pl.BlockSpec((None, None, 6, D_MODEL), lambda i: (l, _cond_id(tile(i)), 0, 0)),
            pl.BlockSpec((None, 4, D_MODEL), lambda i: (l, 0, 0)),
            pl.BlockSpec((None, D_MODEL, IN_W), lambda i: (l, 0, 0), pipeline_mode=pl.Buffered(1)),
            pl.BlockSpec((TOK_TILE, V_DIM), rope_idx),
            pl.BlockSpec((TOK_TILE, V_DIM), rope_idx),
            pl.BlockSpec((TOK_TILE, V_DIM), rope_idx),
            pl.BlockSpec((None, CONV_WIDTH, CONV_CH), lambda i: (l, 0, 0)),
            pl.BlockSpec((None, 1, CONV_CH), lambda i: (l, 0, 0)),
            pl.BlockSpec((None, 1, CONV_CH), lambda i: (l, 0, 0)),
            pl.BlockSpec((None, 1, CONV_CH), lambda i: (l, 0, 0))] + extra_specs,
        out_specs=(pl.BlockSpec((TOK_TILE, 3 * ATT_W), lambda i: (tile(i), 0)),
                   pl.BlockSpec((TOK_TILE, CONV_CH), lambda i: (jnp.maximum(i - CONV_LAG, 0), 0)),
                   cache_spec, cache_spec),
        scratch_shapes=[pltpu.VMEM((D_MODEL, IN_W), BF16),
                        pltpu.VMEM((CONV_RING, TOK_TILE, CONV_CH), F32),
                        pltpu.VMEM((TOK_TILE, CONV_CH), F32),
                        pltpu.VMEM((TOK_TILE, 3 * ATT_W), F32),
                        pltpu.VMEM((8, CONV_TILE + 2 * CONV_HALO, CONV_CH), F32),
                        pltpu.VMEM((CONV_WIDTH, 8, CONV_CH), F32)],
        input_output_aliases=aliases,
        compiler_params=_cparams(1),
        name=f"mixer_in_{l}",
    )(*x_args, mod, norm_g, w_in, cos, sina, sinb, dw_w, vec(dw_b), vec(ln_g), vec(ln_b), *extra_args)


def _rope_tables():
    rows = DEC_SEQ // GRID_W
    row_pos = np.repeat(np.arange(rows, dtype=np.float64), GRID_W)
    col_pos = np.tile(np.arange(GRID_W, dtype=np.float64), rows)
    half = QK_DIM // 2
    inv_freq = 1.0 / (ROPE_THETA ** (np.arange(0, half, 2, dtype=np.float64) / half))
    ang_r = row_pos[:, None] * inv_freq
    ang_c = col_pos[:, None] * inv_freq
    ang = np.concatenate([ang_r, ang_r, ang_c, ang_c], axis=-1)
    cos = np.tile(np.cos(ang), (1, 2)).astype(np.float32)
    sin = np.tile(np.sin(ang), (1, 2)).astype(np.float32)
    first = (np.arange(V_DIM) % (2 * ROT_SPAN)) < ROT_SPAN
    sina = np.where(first[None, :], -sin, 0.0).astype(np.float32)
    sinb = np.where(first[None, :], 0.0, sin).astype(np.float32)
    return jnp.asarray(cos), jnp.asarray(sina), jnp.asarray(sinb)


def _attn_kernel(*refs, lam_init, has_ext):
    if has_ext:
        lamp_ref, sub_ref, q_ref, k_ref, v_ref, ke_ref, ve_ref, o_ref = refs
    else:
        lamp_ref, sub_ref, q_ref, k_ref, v_ref, o_ref = refs
    lp = lamp_ref[...]
    lam = (jnp.exp(jnp.sum(lp[0:1] * lp[1:2], axis=-1, keepdims=True))
           - jnp.exp(jnp.sum(lp[2:3] * lp[3:4], axis=-1, keepdims=True)) + lam_init)
    lane = lax.broadcasted_iota(jnp.int32, (1, V_DIM), 1)
    nt = (((1,), (1,)), ((), ()))
    scale = QK_DIM ** -0.5
    map_scale = [jnp.where(lane < QK_DIM, scale, 0.0).astype(BF16),
                 jnp.where(lane < QK_DIM, 0.0, scale).astype(BF16)]

    tq = q_ref.shape[0]
    head_cols = [slice(V_DIM * hd, V_DIM * (hd + 1)) for hd in range(ATT_HEADS)]

    def q_map(hd, m):
        return q_ref[:, head_cols[hd]] * map_scale[m]

    def with_ones(v):
        return jnp.concatenate([v, jnp.ones_like(v)], axis=1)

    outs = []
    if has_ext:
        for hd in range(ATT_HEADS):
            kh = k_ref[:, head_cols[hd]]
            v_aug = with_ones(v_ref[:, head_cols[hd]])
            head_rows = pl.ds(hd, PAST_LEN, stride=ATT_HEADS)
            keh = ke_ref[head_rows, :].astype(BF16)
            ve_aug = with_ones(ve_ref[head_rows, :].astype(BF16))
            ratio = []
            for m in range(2):
                qm = q_map(hd, m)
                s = lax.dot_general(qm, kh, nt, preferred_element_type=F32)
                se = lax.dot_general(qm, keh, nt, preferred_element_type=F32)
                mx = jnp.maximum(jnp.max(s, axis=-1, keepdims=True), jnp.max(se, axis=-1, keepdims=True))
                pv = (jnp.dot(jnp.exp((s - mx).astype(BF16)), v_aug, preferred_element_type=F32)
                      + jnp.dot(jnp.exp((se - mx).astype(BF16)), ve_aug, preferred_element_type=F32))
                ratio.append(pv[:, :V_DIM] / pv[:, V_DIM:])
            outs.append(ratio[0] - lam * ratio[1])
    else:
        pairs = [(hd, m) for hd in range(ATT_HEADS) for m in range(2)]
        s = jnp.concatenate([lax.dot_general(q_map(hd, m), k_ref[:, head_cols[hd]], nt,
                                             preferred_element_type=F32) for hd, m in pairs], axis=0)
        p = jnp.exp((s - jnp.max(s, axis=-1, keepdims=True)).astype(BF16))
        for hd in range(ATT_HEADS):
            pv = jnp.dot(p[2 * tq * hd:2 * tq * (hd + 1)], with_ones(v_ref[:, head_cols[hd]]),
                         preferred_element_type=F32)
            ratio = pv[:, :V_DIM] / pv[:, V_DIM:]
            outs.append(ratio[:tq] - lam * ratio[tq:])

    heads = [(_rms(o, sub_ref[...]) * (1.0 - lam_init)).astype(BF16) for o in outs]
    o_ref[...] = jnp.concatenate(heads, axis=1)


def _attention(l, lam_init, lam_params, subln_g, qkvg, cache_k, cache_v):
    small = [pl.BlockSpec((None, 4, QK_DIM), lambda *_: (l, 0, 0)),
             pl.BlockSpec((None, 1, V_DIM), lambda *_: (l, 0, 0))]
    sub3 = subln_g.reshape(DEPTH, 1, V_DIM)

    o_ctx = pl.pallas_call(
        functools.partial(_attn_kernel, lam_init=lam_init, has_ext=False),
        out_shape=jax.ShapeDtypeStruct((N_PROMPT, ATT_W), BF16),
        grid=(BATCH,),
        in_specs=small + [pl.BlockSpec((SEQ, ATT_W), lambda b: (b, 0)),
                          pl.BlockSpec((SEQ, ATT_W), lambda b: (b, 1)),
                          pl.BlockSpec((SEQ, ATT_W), lambda b: (b, 2))],
        out_specs=pl.BlockSpec((SEQ, ATT_W), lambda b: (b, 0)),
        compiler_params=_cparams(1),
        name=f"attn_ctx_{l}",
    )(lam_params, sub3, qkvg, qkvg, qkvg)

    tq = 256
    q_tiles = DEC_SEQ // tq
    q_base = N_PROMPT // tq
    kv_base = N_PROMPT // DEC_SEQ

    def q_idx(b, i):
        return (q_base + b * q_tiles + i, 0)

    o_lat = pl.pallas_call(
        functools.partial(_attn_kernel, lam_init=lam_init, has_ext=True),
        out_shape=jax.ShapeDtypeStruct((N_SAMPLE, ATT_W), BF16),
        grid=(DEC_BATCH, q_tiles),
        in_specs=small + [pl.BlockSpec((tq, ATT_W), q_idx),
                          pl.BlockSpec((DEC_SEQ, ATT_W), lambda b, i: (kv_base + b, 1)),
                          pl.BlockSpec((DEC_SEQ, ATT_W), lambda b, i: (kv_base + b, 2)),
                          pl.BlockSpec((None, None, PAST_LEN * ATT_HEADS, V_DIM), lambda b, i: (b, l, 0, 0)),
                          pl.BlockSpec((None, None, PAST_LEN * ATT_HEADS, V_DIM), lambda b, i: (b, l, 0, 0))],
        out_specs=pl.BlockSpec((tq, ATT_W), lambda b, i: (b * q_tiles + i, 0)),
        compiler_params=_cparams(2),
        name=f"attn_lat_{l}",
    )(lam_params, sub3, qkvg, qkvg, qkvg, cache_k, cache_v)
    return o_ctx, o_lat


def _outproj_kernel(*refs, ffn):
    if ffn == "routed":
        (oc_ref, ol_ref, cv_ref, xa_ref, xb_ref, mod_ref, g_ref, w_ref, rw_ref,
         xo_ref, h2_ref, info_ref, info_t_ref, cb_ref, tot_ref, wbf_ref, carry_ref) = refs
    else:
        (oc_ref, ol_ref, cv_ref, xa_ref, xb_ref, mod_ref, g_ref, w_ref, w13_ref, w2_ref,
         xo_ref, wbf_ref, h2_ref) = refs
    i = pl.program_id(0)

    @pl.when(i == 0)
    def _():
        wbf_ref[...] = w_ref[...].astype(BF16)

    half = TOK_TILE // 2
    for rows in (slice(0, half), slice(half, TOK_TILE)):
        o = jnp.where(i < P_TILES, oc_ref[rows, :], ol_ref[rows, :])
        m = (jnp.dot(o, wbf_ref[0:ATT_W, :], preferred_element_type=F32)
             + jnp.dot(cv_ref[rows, :], wbf_ref[ATT_W:, :], preferred_element_type=F32))
        x = jnp.where(i < P_TILES, xa_ref[rows, :], xb_ref[rows, :])
        xn = x + mod_ref[2:3, :] * _rms(m, g_ref[1:2, :])
        xo_ref[rows, :] = xn
        h2_ref[rows, :] = (_rms(xn, g_ref[2:3, :]) * (1.0 + mod_ref[4:5, :]) + mod_ref[3:4, :]).astype(BF16)

    if ffn == "routed":
        @pl.when(i == 0)
        def _():
            carry_ref[...] = jnp.zeros_like(carry_ref)

        _route_tile(h2_ref[...], rw_ref, info_ref, info_t_ref, cb_ref, tot_ref, carry_ref)
    else:
        h = h2_ref[...]
        gt = jnp.dot(h, w13_ref[:, :D_FF], preferred_element_type=F32)
        up = jnp.dot(h, w13_ref[:, D_FF:], preferred_element_type=F32)
        a = (gt * _sigmoid(gt) * up).astype(BF16)
        y = jnp.dot(a, w2_ref[...], preferred_element_type=F32)
        xo_ref[...] = xo_ref[...] + mod_ref[5:6, :] * _rms(y, g_ref[3:4, :])


def _outproj(l, o_ctx, o_lat, cv, x, mod, norm_g, w_out, router_w=None, dense_w=None):
    x_specs, x_args = _x_specs(x)
    out_shape = [jax.ShapeDtypeStruct((N_TOK, D_MODEL), F32)]
    out_specs = [pl.BlockSpec((TOK_TILE, D_MODEL), lambda i: (i, 0))]
    scratch = [pltpu.VMEM((D_MODEL, D_MODEL), BF16)]
    if dense_w is not None:
        extra_specs = [pl.BlockSpec((D_MODEL, 2 * D_FF), lambda i: (0, 0), pipeline_mode=pl.Buffered(1)),
                       pl.BlockSpec((D_FF, D_MODEL), lambda i: (0, 0), pipeline_mode=pl.Buffered(1))]
        extra_args = tuple(dense_w)
        scratch += [pltpu.VMEM((TOK_TILE, D_MODEL), BF16)]
    else:
        rw_pad = jnp.zeros((D_MODEL, LANES), BF16).at[:, :N_EXPERTS].set(router_w.astype(BF16))
        extra_specs, extra_args = [pl.BlockSpec((D_MODEL, LANES), lambda i: (0, 0))], (rw_pad,)
        out_shape += [jax.ShapeDtypeStruct((N_TOK, D_MODEL), BF16)]
        out_specs += [pl.BlockSpec((TOK_TILE, D_MODEL), lambda i: (i, 0))]
        out_shape += [jax.ShapeDtypeStruct((N_TOK, LANES), F32),
                      jax.ShapeDtypeStruct((N_TILES, 8, TOK_TILE), F32),
                      jax.ShapeDtypeStruct((N_TILES, 8, LANES), F32),
                      jax.ShapeDtypeStruct((8, LANES), F32)]
        out_specs += [pl.BlockSpec((TOK_TILE, LANES), lambda i: (i, 0)),
                      pl.BlockSpec((None, 8, TOK_TILE), lambda i: (i, 0, 0)),
                      pl.BlockSpec((None, 8, LANES), lambda i: (i, 0, 0)),
                      pl.BlockSpec((8, LANES), lambda i: (0, 0))]
        scratch += [pltpu.VMEM((8, LANES), F32)]
    return pl.pallas_call(
        functools.partial(_outproj_kernel, ffn="dense" if dense_w is not None else "routed"),
        out_shape=tuple(out_shape),
        grid=(N_TILES,),
        in_specs=[pl.BlockSpec((TOK_TILE, ATT_W), lambda i: (jnp.minimum(i, P_TILES - 1), 0)),
                  pl.BlockSpec((TOK_TILE, ATT_W), lambda i: (jnp.maximum(i - P_TILES, 0), 0)),
                  pl.BlockSpec((TOK_TILE, CONV_CH), lambda i: (i, 0))] + x_specs + [
                  pl.BlockSpec((None, None, 6, D_MODEL), lambda i: (l, _cond_id(i), 0, 0)),
                  pl.BlockSpec((None, 4, D_MODEL), lambda i: (l, 0, 0)),
                  pl.BlockSpec((None, D_MODEL, D_MODEL), lambda i: (l, 0, 0),
                               pipeline_mode=pl.Buffered(1))] + extra_specs,
        out_specs=tuple(out_specs),
        scratch_shapes=scratch,
        compiler_params=_cparams(1),
        name=f"outproj_{l}",
    )(o_ctx, o_lat, cv, *x_args, mod, norm_g, w_out, *extra_args)


def _route_tile(h, rw_ref, info_ref, info_t_ref, cb_ref, tot_ref, carry_ref):
    lane = lax.broadcasted_iota(jnp.int32, (TOK_TILE, LANES), 1)
    lanef = lane.astype(F32)
    logits = jnp.dot(h, rw_ref[...], preferred_element_type=F32)
    logits = jnp.where(lane < N_EXPERTS, logits, -jnp.inf)
    big = jnp.asarray(LANES, F32)
    m1 = jnp.max(logits, axis=-1, keepdims=True)
    e1 = jnp.min(jnp.where(logits == m1, lanef, big), axis=-1, keepdims=True)
    oh1 = lanef == e1
    rest = jnp.where(oh1, -jnp.inf, logits)
    m2 = jnp.max(rest, axis=-1, keepdims=True)
    e2 = jnp.min(jnp.where(rest == m2, lanef, big), axis=-1, keepdims=True)
    oh2 = lanef == e2
    ex = jnp.exp(m2 - m1)
    g1 = 1.0 / (1.0 + ex)
    g2 = ex / (1.0 + ex)

    oh = jnp.where(jnp.logical_or(oh1, oh2), 1.0, 0.0)
    r = lax.broadcasted_iota(jnp.int32, (TOK_TILE, TOK_TILE), 0)
    c = lax.broadcasted_iota(jnp.int32, (TOK_TILE, TOK_TILE), 1)
    tri = jnp.where(c <= r, 1.0, 0.0).astype(BF16)
    incl = jnp.dot(tri, oh.astype(BF16), preferred_element_type=F32)
    carry = carry_ref[0:1, :]
    excl = incl - oh + carry
    rank1 = jnp.sum(jnp.where(oh1, excl, 0.0), axis=-1, keepdims=True)
    rank2 = jnp.sum(jnp.where(oh2, excl, 0.0), axis=-1, keepdims=True)

    info = jnp.where(lane == 0, e1, 0.0)
    for k, col in enumerate((e2, rank1, rank2, g1, g2), start=1):
        info = jnp.where(lane == k, col, info)
    info_ref[...] = info
    info_t_ref[...] = info.T[0:8, :]

    cb_ref[...] = carry_ref[...]
    new_carry = carry + incl[TOK_TILE - 1:TOK_TILE, :]
    carry_ref[...] = jnp.broadcast_to(new_carry, carry_ref.shape)
    tot_ref[...] = jnp.broadcast_to(new_carry, tot_ref.shape)


def _sorted_pos(expert, rank, base_ref):
    start = jnp.zeros_like(rank)
    for e in range(N_EXPERTS):
        start = jnp.where(expert == float(e), base_ref[e].astype(F32), start)
    return start + rank


def _dispatch_kernel(clo_ref, cn_ref, base_ref, info_t_ref, x_ref, o_ref):
    r = pl.program_id(0)
    o_ref[...] = jnp.zeros_like(o_ref)
    rows = (r * DISPATCH_TILE + lax.broadcasted_iota(jnp.int32, (DISPATCH_TILE, TOK_TILE), 0)).astype(F32)

    def body(k, carry):
        c = clo_ref[r] + k
        it = info_t_ref[c]
        pos1 = _sorted_pos(it[0:1, :], it[2:3, :], base_ref)
        pos2 = _sorted_pos(it[1:2, :], it[3:4, :], base_ref)
        hit = jnp.logical_or(rows == pos1, rows == pos2)
        sel = jnp.where(hit, 1.0, 0.0).astype(BF16)
        off = pl.multiple_of(c * TOK_TILE, TOK_TILE)
        o_ref[...] += jnp.dot(sel, x_ref[pl.ds(off, TOK_TILE), :], preferred_element_type=F32).astype(BF16)
        return carry

    lax.fori_loop(0, cn_ref[r], body, 0)


def _dispatch(c_lo, c_n, base, info_t, h2):
    return pl.pallas_call(
        _dispatch_kernel,
        out_shape=jax.ShapeDtypeStruct((SORT_ROWS, D_MODEL), BF16),
        grid_spec=pltpu.PrefetchScalarGridSpec(
            num_scalar_prefetch=3,
            grid=(SORT_ROWS // DISPATCH_TILE,),
            in_specs=[pl.BlockSpec((N_TILES, 8, TOK_TILE), lambda r, *_: (0, 0, 0)),
                      pl.BlockSpec((N_TOK, D_MODEL), lambda r, *_: (0, 0),
                                   pipeline_mode=pl.Buffered(1))],
            out_specs=pl.BlockSpec((DISPATCH_TILE, D_MODEL), lambda r, *_: (r, 0))),
        compiler_params=_cparams(1),
        name="moe_dispatch",
    )(c_lo, c_n, base, info_t, h2)


TILE_UNUSED, TILE_HALF, TILE_FULL = 0, 1, 2


def _expert_weight_ring(c, r, n_c, te_ref, tf_ref, nx_ref, copies, cast):
    @pl.when(jnp.logical_and(c == 0, r == 0))
    def _():
        for cp in copies(te_ref[0], 0):
            cp.start()

    @pl.when(tf_ref[r] == 1)
    def _():
        for cp in copies(te_ref[r], c):
            cp.wait()
        cast()
        nr = nx_ref[r]
        nc = c + (nr <= r).astype(jnp.int32)

        @pl.when(nc < n_c)
        def _():
            for cp in copies(te_ref[nr], nc):
                cp.start()


def _moe_up_kernel(te_ref, tv_ref, tf_ref, nx_ref, x_ref, w_hbm, h_ref, wst_ref, wbf_ref, sem_ref, *, i_moe):
    f = pl.program_id(0)
    r = pl.program_id(1)

    def copies(e, ft):
        out = []
        for part in range(2):
            col = pl.multiple_of((part * UP_TILES + ft) * UP_TILE, LANES)
            out.append(pltpu.make_async_copy(w_hbm.at[i_moe, e, :, pl.ds(col, UP_TILE)],
                                             wst_ref.at[part], sem_ref.at[part]))
        return out

    def cast():
        wbf_ref[...] = wst_ref[...].astype(BF16)

    _expert_weight_ring(f, r, UP_TILES, te_ref, tf_ref, nx_ref, copies, cast)

    def hidden(x):
        gt = jnp.dot(x, wbf_ref[0], preferred_element_type=F32)
        up = jnp.dot(x, wbf_ref[1], preferred_element_type=F32)
        return (gt * _sigmoid(gt) * up).astype(BF16)

    half = ROW_TILE // 2

    @pl.when(tv_ref[r] == TILE_FULL)
    def _():
        h_ref[...] = hidden(x_ref[...])

    @pl.when(tv_ref[r] == TILE_HALF)
    def _():
        h_ref[0:half, :] = hidden(x_ref[0:half, :])
        h_ref[half:, :] = jnp.zeros((half, UP_TILE), BF16)

    @pl.when(tv_ref[r] == TILE_UNUSED)
    def _():
        h_ref[...] = jnp.zeros_like(h_ref)


def _moe_up(i_moe, tile_e, tile_valid, tile_first, tile_next, xs, w13):
    return pl.pallas_call(
        functools.partial(_moe_up_kernel, i_moe=i_moe),
        out_shape=jax.ShapeDtypeStruct((SORT_ROWS, D_FF_EXPERT), BF16),
        grid_spec=pltpu.PrefetchScalarGridSpec(
            num_scalar_prefetch=4,
            grid=(UP_TILES, SORT_TILES),
            in_specs=[pl.BlockSpec((ROW_TILE, D_MODEL), lambda f, r, *_: (r, 0)),
                      pl.BlockSpec(memory_space=pl.ANY)],
            out_specs=pl.BlockSpec((ROW_TILE, UP_TILE), lambda f, r, *_: (r, f)),
            scratch_shapes=[pltpu.VMEM((2, D_MODEL, UP_TILE), F32),
                            pltpu.VMEM((2, D_MODEL, UP_TILE), BF16),
                            pltpu.SemaphoreType.DMA((2,))]),
        compiler_params=_cparams(2),
        name="moe_up",
    )(tile_e, tile_valid, tile_first, tile_next, xs, w13)


def _moe_down_kernel(te_ref, tv_ref, tf_ref, nx_ref, h_ref, w_hbm, y_ref, wst_ref, wbf_ref, sem_ref, *, i_moe):
    n = pl.program_id(0)
    r = pl.program_id(1)

    def copies(e, nt):
        col = pl.multiple_of(nt * DOWN_TILE, LANES)
        return [pltpu.make_async_copy(w_hbm.at[i_moe, e, :, pl.ds(col, DOWN_TILE)], wst_ref, sem_ref.at[0])]

    def cast():
        wbf_ref[...] = wst_ref[...].astype(BF16)

    _expert_weight_ring(n, r, D_MODEL // DOWN_TILE, te_ref, tf_ref, nx_ref, copies, cast)

    half = ROW_TILE // 2

    @pl.when(tv_ref[r] == TILE_FULL)
    def _():
        y_ref[...] = jnp.dot(h_ref[...], wbf_ref[...], preferred_element_type=F32).astype(BF16)

    @pl.when(tv_ref[r] == TILE_HALF)
    def _():
        y_ref[0:half, :] = jnp.dot(h_ref[0:half, :], wbf_ref[...], preferred_element_type=F32).astype(BF16)
        y_ref[half:, :] = jnp.zeros((half, DOWN_TILE), BF16)

    @pl.when(tv_ref[r] == TILE_UNUSED)
    def _():
        y_ref[...] = jnp.zeros_like(y_ref)


def _moe_down(i_moe, tile_e, tile_valid, tile_first, tile_next, hs, w2):
    return pl.pallas_call(
        functools.partial(_moe_down_kernel, i_moe=i_moe),
        out_shape=jax.ShapeDtypeStruct((SORT_ROWS, D_MODEL), BF16),
        grid_spec=pltpu.PrefetchScalarGridSpec(
            num_scalar_prefetch=4,
            grid=(D_MODEL // DOWN_TILE, SORT_TILES),
            in_specs=[pl.BlockSpec((ROW_TILE, D_FF_EXPERT), lambda n, r, *_: (r, 0)),
                      pl.BlockSpec(memory_space=pl.ANY)],
            out_specs=pl.BlockSpec((ROW_TILE, DOWN_TILE), lambda n, r, *_: (r, n)),
            scratch_shapes=[pltpu.VMEM((D_FF_EXPERT, DOWN_TILE), F32),
                            pltpu.VMEM((D_FF_EXPERT, DOWN_TILE), BF16),
                            pltpu.SemaphoreType.DMA((1,))]),
        compiler_params=_cparams(2),
        name="moe_down",
    )(tile_e, tile_valid, tile_first, tile_next, hs, w2)


def _window_copy(y_hbm, ybuf_ref, sem_ref, src, buf, slot):
    return pltpu.make_async_copy(y_hbm.at[pl.ds(src, WIN), :],
                                 ybuf_ref.at[buf, pl.ds(slot * WIN, WIN), :], sem_ref.at[buf, slot])


def _start_windows(src_ref, y_hbm, ybuf_ref, sem_ref, tile, buf):
    for s in range(N_WIN):
        src = pl.multiple_of(src_ref[tile * N_WIN + s], BF16_SUBLANES)
        _window_copy(y_hbm, ybuf_ref, sem_ref, src, buf, s).start()


def _combine_kernel(src_ref, lo_ref, hi_ref, base_ref, info_ref, x_ref, mod_ref, g_ref, y_hbm,
                    yp_ref, ys_ref, ybuf_ref, sem_ref):
    j = pl.program_id(0)
    buf = j % 2

    @pl.when(j == 0)
    def _():
        _start_windows(src_ref, y_hbm, ybuf_ref, sem_ref, 0, 0)

    @pl.when(j + 1 < N_TILES)
    def _():
        _start_windows(src_ref, y_hbm, ybuf_ref, sem_ref, j + 1, 1 - buf)

    info = info_ref[...]
    pos1 = _sorted_pos(info[:, 0:1], info[:, 2:3], base_ref)
    pos2 = _sorted_pos(info[:, 1:2], info[:, 3:4], base_ref)
    g1, g2 = info[:, 4:5], info[:, 5:6]
    iota = lax.broadcasted_iota(jnp.int32, (1, WIN), 1)
    ids = []
    for s in range(N_WIN):
        row = src_ref[j * N_WIN + s] + iota
        ok = jnp.logical_and(row >= lo_ref[j * N_WIN + s], row < hi_ref[j * N_WIN + s])
        ids.append(jnp.where(ok, row, -1))
    row_id = jnp.concatenate(ids, axis=1).astype(F32)
    sel = (jnp.where(pos1 == row_id, g1, 0.0) + jnp.where(pos2 == row_id, g2, 0.0)).astype(BF16)

    for s in range(N_WIN):
        _window_copy(y_hbm, ybuf_ref, sem_ref, 0, buf, s).wait()
    ffn = jnp.dot(sel, ybuf_ref[buf], preferred_element_type=F32)
    out = x_ref[...] + mod_ref[5:6, :] * _rms(ffn, g_ref[3:4, :])

    @pl.when(j < P_TILES)
    def _():
        yp_ref[...] = out

    @pl.when(j >= P_TILES)
    def _():
        ys_ref[...] = out


def _combine(l, win_src, win_lo, win_hi, base, info, x, mod, norm_g, ys):
    return pl.pallas_call(
        _combine_kernel,
        out_shape=(jax.ShapeDtypeStruct((N_PROMPT, D_MODEL), F32),
                   jax.ShapeDtypeStruct((N_SAMPLE, D_MODEL), F32)),
        grid_spec=pltpu.PrefetchScalarGridSpec(
            num_scalar_prefetch=4,
            grid=(N_TILES,),
            in_specs=[pl.BlockSpec((TOK_TILE, LANES), lambda i, *_: (i, 0)),
                      pl.BlockSpec((TOK_TILE, D_MODEL), lambda i, *_: (i, 0)),
                      pl.BlockSpec((None, None, 6, D_MODEL), lambda i, *_: (l, _cond_id(i), 0, 0)),
                      pl.BlockSpec((None, 4, D_MODEL), lambda i, *_: (l, 0, 0)),
                      pl.BlockSpec(memory_space=pl.ANY)],
            out_specs=(pl.BlockSpec((TOK_TILE, D_MODEL), lambda i, *_: (jnp.minimum(i, P_TILES - 1), 0)),
                       pl.BlockSpec((TOK_TILE, D_MODEL), lambda i, *_: (jnp.maximum(i - P_TILES, 0), 0))),
            scratch_shapes=[pltpu.VMEM((2, N_WIN * WIN, D_MODEL), BF16),
                            pltpu.SemaphoreType.DMA((2, N_WIN))]),
        compiler_params=_cparams(1),
        name="moe_combine",
    )(win_src, win_lo, win_hi, base, info, x, mod, norm_g, ys)


def _moe_ffn(l, i_moe, h2, x, mod, norm_g, routing, moe_w13, moe_w2):
    info, info_t, cb, tot = routing

    ar8 = jnp.arange(N_EXPERTS, dtype=jnp.int32)

    def at8(vec, idx):
        return jnp.sum(jnp.where(idx[..., None] == ar8, vec, 0), axis=-1)

    def cumsum8(a):
        return jnp.sum(jnp.where(ar8[None, :] <= ar8[:, None], a[..., None, :], 0), axis=-1)

    counts = tot[0, :N_EXPERTS].astype(jnp.int32)
    padded = (counts + ROW_TILE - 1) // ROW_TILE * ROW_TILE
    seg_end = cumsum8(padded)
    base = (seg_end - padded).astype(jnp.int32)
    cbx = cb[:, 0, :N_EXPERTS].astype(jnp.int32)
    cb_end = jnp.concatenate([cbx[1:], counts[None, :]], axis=0)

    def tile_segments(n_tiles, rows):
        row0 = jnp.arange(n_tiles, dtype=jnp.int32) * rows
        e = jnp.minimum(jnp.sum(seg_end[None, :] <= row0[:, None], axis=1), N_EXPERTS - 1).astype(jnp.int32)
        k0 = row0 - at8(base, e)
        return e, k0, jnp.logical_and(k0 >= 0, k0 < at8(counts, e))

    tile_e, k0, tile_valid = tile_segments(SORT_TILES, ROW_TILE)
    tile_first = jnp.logical_and(tile_valid, k0 == 0)
    d_e, d_k0, d_valid = tile_segments(SORT_ROWS // DISPATCH_TILE, DISPATCH_TILE)
    d_kend = jnp.minimum(d_k0 + DISPATCH_TILE, at8(counts, d_e))
    cb_t = jnp.sum(jnp.where(d_e[:, None, None] == ar8[None, :, None], cbx.T[None, :, :], 0), axis=1)
    c_lo = jnp.sum(cb_t <= d_k0[:, None], axis=1) - 1
    c_hi = jnp.sum(cb_t < d_kend[:, None], axis=1) - 1
    c_n = jnp.where(d_valid, c_hi - c_lo + 1, 0).astype(jnp.int32)
    c_lo = jnp.where(d_valid, c_lo, 0).astype(jnp.int32)
    tile_idx = jnp.arange(SORT_TILES, dtype=jnp.int32)
    first_idx = jnp.where(tile_first, tile_idx, SORT_TILES)
    later = jnp.where(tile_idx[None, :] > tile_idx[:, None], first_idx[None, :], SORT_TILES)
    tile_next = jnp.min(later, axis=1)
    tile_next = jnp.where(tile_next >= SORT_TILES, 0, tile_next).astype(jnp.int32)
    tile_rows = at8(counts, tile_e) - k0
    tile_valid = jnp.where(tile_valid, jnp.where(tile_rows <= ROW_TILE // 2, TILE_HALF, TILE_FULL),
                           TILE_UNUSED).astype(jnp.int32)
    tile_first = tile_first.astype(jnp.int32)

    seg_lo = base[None, :] + cbx
    seg_hi = base[None, :] + cb_end
    seg_n = seg_hi - seg_lo
    w0 = seg_lo // BF16_SUBLANES * BF16_SUBLANES
    nw = jnp.where(seg_n > 0, (seg_lo - w0 + seg_n + WIN - 1) // WIN, 0)
    nw_end = cumsum8(nw)
    slot = jnp.arange(N_WIN, dtype=jnp.int32)
    slot_e = jnp.minimum(jnp.sum(nw_end[:, None, :] <= slot[None, :, None], axis=2), N_EXPERTS - 1)
    pick = slot_e[:, :, None] == jnp.arange(N_EXPERTS, dtype=jnp.int32)[None, None, :]
    take = lambda a: jnp.sum(jnp.where(pick, a[:, None, :], 0), axis=2)
    slot_k = slot[None, :] - (take(nw_end) - take(nw))
    slot_ok = slot[None, :] < nw_end[:, -1:]
    slot_src = take(w0) + WIN * slot_k
    win_src = jnp.where(slot_ok, slot_src, 0).astype(jnp.int32).reshape(-1)
    win_lo = jnp.where(slot_ok, take(seg_lo), 0).astype(jnp.int32).reshape(-1)
    win_hi = jnp.where(slot_ok, take(seg_hi), 0).astype(jnp.int32).reshape(-1)

    xs = _dispatch(c_lo, c_n, base, info_t, h2)
    hs = _moe_up(i_moe, tile_e, tile_valid, tile_first, tile_next, xs, moe_w13)
    ys = _moe_down(i_moe, tile_e, tile_valid, tile_first, tile_next, hs, moe_w2)
    return _combine(l, win_src, win_lo, win_hi, base, info, x, mod, norm_g, ys)


def kernel(x_prompt, x_sample, cache_k, cache_v, c, c_ctx, w_ada, b_ada, norm_g, w_in, w_out,
           lam_params, subln_g, dw_weight, dw_bias, conv_ln_g, conv_ln_b, dense_w13, dense_w2,
           router_w, moe_w13, moe_w2):
    x = (x_prompt.reshape(N_PROMPT, D_MODEL), x_sample.reshape(N_SAMPLE, D_MODEL))
    cond8 = jnp.zeros((8, D_MODEL), F32).at[0].set(c_ctx).at[1:1 + DEC_BATCH].set(c)
    mod = _ada_modulation(cond8, w_ada, b_ada).reshape(DEPTH, 8, 6, D_MODEL)
    rope = _rope_tables()
    ck = cache_k.reshape(DEC_BATCH, DEPTH, PAST_LEN * ATT_HEADS, V_DIM)
    cv = cache_v.reshape(DEC_BATCH, DEPTH, PAST_LEN * ATT_HEADS, V_DIM)

    caches = None
    for l in range(DEPTH):
        lam_init = 0.8 - 0.6 * math.exp(-0.3 * l)
        qkv, cvo, *caches = _mixer_in(l, x, mod, norm_g, w_in, rope,
                                      (dw_weight, dw_bias, conv_ln_g, conv_ln_b), caches)
        o_ctx, o_lat = _attention(l, lam_init, lam_params, subln_g, qkv, ck, cv)
        i = l // 2
        if l % 2 == 0:
            (x,) = _outproj(l, o_ctx, o_lat, cvo, x, mod, norm_g, w_out,
                            dense_w=(dense_w13[i].astype(BF16), dense_w2[i].astype(BF16)))
        else:
            x, h2, *routing = _outproj(l, o_ctx, o_lat, cvo, x, mod, norm_g, w_out, router_w=router_w[i])
            x = _moe_ffn(l, i, h2, x, mod, norm_g, routing, moe_w13, moe_w2)

    xp, xs = x if isinstance(x, tuple) else (x[:N_PROMPT], x[N_PROMPT:])
    new_k, new_v = (a.reshape(BATCH, DEPTH, SEQ, ATT_HEADS, V_DIM) for a in caches)
    return (xp.reshape(BATCH, SEQ, D_MODEL), xs.reshape(DEC_BATCH, DEC_SEQ, D_MODEL), new_k, new_v)
```

```python
import functools
import math

import jax
import jax.numpy as jnp
import numpy as np
from jax import lax
from jax.experimental import pallas as pl
from jax.experimental.pallas import tpu as pltpu

F32 = jnp.float32
BF16 = jnp.bfloat16

D_MODEL = 1024
BATCH = 32
SEQ = 256
DEPTH = 2
DEC_BATCH = 2
DEC_SEQ = 2048
PAST_LEN = 512
GRID_W = 64
ATT_HEADS = 4
QK_DIM = 64
V_DIM = 128
ATT_W = ATT_HEADS * V_DIM
IN_W = 5 * ATT_W
CONV_CH = 512
CONV_WIDTH = 31
D_FF = 2816
N_EXPERTS = 8
D_FF_EXPERT = 3584
ROPE_THETA = 10000.0
NORM_EPS = 1e-6
LN_EPS = 1e-5

N_PROMPT = BATCH * SEQ
N_SAMPLE = DEC_BATCH * DEC_SEQ
N_TOK = N_PROMPT + N_SAMPLE

LANES = 128
BF16_SUBLANES = 16
MXU_TILE = 256
VMEM_BYTES = 64 * 1024 * 1024
VMEM_LIMIT = VMEM_BYTES - 8 * 1024 * 1024

TOK_TILE = 512
N_TILES = N_TOK // TOK_TILE
P_TILES = N_PROMPT // TOK_TILE
S_TILES_PER_BATCH = DEC_SEQ // TOK_TILE

ROW_TILE = 512
SORT_TILES = (2 * N_TOK + N_EXPERTS * ROW_TILE) // ROW_TILE + 1
SORT_ROWS = SORT_TILES * ROW_TILE
DISPATCH_TILE = 256
UP_TILE = 7 * MXU_TILE
UP_TILES = D_FF_EXPERT // UP_TILE
DOWN_TILE = 4 * MXU_TILE
WIN = 128
N_WIN = (2 * TOK_TILE + N_EXPERTS * (BF16_SUBLANES - 1 + WIN - 1)) // WIN

assert D_FF_EXPERT % UP_TILE == 0 and D_MODEL % DOWN_TILE == 0 and D_FF % MXU_TILE == 0
assert ROW_TILE % DISPATCH_TILE == 0 and N_TOK % TOK_TILE == 0 and N_PROMPT % TOK_TILE == 0
assert DEPTH == 2, "layer 0 zero-fills exactly one later cache slab; the last layer must be the routed one"


def _cparams(n_axes):
    return pltpu.CompilerParams(dimension_semantics=("arbitrary",) * n_axes,
                                vmem_limit_bytes=VMEM_LIMIT)


def _cond_id(i):
    return jnp.where(i < P_TILES, 0, 1 + (i - P_TILES) // S_TILES_PER_BATCH)


def _sigmoid(x):
    return 1.0 / (1.0 + jnp.exp(-x))


def _rms(x, g):
    return x * lax.rsqrt(jnp.mean(x * x, axis=-1, keepdims=True) + NORM_EPS) * g


def _ada_kernel(c_ref, w_ref, b_ref, o_ref):
    c = c_ref[...]
    s = (c * _sigmoid(c)).astype(BF16)
    o_ref[...] = jnp.dot(s, w_ref[...].astype(BF16), preferred_element_type=F32) + b_ref[...]


def _ada_modulation(cond8, w_ada, b_ada):
    tn = 1536
    n = 6 * D_MODEL
    return pl.pallas_call(
        _ada_kernel,
        out_shape=jax.ShapeDtypeStruct((DEPTH, 8, n), F32),
        grid=(DEPTH, n // tn),
        in_specs=[pl.BlockSpec((8, D_MODEL), lambda l, j: (0, 0)),
                  pl.BlockSpec((None, D_MODEL, tn), lambda l, j: (l, 0, j)),
                  pl.BlockSpec((None, 1, tn), lambda l, j: (l, 0, j))],
        out_specs=pl.BlockSpec((None, 8, tn), lambda l, j: (l, 0, j)),
        compiler_params=_cparams(2),
        name="ada_modulation",
    )(cond8, w_ada, b_ada.reshape(DEPTH, 1, n))


def _tile_x(xa_ref, xb_ref):
    return jnp.where(pl.program_id(0) < P_TILES, xa_ref[...], xb_ref[...])


def _x_specs(x):
    last = N_TILES - 1
    if isinstance(x, tuple):
        xa, xb = x
        b_idx = lambda i, *_: (jnp.maximum(jnp.minimum(i, last) - P_TILES, 0), 0)
    else:
        xa = xb = x
        b_idx = lambda i, *_: (jnp.maximum(jnp.minimum(i, last), P_TILES), 0)
    a_idx = lambda i, *_: (jnp.minimum(i, P_TILES - 1), 0)
    return [pl.BlockSpec((TOK_TILE, D_MODEL), a_idx), pl.BlockSpec((TOK_TILE, D_MODEL), b_idx)], (xa, xb)


ROT_SPAN = QK_DIM // 4
SEQ_PER_TILE = TOK_TILE // SEQ
CACHE_ROWS = SEQ * ATT_HEADS


CONV_LAG = 2
CONV_RING = 3
CONV_TILE = 256
CONV_HALO = 16
CONV_SUB = 32


def _conv_pass(upad_ref, w_ref, bias_ref, lg_ref, lb_ref, o_ref, row0):
    rows = CONV_TILE + 2 * CONV_HALO - 8
    for s in range(1, 8):
        upad_ref[s, 0:rows, :] = upad_ref[0, s:s + rows, :]
    first_tap = CONV_HALO - CONV_WIDTH // 2
    groups = CONV_SUB // 8
    for t in range(CONV_TILE // CONV_SUB):
        base = t * CONV_SUB
        acc = jnp.zeros((groups, 8, CONV_CH), F32)
        for j in range(CONV_WIDTH):
            lo = base + (first_tap + j) // 8 * 8
            taps = upad_ref[(first_tap + j) % 8, lo:lo + CONV_SUB, :].reshape(groups, 8, CONV_CH)
            acc = acc + taps * w_ref[j][None]
        y = acc.reshape(CONV_SUB, CONV_CH) + bias_ref[...]
        mu = jnp.mean(y, axis=-1, keepdims=True)
        yc = y - mu
        var = jnp.mean(yc * yc, axis=-1, keepdims=True)
        z = yc * lax.rsqrt(var + LN_EPS) * lg_ref[...] + lb_ref[...]
        o_ref[row0 + base:row0 + base + CONV_SUB, :] = (z * _sigmoid(z)).astype(BF16)


def _mixer_in_kernel(*refs, layer):
    n_in = 12 + (2 if layer else 0)
    (xa_ref, xb_ref, mod_ref, g_ref, w_ref, cos_ref, sina_ref, sinb_ref,
     dw_ref, db_ref, lg_ref, lb_ref) = refs[:12]
    qkv_ref, cv_ref, kc_ref, vc_ref, wbf_ref, ring_ref, glu_ref, pj_ref, upad_ref, taps_ref = refs[n_in:]
    i = pl.program_id(0)
    t = jnp.minimum(i, N_TILES - 1)

    @pl.when(i == 0)
    def _():
        wbf_ref[...] = w_ref[...].astype(BF16)
        ring_ref[...] = jnp.zeros_like(ring_ref)
        glu_ref[...] = jnp.zeros_like(glu_ref)
        for j in range(CONV_WIDTH):
            taps_ref[j] = jnp.broadcast_to(dw_ref[j:j + 1, :], (8, CONV_CH))

    ring_ref[(i - 1) % CONV_RING] = glu_ref[...]

    h = _rms(_tile_x(xa_ref, xb_ref), g_ref[0:1, :]) * (1.0 + mod_ref[1:2, :]) + mod_ref[0:1, :]
    hb = h.astype(BF16)

    def proj(part):
        return jnp.dot(hb, wbf_ref[:, part * ATT_W:(part + 1) * ATT_W], preferred_element_type=F32)

    for part in range(3):
        pj_ref[:, part * ATT_W:(part + 1) * ATT_W] = proj(part)
    glu_ref[...] = proj(3) * _sigmoid(proj(4))

    j = i - CONV_LAG
    latent = j >= P_TILES
    q = (j - P_TILES) % S_TILES_PER_BATCH
    has_prev = jnp.logical_and(latent, q != 0)
    has_next = jnp.logical_and(latent, q != S_TILES_PER_BATCH - 1)
    cur = j % CONV_RING
    prev_tail = ring_ref[(j - 1) % CONV_RING, TOK_TILE - CONV_HALO:, :]
    next_head = ring_ref[(j + 1) % CONV_RING, 0:CONV_HALO, :]
    for s in range(TOK_TILE // CONV_TILE):
        lo = s * CONV_TILE
        if s == 0:
            before = jnp.where(has_prev, prev_tail, 0.0)
        else:
            before = jnp.where(latent, ring_ref[cur, lo - CONV_HALO:lo, :], 0.0)
        if s == TOK_TILE // CONV_TILE - 1:
            after = jnp.where(has_next, next_head, 0.0)
        else:
            after = jnp.where(latent, ring_ref[cur, lo + CONV_TILE:lo + CONV_TILE + CONV_HALO, :], 0.0)
        upad_ref[0, 0:CONV_HALO, :] = before
        upad_ref[0, CONV_HALO:CONV_HALO + CONV_TILE, :] = ring_ref[cur, lo:lo + CONV_TILE, :]
        upad_ref[0, CONV_HALO + CONV_TILE:, :] = after
        _conv_pass(upad_ref, taps_ref, db_ref, lg_ref, lb_ref, cv_ref, lo)

    qkv_ref[:, 2 * ATT_W:] = pj_ref[:, 2 * ATT_W:].astype(BF16)

    @pl.when(t < P_TILES)
    def _():
        qkv_ref[:, :2 * ATT_W] = pj_ref[:, :2 * ATT_W].astype(BF16)
        for ref, col0 in ((kc_ref, ATT_W), (vc_ref, 2 * ATT_W)):
            for s in range(SEQ_PER_TILE):
                for hd in range(ATT_HEADS):
                    val = pj_ref[SEQ * s:SEQ * (s + 1), col0 + V_DIM * hd:col0 + V_DIM * (hd + 1)]
                    rows = pl.ds(hd, SEQ, stride=ATT_HEADS)
                    if layer == 0:
                        ref[s, 0, rows, :] = val
                    else:
                        ref[s, rows, :] = val
            if layer == 0:
                ref[:, 1:] = jnp.zeros((SEQ_PER_TILE, DEPTH - 1, CACHE_ROWS, V_DIM), F32)

    @pl.when(t >= P_TILES)
    def _():
        cos = cos_ref[...]
        sina = sina_ref[...]
        sinb = sinb_ref[...]
        for c in range(2 * ATT_W // V_DIM):
            xg = pj_ref[:, V_DIM * c:V_DIM * (c + 1)]
            fwd = pltpu.roll(xg, V_DIM - ROT_SPAN, 1)
            bwd = pltpu.roll(xg, ROT_SPAN, 1)
            qkv_ref[:, V_DIM * c:V_DIM * (c + 1)] = (xg * cos + fwd * sina + bwd * sinb).astype(BF16)


def _mixer_in(l, x, mod, norm_g, w_in, rope, conv_params, caches):
    cos, sina, sinb = rope
    dw_w, dw_b, ln_g, ln_b = conv_params
    x_specs, x_args = _x_specs(x)

    def tile(i):
        return jnp.minimum(i, N_TILES - 1)

    def rope_idx(i):
        return (jnp.maximum(tile(i) - P_TILES, 0) % S_TILES_PER_BATCH, 0)

    def vec(a):
        return a.reshape(DEPTH, 1, CONV_CH)

    cache_shape = jax.ShapeDtypeStruct((BATCH, DEPTH, CACHE_ROWS, V_DIM), F32)
    if l == 0:
        cache_spec = pl.BlockSpec((SEQ_PER_TILE, DEPTH, CACHE_ROWS, V_DIM),
                                  lambda i: (jnp.minimum(i, P_TILES - 1), 0, 0, 0))
        extra_specs, extra_args, aliases = [], (), {}
    else:
        cache_spec = pl.BlockSpec((SEQ_PER_TILE, None, CACHE_ROWS, V_DIM),
                                  lambda i: (jnp.minimum(i, P_TILES - 1), l, 0, 0))
        extra_specs = [pl.BlockSpec(memory_space=pl.ANY)] * 2
        extra_args = tuple(caches)
        aliases = {12: 2, 13: 3}

    return pl.pallas_call(
        functools.partial(_mixer_in_kernel, layer=l),
        out_shape=(jax.ShapeDtypeStruct((N_TOK, 3 * ATT_W), BF16),
                   jax.ShapeDtypeStruct((N_TOK, CONV_CH), BF16), cache_shape, cache_shape),
        grid=(N_TILES + CONV_LAG,),
        in_specs=x_specs + [
            pl.BlockSpec((None, None, 6, D_MODEL), lambda i: (l, _cond_id(tile(i)), 0, 0)),
            pl.BlockSpec((None, 4, D_MODEL), lambda i: (l, 0, 0)),
            pl.BlockSpec((None, D_MODEL, IN_W), lambda i: (l, 0, 0), pipeline_mode=pl.Buffered(1)),
            pl.BlockSpec((TOK_TILE, V_DIM), rope_idx),
            pl.BlockSpec((TOK_TILE, V_DIM), rope_idx),
            pl.BlockSpec((TOK_TILE, V_DIM), rope_idx),
            pl.BlockSpec((None, CONV_WIDTH, CONV_CH), lambda i: (l, 0, 0)),
            pl.BlockSpec((None, 1, CONV_CH), lambda i: (l, 0, 0)),
            pl.BlockSpec((None, 1, CONV_CH), lambda i: (l, 0, 0)),
            pl.BlockSpec((None, 1, CONV_CH), lambda i: (l, 0, 0))] + extra_specs,
        out_specs=(pl.BlockSpec((TOK_TILE, 3 * ATT_W), lambda i: (tile(i), 0)),
                   pl.BlockSpec((TOK_TILE, CONV_CH), lambda i: (jnp.maximum(i - CONV_LAG, 0), 0)),
                   cache_spec, cache_spec),
        scratch_shapes=[pltpu.VMEM((D_MODEL, IN_W), BF16),
                        pltpu.VMEM((CONV_RING, TOK_TILE, CONV_CH), F32),
                        pltpu.VMEM((TOK_TILE, CONV_CH), F32),
                        pltpu.VMEM((TOK_TILE, 3 * ATT_W), F32),
                        pltpu.VMEM((8, CONV_TILE + 2 * CONV_HALO, CONV_CH), F32),
                        pltpu.VMEM((CONV_WIDTH, 8, CONV_CH), F32)],
        input_output_aliases=aliases,
        compiler_params=_cparams(1),
        name=f"mixer_in_{l}",
    )(*x_args, mod, norm_g, w_in, cos, sina, sinb, dw_w, vec(dw_b), vec(ln_g), vec(ln_b), *extra_args)


def _rope_tables():
    rows = DEC_SEQ // GRID_W
    row_pos = np.repeat(np.arange(rows, dtype=np.float64), GRID_W)
    col_pos = np.tile(np.arange(GRID_W, dtype=np.float64), rows)
    half = QK_DIM // 2
    inv_freq = 1.0 / (ROPE_THETA ** (np.arange(0, half, 2, dtype=np.float64) / half))
    ang_r = row_pos[:, None] * inv_freq
    ang_c = col_pos[:, None] * inv_freq
    ang = np.concatenate([ang_r, ang_r, ang_c, ang_c], axis=-1)
    cos = np.tile(np.cos(ang), (1, 2)).astype(np.float32)
    sin = np.tile(np.sin(ang), (1, 2)).astype(np.float32)
    first = (np.arange(V_DIM) % (2 * ROT_SPAN)) < ROT_SPAN
    sina = np.where(first[None, :], -sin, 0.0).astype(np.float32)
    sinb = np.where(first[None, :], 0.0, sin).astype(np.float32)
    return jnp.asarray(cos), jnp.asarray(sina), jnp.asarray(sinb)


def _attn_kernel(*refs, lam_init, has_ext):
    if has_ext:
        lamp_ref, sub_ref, q_ref, k_ref, v_ref, ke_ref, ve_ref, o_ref = refs
    else:
        lamp_ref, sub_ref, q_ref, k_ref, v_ref, o_ref = refs
    lp = lamp_ref[...]
    lam = (jnp.exp(jnp.sum(lp[0:1] * lp[1:2], axis=-1, keepdims=True))
           - jnp.exp(jnp.sum(lp[2:3] * lp[3:4], axis=-1, keepdims=True)) + lam_init)
    lane = lax.broadcasted_iota(jnp.int32, (1, V_DIM), 1)
    nt = (((1,), (1,)), ((), ()))
    scale = QK_DIM ** -0.5
    map_scale = [jnp.where(lane < QK_DIM, scale, 0.0).astype(BF16),
                 jnp.where(lane < QK_DIM, 0.0, scale).astype(BF16)]

    tq = q_ref.shape[0]
    head_cols = [slice(V_DIM * hd, V_DIM * (hd + 1)) for hd in range(ATT_HEADS)]

    def q_map(hd, m):
        return q_ref[:, head_cols[hd]] * map_scale[m]

    def with_ones(v):
        return jnp.concatenate([v, jnp.ones_like(v)], axis=1)

    outs = []
    if has_ext:
        for hd in range(ATT_HEADS):
            kh = k_ref[:, head_cols[hd]]
            v_aug = with_ones(v_ref[:, head_cols[hd]])
            head_rows = pl.ds(hd, PAST_LEN, stride=ATT_HEADS)
            keh = ke_ref[head_rows, :].astype(BF16)
            ve_aug = with_ones(ve_ref[head_rows, :].astype(BF16))
            ratio = []
            for m in range(2):
                qm = q_map(hd, m)
                s = lax.dot_general(qm, kh, nt, preferred_element_type=F32)
                se = lax.dot_general(qm, keh, nt, preferred_element_type=F32)
                mx = jnp.maximum(jnp.max(s, axis=-1, keepdims=True), jnp.max(se, axis=-1, keepdims=True))
                pv = (jnp.dot(jnp.exp((s - mx).astype(BF16)), v_aug, preferred_element_type=F32)
                      + jnp.dot(jnp.exp((se - mx).astype(BF16)), ve_aug, preferred_element_type=F32))
                ratio.append(pv[:, :V_DIM] / pv[:, V_DIM:])
            outs.append(ratio[0] - lam * ratio[1])
    else:
        pairs = [(hd, m) for hd in range(ATT_HEADS) for m in range(2)]
        s = jnp.concatenate([lax.dot_general(q_map(hd, m), k_ref[:, head_cols[hd]], nt,
                                             preferred_element_type=F32) for hd, m in pairs], axis=0)
        p = jnp.exp((s - jnp.max(s, axis=-1, keepdims=True)).astype(BF16))
        for hd in range(ATT_HEADS):
            pv = jnp.dot(p[2 * tq * hd:2 * tq * (hd + 1)], with_ones(v_ref[:, head_cols[hd]]),
                         preferred_element_type=F32)
            ratio = pv[:, :V_DIM] / pv[:, V_DIM:]
            outs.append(ratio[:tq] - lam * ratio[tq:])

    heads = [(_rms(o, sub_ref[...]) * (1.0 - lam_init)).astype(BF16) for o in outs]
    o_ref[...] = jnp.concatenate(heads, axis=1)


def _attention(l, lam_init, lam_params, subln_g, qkvg, cache_k, cache_v):
    small = [pl.BlockSpec((None, 4, QK_DIM), lambda *_: (l, 0, 0)),
             pl.BlockSpec((None, 1, V_DIM), lambda *_: (l, 0, 0))]
    sub3 = subln_g.reshape(DEPTH, 1, V_DIM)

    o_ctx = pl.pallas_call(
        functools.partial(_attn_kernel, lam_init=lam_init, has_ext=False),
        out_shape=jax.ShapeDtypeStruct((N_PROMPT, ATT_W), BF16),
        grid=(BATCH,),
        in_specs=small + [pl.BlockSpec((SEQ, ATT_W), lambda b: (b, 0)),
                          pl.BlockSpec((SEQ, ATT_W), lambda b: (b, 1)),
                          pl.BlockSpec((SEQ, ATT_W), lambda b: (b, 2))],
        out_specs=pl.BlockSpec((SEQ, ATT_W), lambda b: (b, 0)),
        compiler_params=_cparams(1),
        name=f"attn_ctx_{l}",
    )(lam_params, sub3, qkvg, qkvg, qkvg)

    tq = 256
    q_tiles = DEC_SEQ // tq
    q_base = N_PROMPT // tq
    kv_base = N_PROMPT // DEC_SEQ

    def q_idx(b, i):
        return (q_base + b * q_tiles + i, 0)

    o_lat = pl.pallas_call(
        functools.partial(_attn_kernel, lam_init=lam_init, has_ext=True),
        out_shape=jax.ShapeDtypeStruct((N_SAMPLE, ATT_W), BF16),
        grid=(DEC_BATCH, q_tiles),
        in_specs=small + [pl.BlockSpec((tq, ATT_W), q_idx),
                          pl.BlockSpec((DEC_SEQ, ATT_W), lambda b, i: (kv_base + b, 1)),
                          pl.BlockSpec((DEC_SEQ, ATT_W), lambda b, i: (kv_base + b, 2)),
                          pl.BlockSpec((None, None, PAST_LEN * ATT_HEADS, V_DIM), lambda b, i: (b, l, 0, 0)),
                          pl.BlockSpec((None, None, PAST_LEN * ATT_HEADS, V_DIM), lambda b, i: (b, l, 0, 0))],
        out_specs=pl.BlockSpec((tq, ATT_W), lambda b, i: (b * q_tiles + i, 0)),
        compiler_params=_cparams(2),
        name=f"attn_lat_{l}",
    )(lam_params, sub3, qkvg, qkvg, qkvg, cache_k, cache_v)
    return o_ctx, o_lat


def _outproj_kernel(*refs, ffn):
    if ffn == "routed":
        (oc_ref, ol_ref, cv_ref, xa_ref, xb_ref, mod_ref, g_ref, w_ref, rw_ref,
         xo_ref, h2_ref, info_ref, info_t_ref, cb_ref, tot_ref, wbf_ref, carry_ref) = refs
    else:
        (oc_ref, ol_ref, cv_ref, xa_ref, xb_ref, mod_ref, g_ref, w_ref, w13_ref, w2_ref,
         xo_ref, wbf_ref, h2_ref) = refs
    i = pl.program_id(0)

    @pl.when(i == 0)
    def _():
        wbf_ref[...] = w_ref[...].astype(BF16)

    half = TOK_TILE // 2
    for rows in (slice(0, half), slice(half, TOK_TILE)):
        o = jnp.where(i < P_TILES, oc_ref[rows, :], ol_ref[rows, :])
        m = (jnp.dot(o, wbf_ref[0:ATT_W, :], preferred_element_type=F32)
             + jnp.dot(cv_ref[rows, :], wbf_ref[ATT_W:, :], preferred_element_type=F32))
        x = jnp.where(i < P_TILES, xa_ref[rows, :], xb_ref[rows, :])
        xn = x + mod_ref[2:3, :] * _rms(m, g_ref[1:2, :])
        xo_ref[rows, :] = xn
        h2_ref[rows, :] = (_rms(xn, g_ref[2:3, :]) * (1.0 + mod_ref[4:5, :]) + mod_ref[3:4, :]).astype(BF16)

    if ffn == "routed":
        @pl.when(i == 0)
        def _():
            carry_ref[...] = jnp.zeros_like(carry_ref)

        _route_tile(h2_ref[...], rw_ref, info_ref, info_t_ref, cb_ref, tot_ref, carry_ref)
    else:
        h = h2_ref[...]
        gt = jnp.dot(h, w13_ref[:, :D_FF], preferred_element_type=F32)
        up = jnp.dot(h, w13_ref[:, D_FF:], preferred_element_type=F32)
        a = (gt * _sigmoid(gt) * up).astype(BF16)
        y = jnp.dot(a, w2_ref[...], preferred_element_type=F32)
        xo_ref[...] = xo_ref[...] + mod_ref[5:6, :] * _rms(y, g_ref[3:4, :])


def _outproj(l, o_ctx, o_lat, cv, x, mod, norm_g, w_out, router_w=None, dense_w=None):
    x_specs, x_args = _x_specs(x)
    out_shape = [jax.ShapeDtypeStruct((N_TOK, D_MODEL), F32)]
    out_specs = [pl.BlockSpec((TOK_TILE, D_MODEL), lambda i: (i, 0))]
    scratch = [pltpu.VMEM((D_MODEL, D_MODEL), BF16)]
    if dense_w is not None:
        extra_specs = [pl.BlockSpec((D_MODEL, 2 * D_FF), lambda i: (0, 0), pipeline_mode=pl.Buffered(1)),
                       pl.BlockSpec((D_FF, D_MODEL), lambda i: (0, 0), pipeline_mode=pl.Buffered(1))]
        extra_args = tuple(dense_w)
        scratch += [pltpu.VMEM((TOK_TILE, D_MODEL), BF16)]
    else:
        rw_pad = jnp.zeros((D_MODEL, LANES), BF16).at[:, :N_EXPERTS].set(router_w.astype(BF16))
        extra_specs, extra_args = [pl.BlockSpec((D_MODEL, LANES), lambda i: (0, 0))], (rw_pad,)
        out_shape += [jax.ShapeDtypeStruct((N_TOK, D_MODEL), BF16)]
        out_specs += [pl.BlockSpec((TOK_TILE, D_MODEL), lambda i: (i, 0))]
        out_shape += [jax.ShapeDtypeStruct((N_TOK, LANES), F32),
                      jax.ShapeDtypeStruct((N_TILES, 8, TOK_TILE), F32),
                      jax.ShapeDtypeStruct((N_TILES, 8, LANES), F32),
                      jax.ShapeDtypeStruct((8, LANES), F32)]
        out_specs += [pl.BlockSpec((TOK_TILE, LANES), lambda i: (i, 0)),
                      pl.BlockSpec((None, 8, TOK_TILE), lambda i: (i, 0, 0)),
                      pl.BlockSpec((None, 8, LANES), lambda i: (i, 0, 0)),
                      pl.BlockSpec((8, LANES), lambda i: (0, 0))]
        scratch += [pltpu.VMEM((8, LANES), F32)]
    return pl.pallas_call(
        functools.partial(_outproj_kernel, ffn="dense" if dense_w is not None else "routed"),
        out_shape=tuple(out_shape),
        grid=(N_TILES,),
        in_specs=[pl.BlockSpec((TOK_TILE, ATT_W), lambda i: (jnp.minimum(i, P_TILES - 1), 0)),
                  pl.BlockSpec((TOK_TILE, ATT_W), lambda i: (jnp.maximum(i - P_TILES, 0), 0)),
                  pl.BlockSpec((TOK_TILE, CONV_CH), lambda i: (i, 0))] + x_specs + [
                  pl.BlockSpec((None, None, 6, D_MODEL), lambda i: (l, _cond_id(i), 0, 0)),
                  pl.BlockSpec((None, 4, D_MODEL), lambda i: (l, 0, 0)),
                  pl.BlockSpec((None, D_MODEL, D_MODEL), lambda i: (l, 0, 0),
                               pipeline_mode=pl.Buffered(1))] + extra_specs,
        out_specs=tuple(out_specs),
        scratch_shapes=scratch,
        compiler_params=_cparams(1),
        name=f"outproj_{l}",
    )(o_ctx, o_lat, cv, *x_args, mod, norm_g, w_out, *extra_args)


def _route_tile(h, rw_ref, info_ref, info_t_ref, cb_ref, tot_ref, carry_ref):
    lane = lax.broadcasted_iota(jnp.int32, (TOK_TILE, LANES), 1)
    lanef = lane.astype(F32)
    logits = jnp.dot(h, rw_ref[...], preferred_element_type=F32)
    logits = jnp.where(lane < N_EXPERTS, logits, -jnp.inf)
    big = jnp.asarray(LANES, F32)
    m1 = jnp.max(logits, axis=-1, keepdims=True)
    e1 = jnp.min(jnp.where(logits == m1, lanef, big), axis=-1, keepdims=True)
    oh1 = lanef == e1
    rest = jnp.where(oh1, -jnp.inf, logits)
    m2 = jnp.max(rest, axis=-1, keepdims=True)
    e2 = jnp.min(jnp.where(rest == m2, lanef, big), axis=-1, keepdims=True)
    oh2 = lanef == e2
    ex = jnp.exp(m2 - m1)
    g1 = 1.0 / (1.0 + ex)
    g2 = ex / (1.0 + ex)

    oh = jnp.where(jnp.logical_or(oh1, oh2), 1.0, 0.0)
    r = lax.broadcasted_iota(jnp.int32, (TOK_TILE, TOK_TILE), 0)
    c = lax.broadcasted_iota(jnp.int32, (TOK_TILE, TOK_TILE), 1)
    tri = jnp.where(c <= r, 1.0, 0.0).astype(BF16)
    incl = jnp.dot(tri, oh.astype(BF16), preferred_element_type=F32)
    carry = carry_ref[0:1, :]
    excl = incl - oh + carry
    rank1 = jnp.sum(jnp.where(oh1, excl, 0.0), axis=-1, keepdims=True)
    rank2 = jnp.sum(jnp.where(oh2, excl, 0.0), axis=-1, keepdims=True)

    info = jnp.where(lane == 0, e1, 0.0)
    for k, col in enumerate((e2, rank1, rank2, g1, g2), start=1):
        info = jnp.where(lane == k, col, info)
    info_ref[...] = info
    info_t_ref[...] = info.T[0:8, :]

    cb_ref[...] = carry_ref[...]
    new_carry = carry + incl[TOK_TILE - 1:TOK_TILE, :]
    carry_ref[...] = jnp.broadcast_to(new_carry, carry_ref.shape)
    tot_ref[...] = jnp.broadcast_to(new_carry, tot_ref.shape)


def _sorted_pos(expert, rank, base_ref):
    start = jnp.zeros_like(rank)
    for e in range(N_EXPERTS):
        start = jnp.where(expert == float(e), base_ref[e].astype(F32), start)
    return start + rank


def _dispatch_kernel(clo_ref, cn_ref, base_ref, info_t_ref, x_ref, o_ref):
    r = pl.program_id(0)
    o_ref[...] = jnp.zeros_like(o_ref)
    rows = (r * DISPATCH_TILE + lax.broadcasted_iota(jnp.int32, (DISPATCH_TILE, TOK_TILE), 0)).astype(F32)

    def body(k, carry):
        c = clo_ref[r] + k
        it = info_t_ref[c]
        pos1 = _sorted_pos(it[0:1, :], it[2:3, :], base_ref)
        pos2 = _sorted_pos(it[1:2, :], it[3:4, :], base_ref)
        hit = jnp.logical_or(rows == pos1, rows == pos2)
        sel = jnp.where(hit, 1.0, 0.0).astype(BF16)
        off = pl.multiple_of(c * TOK_TILE, TOK_TILE)
        o_ref[...] += jnp.dot(sel, x_ref[pl.ds(off, TOK_TILE), :], preferred_element_type=F32).astype(BF16)
        return carry

    lax.fori_loop(0, cn_ref[r], body, 0)


def _dispatch(c_lo, c_n, base, info_t, h2):
    return pl.pallas_call(
        _dispatch_kernel,
        out_shape=jax.ShapeDtypeStruct((SORT_ROWS, D_MODEL), BF16),
        grid_spec=pltpu.PrefetchScalarGridSpec(
            num_scalar_prefetch=3,
            grid=(SORT_ROWS // DISPATCH_TILE,),
            in_specs=[pl.BlockSpec((N_TILES, 8, TOK_TILE), lambda r, *_: (0, 0, 0)),
                      pl.BlockSpec((N_TOK, D_MODEL), lambda r, *_: (0, 0),
                                   pipeline_mode=pl.Buffered(1))],
            out_specs=pl.BlockSpec((DISPATCH_TILE, D_MODEL), lambda r, *_: (r, 0))),
        compiler_params=_cparams(1),
        name="moe_dispatch",
    )(c_lo, c_n, base, info_t, h2)


TILE_UNUSED, TILE_HALF, TILE_FULL = 0, 1, 2
CAST_ROWS = 64


def _expert_weight_ring(c, r, n_c, te_ref, tf_ref, nx_ref, copies, cast):
    @pl.when(jnp.logical_and(c == 0, r == 0))
    def _():
        for cp in copies(te_ref[0], 0):
            cp.start()

    @pl.when(tf_ref[r] == 1)
    def _():
        for cp in copies(te_ref[r], c):
            cp.wait()
        cast()
        nr = nx_ref[r]
        nc = c + (nr <= r).astype(jnp.int32)

        @pl.when(nc < n_c)
        def _():
            for cp in copies(te_ref[nr], nc):
                cp.start()


def _moe_up_kernel(te_ref, tv_ref, tf_ref, nx_ref, x_ref, w_hbm, h_ref, wst_ref, wbf_ref, sem_ref, *, i_moe):
    f = pl.program_id(0)
    r = pl.program_id(1)

    def copies(e, ft):
        out = []
        for part in range(2):
            col = pl.multiple_of((part * UP_TILES + ft) * UP_TILE, LANES)
            out.append(pltpu.make_async_copy(w_hbm.at[i_moe, e, :, pl.ds(col, UP_TILE)],
                                             wst_ref.at[part], sem_ref.at[part]))
        return out

    def cast():
        def rows_block(b, carry):
            rows = pl.ds(pl.multiple_of(b * CAST_ROWS, CAST_ROWS), CAST_ROWS)
            wbf_ref[:, rows, :] = wst_ref[:, rows, :].astype(BF16)
            return carry

        lax.fori_loop(0, D_MODEL // CAST_ROWS, rows_block, 0)

    _expert_weight_ring(f, r, UP_TILES, te_ref, tf_ref, nx_ref, copies, cast)

    def hidden(x):
        gt = jnp.dot(x, wbf_ref[0], preferred_element_type=F32)
        up = jnp.dot(x, wbf_ref[1], preferred_element_type=F32)
        return (gt * _sigmoid(gt) * up).astype(BF16)

    half = ROW_TILE // 2

    @pl.when(tv_ref[r] == TILE_FULL)
    def _():
        h_ref[...] = hidden(x_ref[...])

    @pl.when(tv_ref[r] == TILE_HALF)
    def _():
        h_ref[0:half, :] = hidden(x_ref[0:half, :])
        h_ref[half:, :] = jnp.zeros((half, UP_TILE), BF16)

    @pl.when(tv_ref[r] == TILE_UNUSED)
    def _():
        h_ref[...] = jnp.zeros_like(h_ref)


def _moe_up(i_moe, tile_e, tile_valid, tile_first, tile_next, xs, w13):
    return pl.pallas_call(
        functools.partial(_moe_up_kernel, i_moe=i_moe),
        out_shape=jax.ShapeDtypeStruct((SORT_ROWS, D_FF_EXPERT), BF16),
        grid_spec=pltpu.PrefetchScalarGridSpec(
            num_scalar_prefetch=4,
            grid=(UP_TILES, SORT_TILES),
            in_specs=[pl.BlockSpec((ROW_TILE, D_MODEL), lambda f, r, *_: (r, 0)),
                      pl.BlockSpec(memory_space=pl.ANY)],
            out_specs=pl.BlockSpec((ROW_TILE, UP_TILE), lambda f, r, *_: (r, f)),
            scratch_shapes=[pltpu.VMEM((2, D_MODEL, UP_TILE), F32),
                            pltpu.VMEM((2, D_MODEL, UP_TILE), BF16),
                            pltpu.SemaphoreType.DMA((2,))]),
        compiler_params=_cparams(2),
        name="moe_up",
    )(tile_e, tile_valid, tile_first, tile_next, xs, w13)


def _moe_down_kernel(te_ref, tv_ref, tf_ref, nx_ref, h_ref, w_hbm, y_ref, wst_ref, wbf_ref, sem_ref, *, i_moe):
    n = pl.program_id(0)
    r = pl.program_id(1)

    def copies(e, nt):
        col = pl.multiple_of(nt * DOWN_TILE, LANES)
        return [pltpu.make_async_copy(w_hbm.at[i_moe, e, :, pl.ds(col, DOWN_TILE)], wst_ref, sem_ref.at[0])]

    def cast():
        wbf_ref[...] = wst_ref[...].astype(BF16)

    _expert_weight_ring(n, r, D_MODEL // DOWN_TILE, te_ref, tf_ref, nx_ref, copies, cast)

    half = ROW_TILE // 2

    @pl.when(tv_ref[r] == TILE_FULL)
    def _():
        y_ref[...] = jnp.dot(h_ref[...], wbf_ref[...], preferred_element_type=F32).astype(BF16)

    @pl.when(tv_ref[r] == TILE_HALF)
    def _():
        y_ref[0:half, :] = jnp.dot(h_ref[0:half, :], wbf_ref[...], preferred_element_type=F32).astype(BF16)
        y_ref[half:, :] = jnp.zeros((half, DOWN_TILE), BF16)

    @pl.when(tv_ref[r] == TILE_UNUSED)
    def _():
        y_ref[...] = jnp.zeros_like(y_ref)


def _moe_down(i_moe, tile_e, tile_valid, tile_first, tile_next, hs, w2):
    return pl.pallas_call(
        functools.partial(_moe_down_kernel, i_moe=i_moe),
        out_shape=jax.ShapeDtypeStruct((SORT_ROWS, D_MODEL), BF16),
        grid_spec=pltpu.PrefetchScalarGridSpec(
            num_scalar_prefetch=4,
            grid=(D_MODEL // DOWN_TILE, SORT_TILES),
            in_specs=[pl.BlockSpec((ROW_TILE, D_FF_EXPERT), lambda n, r, *_: (r, 0)),
                      pl.BlockSpec(memory_space=pl.ANY)],
            out_specs=pl.BlockSpec((ROW_TILE, DOWN_TILE), lambda n, r, *_: (r, n)),
            scratch_shapes=[pltpu.VMEM((D_FF_EXPERT, DOWN_TILE), F32),
                            pltpu.VMEM((D_FF_EXPERT, DOWN_TILE), BF16),
                            pltpu.SemaphoreType.DMA((1,))]),
        compiler_params=_cparams(2),
        name="moe_down",
    )(tile_e, tile_valid, tile_first, tile_next, hs, w2)


def _window_copy(y_hbm, ybuf_ref, sem_ref, src, buf, slot):
    return pltpu.make_async_copy(y_hbm.at[pl.ds(src, WIN), :],
                                 ybuf_ref.at[buf, pl.ds(slot * WIN, WIN), :], sem_ref.at[buf, slot])


def _start_windows(src_ref, y_hbm, ybuf_ref, sem_ref, tile, buf):
    for s in range(N_WIN):
        src = pl.multiple_of(src_ref[tile * N_WIN + s], BF16_SUBLANES)
        _window_copy(y_hbm, ybuf_ref, sem_ref, src, buf, s).start()


def _combine_kernel(src_ref, lo_ref, hi_ref, base_ref, info_ref, x_ref, mod_ref, g_ref, y_hbm,
                    yp_ref, ys_ref, ybuf_ref, sem_ref):
    j = pl.program_id(0)
    buf = j % 2

    @pl.when(j == 0)
    def _():
        _start_windows(src_ref, y_hbm, ybuf_ref, sem_ref, 0, 0)

    @pl.when(j + 1 < N_TILES)
    def _():
        _start_windows(src_ref, y_hbm, ybuf_ref, sem_ref, j + 1, 1 - buf)

    info = info_ref[...]
    pos1 = _sorted_pos(info[:, 0:1], info[:, 2:3], base_ref)
    pos2 = _sorted_pos(info[:, 1:2], info[:, 3:4], base_ref)
    g1, g2 = info[:, 4:5], info[:, 5:6]
    iota = lax.broadcasted_iota(jnp.int32, (1, WIN), 1)
    ids = []
    for s in range(N_WIN):
        row = src_ref[j * N_WIN + s] + iota
        ok = jnp.logical_and(row >= lo_ref[j * N_WIN + s], row < hi_ref[j * N_WIN + s])
        ids.append(jnp.where(ok, row, -1))
    row_id = jnp.concatenate(ids, axis=1).astype(F32)
    sel = (jnp.where(pos1 == row_id, g1, 0.0) + jnp.where(pos2 == row_id, g2, 0.0)).astype(BF16)

    for s in range(N_WIN):
        _window_copy(y_hbm, ybuf_ref, sem_ref, 0, buf, s).wait()
    ffn = jnp.dot(sel, ybuf_ref[buf], preferred_element_type=F32)
    out = x_ref[...] + mod_ref[5:6, :] * _rms(ffn, g_ref[3:4, :])

    @pl.when(j < P_TILES)
    def _():
        yp_ref[...] = out

    @pl.when(j >= P_TILES)
    def _():
        ys_ref[...] = out


def _combine(l, win_src, win_lo, win_hi, base, info, x, mod, norm_g, ys):
    return pl.pallas_call(
        _combine_kernel,
        out_shape=(jax.ShapeDtypeStruct((N_PROMPT, D_MODEL), F32),
                   jax.ShapeDtypeStruct((N_SAMPLE, D_MODEL), F32)),
        grid_spec=pltpu.PrefetchScalarGridSpec(
            num_scalar_prefetch=4,
            grid=(N_TILES,),
            in_specs=[pl.BlockSpec((TOK_TILE, LANES), lambda i, *_: (i, 0)),
                      pl.BlockSpec((TOK_TILE, D_MODEL), lambda i, *_: (i, 0)),
                      pl.BlockSpec((None, None, 6, D_MODEL), lambda i, *_: (l, _cond_id(i), 0, 0)),
                      pl.BlockSpec((None, 4, D_MODEL), lambda i, *_: (l, 0, 0)),
                      pl.BlockSpec(memory_space=pl.ANY)],
            out_specs=(pl.BlockSpec((TOK_TILE, D_MODEL), lambda i, *_: (jnp.minimum(i, P_TILES - 1), 0)),
                       pl.BlockSpec((TOK_TILE, D_MODEL), lambda i, *_: (jnp.maximum(i - P_TILES, 0), 0))),
            scratch_shapes=[pltpu.VMEM((2, N_WIN * WIN, D_MODEL), BF16),
                            pltpu.SemaphoreType.DMA((2, N_WIN))]),
        compiler_params=_cparams(1),
        name="moe_combine",
    )(win_src, win_lo, win_hi, base, info, x, mod, norm_g, ys)


def _moe_ffn(l, i_moe, h2, x, mod, norm_g, routing, moe_w13, moe_w2):
    info, info_t, cb, tot = routing

    ar8 = jnp.arange(N_EXPERTS, dtype=jnp.int32)

    def at8(vec, idx):
        return jnp.sum(jnp.where(idx[..., None] == ar8, vec, 0), axis=-1)

    def cumsum8(a):
        return jnp.sum(jnp.where(ar8[None, :] <= ar8[:, None], a[..., None, :], 0), axis=-1)

    counts = tot[0, :N_EXPERTS].astype(jnp.int32)
    padded = (counts + ROW_TILE - 1) // ROW_TILE * ROW_TILE
    seg_end = cumsum8(padded)
    base = (seg_end - padded).astype(jnp.int32)
    cbx = cb[:, 0, :N_EXPERTS].astype(jnp.int32)
    cb_end = jnp.concatenate([cbx[1:], counts[None, :]], axis=0)

    def tile_segments(n_tiles, rows):
        row0 = jnp.arange(n_tiles, dtype=jnp.int32) * rows
        e = jnp.minimum(jnp.sum(seg_end[None, :] <= row0[:, None], axis=1), N_EXPERTS - 1).astype(jnp.int32)
        k0 = row0 - at8(base, e)
        return e, k0, jnp.logical_and(k0 >= 0, k0 < at8(counts, e))

    tile_e, k0, tile_valid = tile_segments(SORT_TILES, ROW_TILE)
    tile_first = jnp.logical_and(tile_valid, k0 == 0)
    d_e, d_k0, d_valid = tile_segments(SORT_ROWS // DISPATCH_TILE, DISPATCH_TILE)
    d_kend = jnp.minimum(d_k0 + DISPATCH_TILE, at8(counts, d_e))
    cb_t = jnp.sum(jnp.where(d_e[:, None, None] == ar8[None, :, None], cbx.T[None, :, :], 0), axis=1)
    c_lo = jnp.sum(cb_t <= d_k0[:, None], axis=1) - 1
    c_hi = jnp.sum(cb_t < d_kend[:, None], axis=1) - 1
    c_n = jnp.where(d_valid, c_hi - c_lo + 1, 0).astype(jnp.int32)
    c_lo = jnp.where(d_valid, c_lo, 0).astype(jnp.int32)
    tile_idx = jnp.arange(SORT_TILES, dtype=jnp.int32)
    first_idx = jnp.where(tile_first, tile_idx, SORT_TILES)
    later = jnp.where(tile_idx[None, :] > tile_idx[:, None], first_idx[None, :], SORT_TILES)
    tile_next = jnp.min(later, axis=1)
    tile_next = jnp.where(tile_next >= SORT_TILES, 0, tile_next).astype(jnp.int32)
    tile_rows = at8(counts, tile_e) - k0
    tile_valid = jnp.where(tile_valid, jnp.where(tile_rows <= ROW_TILE // 2, TILE_HALF, TILE_FULL),
                           TILE_UNUSED).astype(jnp.int32)
    tile_first = tile_first.astype(jnp.int32)

    seg_lo = base[None, :] + cbx
    seg_hi = base[None, :] + cb_end
    seg_n = seg_hi - seg_lo
    w0 = seg_lo // BF16_SUBLANES * BF16_SUBLANES
    nw = jnp.where(seg_n > 0, (seg_lo - w0 + seg_n + WIN - 1) // WIN, 0)
    nw_end = cumsum8(nw)
    slot = jnp.arange(N_WIN, dtype=jnp.int32)
    slot_e = jnp.minimum(jnp.sum(nw_end[:, None, :] <= slot[None, :, None], axis=2), N_EXPERTS - 1)
    pick = slot_e[:, :, None] == jnp.arange(N_EXPERTS, dtype=jnp.int32)[None, None, :]
    take = lambda a: jnp.sum(jnp.where(pick, a[:, None, :], 0), axis=2)
    slot_k = slot[None, :] - (take(nw_end) - take(nw))
    slot_ok = slot[None, :] < nw_end[:, -1:]
    slot_src = take(w0) + WIN * slot_k
    win_src = jnp.where(slot_ok, slot_src, 0).astype(jnp.int32).reshape(-1)
    win_lo = jnp.where(slot_ok, take(seg_lo), 0).astype(jnp.int32).reshape(-1)
    win_hi = jnp.where(slot_ok, take(seg_hi), 0).astype(jnp.int32).reshape(-1)

    xs = _dispatch(c_lo, c_n, base, info_t, h2)
    hs = _moe_up(i_moe, tile_e, tile_valid, tile_first, tile_next, xs, moe_w13)
    ys = _moe_down(i_moe, tile_e, tile_valid, tile_first, tile_next, hs, moe_w2)
    return _combine(l, win_src, win_lo, win_hi, base, info, x, mod, norm_g, ys)


def kernel(x_prompt, x_sample, cache_k, cache_v, c, c_ctx, w_ada, b_ada, norm_g, w_in, w_out,
           lam_params, subln_g, dw_weight, dw_bias, conv_ln_g, conv_ln_b, dense_w13, dense_w2,
           router_w, moe_w13, moe_w2):
    x = (x_prompt.reshape(N_PROMPT, D_MODEL), x_sample.reshape(N_SAMPLE, D_MODEL))
    cond8 = jnp.zeros((8, D_MODEL), F32).at[0].set(c_ctx).at[1:1 + DEC_BATCH].set(c)
    mod = _ada_modulation(cond8, w_ada, b_ada).reshape(DEPTH, 8, 6, D_MODEL)
    rope = _rope_tables()
    ck = cache_k.reshape(DEC_BATCH, DEPTH, PAST_LEN * ATT_HEADS, V_DIM)
    cv = cache_v.reshape(DEC_BATCH, DEPTH, PAST_LEN * ATT_HEADS, V_DIM)

    caches = None
    for l in range(DEPTH):
        lam_init = 0.8 - 0.6 * math.exp(-0.3 * l)
        qkv, cvo, *caches = _mixer_in(l, x, mod, norm_g, w_in, rope,
                                      (dw_weight, dw_bias, conv_ln_g, conv_ln_b), caches)
        o_ctx, o_lat = _attention(l, lam_init, lam_params, subln_g, qkv, ck, cv)
        i = l // 2
        if l % 2 == 0:
            (x,) = _outproj(l, o_ctx, o_lat, cvo, x, mod, norm_g, w_out,
                            dense_w=(dense_w13[i].astype(BF16), dense_w2[i].astype(BF16)))
        else:
            x, h2, *routing = _outproj(l, o_ctx, o_lat, cvo, x, mod, norm_g, w_out, router_w=router_w[i])
            x = _moe_ffn(l, i, h2, x, mod, norm_g, routing, moe_w13, moe_w2)

    xp, xs = x if isinstance(x, tuple) else (x[:N_PROMPT], x[N_PROMPT:])
    new_k, new_v = (a.reshape(BATCH, DEPTH, SEQ, ATT_HEADS, V_DIM) for a in caches)
    return (xp.reshape(BATCH, SEQ, D_MODEL), xs.reshape(DEC_BATCH, DEC_SEQ, D_MODEL), new_k, new_v)
```

```python
import functools
import math

import jax
import jax.numpy as jnp
import numpy as np
from jax import lax
from jax.experimental import pallas as pl
from jax.experimental.pallas import tpu as pltpu

F32 = jnp.float32
BF16 = jnp.bfloat16

D_MODEL = 1024
BATCH = 32
SEQ = 256
DEPTH = 2
DEC_BATCH = 2
DEC_SEQ = 2048
PAST_LEN = 512
GRID_W = 64
ATT_HEADS = 4
QK_DIM = 64
V_DIM = 128
ATT_W = ATT_HEADS * V_DIM
IN_W = 5 * ATT_W
CONV_CH = 512
CONV_WIDTH = 31
D_FF = 2816
N_EXPERTS = 8
D_FF_EXPERT = 3584
ROPE_THETA = 10000.0
NORM_EPS = 1e-6
LN_EPS = 1e-5

N_PROMPT = BATCH * SEQ
N_SAMPLE = DEC_BATCH * DEC_SEQ
N_TOK = N_PROMPT + N_SAMPLE

LANES = 128
BF16_SUBLANES = 16
MXU_TILE = 256
VMEM_BYTES = 64 * 1024 * 1024
VMEM_LIMIT = VMEM_BYTES - 8 * 1024 * 1024

TOK_TILE = 512
N_TILES = N_TOK // TOK_TILE
P_TILES = N_PROMPT // TOK_TILE
S_TILES_PER_BATCH = DEC_SEQ // TOK_TILE

ROW_TILE = 512
SORT_TILES = (2 * N_TOK + N_EXPERTS * ROW_TILE) // ROW_TILE + 1
SORT_ROWS = SORT_TILES * ROW_TILE
DISPATCH_TILE = 256
UP_TILE = 7 * MXU_TILE
UP_TILES = D_FF_EXPERT // UP_TILE
DOWN_TILE = 4 * MXU_TILE
WIN = 128
N_WIN = (2 * TOK_TILE + N_EXPERTS * (BF16_SUBLANES - 1 + WIN - 1)) // WIN

assert D_FF_EXPERT % UP_TILE == 0 and D_MODEL % DOWN_TILE == 0 and D_FF % MXU_TILE == 0
assert ROW_TILE % DISPATCH_TILE == 0 and N_TOK % TOK_TILE == 0 and N_PROMPT % TOK_TILE == 0
assert DEPTH == 2, "layer 0 zero-fills exactly one later cache slab; the last layer must be the routed one"


def _cparams(n_axes):
    return pltpu.CompilerParams(dimension_semantics=("arbitrary",) * n_axes,
                                vmem_limit_bytes=VMEM_LIMIT)


def _cond_id(i):
    return jnp.where(i < P_TILES, 0, 1 + (i - P_TILES) // S_TILES_PER_BATCH)


def _sigmoid(x):
    return 1.0 / (1.0 + jnp.exp(-x))


def _rms(x, g):
    return x * lax.rsqrt(jnp.mean(x * x, axis=-1, keepdims=True) + NORM_EPS) * g


def _ada_kernel(c_ref, w_ref, b_ref, o_ref):
    c = c_ref[...]
    s = (c * _sigmoid(c)).astype(BF16)
    o_ref[...] = jnp.dot(s, w_ref[...].astype(BF16), preferred_element_type=F32) + b_ref[...]


def _ada_modulation(cond8, w_ada, b_ada):
    tn = 1536
    n = 6 * D_MODEL
    return pl.pallas_call(
        _ada_kernel,
        out_shape=jax.ShapeDtypeStruct((DEPTH, 8, n), F32),
        grid=(DEPTH, n // tn),
        in_specs=[pl.BlockSpec((8, D_MODEL), lambda l, j: (0, 0)),
                  pl.BlockSpec((None, D_MODEL, tn), lambda l, j: (l, 0, j)),
                  pl.BlockSpec((None, 1, tn), lambda l, j: (l, 0, j))],
        out_specs=pl.BlockSpec((None, 8, tn), lambda l, j: (l, 0, j)),
        compiler_params=_cparams(2),
        name="ada_modulation",
    )(cond8, w_ada, b_ada.reshape(DEPTH, 1, n))


def _tile_x(xa_ref, xb_ref):
    return jnp.where(pl.program_id(0) < P_TILES, xa_ref[...], xb_ref[...])


def _x_specs(x):
    last = N_TILES - 1
    if isinstance(x, tuple):
        xa, xb = x
        b_idx = lambda i, *_: (jnp.maximum(jnp.minimum(i, last) - P_TILES, 0), 0)
    else:
        xa = xb = x
        b_idx = lambda i, *_: (jnp.maximum(jnp.minimum(i, last), P_TILES), 0)
    a_idx = lambda i, *_: (jnp.minimum(i, P_TILES - 1), 0)
    return [pl.BlockSpec((TOK_TILE, D_MODEL), a_idx), pl.BlockSpec((TOK_TILE, D_MODEL), b_idx)], (xa, xb)


ROT_SPAN = QK_DIM // 4
SEQ_PER_TILE = TOK_TILE // SEQ
CACHE_ROWS = SEQ * ATT_HEADS


CONV_LAG = 2
CONV_RING = 3
CONV_TILE = 256
CONV_HALO = 16
CONV_SUB = 32


def _conv_pass(upad_ref, w_ref, bias_ref, lg_ref, lb_ref, o_ref, row0):
    rows = CONV_TILE + 2 * CONV_HALO - 8
    for s in range(1, 8):
        upad_ref[s, 0:rows, :] = upad_ref[0, s:s + rows, :]
    first_tap = CONV_HALO - CONV_WIDTH // 2
    groups = CONV_SUB // 8
    for t in range(CONV_TILE // CONV_SUB):
        base = t * CONV_SUB
        acc = jnp.zeros((groups, 8, CONV_CH), F32)
        for j in range(CONV_WIDTH):
            lo = base + (first_tap + j) // 8 * 8
            taps = upad_ref[(first_tap + j) % 8, lo:lo + CONV_SUB, :].reshape(groups, 8, CONV_CH)
            acc = acc + taps * w_ref[j][None]
        y = acc.reshape(CONV_SUB, CONV_CH) + bias_ref[...]
        mu = jnp.mean(y, axis=-1, keepdims=True)
        yc = y - mu
        var = jnp.mean(yc * yc, axis=-1, keepdims=True)
        z = yc * lax.rsqrt(var + LN_EPS) * lg_ref[...] + lb_ref[...]
        o_ref[row0 + base:row0 + base + CONV_SUB, :] = (z * _sigmoid(z)).astype(BF16)


def _mixer_in_kernel(*refs, layer):
    n_in = 12 + (2 if layer else 0)
    (xa_ref, xb_ref, mod_ref, g_ref, w_ref, cos_ref, sina_ref, sinb_ref,
     dw_ref, db_ref, lg_ref, lb_ref) = refs[:12]
    qkv_ref, cv_ref, kc_ref, vc_ref, wbf_ref, ring_ref, glu_ref, pj_ref, upad_ref, taps_ref = refs[n_in:]
    i = pl.program_id(0)
    t = jnp.minimum(i, N_TILES - 1)

    @pl.when(i == 0)
    def _():
        wbf_ref[...] = w_ref[...].astype(BF16)
        ring_ref[...] = jnp.zeros_like(ring_ref)
        glu_ref[...] = jnp.zeros_like(glu_ref)
        for j in range(CONV_WIDTH):
            taps_ref[j] = jnp.broadcast_to(dw_ref[j:j + 1, :], (8, CONV_CH))

    ring_ref[(i - 1) % CONV_RING] = glu_ref[...]

    h = _rms(_tile_x(xa_ref, xb_ref), g_ref[0:1, :]) * (1.0 + mod_ref[1:2, :]) + mod_ref[0:1, :]
    hb = h.astype(BF16)

    def proj(part):
        return jnp.dot(hb, wbf_ref[:, part * ATT_W:(part + 1) * ATT_W], preferred_element_type=F32)

    for part in range(3):
        pj_ref[:, part * ATT_W:(part + 1) * ATT_W] = proj(part)
    glu_ref[...] = proj(3) * _sigmoid(proj(4))

    j = i - CONV_LAG
    latent = j >= P_TILES
    q = (j - P_TILES) % S_TILES_PER_BATCH
    has_prev = jnp.logical_and(latent, q != 0)
    has_next = jnp.logical_and(latent, q != S_TILES_PER_BATCH - 1)
    cur = j % CONV_RING
    prev_tail = ring_ref[(j - 1) % CONV_RING, TOK_TILE - CONV_HALO:, :]
    next_head = ring_ref[(j + 1) % CONV_RING, 0:CONV_HALO, :]
    for s in range(TOK_TILE // CONV_TILE):
        lo = s * CONV_TILE
        if s == 0:
            before = jnp.where(has_prev, prev_tail, 0.0)
        else:
            before = jnp.where(latent, ring_ref[cur, lo - CONV_HALO:lo, :], 0.0)
        if s == TOK_TILE // CONV_TILE - 1:
            after = jnp.where(has_next, next_head, 0.0)
        else:
            after = jnp.where(latent, ring_ref[cur, lo + CONV_TILE:lo + CONV_TILE + CONV_HALO, :], 0.0)
        upad_ref[0, 0:CONV_HALO, :] = before
        upad_ref[0, CONV_HALO:CONV_HALO + CONV_TILE, :] = ring_ref[cur, lo:lo + CONV_TILE, :]
        upad_ref[0, CONV_HALO + CONV_TILE:, :] = after
        _conv_pass(upad_ref, taps_ref, db_ref, lg_ref, lb_ref, cv_ref, lo)

    qkv_ref[:, 2 * ATT_W:] = pj_ref[:, 2 * ATT_W:].astype(BF16)

    @pl.when(t < P_TILES)
    def _():
        qkv_ref[:, :2 * ATT_W] = pj_ref[:, :2 * ATT_W].astype(BF16)
        for ref, col0 in ((kc_ref, ATT_W), (vc_ref, 2 * ATT_W)):
            for s in range(SEQ_PER_TILE):
                for hd in range(ATT_HEADS):
                    val = pj_ref[SEQ * s:SEQ * (s + 1), col0 + V_DIM * hd:col0 + V_DIM * (hd + 1)]
                    rows = pl.ds(hd, SEQ, stride=ATT_HEADS)
                    if layer == 0:
                        ref[s, 0, rows, :] = val
                    else:
                        ref[s, rows, :] = val
            if layer == 0:
                ref[:, 1:] = jnp.zeros((SEQ_PER_TILE, DEPTH - 1, CACHE_ROWS, V_DIM), F32)

    @pl.when(t >= P_TILES)
    def _():
        cos = cos_ref[...]
        sina = sina_ref[...]
        sinb = sinb_ref[...]
        for c in range(2 * ATT_W // V_DIM):
            xg = pj_ref[:, V_DIM * c:V_DIM * (c + 1)]
            fwd = pltpu.roll(xg, V_DIM - ROT_SPAN, 1)
            bwd = pltpu.roll(xg, ROT_SPAN, 1)
            qkv_ref[:, V_DIM * c:V_DIM * (c + 1)] = (xg * cos + fwd * sina + bwd * sinb).astype(BF16)


def _mixer_in(l, x, mod, norm_g, w_in, rope, conv_params, caches):
    cos, sina, sinb = rope
    dw_w, dw_b, ln_g, ln_b = conv_params
    x_specs, x_args = _x_specs(x)

    def tile(i):
        return jnp.minimum(i, N_TILES - 1)

    def rope_idx(i):
        return (jnp.maximum(tile(i) - P_TILES, 0) % S_TILES_PER_BATCH, 0)

    def vec(a):
        return a.reshape(DEPTH, 1, CONV_CH)

    cache_shape = jax.ShapeDtypeStruct((BATCH, DEPTH, CACHE_ROWS, V_DIM), F32)
    if l == 0:
        cache_spec = pl.BlockSpec((SEQ_PER_TILE, DEPTH, CACHE_ROWS, V_DIM),
                                  lambda i: (jnp.minimum(i, P_TILES - 1), 0, 0, 0))
        extra_specs, extra_args, aliases = [], (), {}
    else:
        cache_spec = pl.BlockSpec((SEQ_PER_TILE, None, CACHE_ROWS, V_DIM),
                                  lambda i: (jnp.minimum(i, P_TILES - 1), l, 0, 0))
        extra_specs = [pl.BlockSpec(memory_space=pl.ANY)] * 2
        extra_args = tuple(caches)
        aliases = {12: 2, 13: 3}

    return pl.pallas_call(
        functools.partial(_mixer_in_kernel, layer=l),
        out_shape=(jax.ShapeDtypeStruct((N_TOK, 3 * ATT_W), BF16),
                   jax.ShapeDtypeStruct((N_TOK, CONV_CH), BF16), cache_shape, cache_shape),
        grid=(N_TILES + CONV_LAG,),
        in_specs=x_specs + [
            pl.BlockSpec((None, None, 6, D_MODEL), lambda i: (l, _cond_id(tile(i)), 0, 0)),
            pl.BlockSpec((None, 4, D_MODEL), lambda i: (l, 0, 0)),
            pl.BlockSpec((None, D_MODEL, IN_W), lambda i: (l, 0, 0), pipeline_mode=pl.Buffered(1)),
            pl.BlockSpec((TOK_TILE, V_DIM), rope_idx),
            pl.BlockSpec((TOK_TILE, V_DIM), rope_idx),
            pl.BlockSpec((TOK_TILE, V_DIM), rope_idx),
            pl.BlockSpec((None, CONV_WIDTH, CONV_CH), lambda i: (l, 0, 0)),
            pl.BlockSpec((None, 1, CONV_CH), lambda i: (l, 0, 0)),
            pl.BlockSpec((None, 1, CONV_CH), lambda i: (l, 0, 0)),
            pl.BlockSpec((None, 1, CONV_CH), lambda i: (l, 0, 0))] + extra_specs,
        out_specs=(pl.BlockSpec((TOK_TILE, 3 * ATT_W), lambda i: (tile(i), 0)),
                   pl.BlockSpec((TOK_TILE, CONV_CH), lambda i: (jnp.maximum(i - CONV_LAG, 0), 0)),
                   cache_spec, cache_spec),
        scratch_shapes=[pltpu.VMEM((D_MODEL, IN_W), BF16),
                        pltpu.VMEM((CONV_RING, TOK_TILE, CONV_CH), F32),
                        pltpu.VMEM((TOK_TILE, CONV_CH), F32),
                        pltpu.VMEM((TOK_TILE, 3 * ATT_W), F32),
                        pltpu.VMEM((8, CONV_TILE + 2 * CONV_HALO, CONV_CH), F32),
                        pltpu.VMEM((CONV_WIDTH, 8, CONV_CH), F32)],
        input_output_aliases=aliases,
        compiler_params=_cparams(1),
        name=f"mixer_in_{l}",
    )(*x_args, mod, norm_g, w_in, cos, sina, sinb, dw_w, vec(dw_b), vec(ln_g), vec(ln_b), *extra_args)


def _rope_tables():
    rows = DEC_SEQ // GRID_W
    row_pos = np.repeat(np.arange(rows, dtype=np.float64), GRID_W)
    col_pos = np.tile(np.arange(GRID_W, dtype=np.float64), rows)
    half = QK_DIM // 2
    inv_freq = 1.0 / (ROPE_THETA ** (np.arange(0, half, 2, dtype=np.float64) / half))
    ang_r = row_pos[:, None] * inv_freq
    ang_c = col_pos[:, None] * inv_freq
    ang = np.concatenate([ang_r, ang_r, ang_c, ang_c], axis=-1)
    cos = np.tile(np.cos(ang), (1, 2)).astype(np.float32)
    sin = np.tile(np.sin(ang), (1, 2)).astype(np.float32)
    first = (np.arange(V_DIM) % (2 * ROT_SPAN)) < ROT_SPAN
    sina = np.where(first[None, :], -sin, 0.0).astype(np.float32)
    sinb = np.where(first[None, :], 0.0, sin).astype(np.float32)
    return jnp.asarray(cos), jnp.asarray(sina), jnp.asarray(sinb)


def _attn_kernel(*refs, lam_init, has_ext):
    if has_ext:
        lamp_ref, sub_ref, q_ref, k_ref, v_ref, ke_ref, ve_ref, o_ref = refs
    else:
        lamp_ref, sub_ref, q_ref, k_ref, v_ref, o_ref = refs
    lp = lamp_ref[...]
    lam = (jnp.exp(jnp.sum(lp[0:1] * lp[1:2], axis=-1, keepdims=True))
           - jnp.exp(jnp.sum(lp[2:3] * lp[3:4], axis=-1, keepdims=True)) + lam_init)
    lane = lax.broadcasted_iota(jnp.int32, (1, V_DIM), 1)
    nt = (((1,), (1,)), ((), ()))
    scale = QK_DIM ** -0.5
    map_scale = [jnp.where(lane < QK_DIM, scale, 0.0).astype(BF16),
                 jnp.where(lane < QK_DIM, 0.0, scale).astype(BF16)]

    tq = q_ref.shape[0]
    head_cols = [slice(V_DIM * hd, V_DIM * (hd + 1)) for hd in range(ATT_HEADS)]

    def q_map(hd, m):
        return q_ref[:, head_cols[hd]] * map_scale[m]

    def with_ones(v):
        return jnp.concatenate([v, jnp.ones_like(v)], axis=1)

    outs = []
    if has_ext:
        for hd in range(ATT_HEADS):
            kh = k_ref[:, head_cols[hd]]
            v_aug = with_ones(v_ref[:, head_cols[hd]])
            head_rows = pl.ds(hd, PAST_LEN, stride=ATT_HEADS)
            keh = ke_ref[head_rows, :].astype(BF16)
            ve_aug = with_ones(ve_ref[head_rows, :].astype(BF16))
            ratio = []
            for m in range(2):
                qm = q_map(hd, m)
                s = lax.dot_general(qm, kh, nt, preferred_element_type=F32)
                se = lax.dot_general(qm, keh, nt, preferred_element_type=F32)
                mx = jnp.maximum(jnp.max(s, axis=-1, keepdims=True), jnp.max(se, axis=-1, keepdims=True))
                pv = (jnp.dot(jnp.exp((s - mx).astype(BF16)), v_aug, preferred_element_type=F32)
                      + jnp.dot(jnp.exp((se - mx).astype(BF16)), ve_aug, preferred_element_type=F32))
                ratio.append(pv[:, :V_DIM] / pv[:, V_DIM:])
            outs.append(ratio[0] - lam * ratio[1])
    else:
        pairs = [(hd, m) for hd in range(ATT_HEADS) for m in range(2)]
        s = jnp.concatenate([lax.dot_general(q_map(hd, m), k_ref[:, head_cols[hd]], nt,
                                             preferred_element_type=F32) for hd, m in pairs], axis=0)
        p = jnp.exp((s - jnp.max(s, axis=-1, keepdims=True)).astype(BF16))
        for hd in range(ATT_HEADS):
            pv = jnp.dot(p[2 * tq * hd:2 * tq * (hd + 1)], with_ones(v_ref[:, head_cols[hd]]),
                         preferred_element_type=F32)
            ratio = pv[:, :V_DIM] / pv[:, V_DIM:]
            outs.append(ratio[:tq] - lam * ratio[tq:])

    heads = [(_rms(o, sub_ref[...]) * (1.0 - lam_init)).astype(BF16) for o in outs]
    o_ref[...] = jnp.concatenate(heads, axis=1)


def _attention(l, lam_init, lam_params, subln_g, qkvg, cache_k, cache_v):
    small = [pl.BlockSpec((None, 4, QK_DIM), lambda *_: (l, 0, 0)),
             pl.BlockSpec((None, 1, V_DIM), lambda *_: (l, 0, 0))]
    sub3 = subln_g.reshape(DEPTH, 1, V_DIM)

    o_ctx = pl.pallas_call(
        functools.partial(_attn_kernel, lam_init=lam_init, has_ext=False),
        out_shape=jax.ShapeDtypeStruct((N_PROMPT, ATT_W), BF16),
        grid=(BATCH,),
        in_specs=small + [pl.BlockSpec((SEQ, ATT_W), lambda b: (b, 0)),
                          pl.BlockSpec((SEQ, ATT_W), lambda b: (b, 1)),
                          pl.BlockSpec((SEQ, ATT_W), lambda b: (b, 2))],
        out_specs=pl.BlockSpec((SEQ, ATT_W), lambda b: (b, 0)),
        compiler_params=_cparams(1),
        name=f"attn_ctx_{l}",
    )(lam_params, sub3, qkvg, qkvg, qkvg)

    tq = 256
    q_tiles = DEC_SEQ // tq
    q_base = N_PROMPT // tq
    kv_base = N_PROMPT // DEC_SEQ

    def q_idx(b, i):
        return (q_base + b * q_tiles + i, 0)

    o_lat = pl.pallas_call(
        functools.partial(_attn_kernel, lam_init=lam_init, has_ext=True),
        out_shape=jax.ShapeDtypeStruct((N_SAMPLE, ATT_W), BF16),
        grid=(DEC_BATCH, q_tiles),
        in_specs=small + [pl.BlockSpec((tq, ATT_W), q_idx),
                          pl.BlockSpec((DEC_SEQ, ATT_W), lambda b, i: (kv_base + b, 1)),
                          pl.BlockSpec((DEC_SEQ, ATT_W), lambda b, i: (kv_base + b, 2)),
                          pl.BlockSpec((None, None, PAST_LEN * ATT_HEADS, V_DIM), lambda b, i: (b, l, 0, 0)),
                          pl.BlockSpec((None, None, PAST_LEN * ATT_HEADS, V_DIM), lambda b, i: (b, l, 0, 0))],
        out_specs=pl.BlockSpec((tq, ATT_W), lambda b, i: (b * q_tiles + i, 0)),
        compiler_params=_cparams(2),
        name=f"attn_lat_{l}",
    )(lam_params, sub3, qkvg, qkvg, qkvg, cache_k, cache_v)
    return o_ctx, o_lat


def _outproj_kernel(*refs, ffn):
    if ffn == "routed":
        (oc_ref, ol_ref, cv_ref, xa_ref, xb_ref, mod_ref, g_ref, w_ref, rw_ref,
         xo_ref, h2_ref, info_ref, info_t_ref, cb_ref, tot_ref, wbf_ref, carry_ref) = refs
    else:
        (oc_ref, ol_ref, cv_ref, xa_ref, xb_ref, mod_ref, g_ref, w_ref, w13_ref, w2_ref,
         xo_ref, wbf_ref, h2_ref) = refs
    i = pl.program_id(0)

    @pl.when(i == 0)
    def _():
        wbf_ref[...] = w_ref[...].astype(BF16)

    half = TOK_TILE // 2
    for rows in (slice(0, half), slice(half, TOK_TILE)):
        o = jnp.where(i < P_TILES, oc_ref[rows, :], ol_ref[rows, :])
        m = (jnp.dot(o, wbf_ref[0:ATT_W, :], preferred_element_type=F32)
             + jnp.dot(cv_ref[rows, :], wbf_ref[ATT_W:, :], preferred_element_type=F32))
        x = jnp.where(i < P_TILES, xa_ref[rows, :], xb_ref[rows, :])
        xn = x + mod_ref[2:3, :] * _rms(m, g_ref[1:2, :])
        xo_ref[rows, :] = xn
        h2_ref[rows, :] = (_rms(xn, g_ref[2:3, :]) * (1.0 + mod_ref[4:5, :]) + mod_ref[3:4, :]).astype(BF16)

    if ffn == "routed":
        @pl.when(i == 0)
        def _():
            carry_ref[...] = jnp.zeros_like(carry_ref)

        _route_tile(h2_ref[...], rw_ref, info_ref, info_t_ref, cb_ref, tot_ref, carry_ref)
    else:
        h = h2_ref[...]
        gt = jnp.dot(h, w13_ref[:, :D_FF], preferred_element_type=F32)
        up = jnp.dot(h, w13_ref[:, D_FF:], preferred_element_type=F32)
        a = (gt * _sigmoid(gt) * up).astype(BF16)
        y = jnp.dot(a, w2_ref[...], preferred_element_type=F32)
        xo_ref[...] = xo_ref[...] + mod_ref[5:6, :] * _rms(y, g_ref[3:4, :])


def _outproj(l, o_ctx, o_lat, cv, x, mod, norm_g, w_out, router_w=None, dense_w=None):
    x_specs, x_args = _x_specs(x)
    out_shape = [jax.ShapeDtypeStruct((N_TOK, D_MODEL), F32)]
    out_specs = [pl.BlockSpec((TOK_TILE, D_MODEL), lambda i: (i, 0))]
    scratch = [pltpu.VMEM((D_MODEL, D_MODEL), BF16)]
    if dense_w is not None:
        extra_specs = [pl.BlockSpec((D_MODEL, 2 * D_FF), lambda i: (0, 0), pipeline_mode=pl.Buffered(1)),
                       pl.BlockSpec((D_FF, D_MODEL), lambda i: (0, 0), pipeline_mode=pl.Buffered(1))]
        extra_args = tuple(dense_w)
        scratch += [pltpu.VMEM((TOK_TILE, D_MODEL), BF16)]
    else:
        rw_pad = jnp.zeros((D_MODEL, LANES), BF16).at[:, :N_EXPERTS].set(router_w.astype(BF16))
        extra_specs, extra_args = [pl.BlockSpec((D_MODEL, LANES), lambda i: (0, 0))], (rw_pad,)
        out_shape += [jax.ShapeDtypeStruct((N_TOK, D_MODEL), BF16)]
        out_specs += [pl.BlockSpec((TOK_TILE, D_MODEL), lambda i: (i, 0))]
        out_shape += [jax.ShapeDtypeStruct((N_TOK, LANES), F32),
                      jax.ShapeDtypeStruct((N_TILES, 8, TOK_TILE), F32),
                      jax.ShapeDtypeStruct((N_TILES, 8, LANES), F32),
                      jax.ShapeDtypeStruct((8, LANES), F32)]
        out_specs += [pl.BlockSpec((TOK_TILE, LANES), lambda i: (i, 0)),
                      pl.BlockSpec((None, 8, TOK_TILE), lambda i: (i, 0, 0)),
                      pl.BlockSpec((None, 8, LANES), lambda i: (i, 0, 0)),
                      pl.BlockSpec((8, LANES), lambda i: (0, 0))]
        scratch += [pltpu.VMEM((8, LANES), F32)]
    return pl.pallas_call(
        functools.partial(_outproj_kernel, ffn="dense" if dense_w is not None else "routed"),
        out_shape=tuple(out_shape),
        grid=(N_TILES,),
        in_specs=[pl.BlockSpec((TOK_TILE, ATT_W), lambda i: (jnp.minimum(i, P_TILES - 1), 0)),
                  pl.BlockSpec((TOK_TILE, ATT_W), lambda i: (jnp.maximum(i - P_TILES, 0), 0)),
                  pl.BlockSpec((TOK_TILE, CONV_CH), lambda i: (i, 0))] + x_specs + [
                  pl.BlockSpec((None, None, 6, D_MODEL), lambda i: (l, _cond_id(i), 0, 0)),
                  pl.BlockSpec((None, 4, D_MODEL), lambda i: (l, 0, 0)),
                  pl.BlockSpec((None, D_MODEL, D_MODEL), lambda i: (l, 0, 0),
                               pipeline_mode=pl.Buffered(1))] + extra_specs,
        out_specs=tuple(out_specs),
        scratch_shapes=scratch,
        compiler_params=_cparams(1),
        name=f"outproj_{l}",
    )(o_ctx, o_lat, cv, *x_args, mod, norm_g, w_out, *extra_args)


def _route_tile(h, rw_ref, info_ref, info_t_ref, cb_ref, tot_ref, carry_ref):
    lane = lax.broadcasted_iota(jnp.int32, (TOK_TILE, LANES), 1)
    lanef = lane.astype(F32)
    logits = jnp.dot(h, rw_ref[...], preferred_element_type=F32)
    logits = jnp.where(lane < N_EXPERTS, logits, -jnp.inf)
    big = jnp.asarray(LANES, F32)
    m1 = jnp.max(logits, axis=-1, keepdims=True)
    e1 = jnp.min(jnp.where(logits == m1, lanef, big), axis=-1, keepdims=True)
    oh1 = lanef == e1
    rest = jnp.where(oh1, -jnp.inf, logits)
    m2 = jnp.max(rest, axis=-1, keepdims=True)
    e2 = jnp.min(jnp.where(rest == m2, lanef, big), axis=-1, keepdims=True)
    oh2 = lanef == e2
    ex = jnp.exp(m2 - m1)
    g1 = 1.0 / (1.0 + ex)
    g2 = ex / (1.0 + ex)

    oh = jnp.where(jnp.logical_or(oh1, oh2), 1.0, 0.0)
    r = lax.broadcasted_iota(jnp.int32, (TOK_TILE, TOK_TILE), 0)
    c = lax.broadcasted_iota(jnp.int32, (TOK_TILE, TOK_TILE), 1)
    tri = jnp.where(c <= r, 1.0, 0.0).astype(BF16)
    incl = jnp.dot(tri, oh.astype(BF16), preferred_element_type=F32)
    carry = carry_ref[0:1, :]
    excl = incl - oh + carry
    rank1 = jnp.sum(jnp.where(oh1, excl, 0.0), axis=-1, keepdims=True)
    rank2 = jnp.sum(jnp.where(oh2, excl, 0.0), axis=-1, keepdims=True)

    info = jnp.where(lane == 0, e1, 0.0)
    for k, col in enumerate((e2, rank1, rank2, g1, g2), start=1):
        info = jnp.where(lane == k, col, info)
    info_ref[...] = info
    info_t_ref[...] = info.T[0:8, :]

    cb_ref[...] = carry_ref[...]
    new_carry = carry + incl[TOK_TILE - 1:TOK_TILE, :]
    carry_ref[...] = jnp.broadcast_to(new_carry, carry_ref.shape)
    tot_ref[...] = jnp.broadcast_to(new_carry, tot_ref.shape)


def _sorted_pos(expert, rank, base_ref):
    start = jnp.zeros_like(rank)
    for e in range(N_EXPERTS):
        start = jnp.where(expert == float(e), base_ref[e].astype(F32), start)
    return start + rank


def _dispatch_kernel(clo_ref, cn_ref, base_ref, info_t_ref, x_ref, o_ref):
    r = pl.program_id(0)
    o_ref[...] = jnp.zeros_like(o_ref)
    rows = (r * DISPATCH_TILE + lax.broadcasted_iota(jnp.int32, (DISPATCH_TILE, TOK_TILE), 0)).astype(F32)

    def body(k, carry):
        c = clo_ref[r] + k
        it = info_t_ref[c]
        pos1 = _sorted_pos(it[0:1, :], it[2:3, :], base_ref)
        pos2 = _sorted_pos(it[1:2, :], it[3:4, :], base_ref)
        hit = jnp.logical_or(rows == pos1, rows == pos2)
        sel = jnp.where(hit, 1.0, 0.0).astype(BF16)
        off = pl.multiple_of(c * TOK_TILE, TOK_TILE)
        o_ref[...] += jnp.dot(sel, x_ref[pl.ds(off, TOK_TILE), :], preferred_element_type=F32).astype(BF16)
        return carry

    lax.fori_loop(0, cn_ref[r], body, 0)


def _dispatch(c_lo, c_n, base, info_t, h2):
    return pl.pallas_call(
        _dispatch_kernel,
        out_shape=jax.ShapeDtypeStruct((SORT_ROWS, D_MODEL), BF16),
        grid_spec=pltpu.PrefetchScalarGridSpec(
            num_scalar_prefetch=3,
            grid=(SORT_ROWS // DISPATCH_TILE,),
            in_specs=[pl.BlockSpec((N_TILES, 8, TOK_TILE), lambda r, *_: (0, 0, 0)),
                      pl.BlockSpec((N_TOK, D_MODEL), lambda r, *_: (0, 0),
                                   pipeline_mode=pl.Buffered(1))],
            out_specs=pl.BlockSpec((DISPATCH_TILE, D_MODEL), lambda r, *_: (r, 0))),
        compiler_params=_cparams(1),
        name="moe_dispatch",
    )(c_lo, c_n, base, info_t, h2)


TILE_UNUSED, TILE_HALF, TILE_FULL = 0, 1, 2
CAST_ROWS = 64


def _expert_weight_ring(c, r, n_c, te_ref, tf_ref, nx_ref, copies, cast):
    @pl.when(jnp.logical_and(c == 0, r == 0))
    def _():
        for cp in copies(te_ref[0], 0):
            cp.start()

    @pl.when(tf_ref[r] == 1)
    def _():
        for cp in copies(te_ref[r], c):
            cp.wait()
        cast()
        nr = nx_ref[r]
        nc = c + (nr <= r).astype(jnp.int32)

        @pl.when(nc < n_c)
        def _():
            for cp in copies(te_ref[nr], nc):
                cp.start()


def _moe_up_kernel(te_ref, tv_ref, tf_ref, nx_ref, x_ref, w_hbm, h_ref, wst_ref, wbf_ref, sem_ref, *, i_moe):
    f = pl.program_id(0)
    r = pl.program_id(1)

    def copies(e, ft):
        out = []
        for part in range(2):
            col = pl.multiple_of((part * UP_TILES + ft) * UP_TILE, LANES)
            out.append(pltpu.make_async_copy(w_hbm.at[i_moe, e, :, pl.ds(col, UP_TILE)],
                                             wst_ref.at[part], sem_ref.at[part]))
        return out

    def cast():
        def rows_block(b, carry):
            rows = pl.ds(pl.multiple_of(b * CAST_ROWS, CAST_ROWS), CAST_ROWS)
            wbf_ref[:, rows, :] = wst_ref[:, rows, :].astype(BF16)
            return carry

        lax.fori_loop(0, D_MODEL // CAST_ROWS, rows_block, 0)

    _expert_weight_ring(f, r, UP_TILES, te_ref, tf_ref, nx_ref, copies, cast)

    def hidden(x):
        gt = jnp.dot(x, wbf_ref[0], preferred_element_type=F32)
        up = jnp.dot(x, wbf_ref[1], preferred_element_type=F32)
        return (gt * _sigmoid(gt) * up).astype(BF16)

    half = ROW_TILE // 2

    @pl.when(tv_ref[r] == TILE_FULL)
    def _():
        h_ref[...] = hidden(x_ref[...])

    @pl.when(tv_ref[r] == TILE_HALF)
    def _():
        h_ref[0:half, :] = hidden(x_ref[0:half, :])
        h_ref[half:, :] = jnp.zeros((half, UP_TILE), BF16)

    @pl.when(tv_ref[r] == TILE_UNUSED)
    def _():
        h_ref[...] = jnp.zeros_like(h_ref)


def _moe_up(i_moe, tile_e, tile_valid, tile_first, tile_next, xs, w13):
    return pl.pallas_call(
        functools.partial(_moe_up_kernel, i_moe=i_moe),
        out_shape=jax.ShapeDtypeStruct((SORT_ROWS, D_FF_EXPERT), BF16),
        grid_spec=pltpu.PrefetchScalarGridSpec(
            num_scalar_prefetch=4,
            grid=(UP_TILES, SORT_TILES),
            in_specs=[pl.BlockSpec((ROW_TILE, D_MODEL), lambda f, r, *_: (r, 0)),
                      pl.BlockSpec(memory_space=pl.ANY)],
            out_specs=pl.BlockSpec((ROW_TILE, UP_TILE), lambda f, r, *_: (r, f)),
            scratch_shapes=[pltpu.VMEM((2, D_MODEL, UP_TILE), F32),
                            pltpu.VMEM((2, D_MODEL, UP_TILE), BF16),
                            pltpu.SemaphoreType.DMA((2,))]),
        compiler_params=_cparams(2),
        name="moe_up",
    )(tile_e, tile_valid, tile_first, tile_next, xs, w13)


def _moe_down_kernel(te_ref, tv_ref, tf_ref, nx_ref, h_ref, w_hbm, y_ref, wst_ref, wbf_ref, sem_ref, *, i_moe):
    n = pl.program_id(0)
    r = pl.program_id(1)

    def copies(e, nt):
        col = pl.multiple_of(nt * DOWN_TILE, LANES)
        return [pltpu.make_async_copy(w_hbm.at[i_moe, e, :, pl.ds(col, DOWN_TILE)], wst_ref, sem_ref.at[0])]

    def cast():
        wbf_ref[...] = wst_ref[...].astype(BF16)

    _expert_weight_ring(n, r, D_MODEL // DOWN_TILE, te_ref, tf_ref, nx_ref, copies, cast)

    half = ROW_TILE // 2

    @pl.when(tv_ref[r] == TILE_FULL)
    def _():
        y_ref[...] = jnp.dot(h_ref[...], wbf_ref[...], preferred_element_type=F32).astype(BF16)

    @pl.when(tv_ref[r] == TILE_HALF)
    def _():
        y_ref[0:half, :] = jnp.dot(h_ref[0:half, :], wbf_ref[...], preferred_element_type=F32).astype(BF16)
        y_ref[half:, :] = jnp.zeros((half, DOWN_TILE), BF16)

    @pl.when(tv_ref[r] == TILE_UNUSED)
    def _():
        y_ref[...] = jnp.zeros_like(y_ref)


def _moe_down(i_moe, tile_e, tile_valid, tile_first, tile_next, hs, w2):
    return pl.pallas_call(
        functools.partial(_moe_down_kernel, i_moe=i_moe),
        out_shape=jax.ShapeDtypeStruct((SORT_ROWS, D_MODEL), BF16),
        grid_spec=pltpu.PrefetchScalarGridSpec(
            num_scalar_prefetch=4,
            grid=(D_MODEL // DOWN_TILE, SORT_TILES),
            in_specs=[pl.BlockSpec((ROW_TILE, D_FF_EXPERT), lambda n, r, *_: (r, 0)),
                      pl.BlockSpec(memory_space=pl.ANY)],
            out_specs=pl.BlockSpec((ROW_TILE, DOWN_TILE), lambda n, r, *_: (r, n)),
            scratch_shapes=[pltpu.VMEM((D_FF_EXPERT, DOWN_TILE), F32),
                            pltpu.VMEM((D_FF_EXPERT, DOWN_TILE), BF16),
                            pltpu.SemaphoreType.DMA((1,))]),
        compiler_params=_cparams(2),
        name="moe_down",
    )(tile_e, tile_valid, tile_first, tile_next, hs, w2)


def _window_copy(y_hbm, ybuf_ref, sem_ref, src, buf, slot):
    return pltpu.make_async_copy(y_hbm.at[pl.ds(src, WIN), :],
                                 ybuf_ref.at[buf, pl.ds(slot * WIN, WIN), :], sem_ref.at[buf, slot])


def _start_windows(src_ref, y_hbm, ybuf_ref, sem_ref, tile, buf):
    for s in range(N_WIN):
        src = pl.multiple_of(src_ref[tile * N_WIN + s], BF16_SUBLANES)
        _window_copy(y_hbm, ybuf_ref, sem_ref, src, buf, s).start()


def _combine_kernel(src_ref, lo_ref, hi_ref, base_ref, info_ref, x_ref, mod_ref, g_ref, y_hbm,
                    yp_ref, ys_ref, ybuf_ref, sem_ref):
    j = pl.program_id(0)
    buf = j % 2

    @pl.when(j == 0)
    def _():
        _start_windows(src_ref, y_hbm, ybuf_ref, sem_ref, 0, 0)

    @pl.when(j + 1 < N_TILES)
    def _():
        _start_windows(src_ref, y_hbm, ybuf_ref, sem_ref, j + 1, 1 - buf)

    info = info_ref[...]
    pos1 = _sorted_pos(info[:, 0:1], info[:, 2:3], base_ref)
    pos2 = _sorted_pos(info[:, 1:2], info[:, 3:4], base_ref)
    g1, g2 = info[:, 4:5], info[:, 5:6]
    iota = lax.broadcasted_iota(jnp.int32, (1, WIN), 1)
    ids = []
    for s in range(N_WIN):
        row = src_ref[j * N_WIN + s] + iota
        ok = jnp.logical_and(row >= lo_ref[j * N_WIN + s], row < hi_ref[j * N_WIN + s])
        ids.append(jnp.where(ok, row, -1))
    row_id = jnp.concatenate(ids, axis=1).astype(F32)

    for s in range(N_WIN):
        _window_copy(y_hbm, ybuf_ref, sem_ref, 0, buf, s).wait()
    halves = []
    for rows in (slice(0, TOK_TILE // 2), slice(TOK_TILE // 2, TOK_TILE)):
        sel = (jnp.where(pos1[rows] == row_id, g1[rows], 0.0)
               + jnp.where(pos2[rows] == row_id, g2[rows], 0.0)).astype(BF16)
        ffn = jnp.dot(sel, ybuf_ref[buf], preferred_element_type=F32)
        halves.append(x_ref[rows, :] + mod_ref[5:6, :] * _rms(ffn, g_ref[3:4, :]))
    out = jnp.concatenate(halves, axis=0)

    @pl.when(j < P_TILES)
    def _():
        yp_ref[...] = out

    @pl.when(j >= P_TILES)
    def _():
        ys_ref[...] = out


def _combine(l, win_src, win_lo, win_hi, base, info, x, mod, norm_g, ys):
    return pl.pallas_call(
        _combine_kernel,
        out_shape=(jax.ShapeDtypeStruct((N_PROMPT, D_MODEL), F32),
                   jax.ShapeDtypeStruct((N_SAMPLE, D_MODEL), F32)),
        grid_spec=pltpu.PrefetchScalarGridSpec(
            num_scalar_prefetch=4,
            grid=(N_TILES,),
            in_specs=[pl.BlockSpec((TOK_TILE, LANES), lambda i, *_: (i, 0)),
                      pl.BlockSpec((TOK_TILE, D_MODEL), lambda i, *_: (i, 0)),
                      pl.BlockSpec((None, None, 6, D_MODEL), lambda i, *_: (l, _cond_id(i), 0, 0)),
                      pl.BlockSpec((None, 4, D_MODEL), lambda i, *_: (l, 0, 0)),
                      pl.BlockSpec(memory_space=pl.ANY)],
            out_specs=(pl.BlockSpec((TOK_TILE, D_MODEL), lambda i, *_: (jnp.minimum(i, P_TILES - 1), 0)),
                       pl.BlockSpec((TOK_TILE, D_MODEL), lambda i, *_: (jnp.maximum(i - P_TILES, 0), 0))),
            scratch_shapes=[pltpu.VMEM((2, N_WIN * WIN, D_MODEL), BF16),
                            pltpu.SemaphoreType.DMA((2, N_WIN))]),
        compiler_params=_cparams(1),
        name="moe_combine",
    )(win_src, win_lo, win_hi, base, info, x, mod, norm_g, ys)


def _moe_ffn(l, i_moe, h2, x, mod, norm_g, routing, moe_w13, moe_w2):
    info, info_t, cb, tot = routing

    ar8 = jnp.arange(N_EXPERTS, dtype=jnp.int32)

    def at8(vec, idx):
        return jnp.sum(jnp.where(idx[..., None] == ar8, vec, 0), axis=-1)

    def cumsum8(a):
        return jnp.sum(jnp.where(ar8[None, :] <= ar8[:, None], a[..., None, :], 0), axis=-1)

    counts = tot[0, :N_EXPERTS].astype(jnp.int32)
    padded = (counts + ROW_TILE - 1) // ROW_TILE * ROW_TILE
    seg_end = cumsum8(padded)
    base = (seg_end - padded).astype(jnp.int32)
    cbx = cb[:, 0, :N_EXPERTS].astype(jnp.int32)
    cb_end = jnp.concatenate([cbx[1:], counts[None, :]], axis=0)

    def tile_segments(n_tiles, rows):
        row0 = jnp.arange(n_tiles, dtype=jnp.int32) * rows
        e = jnp.minimum(jnp.sum(seg_end[None, :] <= row0[:, None], axis=1), N_EXPERTS - 1).astype(jnp.int32)
        k0 = row0 - at8(base, e)
        return e, k0, jnp.logical_and(k0 >= 0, k0 < at8(counts, e))

    tile_e, k0, tile_valid = tile_segments(SORT_TILES, ROW_TILE)
    tile_first = jnp.logical_and(tile_valid, k0 == 0)
    d_e, d_k0, d_valid = tile_segments(SORT_ROWS // DISPATCH_TILE, DISPATCH_TILE)
    d_kend = jnp.minimum(d_k0 + DISPATCH_TILE, at8(counts, d_e))
    cb_t = jnp.sum(jnp.where(d_e[:, None, None] == ar8[None, :, None], cbx.T[None, :, :], 0), axis=1)
    c_lo = jnp.sum(cb_t <= d_k0[:, None], axis=1) - 1
    c_hi = jnp.sum(cb_t < d_kend[:, None], axis=1) - 1
    c_n = jnp.where(d_valid, c_hi - c_lo + 1, 0).astype(jnp.int32)
    c_lo = jnp.where(d_valid, c_lo, 0).astype(jnp.int32)
    tile_idx = jnp.arange(SORT_TILES, dtype=jnp.int32)
    first_idx = jnp.where(tile_first, tile_idx, SORT_TILES)
    later = jnp.where(tile_idx[None, :] > tile_idx[:, None], first_idx[None, :], SORT_TILES)
    tile_next = jnp.min(later, axis=1)
    tile_next = jnp.where(tile_next >= SORT_TILES, 0, tile_next).astype(jnp.int32)
    tile_rows = at8(counts, tile_e) - k0
    tile_valid = jnp.where(tile_valid, jnp.where(tile_rows <= ROW_TILE // 2, TILE_HALF, TILE_FULL),
                           TILE_UNUSED).astype(jnp.int32)
    tile_first = tile_first.astype(jnp.int32)

    seg_lo = base[None, :] + cbx
    seg_hi = base[None, :] + cb_end
    seg_n = seg_hi - seg_lo
    w0 = seg_lo // BF16_SUBLANES * BF16_SUBLANES
    nw = jnp.where(seg_n > 0, (seg_lo - w0 + seg_n + WIN - 1) // WIN, 0)
    nw_end = cumsum8(nw)
    slot = jnp.arange(N_WIN, dtype=jnp.int32)
    slot_e = jnp.minimum(jnp.sum(nw_end[:, None, :] <= slot[None, :, None], axis=2), N_EXPERTS - 1)
    pick = slot_e[:, :, None] == jnp.arange(N_EXPERTS, dtype=jnp.int32)[None, None, :]
    take = lambda a: jnp.sum(jnp.where(pick, a[:, None, :], 0), axis=2)
    slot_k = slot[None, :] - (take(nw_end) - take(nw))
    slot_ok = slot[None, :] < nw_end[:, -1:]
    slot_src = take(w0) + WIN * slot_k
    win_src = jnp.where(slot_ok, slot_src, 0).astype(jnp.int32).reshape(-1)
    win_lo = jnp.where(slot_ok, take(seg_lo), 0).astype(jnp.int32).reshape(-1)
    win_hi = jnp.where(slot_ok, take(seg_hi), 0).astype(jnp.int32).reshape(-1)

    xs = _dispatch(c_lo, c_n, base, info_t, h2)
    hs = _moe_up(i_moe, tile_e, tile_valid, tile_first, tile_next, xs, moe_w13)
    ys = _moe_down(i_moe, tile_e, tile_valid, tile_first, tile_next, hs, moe_w2)
    return _combine(l, win_src, win_lo, win_hi, base, info, x, mod, norm_g, ys)


def kernel(x_prompt, x_sample, cache_k, cache_v, c, c_ctx, w_ada, b_ada, norm_g, w_in, w_out,
           lam_params, subln_g, dw_weight, dw_bias, conv_ln_g, conv_ln_b, dense_w13, dense_w2,
           router_w, moe_w13, moe_w2):
    x = (x_prompt.reshape(N_PROMPT, D_MODEL), x_sample.reshape(N_SAMPLE, D_MODEL))
    cond8 = jnp.zeros((8, D_MODEL), F32).at[0].set(c_ctx).at[1:1 + DEC_BATCH].set(c)
    mod = _ada_modulation(cond8, w_ada, b_ada).reshape(DEPTH, 8, 6, D_MODEL)
    rope = _rope_tables()
    ck = cache_k.reshape(DEC_BATCH, DEPTH, PAST_LEN * ATT_HEADS, V_DIM)
    cv = cache_v.reshape(DEC_BATCH, DEPTH, PAST_LEN * ATT_HEADS, V_DIM)

    caches = None
    for l in range(DEPTH):
        lam_init = 0.8 - 0.6 * math.exp(-0.3 * l)
        qkv, cvo, *caches = _mixer_in(l, x, mod, norm_g, w_in, rope,
                                      (dw_weight, dw_bias, conv_ln_g, conv_ln_b), caches)
        o_ctx, o_lat = _attention(l, lam_init, lam_params, subln_g, qkv, ck, cv)
        i = l // 2
        if l % 2 == 0:
            (x,) = _outproj(l, o_ctx, o_lat, cvo, x, mod, norm_g, w_out,
                            dense_w=(dense_w13[i].astype(BF16), dense_w2[i].astype(BF16)))
        else:
            x, h2, *routing = _outproj(l, o_ctx, o_lat, cvo, x, mod, norm_g, w_out, router_w=router_w[i])
            x = _moe_ffn(l, i, h2, x, mod, norm_g, routing, moe_w13, moe_w2)

    xp, xs = x if isinstance(x, tuple) else (x[:N_PROMPT], x[N_PROMPT:])
    new_k, new_v = (a.reshape(BATCH, DEPTH, SEQ, ATT_HEADS, V_DIM) for a in caches)
    return (xp.reshape(BATCH, SEQ, D_MODEL), xs.reshape(DEC_BATCH, DEC_SEQ, D_MODEL), new_k, new_v)
```

```python
import functools
import math

import jax
import jax.numpy as jnp
import numpy as np
from jax import lax
from jax.experimental import pallas as pl
from jax.experimental.pallas import tpu as pltpu

F32 = jnp.float32
BF16 = jnp.bfloat16

D_MODEL = 1024
BATCH = 32
SEQ = 256
DEPTH = 2
DEC_BATCH = 2
DEC_SEQ = 2048
PAST_LEN = 512
GRID_W = 64
ATT_HEADS = 4
QK_DIM = 64
V_DIM = 128
ATT_W = ATT_HEADS * V_DIM
IN_W = 5 * ATT_W
CONV_CH = 512
CONV_WIDTH = 31
D_FF = 2816
N_EXPERTS = 8
D_FF_EXPERT = 3584
ROPE_THETA = 10000.0
NORM_EPS = 1e-6
LN_EPS = 1e-5

N_PROMPT = BATCH * SEQ
N_SAMPLE = DEC_BATCH * DEC_SEQ
N_TOK = N_PROMPT + N_SAMPLE

LANES = 128
BF16_SUBLANES = 16
MXU_TILE = 256
VMEM_BYTES = 64 * 1024 * 1024
VMEM_LIMIT = VMEM_BYTES - 8 * 1024 * 1024

TOK_TILE = 512
N_TILES = N_TOK // TOK_TILE
P_TILES = N_PROMPT // TOK_TILE
S_TILES_PER_BATCH = DEC_SEQ // TOK_TILE

ROW_TILE = 512
SORT_TILES = (2 * N_TOK + N_EXPERTS * ROW_TILE) // ROW_TILE + 1
SORT_ROWS = SORT_TILES * ROW_TILE
DISPATCH_TILE = 256
UP_TILE = 7 * MXU_TILE
UP_TILES = D_FF_EXPERT // UP_TILE
DOWN_TILE = 4 * MXU_TILE
WIN = 128
N_WIN = (2 * TOK_TILE + N_EXPERTS * (BF16_SUBLANES - 1 + WIN - 1)) // WIN

assert D_FF_EXPERT % UP_TILE == 0 and D_MODEL % DOWN_TILE == 0 and D_FF % MXU_TILE == 0
assert ROW_TILE % DISPATCH_TILE == 0 and N_TOK % TOK_TILE == 0 and N_PROMPT % TOK_TILE == 0
assert DEPTH == 2, "layer 0 zero-fills exactly one later cache slab; the last layer must be the routed one"


def _cparams(n_axes):
    return pltpu.CompilerParams(dimension_semantics=("arbitrary",) * n_axes,
                                vmem_limit_bytes=VMEM_LIMIT)


def _cond_id(i):
    return jnp.where(i < P_TILES, 0, 1 + (i - P_TILES) // S_TILES_PER_BATCH)


def _sigmoid(x):
    return 1.0 / (1.0 + jnp.exp(-x))


def _rms(x, g):
    return x * lax.rsqrt(jnp.mean(x * x, axis=-1, keepdims=True) + NORM_EPS) * g


def _ada_kernel(c_ref, w_ref, b_ref, o_ref):
    c = c_ref[...]
    s = (c * _sigmoid(c)).astype(BF16)
    o_ref[...] = jnp.dot(s, w_ref[...].astype(BF16), preferred_element_type=F32) + b_ref[...]


def _ada_modulation(cond8, w_ada, b_ada):
    tn = 1536
    n = 6 * D_MODEL
    return pl.pallas_call(
        _ada_kernel,
        out_shape=jax.ShapeDtypeStruct((DEPTH, 8, n), F32),
        grid=(DEPTH, n // tn),
        in_specs=[pl.BlockSpec((8, D_MODEL), lambda l, j: (0, 0)),
                  pl.BlockSpec((None, D_MODEL, tn), lambda l, j: (l, 0, j)),
                  pl.BlockSpec((None, 1, tn), lambda l, j: (l, 0, j))],
        out_specs=pl.BlockSpec((None, 8, tn), lambda l, j: (l, 0, j)),
        compiler_params=_cparams(2),
        name="ada_modulation",
    )(cond8, w_ada, b_ada.reshape(DEPTH, 1, n))


def _tile_x(xa_ref, xb_ref):
    return jnp.where(pl.program_id(0) < P_TILES, xa_ref[...], xb_ref[...])


def _x_specs(x):
    last = N_TILES - 1
    if isinstance(x, tuple):
        xa, xb = x
        b_idx = lambda i, *_: (jnp.maximum(jnp.minimum(i, last) - P_TILES, 0), 0)
    else:
        xa = xb = x
        b_idx = lambda i, *_: (jnp.maximum(jnp.minimum(i, last), P_TILES), 0)
    a_idx = lambda i, *_: (jnp.minimum(i, P_TILES - 1), 0)
    return [pl.BlockSpec((TOK_TILE, D_MODEL), a_idx), pl.BlockSpec((TOK_TILE, D_MODEL), b_idx)], (xa, xb)


ROT_SPAN = QK_DIM // 4
SEQ_PER_TILE = TOK_TILE // SEQ
CACHE_ROWS = SEQ * ATT_HEADS


CONV_LAG = 2
CONV_RING = 3
CONV_TILE = 256
CONV_HALO = 16
CONV_SUB = 32


def _conv_pass(upad_ref, w_ref, bias_ref, lg_ref, lb_ref, o_ref, row0):
    rows = CONV_TILE + 2 * CONV_HALO - 8
    for s in range(1, 8):
        upad_ref[s, 0:rows, :] = upad_ref[0, s:s + rows, :]
    first_tap = CONV_HALO - CONV_WIDTH // 2
    groups = CONV_SUB // 8
    for t in range(CONV_TILE // CONV_SUB):
        base = t * CONV_SUB
        acc = jnp.zeros((groups, 8, CONV_CH), F32)
        for j in range(CONV_WIDTH):
            lo = base + (first_tap + j) // 8 * 8
            taps = upad_ref[(first_tap + j) % 8, lo:lo + CONV_SUB, :].reshape(groups, 8, CONV_CH)
            acc = acc + taps * w_ref[j][None]
        y = acc.reshape(CONV_SUB, CONV_CH) + bias_ref[...]
        mu = jnp.mean(y, axis=-1, keepdims=True)
        yc = y - mu
        var = jnp.mean(yc * yc, axis=-1, keepdims=True)
        z = yc * lax.rsqrt(var + LN_EPS) * lg_ref[...] + lb_ref[...]
        o_ref[row0 + base:row0 + base + CONV_SUB, :] = (z * _sigmoid(z)).astype(BF16)


def _mixer_in_kernel(*refs, layer):
    n_in = 12 + (2 if layer else 0)
    (xa_ref, xb_ref, mod_ref, g_ref, w_ref, cos_ref, sina_ref, sinb_ref,
     dw_ref, db_ref, lg_ref, lb_ref) = refs[:12]
    qkv_ref, cv_ref, kc_ref, vc_ref, wbf_ref, ring_ref, glu_ref, pj_ref, upad_ref, taps_ref = refs[n_in:]
    i = pl.program_id(0)
    t = jnp.minimum(i, N_TILES - 1)

    @pl.when(i == 0)
    def _():
        wbf_ref[...] = w_ref[...].astype(BF16)
        ring_ref[...] = jnp.zeros_like(ring_ref)
        glu_ref[...] = jnp.zeros_like(glu_ref)
        for j in range(CONV_WIDTH):
            taps_ref[j] = jnp.broadcast_to(dw_ref[j:j + 1, :], (8, CONV_CH))

    ring_ref[(i - 1) % CONV_RING] = glu_ref[...]

    h = _rms(_tile_x(xa_ref, xb_ref), g_ref[0:1, :]) * (1.0 + mod_ref[1:2, :]) + mod_ref[0:1, :]
    hb = h.astype(BF16)

    def proj(part):
        return jnp.dot(hb, wbf_ref[:, part * ATT_W:(part + 1) * ATT_W], preferred_element_type=F32)

    for part in range(3):
        pj_ref[:, part * ATT_W:(part + 1) * ATT_W] = proj(part)
    glu_ref[...] = proj(3) * _sigmoid(proj(4))

    j = i - CONV_LAG
    latent = j >= P_TILES
    q = (j - P_TILES) % S_TILES_PER_BATCH
    has_prev = jnp.logical_and(latent, q != 0)
    has_next = jnp.logical_and(latent, q != S_TILES_PER_BATCH - 1)
    cur = j % CONV_RING
    prev_tail = ring_ref[(j - 1) % CONV_RING, TOK_TILE - CONV_HALO:, :]
    next_head = ring_ref[(j + 1) % CONV_RING, 0:CONV_HALO, :]
    for s in range(TOK_TILE // CONV_TILE):
        lo = s * CONV_TILE
        if s == 0:
            before = jnp.where(has_prev, prev_tail, 0.0)
        else:
            before = jnp.where(latent, ring_ref[cur, lo - CONV_HALO:lo, :], 0.0)
        if s == TOK_TILE // CONV_TILE - 1:
            after = jnp.where(has_next, next_head, 0.0)
        else:
            after = jnp.where(latent, ring_ref[cur, lo + CONV_TILE:lo + CONV_TILE + CONV_HALO, :], 0.0)
        upad_ref[0, 0:CONV_HALO, :] = before
        upad_ref[0, CONV_HALO:CONV_HALO + CONV_TILE, :] = ring_ref[cur, lo:lo + CONV_TILE, :]
        upad_ref[0, CONV_HALO + CONV_TILE:, :] = after
        _conv_pass(upad_ref, taps_ref, db_ref, lg_ref, lb_ref, cv_ref, lo)

    qkv_ref[:, 2 * ATT_W:] = pj_ref[:, 2 * ATT_W:].astype(BF16)

    @pl.when(t < P_TILES)
    def _():
        qkv_ref[:, :2 * ATT_W] = pj_ref[:, :2 * ATT_W].astype(BF16)
        for ref, col0 in ((kc_ref, ATT_W), (vc_ref, 2 * ATT_W)):
            for s in range(SEQ_PER_TILE):
                for hd in range(ATT_HEADS):
                    val = pj_ref[SEQ * s:SEQ * (s + 1), col0 + V_DIM * hd:col0 + V_DIM * (hd + 1)]
                    rows = pl.ds(hd, SEQ, stride=ATT_HEADS)
                    if layer == 0:
                        ref[s, 0, rows, :] = val
                    else:
                        ref[s, rows, :] = val
            if layer == 0:
                ref[:, 1:] = jnp.zeros((SEQ_PER_TILE, DEPTH - 1, CACHE_ROWS, V_DIM), F32)

    @pl.when(t >= P_TILES)
    def _():
        cos = cos_ref[...]
        sina = sina_ref[...]
        sinb = sinb_ref[...]
        for c in range(2 * ATT_W // V_DIM):
            xg = pj_ref[:, V_DIM * c:V_DIM * (c + 1)]
            fwd = pltpu.roll(xg, V_DIM - ROT_SPAN, 1)
            bwd = pltpu.roll(xg, ROT_SPAN, 1)
            qkv_ref[:, V_DIM * c:V_DIM * (c + 1)] = (xg * cos + fwd * sina + bwd * sinb).astype(BF16)


def _mixer_in(l, x, mod, norm_g, w_in, rope, conv_params, caches):
    cos, sina, sinb = rope
    dw_w, dw_b, ln_g, ln_b = conv_params
    x_specs, x_args = _x_specs(x)

    def tile(i):
        return jnp.minimum(i, N_TILES - 1)

    def rope_idx(i):
        return (jnp.maximum(tile(i) - P_TILES, 0) % S_TILES_PER_BATCH, 0)

    def vec(a):
        return a.reshape(DEPTH, 1, CONV_CH)

    cache_shape = jax.ShapeDtypeStruct((BATCH, DEPTH, CACHE_ROWS, V_DIM), F32)
    if l == 0:
        cache_spec = pl.BlockSpec((SEQ_PER_TILE, DEPTH, CACHE_ROWS, V_DIM),
                                  lambda i: (jnp.minimum(i, P_TILES - 1), 0, 0, 0))
        extra_specs, extra_args, aliases = [], (), {}
    else:
        cache_spec = pl.BlockSpec((SEQ_PER_TILE, None, CACHE_ROWS, V_DIM),
                                  lambda i: (jnp.minimum(i, P_TILES - 1), l, 0, 0))
        extra_specs = [pl.BlockSpec(memory_space=pl.ANY)] * 2
        extra_args = tuple(caches)
        aliases = {12: 2, 13: 3}

    return pl.pallas_call(
        functools.partial(_mixer_in_kernel, layer=l),
        out_shape=(jax.ShapeDtypeStruct((N_TOK, 3 * ATT_W), BF16),
                   jax.ShapeDtypeStruct((N_TOK, CONV_CH), BF16), cache_shape, cache_shape),
        grid=(N_TILES + CONV_LAG,),
        in_specs=x_specs + [
            pl.BlockSpec((None, None, 6, D_MODEL), lambda i: (l, _cond_id(tile(i)), 0, 0)),
            pl.BlockSpec((None, 4, D_MODEL), lambda i: (l, 0, 0)),
            pl.BlockSpec((None, D_MODEL, IN_W), lambda i: (l, 0, 0), pipeline_mode=pl.Buffered(1)),
            pl.BlockSpec((TOK_TILE, V_DIM), rope_idx),
            pl.BlockSpec((TOK_TILE, V_DIM), rope_idx),
            pl.BlockSpec((TOK_TILE, V_DIM), rope_idx),
            pl.BlockSpec((None, CONV_WIDTH, CONV_CH), lambda i: (l, 0, 0)),
            pl.BlockSpec((None, 1, CONV_CH), lambda i: (l, 0, 0)),
            pl.BlockSpec((None, 1, CONV_CH), lambda i: (l, 0, 0)),
            pl.BlockSpec((None, 1, CONV_CH), lambda i: (l, 0, 0))] + extra_specs,
        out_specs=(pl.BlockSpec((TOK_TILE, 3 * ATT_W), lambda i: (tile(i), 0)),
                   pl.BlockSpec((TOK_TILE, CONV_CH), lambda i: (jnp.maximum(i - CONV_LAG, 0), 0)),
                   cache_spec, cache_spec),
        scratch_shapes=[pltpu.VMEM((D_MODEL, IN_W), BF16),
                        pltpu.VMEM((CONV_RING, TOK_TILE, CONV_CH), F32),
                        pltpu.VMEM((TOK_TILE, CONV_CH), F32),
                        pltpu.VMEM((TOK_TILE, 3 * ATT_W), F32),
                        pltpu.VMEM((8, CONV_TILE + 2 * CONV_HALO, CONV_CH), F32),
                        pltpu.VMEM((CONV_WIDTH, 8, CONV_CH), F32)],
        input_output_aliases=aliases,
        compiler_params=_cparams(1),
        name=f"mixer_in_{l}",
    )(*x_args, mod, norm_g, w_in, cos, sina, sinb, dw_w, vec(dw_b), vec(ln_g), vec(ln_b), *extra_args)


def _rope_tables():
    rows = DEC_SEQ // GRID_W
    row_pos = np.repeat(np.arange(rows, dtype=np.float64), GRID_W)
    col_pos = np.tile(np.arange(GRID_W, dtype=np.float64), rows)
    half = QK_DIM // 2
    inv_freq = 1.0 / (ROPE_THETA ** (np.arange(0, half, 2, dtype=np.float64) / half))
    ang_r = row_pos[:, None] * inv_freq
    ang_c = col_pos[:, None] * inv_freq
    ang = np.concatenate([ang_r, ang_r, ang_c, ang_c], axis=-1)
    cos = np.tile(np.cos(ang), (1, 2)).astype(np.float32)
    sin = np.tile(np.sin(ang), (1, 2)).astype(np.float32)
    first = (np.arange(V_DIM) % (2 * ROT_SPAN)) < ROT_SPAN
    sina = np.where(first[None, :], -sin, 0.0).astype(np.float32)
    sinb = np.where(first[None, :], 0.0, sin).astype(np.float32)
    return jnp.asarray(cos), jnp.asarray(sina), jnp.asarray(sinb)


def _attn_kernel(*refs, lam_init, has_ext):
    if has_ext:
        lamp_ref, sub_ref, q_ref, k_ref, v_ref, ke_ref, ve_ref, o_ref = refs
    else:
        lamp_ref, sub_ref, q_ref, k_ref, v_ref, o_ref = refs
    lp = lamp_ref[...]
    lam = (jnp.exp(jnp.sum(lp[0:1] * lp[1:2], axis=-1, keepdims=True))
           - jnp.exp(jnp.sum(lp[2:3] * lp[3:4], axis=-1, keepdims=True)) + lam_init)
    lane = lax.broadcasted_iota(jnp.int32, (1, V_DIM), 1)
    nt = (((1,), (1,)), ((), ()))
    scale = QK_DIM ** -0.5
    map_scale = [jnp.where(lane < QK_DIM, scale, 0.0).astype(BF16),
                 jnp.where(lane < QK_DIM, 0.0, scale).astype(BF16)]

    tq = q_ref.shape[0]
    head_cols = [slice(V_DIM * hd, V_DIM * (hd + 1)) for hd in range(ATT_HEADS)]

    def q_map(hd, m):
        return q_ref[:, head_cols[hd]] * map_scale[m]

    def with_ones(v):
        return jnp.concatenate([v, jnp.ones_like(v)], axis=1)

    outs = []
    if has_ext:
        for hd in range(ATT_HEADS):
            kh = k_ref[:, head_cols[hd]]
            v_aug = with_ones(v_ref[:, head_cols[hd]])
            head_rows = pl.ds(hd, PAST_LEN, stride=ATT_HEADS)
            keh = ke_ref[head_rows, :].astype(BF16)
            ve_aug = with_ones(ve_ref[head_rows, :].astype(BF16))
            ratio = []
            for m in range(2):
                qm = q_map(hd, m)
                s = lax.dot_general(qm, kh, nt, preferred_element_type=F32)
                se = lax.dot_general(qm, keh, nt, preferred_element_type=F32)
                mx = jnp.maximum(jnp.max(s, axis=-1, keepdims=True), jnp.max(se, axis=-1, keepdims=True))
                pv = (jnp.dot(jnp.exp((s - mx).astype(BF16)), v_aug, preferred_element_type=F32)
                      + jnp.dot(jnp.exp((se - mx).astype(BF16)), ve_aug, preferred_element_type=F32))
                ratio.append(pv[:, :V_DIM] / pv[:, V_DIM:])
            outs.append(ratio[0] - lam * ratio[1])
    else:
        pairs = [(hd, m) for hd in range(ATT_HEADS) for m in range(2)]
        s = jnp.concatenate([lax.dot_general(q_map(hd, m), k_ref[:, head_cols[hd]], nt,
                                             preferred_element_type=F32) for hd, m in pairs], axis=0)
        p = jnp.exp((s - jnp.max(s, axis=-1, keepdims=True)).astype(BF16))
        for hd in range(ATT_HEADS):
            pv = jnp.dot(p[2 * tq * hd:2 * tq * (hd + 1)], with_ones(v_ref[:, head_cols[hd]]),
                         preferred_element_type=F32)
            ratio = pv[:, :V_DIM] / pv[:, V_DIM:]
            outs.append(ratio[:tq] - lam * ratio[tq:])

    heads = [(_rms(o, sub_ref[...]) * (1.0 - lam_init)).astype(BF16) for o in outs]
    o_ref[...] = jnp.concatenate(heads, axis=1)


def _attention(l, lam_init, lam_params, subln_g, qkvg, cache_k, cache_v):
    small = [pl.BlockSpec((None, 4, QK_DIM), lambda *_: (l, 0, 0)),
             pl.BlockSpec((None, 1, V_DIM), lambda *_: (l, 0, 0))]
    sub3 = subln_g.reshape(DEPTH, 1, V_DIM)

    o_ctx = pl.pallas_call(
        functools.partial(_attn_kernel, lam_init=lam_init, has_ext=False),
        out_shape=jax.ShapeDtypeStruct((N_PROMPT, ATT_W), BF16),
        grid=(BATCH,),
        in_specs=small + [pl.BlockSpec((SEQ, ATT_W), lambda b: (b, 0)),
                          pl.BlockSpec((SEQ, ATT_W), lambda b: (b, 1)),
                          pl.BlockSpec((SEQ, ATT_W), lambda b: (b, 2))],
        out_specs=pl.BlockSpec((SEQ, ATT_W), lambda b: (b, 0)),
        compiler_params=_cparams(1),
        name=f"attn_ctx_{l}",
    )(lam_params, sub3, qkvg, qkvg, qkvg)

    tq = 256
    q_tiles = DEC_SEQ // tq
    q_base = N_PROMPT // tq
    kv_base = N_PROMPT // DEC_SEQ

    def q_idx(b, i):
        return (q_base + b * q_tiles + i, 0)

    o_lat = pl.pallas_call(
        functools.partial(_attn_kernel, lam_init=lam_init, has_ext=True),
        out_shape=jax.ShapeDtypeStruct((N_SAMPLE, ATT_W), BF16),
        grid=(DEC_BATCH, q_tiles),
        in_specs=small + [pl.BlockSpec((tq, ATT_W), q_idx),
                          pl.BlockSpec((DEC_SEQ, ATT_W), lambda b, i: (kv_base + b, 1)),
                          pl.BlockSpec((DEC_SEQ, ATT_W), lambda b, i: (kv_base + b, 2)),
                          pl.BlockSpec((None, None, PAST_LEN * ATT_HEADS, V_DIM), lambda b, i: (b, l, 0, 0)),
                          pl.BlockSpec((None, None, PAST_LEN * ATT_HEADS, V_DIM), lambda b, i: (b, l, 0, 0))],
        out_specs=pl.BlockSpec((tq, ATT_W), lambda b, i: (b * q_tiles + i, 0)),
        compiler_params=_cparams(2),
        name=f"attn_lat_{l}",
    )(lam_params, sub3, qkvg, qkvg, qkvg, cache_k, cache_v)
    return o_ctx, o_lat


def _outproj_kernel(*refs, ffn):
    if ffn == "routed":
        (oc_ref, ol_ref, cv_ref, xa_ref, xb_ref, mod_ref, g_ref, w_ref, rw_ref,
         xo_ref, h2_ref, info_ref, info_t_ref, cb_ref, tot_ref, wbf_ref, carry_ref, tri_ref) = refs
    else:
        (oc_ref, ol_ref, cv_ref, xa_ref, xb_ref, mod_ref, g_ref, w_ref, w13_ref, w2_ref,
         xo_ref, wbf_ref, h2_ref) = refs
    i = pl.program_id(0)

    @pl.when(i == 0)
    def _():
        wbf_ref[...] = w_ref[...].astype(BF16)
        if ffn == "routed":
            carry_ref[...] = jnp.zeros_like(carry_ref)
            tri_ref[...] = _lower_triangle()

    half = TOK_TILE // 2
    for rows in (slice(0, half), slice(half, TOK_TILE)):
        o = jnp.where(i < P_TILES, oc_ref[rows, :], ol_ref[rows, :])
        m = (jnp.dot(o, wbf_ref[0:ATT_W, :], preferred_element_type=F32)
             + jnp.dot(cv_ref[rows, :], wbf_ref[ATT_W:, :], preferred_element_type=F32))
        x = jnp.where(i < P_TILES, xa_ref[rows, :], xb_ref[rows, :])
        xn = x + mod_ref[2:3, :] * _rms(m, g_ref[1:2, :])
        xo_ref[rows, :] = xn
        h2_ref[rows, :] = (_rms(xn, g_ref[2:3, :]) * (1.0 + mod_ref[4:5, :]) + mod_ref[3:4, :]).astype(BF16)

    if ffn == "routed":
        _route_tile(h2_ref[...], rw_ref, info_ref, info_t_ref, cb_ref, tot_ref, carry_ref, tri_ref)
    else:
        h = h2_ref[...]
        gt = jnp.dot(h, w13_ref[:, :D_FF], preferred_element_type=F32)
        up = jnp.dot(h, w13_ref[:, D_FF:], preferred_element_type=F32)
        a = (gt * _sigmoid(gt) * up).astype(BF16)
        y = jnp.dot(a, w2_ref[...], preferred_element_type=F32)
        xo_ref[...] = xo_ref[...] + mod_ref[5:6, :] * _rms(y, g_ref[3:4, :])


def _outproj(l, o_ctx, o_lat, cv, x, mod, norm_g, w_out, router_w=None, dense_w=None):
    x_specs, x_args = _x_specs(x)
    out_shape = [jax.ShapeDtypeStruct((N_TOK, D_MODEL), F32)]
    out_specs = [pl.BlockSpec((TOK_TILE, D_MODEL), lambda i: (i, 0))]
    scratch = [pltpu.VMEM((D_MODEL, D_MODEL), BF16)]
    if dense_w is not None:
        extra_specs = [pl.BlockSpec((D_MODEL, 2 * D_FF), lambda i: (0, 0), pipeline_mode=pl.Buffered(1)),
                       pl.BlockSpec((D_FF, D_MODEL), lambda i: (0, 0), pipeline_mode=pl.Buffered(1))]
        extra_args = tuple(dense_w)
        scratch += [pltpu.VMEM((TOK_TILE, D_MODEL), BF16)]
    else:
        rw_pad = jnp.zeros((D_MODEL, LANES), BF16).at[:, :N_EXPERTS].set(router_w.astype(BF16))
        extra_specs, extra_args = [pl.BlockSpec((D_MODEL, LANES), lambda i: (0, 0))], (rw_pad,)
        out_shape += [jax.ShapeDtypeStruct((N_TOK, D_MODEL), BF16)]
        out_specs += [pl.BlockSpec((TOK_TILE, D_MODEL), lambda i: (i, 0))]
        out_shape += [jax.ShapeDtypeStruct((N_TOK, LANES), F32),
                      jax.ShapeDtypeStruct((N_TILES, 8, TOK_TILE), F32),
                      jax.ShapeDtypeStruct((N_TILES, 8, LANES), F32),
                      jax.ShapeDtypeStruct((8, LANES), F32)]
        out_specs += [pl.BlockSpec((TOK_TILE, LANES), lambda i: (i, 0)),
                      pl.BlockSpec((None, 8, TOK_TILE), lambda i: (i, 0, 0)),
                      pl.BlockSpec((None, 8, LANES), lambda i: (i, 0, 0)),
                      pl.BlockSpec((8, LANES), lambda i: (0, 0))]
        scratch += [pltpu.VMEM((8, LANES), F32), pltpu.VMEM((TOK_TILE, TOK_TILE), BF16)]
    return pl.pallas_call(
        functools.partial(_outproj_kernel, ffn="dense" if dense_w is not None else "routed"),
        out_shape=tuple(out_shape),
        grid=(N_TILES,),
        in_specs=[pl.BlockSpec((TOK_TILE, ATT_W), lambda i: (jnp.minimum(i, P_TILES - 1), 0)),
                  pl.BlockSpec((TOK_TILE, ATT_W), lambda i: (jnp.maximum(i - P_TILES, 0), 0)),
                  pl.BlockSpec((TOK_TILE, CONV_CH), lambda i: (i, 0))] + x_specs + [
                  pl.BlockSpec((None, None, 6, D_MODEL), lambda i: (l, _cond_id(i), 0, 0)),
                  pl.BlockSpec((None, 4, D_MODEL), lambda i: (l, 0, 0)),
                  pl.BlockSpec((None, D_MODEL, D_MODEL), lambda i: (l, 0, 0),
                               pipeline_mode=pl.Buffered(1))] + extra_specs,
        out_specs=tuple(out_specs),
        scratch_shapes=scratch,
        compiler_params=_cparams(1),
        name=f"outproj_{l}",
    )(o_ctx, o_lat, cv, *x_args, mod, norm_g, w_out, *extra_args)


def _lower_triangle():
    r = lax.broadcasted_iota(jnp.int32, (TOK_TILE, TOK_TILE), 0)
    c = lax.broadcasted_iota(jnp.int32, (TOK_TILE, TOK_TILE), 1)
    return jnp.where(c <= r, 1.0, 0.0).astype(BF16)


def _route_tile(h, rw_ref, info_ref, info_t_ref, cb_ref, tot_ref, carry_ref, tri_ref):
    lane = lax.broadcasted_iota(jnp.int32, (TOK_TILE, LANES), 1)
    lanef = lane.astype(F32)
    logits = jnp.dot(h, rw_ref[...], preferred_element_type=F32)
    logits = jnp.where(lane < N_EXPERTS, logits, -jnp.inf)
    big = jnp.asarray(LANES, F32)
    m1 = jnp.max(logits, axis=-1, keepdims=True)
    e1 = jnp.min(jnp.where(logits == m1, lanef, big), axis=-1, keepdims=True)
    oh1 = lanef == e1
    rest = jnp.where(oh1, -jnp.inf, logits)
    m2 = jnp.max(rest, axis=-1, keepdims=True)
    e2 = jnp.min(jnp.where(rest == m2, lanef, big), axis=-1, keepdims=True)
    oh2 = lanef == e2
    ex = jnp.exp(m2 - m1)
    g1 = 1.0 / (1.0 + ex)
    g2 = ex / (1.0 + ex)

    oh = jnp.where(jnp.logical_or(oh1, oh2), 1.0, 0.0)
    incl = jnp.dot(tri_ref[...], oh.astype(BF16), preferred_element_type=F32)
    carry = carry_ref[0:1, :]
    excl = incl - oh + carry
    rank1 = jnp.sum(jnp.where(oh1, excl, 0.0), axis=-1, keepdims=True)
    rank2 = jnp.sum(jnp.where(oh2, excl, 0.0), axis=-1, keepdims=True)

    info = jnp.where(lane == 0, e1, 0.0)
    for k, col in enumerate((e2, rank1, rank2, g1, g2), start=1):
        info = jnp.where(lane == k, col, info)
    info_ref[...] = info
    info_t_ref[...] = info.T[0:8, :]

    cb_ref[...] = carry_ref[...]
    new_carry = carry + incl[TOK_TILE - 1:TOK_TILE, :]
    carry_ref[...] = jnp.broadcast_to(new_carry, carry_ref.shape)
    tot_ref[...] = jnp.broadcast_to(new_carry, tot_ref.shape)


def _sorted_pos(expert, rank, base_ref):
    start = jnp.zeros_like(rank)
    for e in range(N_EXPERTS):
        start = jnp.where(expert == float(e), base_ref[e].astype(F32), start)
    return start + rank


def _dispatch_kernel(clo_ref, cn_ref, base_ref, info_t_ref, x_ref, o_ref):
    r = pl.program_id(0)
    o_ref[...] = jnp.zeros_like(o_ref)
    rows = (r * DISPATCH_TILE + lax.broadcasted_iota(jnp.int32, (DISPATCH_TILE, TOK_TILE), 0)).astype(F32)

    def body(k, carry):
        c = clo_ref[r] + k
        it = info_t_ref[c]
        pos1 = _sorted_pos(it[0:1, :], it[2:3, :], base_ref)
        pos2 = _sorted_pos(it[1:2, :], it[3:4, :], base_ref)
        hit = jnp.logical_or(rows == pos1, rows == pos2)
        sel = jnp.where(hit, 1.0, 0.0).astype(BF16)
        off = pl.multiple_of(c * TOK_TILE, TOK_TILE)
        o_ref[...] += jnp.dot(sel, x_ref[pl.ds(off, TOK_TILE), :], preferred_element_type=F32).astype(BF16)
        return carry

    lax.fori_loop(0, cn_ref[r], body, 0)


def _dispatch(c_lo, c_n, base, info_t, h2):
    return pl.pallas_call(
        _dispatch_kernel,
        out_shape=jax.ShapeDtypeStruct((SORT_ROWS, D_MODEL), BF16),
        grid_spec=pltpu.PrefetchScalarGridSpec(
            num_scalar_prefetch=3,
            grid=(SORT_ROWS // DISPATCH_TILE,),
            in_specs=[pl.BlockSpec((N_TILES, 8, TOK_TILE), lambda r, *_: (0, 0, 0)),
                      pl.BlockSpec((N_TOK, D_MODEL), lambda r, *_: (0, 0),
                                   pipeline_mode=pl.Buffered(1))],
            out_specs=pl.BlockSpec((DISPATCH_TILE, D_MODEL), lambda r, *_: (r, 0))),
        compiler_params=_cparams(1),
        name="moe_dispatch",
    )(c_lo, c_n, base, info_t, h2)


TILE_UNUSED, TILE_HALF, TILE_FULL = 0, 1, 2
CAST_ROWS = 64


def _expert_weight_ring(c, r, n_c, te_ref, tf_ref, nx_ref, copies, cast):
    @pl.when(jnp.logical_and(c == 0, r == 0))
    def _():
        for cp in copies(te_ref[0], 0):
            cp.start()

    @pl.when(tf_ref[r] == 1)
    def _():
        for cp in copies(te_ref[r], c):
            cp.wait()
        cast()
        nr = nx_ref[r]
        nc = c + (nr <= r).astype(jnp.int32)

        @pl.when(nc < n_c)
        def _():
            for cp in copies(te_ref[nr], nc):
                cp.start()


def _moe_up_kernel(te_ref, tv_ref, tf_ref, nx_ref, x_ref, w_hbm, h_ref, wst_ref, wbf_ref, sem_ref, *, i_moe):
    f = pl.program_id(0)
    r = pl.program_id(1)

    def copies(e, ft):
        out = []
        for part in range(2):
            col = pl.multiple_of((part * UP_TILES + ft) * UP_TILE, LANES)
            out.append(pltpu.make_async_copy(w_hbm.at[i_moe, e, :, pl.ds(col, UP_TILE)],
                                             wst_ref.at[part], sem_ref.at[part]))
        return out

    def cast():
        def rows_block(b, carry):
            rows = pl.ds(pl.multiple_of(b * CAST_ROWS, CAST_ROWS), CAST_ROWS)
            wbf_ref[:, rows, :] = wst_ref[:, rows, :].astype(BF16)
            return carry

        lax.fori_loop(0, D_MODEL // CAST_ROWS, rows_block, 0)

    _expert_weight_ring(f, r, UP_TILES, te_ref, tf_ref, nx_ref, copies, cast)

    def hidden(x):
        gt = jnp.dot(x, wbf_ref[0], preferred_element_type=F32)
        up = jnp.dot(x, wbf_ref[1], preferred_element_type=F32)
        return (gt * _sigmoid(gt) * up).astype(BF16)

    half = ROW_TILE // 2

    @pl.when(tv_ref[r] == TILE_FULL)
    def _():
        h_ref[...] = hidden(x_ref[...])

    @pl.when(tv_ref[r] == TILE_HALF)
    def _():
        h_ref[0:half, :] = hidden(x_ref[0:half, :])
        h_ref[half:, :] = jnp.zeros((half, UP_TILE), BF16)

    @pl.when(tv_ref[r] == TILE_UNUSED)
    def _():
        h_ref[...] = jnp.zeros_like(h_ref)


def _moe_up(i_moe, tile_e, tile_valid, tile_first, tile_next, xs, w13):
    return pl.pallas_call(
        functools.partial(_moe_up_kernel, i_moe=i_moe),
        out_shape=jax.ShapeDtypeStruct((SORT_ROWS, D_FF_EXPERT), BF16),
        grid_spec=pltpu.PrefetchScalarGridSpec(
            num_scalar_prefetch=4,
            grid=(UP_TILES, SORT_TILES),
            in_specs=[pl.BlockSpec((ROW_TILE, D_MODEL), lambda f, r, *_: (r, 0)),
                      pl.BlockSpec(memory_space=pl.ANY)],
            out_specs=pl.BlockSpec((ROW_TILE, UP_TILE), lambda f, r, *_: (r, f)),
            scratch_shapes=[pltpu.VMEM((2, D_MODEL, UP_TILE), F32),
                            pltpu.VMEM((2, D_MODEL, UP_TILE), BF16),
                            pltpu.SemaphoreType.DMA((2,))]),
        compiler_params=_cparams(2),
        name="moe_up",
    )(tile_e, tile_valid, tile_first, tile_next, xs, w13)


def _moe_down_kernel(te_ref, tv_ref, tf_ref, nx_ref, h_ref, w_hbm, y_ref, wst_ref, wbf_ref, sem_ref, *, i_moe):
    n = pl.program_id(0)
    r = pl.program_id(1)

    def copies(e, nt):
        col = pl.multiple_of(nt * DOWN_TILE, LANES)
        return [pltpu.make_async_copy(w_hbm.at[i_moe, e, :, pl.ds(col, DOWN_TILE)], wst_ref, sem_ref.at[0])]

    def cast():
        wbf_ref[...] = wst_ref[...].astype(BF16)

    _expert_weight_ring(n, r, D_MODEL // DOWN_TILE, te_ref, tf_ref, nx_ref, copies, cast)

    half = ROW_TILE // 2

    @pl.when(tv_ref[r] == TILE_FULL)
    def _():
        y_ref[...] = jnp.dot(h_ref[...], wbf_ref[...], preferred_element_type=F32).astype(BF16)

    @pl.when(tv_ref[r] == TILE_HALF)
    def _():
        y_ref[0:half, :] = jnp.dot(h_ref[0:half, :], wbf_ref[...], preferred_element_type=F32).astype(BF16)
        y_ref[half:, :] = jnp.zeros((half, DOWN_TILE), BF16)

    @pl.when(tv_ref[r] == TILE_UNUSED)
    def _():
        y_ref[...] = jnp.zeros_like(y_ref)


def _moe_down(i_moe, tile_e, tile_valid, tile_first, tile_next, hs, w2):
    return pl.pallas_call(
        functools.partial(_moe_down_kernel, i_moe=i_moe),
        out_shape=jax.ShapeDtypeStruct((SORT_ROWS, D_MODEL), BF16),
        grid_spec=pltpu.PrefetchScalarGridSpec(
            num_scalar_prefetch=4,
            grid=(D_MODEL // DOWN_TILE, SORT_TILES),
            in_specs=[pl.BlockSpec((ROW_TILE, D_FF_EXPERT), lambda n, r, *_: (r, 0)),
                      pl.BlockSpec(memory_space=pl.ANY)],
            out_specs=pl.BlockSpec((ROW_TILE, DOWN_TILE), lambda n, r, *_: (r, n)),
            scratch_shapes=[pltpu.VMEM((D_FF_EXPERT, DOWN_TILE), F32),
                            pltpu.VMEM((D_FF_EXPERT, DOWN_TILE), BF16),
                            pltpu.SemaphoreType.DMA((1,))]),
        compiler_params=_cparams(2),
        name="moe_down",
    )(tile_e, tile_valid, tile_first, tile_next, hs, w2)


def _window_copy(y_hbm, ybuf_ref, sem_ref, src, buf, slot):
    return pltpu.make_async_copy(y_hbm.at[pl.ds(src, WIN), :],
                                 ybuf_ref.at[buf, pl.ds(slot * WIN, WIN), :], sem_ref.at[buf, slot])


def _start_windows(src_ref, y_hbm, ybuf_ref, sem_ref, tile, buf):
    for s in range(N_WIN):
        src = pl.multiple_of(src_ref[tile * N_WIN + s], BF16_SUBLANES)
        _window_copy(y_hbm, ybuf_ref, sem_ref, src, buf, s).start()


def _combine_kernel(src_ref, lo_ref, hi_ref, base_ref, info_ref, x_ref, mod_ref, g_ref, y_hbm,
                    yp_ref, ys_ref, ybuf_ref, sem_ref):
    j = pl.program_id(0)
    buf = j % 2

    @pl.when(j == 0)
    def _():
        _start_windows(src_ref, y_hbm, ybuf_ref, sem_ref, 0, 0)

    @pl.when(j + 1 < N_TILES)
    def _():
        _start_windows(src_ref, y_hbm, ybuf_ref, sem_ref, j + 1, 1 - buf)

    info = info_ref[...]
    pos1 = _sorted_pos(info[:, 0:1], info[:, 2:3], base_ref)
    pos2 = _sorted_pos(info[:, 1:2], info[:, 3:4], base_ref)
    g1, g2 = info[:, 4:5], info[:, 5:6]
    iota = lax.broadcasted_iota(jnp.int32, (1, WIN), 1)
    ids = []
    for s in range(N_WIN):
        row = src_ref[j * N_WIN + s] + iota
        ok = jnp.logical_and(row >= lo_ref[j * N_WIN + s], row < hi_ref[j * N_WIN + s])
        ids.append(jnp.where(ok, row, -1))
    row_id = jnp.concatenate(ids, axis=1).astype(F32)

    for s in range(N_WIN):
        _window_copy(y_hbm, ybuf_ref, sem_ref, 0, buf, s).wait()
    halves = []
    for rows in (slice(0, TOK_TILE // 2), slice(TOK_TILE // 2, TOK_TILE)):
        sel = (jnp.where(pos1[rows] == row_id, g1[rows], 0.0)
               + jnp.where(pos2[rows] == row_id, g2[rows], 0.0)).astype(BF16)
        ffn = jnp.dot(sel, ybuf_ref[buf], preferred_element_type=F32)
        halves.append(x_ref[rows, :] + mod_ref[5:6, :] * _rms(ffn, g_ref[3:4, :]))
    out = jnp.concatenate(halves, axis=0)

    @pl.when(j < P_TILES)
    def _():
        yp_ref[...] = out

    @pl.when(j >= P_TILES)
    def _():
        ys_ref[...] = out


def _combine(l, win_src, win_lo, win_hi, base, info, x, mod, norm_g, ys):
    return pl.pallas_call(
        _combine_kernel,
        out_shape=(jax.ShapeDtypeStruct((N_PROMPT, D_MODEL), F32),
                   jax.ShapeDtypeStruct((N_SAMPLE, D_MODEL), F32)),
        grid_spec=pltpu.PrefetchScalarGridSpec(
            num_scalar_prefetch=4,
            grid=(N_TILES,),
            in_specs=[pl.BlockSpec((TOK_TILE, LANES), lambda i, *_: (i, 0)),
                      pl.BlockSpec((TOK_TILE, D_MODEL), lambda i, *_: (i, 0)),
                      pl.BlockSpec((None, None, 6, D_MODEL), lambda i, *_: (l, _cond_id(i), 0, 0)),
                      pl.BlockSpec((None, 4, D_MODEL), lambda i, *_: (l, 0, 0)),
                      pl.BlockSpec(memory_space=pl.ANY)],
            out_specs=(pl.BlockSpec((TOK_TILE, D_MODEL), lambda i, *_: (jnp.minimum(i, P_TILES - 1), 0)),
                       pl.BlockSpec((TOK_TILE, D_MODEL), lambda i, *_: (jnp.maximum(i - P_TILES, 0), 0))),
            scratch_shapes=[pltpu.VMEM((2, N_WIN * WIN, D_MODEL), BF16),
                            pltpu.SemaphoreType.DMA((2, N_WIN))]),
        compiler_params=_cparams(1),
        name="moe_combine",
    )(win_src, win_lo, win_hi, base, info, x, mod, norm_g, ys)


def _moe_ffn(l, i_moe, h2, x, mod, norm_g, routing, moe_w13, moe_w2):
    info, info_t, cb, tot = routing

    ar8 = jnp.arange(N_EXPERTS, dtype=jnp.int32)

    def at8(vec, idx):
        return jnp.sum(jnp.where(idx[..., None] == ar8, vec, 0), axis=-1)

    def cumsum8(a):
        return jnp.sum(jnp.where(ar8[None, :] <= ar8[:, None], a[..., None, :], 0), axis=-1)

    counts = tot[0, :N_EXPERTS].astype(jnp.int32)
    padded = (counts + ROW_TILE - 1) // ROW_TILE * ROW_TILE
    seg_end = cumsum8(padded)
    base = (seg_end - padded).astype(jnp.int32)
    cbx = cb[:, 0, :N_EXPERTS].astype(jnp.int32)
    cb_end = jnp.concatenate([cbx[1:], counts[None, :]], axis=0)

    def tile_segments(n_tiles, rows):
        row0 = jnp.arange(n_tiles, dtype=jnp.int32) * rows
        e = jnp.minimum(jnp.sum(seg_end[None, :] <= row0[:, None], axis=1), N_EXPERTS - 1).astype(jnp.int32)
        k0 = row0 - at8(base, e)
        return e, k0, jnp.logical_and(k0 >= 0, k0 < at8(counts, e))

    tile_e, k0, tile_valid = tile_segments(SORT_TILES, ROW_TILE)
    tile_first = jnp.logical_and(tile_valid, k0 == 0)
    d_e, d_k0, d_valid = tile_segments(SORT_ROWS // DISPATCH_TILE, DISPATCH_TILE)
    d_kend = jnp.minimum(d_k0 + DISPATCH_TILE, at8(counts, d_e))
    cb_t = jnp.sum(jnp.where(d_e[:, None, None] == ar8[None, :, None], cbx.T[None, :, :], 0), axis=1)
    c_lo = jnp.sum(cb_t <= d_k0[:, None], axis=1) - 1
    c_hi = jnp.sum(cb_t < d_kend[:, None], axis=1) - 1
    c_n = jnp.where(d_valid, c_hi - c_lo + 1, 0).astype(jnp.int32)
    c_lo = jnp.where(d_valid, c_lo, 0).astype(jnp.int32)
    tile_idx = jnp.arange(SORT_TILES, dtype=jnp.int32)
    first_idx = jnp.where(tile_first, tile_idx, SORT_TILES)
    later = jnp.where(tile_idx[None, :] > tile_idx[:, None], first_idx[None, :], SORT_TILES)
    tile_next = jnp.min(later, axis=1)
    tile_next = jnp.where(tile_next >= SORT_TILES, 0, tile_next).astype(jnp.int32)
    tile_rows = at8(counts, tile_e) - k0
    tile_valid = jnp.where(tile_valid, jnp.where(tile_rows <= ROW_TILE // 2, TILE_HALF, TILE_FULL),
                           TILE_UNUSED).astype(jnp.int32)
    tile_first = tile_first.astype(jnp.int32)

    seg_lo = base[None, :] + cbx
    seg_hi = base[None, :] + cb_end
    seg_n = seg_hi - seg_lo
    w0 = seg_lo // BF16_SUBLANES * BF16_SUBLANES
    nw = jnp.where(seg_n > 0, (seg_lo - w0 + seg_n + WIN - 1) // WIN, 0)
    nw_end = cumsum8(nw)
    slot = jnp.arange(N_WIN, dtype=jnp.int32)
    slot_e = jnp.minimum(jnp.sum(nw_end[:, None, :] <= slot[None, :, None], axis=2), N_EXPERTS - 1)
    pick = slot_e[:, :, None] == jnp.arange(N_EXPERTS, dtype=jnp.int32)[None, None, :]
    take = lambda a: jnp.sum(jnp.where(pick, a[:, None, :], 0), axis=2)
    slot_k = slot[None, :] - (take(nw_end) - take(nw))
    slot_ok = slot[None, :] < nw_end[:, -1:]
    slot_src = take(w0) + WIN * slot_k
    win_src = jnp.where(slot_ok, slot_src, 0).astype(jnp.int32).reshape(-1)
    win_lo = jnp.where(slot_ok, take(seg_lo), 0).astype(jnp.int32).reshape(-1)
    win_hi = jnp.where(slot_ok, take(seg_hi), 0).astype(jnp.int32).reshape(-1)

    xs = _dispatch(c_lo, c_n, base, info_t, h2)
    hs = _moe_up(i_moe, tile_e, tile_valid, tile_first, tile_next, xs, moe_w13)
    ys = _moe_down(i_moe, tile_e, tile_valid, tile_first, tile_next, hs, moe_w2)
    return _combine(l, win_src, win_lo, win_hi, base, info, x, mod, norm_g, ys)


def kernel(x_prompt, x_sample, cache_k, cache_v, c, c_ctx, w_ada, b_ada, norm_g, w_in, w_out,
           lam_params, subln_g, dw_weight, dw_bias, conv_ln_g, conv_ln_b, dense_w13, dense_w2,
           router_w, moe_w13, moe_w2):
    x = (x_prompt.reshape(N_PROMPT, D_MODEL), x_sample.reshape(N_SAMPLE, D_MODEL))
    cond8 = jnp.zeros((8, D_MODEL), F32).at[0].set(c_ctx).at[1:1 + DEC_BATCH].set(c)
    mod = _ada_modulation(cond8, w_ada, b_ada).reshape(DEPTH, 8, 6, D_MODEL)
    rope = _rope_tables()
    ck = cache_k.reshape(DEC_BATCH, DEPTH, PAST_LEN * ATT_HEADS, V_DIM)
    cv = cache_v.reshape(DEC_BATCH, DEPTH, PAST_LEN * ATT_HEADS, V_DIM)

    caches = None
    for l in range(DEPTH):
        lam_init = 0.8 - 0.6 * math.exp(-0.3 * l)
        qkv, cvo, *caches = _mixer_in(l, x, mod, norm_g, w_in, rope,
                                      (dw_weight, dw_bias, conv_ln_g, conv_ln_b), caches)
        o_ctx, o_lat = _attention(l, lam_init, lam_params, subln_g, qkv, ck, cv)
        i = l // 2
        if l % 2 == 0:
            (x,) = _outproj(l, o_ctx, o_lat, cvo, x, mod, norm_g, w_out,
                            dense_w=(dense_w13[i].astype(BF16), dense_w2[i].astype(BF16)))
        else:
            x, h2, *routing = _outproj(l, o_ctx, o_lat, cvo, x, mod, norm_g, w_out, router_w=router_w[i])
            x = _moe_ffn(l, i, h2, x, mod, norm_g, routing, moe_w13, moe_w2)

    xp, xs = x if isinstance(x, tuple) else (x[:N_PROMPT], x[N_PROMPT:])
    new_k, new_v = (a.reshape(BATCH, DEPTH, SEQ, ATT_HEADS, V_DIM) for a in caches)
    return (xp.reshape(BATCH, SEQ, D_MODEL), xs.reshape(DEC_BATCH, DEC_SEQ, D_MODEL), new_k, new_v)
```

```python
import functools
import math

import jax
import jax.numpy as jnp
import numpy as np
from jax import lax
from jax.experimental import pallas as pl
from jax.experimental.pallas import tpu as pltpu

F32 = jnp.float32
BF16 = jnp.bfloat16

D_MODEL = 1024
BATCH = 32
SEQ = 256
DEPTH = 2
DEC_BATCH = 2
DEC_SEQ = 2048
PAST_LEN = 512
GRID_W = 64
ATT_HEADS = 4
QK_DIM = 64
V_DIM = 128
ATT_W = ATT_HEADS * V_DIM
IN_W = 5 * ATT_W
CONV_CH = 512
CONV_WIDTH = 31
D_FF = 2816
N_EXPERTS = 8
D_FF_EXPERT = 3584
ROPE_THETA = 10000.0
NORM_EPS = 1e-6
LN_EPS = 1e-5

N_PROMPT = BATCH * SEQ
N_SAMPLE = DEC_BATCH * DEC_SEQ
N_TOK = N_PROMPT + N_SAMPLE

LANES = 128
BF16_SUBLANES = 16
MXU_TILE = 256
VMEM_BYTES = 64 * 1024 * 1024
VMEM_LIMIT = VMEM_BYTES - 8 * 1024 * 1024

TOK_TILE = 512
N_TILES = N_TOK // TOK_TILE
P_TILES = N_PROMPT // TOK_TILE
S_TILES_PER_BATCH = DEC_SEQ // TOK_TILE

ROW_TILE = 512
SORT_TILES = (2 * N_TOK + N_EXPERTS * ROW_TILE) // ROW_TILE + 1
SORT_ROWS = SORT_TILES * ROW_TILE
DISPATCH_TILE = 256
UP_TILE = 7 * MXU_TILE
UP_TILES = D_FF_EXPERT // UP_TILE
DOWN_TILE = 4 * MXU_TILE
WIN = 128
N_WIN = (2 * TOK_TILE + N_EXPERTS * (BF16_SUBLANES - 1 + WIN - 1)) // WIN

assert D_FF_EXPERT % UP_TILE == 0 and D_MODEL % DOWN_TILE == 0 and D_FF % MXU_TILE == 0
assert ROW_TILE % DISPATCH_TILE == 0 and N_TOK % TOK_TILE == 0 and N_PROMPT % TOK_TILE == 0
assert DEPTH == 2, "layer 0 zero-fills exactly one later cache slab; the last layer must be the routed one"


def _cparams(n_axes):
    return pltpu.CompilerParams(dimension_semantics=("arbitrary",) * n_axes,
                                vmem_limit_bytes=VMEM_LIMIT)


def _cond_id(i):
    return jnp.where(i < P_TILES, 0, 1 + (i - P_TILES) // S_TILES_PER_BATCH)


def _sigmoid(x):
    return 1.0 / (1.0 + jnp.exp(-x))


def _rms(x, g):
    return x * lax.rsqrt(jnp.mean(x * x, axis=-1, keepdims=True) + NORM_EPS) * g


def _ada_kernel(c_ref, w_ref, b_ref, o_ref):
    c = c_ref[...]
    s = (c * _sigmoid(c)).astype(BF16)
    o_ref[...] = jnp.dot(s, w_ref[...].astype(BF16), preferred_element_type=F32) + b_ref[...]


def _ada_modulation(cond8, w_ada, b_ada):
    tn = 1536
    n = 6 * D_MODEL
    return pl.pallas_call(
        _ada_kernel,
        out_shape=jax.ShapeDtypeStruct((DEPTH, 8, n), F32),
        grid=(DEPTH, n // tn),
        in_specs=[pl.BlockSpec((8, D_MODEL), lambda l, j: (0, 0)),
                  pl.BlockSpec((None, D_MODEL, tn), lambda l, j: (l, 0, j)),
                  pl.BlockSpec((None, 1, tn), lambda l, j: (l, 0, j))],
        out_specs=pl.BlockSpec((None, 8, tn), lambda l, j: (l, 0, j)),
        compiler_params=_cparams(2),
        name="ada_modulation",
    )(cond8, w_ada, b_ada.reshape(DEPTH, 1, n))


def _tile_x(xa_ref, xb_ref):
    return jnp.where(pl.program_id(0) < P_TILES, xa_ref[...], xb_ref[...])


def _x_specs(x):
    last = N_TILES - 1
    if isinstance(x, tuple):
        xa, xb = x
        b_idx = lambda i, *_: (jnp.maximum(jnp.minimum(i, last) - P_TILES, 0), 0)
    else:
        xa = xb = x
        b_idx = lambda i, *_: (jnp.maximum(jnp.minimum(i, last), P_TILES), 0)
    a_idx = lambda i, *_: (jnp.minimum(i, P_TILES - 1), 0)
    return [pl.BlockSpec((TOK_TILE, D_MODEL), a_idx), pl.BlockSpec((TOK_TILE, D_MODEL), b_idx)], (xa, xb)


ROT_SPAN = QK_DIM // 4
SEQ_PER_TILE = TOK_TILE // SEQ
CACHE_ROWS = SEQ * ATT_HEADS


CONV_LAG = 2
CONV_RING = 3
CONV_TILE = 256
CONV_HALO = 16
CONV_SUB = 32


def _conv_pass(upad_ref, w_ref, bias_ref, lg_ref, lb_ref, o_ref, row0):
    rows = CONV_TILE + 2 * CONV_HALO - 8
    for s in range(1, 8):
        upad_ref[s, 0:rows, :] = upad_ref[0, s:s + rows, :]
    first_tap = CONV_HALO - CONV_WIDTH // 2
    groups = CONV_SUB // 8
    for t in range(CONV_TILE // CONV_SUB):
        base = t * CONV_SUB
        acc = jnp.zeros((groups, 8, CONV_CH), F32)
        for j in range(CONV_WIDTH):
            lo = base + (first_tap + j) // 8 * 8
            taps = upad_ref[(first_tap + j) % 8, lo:lo + CONV_SUB, :].reshape(groups, 8, CONV_CH)
            acc = acc + taps * w_ref[j][None]
        y = acc.reshape(CONV_SUB, CONV_CH) + bias_ref[...]
        mu = jnp.mean(y, axis=-1, keepdims=True)
        yc = y - mu
        var = jnp.mean(yc * yc, axis=-1, keepdims=True)
        z = yc * lax.rsqrt(var + LN_EPS) * lg_ref[...] + lb_ref[...]
        o_ref[row0 + base:row0 + base + CONV_SUB, :] = (z * _sigmoid(z)).astype(BF16)


def _mixer_in_kernel(*refs, layer):
    n_in = 12 + (2 if layer else 0)
    (xa_ref, xb_ref, mod_ref, g_ref, w_ref, cos_ref, sina_ref, sinb_ref,
     dw_ref, db_ref, lg_ref, lb_ref) = refs[:12]
    qkv_ref, cv_ref, kc_ref, vc_ref, wbf_ref, ring_ref, glu_ref, pj_ref, upad_ref, taps_ref = refs[n_in:]
    i = pl.program_id(0)
    t = jnp.minimum(i, N_TILES - 1)

    @pl.when(i == 0)
    def _():
        wbf_ref[...] = w_ref[...].astype(BF16)
        ring_ref[...] = jnp.zeros_like(ring_ref)
        glu_ref[...] = jnp.zeros_like(glu_ref)
        for j in range(CONV_WIDTH):
            taps_ref[j] = jnp.broadcast_to(dw_ref[j:j + 1, :], (8, CONV_CH))

    ring_ref[(i - 1) % CONV_RING] = glu_ref[...]

    h = _rms(_tile_x(xa_ref, xb_ref), g_ref[0:1, :]) * (1.0 + mod_ref[1:2, :]) + mod_ref[0:1, :]
    hb = h.astype(BF16)

    def proj(part):
        return jnp.dot(hb, wbf_ref[:, part * ATT_W:(part + 1) * ATT_W], preferred_element_type=F32)

    for part in range(3):
        pj_ref[:, part * ATT_W:(part + 1) * ATT_W] = proj(part)
    glu_ref[...] = proj(3) * _sigmoid(proj(4))

    j = i - CONV_LAG
    latent = j >= P_TILES
    q = (j - P_TILES) % S_TILES_PER_BATCH
    has_prev = jnp.logical_and(latent, q != 0)
    has_next = jnp.logical_and(latent, q != S_TILES_PER_BATCH - 1)
    cur = j % CONV_RING
    prev_tail = ring_ref[(j - 1) % CONV_RING, TOK_TILE - CONV_HALO:, :]
    next_head = ring_ref[(j + 1) % CONV_RING, 0:CONV_HALO, :]
    for s in range(TOK_TILE // CONV_TILE):
        lo = s * CONV_TILE
        if s == 0:
            before = jnp.where(has_prev, prev_tail, 0.0)
        else:
            before = jnp.where(latent, ring_ref[cur, lo - CONV_HALO:lo, :], 0.0)
        if s == TOK_TILE // CONV_TILE - 1:
            after = jnp.where(has_next, next_head, 0.0)
        else:
            after = jnp.where(latent, ring_ref[cur, lo + CONV_TILE:lo + CONV_TILE + CONV_HALO, :], 0.0)
        upad_ref[0, 0:CONV_HALO, :] = before
        upad_ref[0, CONV_HALO:CONV_HALO + CONV_TILE, :] = ring_ref[cur, lo:lo + CONV_TILE, :]
        upad_ref[0, CONV_HALO + CONV_TILE:, :] = after
        _conv_pass(upad_ref, taps_ref, db_ref, lg_ref, lb_ref, cv_ref, lo)

    qkv_ref[:, 2 * ATT_W:] = pj_ref[:, 2 * ATT_W:].astype(BF16)

    @pl.when(t < P_TILES)
    def _():
        qkv_ref[:, :2 * ATT_W] = pj_ref[:, :2 * ATT_W].astype(BF16)
        for ref, col0 in ((kc_ref, ATT_W), (vc_ref, 2 * ATT_W)):
            for s in range(SEQ_PER_TILE):
                for hd in range(ATT_HEADS):
                    val = pj_ref[SEQ * s:SEQ * (s + 1), col0 + V_DIM * hd:col0 + V_DIM * (hd + 1)]
                    rows = pl.ds(hd, SEQ, stride=ATT_HEADS)
                    if layer == 0:
                        ref[s, 0, rows, :] = val
                    else:
                        ref[s, rows, :] = val
            if layer == 0:
                ref[:, 1:] = jnp.zeros((SEQ_PER_TILE, DEPTH - 1, CACHE_ROWS, V_DIM), F32)

    @pl.when(t >= P_TILES)
    def _():
        cos = cos_ref[...]
        sina = sina_ref[...]
        sinb = sinb_ref[...]
        for c in range(2 * ATT_W // V_DIM):
            xg = pj_ref[:, V_DIM * c:V_DIM * (c + 1)]
            fwd = pltpu.roll(xg, V_DIM - ROT_SPAN, 1)
            bwd = pltpu.roll(xg, ROT_SPAN, 1)
            qkv_ref[:, V_DIM * c:V_DIM * (c + 1)] = (xg * cos + fwd * sina + bwd * sinb).astype(BF16)


def _mixer_in(l, x, mod, norm_g, w_in, rope, conv_params, caches):
    cos, sina, sinb = rope
    dw_w, dw_b, ln_g, ln_b = conv_params
    x_specs, x_args = _x_specs(x)

    def tile(i):
        return jnp.minimum(i, N_TILES - 1)

    def rope_idx(i):
        return (jnp.maximum(tile(i) - P_TILES, 0) % S_TILES_PER_BATCH, 0)

    def vec(a):
        return a.reshape(DEPTH, 1, CONV_CH)

    cache_shape = jax.ShapeDtypeStruct((BATCH, DEPTH, CACHE_ROWS, V_DIM), F32)
    if l == 0:
        cache_spec = pl.BlockSpec((SEQ_PER_TILE, DEPTH, CACHE_ROWS, V_DIM),
                                  lambda i: (jnp.minimum(i, P_TILES - 1), 0, 0, 0))
        extra_specs, extra_args, aliases = [], (), {}
    else:
        cache_spec = pl.BlockSpec((SEQ_PER_TILE, None, CACHE_ROWS, V_DIM),
                                  lambda i: (jnp.minimum(i, P_TILES - 1), l, 0, 0))
        extra_specs = [pl.BlockSpec(memory_space=pl.ANY)] * 2
        extra_args = tuple(caches)
        aliases = {12: 2, 13: 3}

    return pl.pallas_call(
        functools.partial(_mixer_in_kernel, layer=l),
        out_shape=(jax.ShapeDtypeStruct((N_TOK, 3 * ATT_W), BF16),
                   jax.ShapeDtypeStruct((N_TOK, CONV_CH), BF16), cache_shape, cache_shape),
        grid=(N_TILES + CONV_LAG,),
        in_specs=x_specs + [
            pl.BlockSpec((None, None, 6, D_MODEL), lambda i: (l, _cond_id(tile(i)), 0, 0)),
            pl.BlockSpec((None, 4, D_MODEL), lambda i: (l, 0, 0)),
            pl.BlockSpec((None, D_MODEL, IN_W), lambda i: (l, 0, 0), pipeline_mode=pl.Buffered(1)),
            pl.BlockSpec((TOK_TILE, V_DIM), rope_idx),
            pl.BlockSpec((TOK_TILE, V_DIM), rope_idx),
            pl.BlockSpec((TOK_TILE, V_DIM), rope_idx),
            pl.BlockSpec((None, CONV_WIDTH, CONV_CH), lambda i: (l, 0, 0)),
            pl.BlockSpec((None, 1, CONV_CH), lambda i: (l, 0, 0)),
            pl.BlockSpec((None, 1, CONV_CH), lambda i: (l, 0, 0)),
            pl.BlockSpec((None, 1, CONV_CH), lambda i: (l, 0, 0))] + extra_specs,
        out_specs=(pl.BlockSpec((TOK_TILE, 3 * ATT_W), lambda i: (tile(i), 0)),
                   pl.BlockSpec((TOK_TILE, CONV_CH), lambda i: (jnp.maximum(i - CONV_LAG, 0), 0)),
                   cache_spec, cache_spec),
        scratch_shapes=[pltpu.VMEM((D_MODEL, IN_W), BF16),
                        pltpu.VMEM((CONV_RING, TOK_TILE, CONV_CH), F32),
                        pltpu.VMEM((TOK_TILE, CONV_CH), F32),
                        pltpu.VMEM((TOK_TILE, 3 * ATT_W), F32),
                        pltpu.VMEM((8, CONV_TILE + 2 * CONV_HALO, CONV_CH), F32),
                        pltpu.VMEM((CONV_WIDTH, 8, CONV_CH), F32)],
        input_output_aliases=aliases,
        compiler_params=_cparams(1),
        name=f"mixer_in_{l}",
    )(*x_args, mod, norm_g, w_in, cos, sina, sinb, dw_w, vec(dw_b), vec(ln_g), vec(ln_b), *extra_args)


def _rope_tables():
    rows = DEC_SEQ // GRID_W
    row_pos = np.repeat(np.arange(rows, dtype=np.float64), GRID_W)
    col_pos = np.tile(np.arange(GRID_W, dtype=np.float64), rows)
    half = QK_DIM // 2
    inv_freq = 1.0 / (ROPE_THETA ** (np.arange(0, half, 2, dtype=np.float64) / half))
    ang_r = row_pos[:, None] * inv_freq
    ang_c = col_pos[:, None] * inv_freq
    ang = np.concatenate([ang_r, ang_r, ang_c, ang_c], axis=-1)
    cos = np.tile(np.cos(ang), (1, 2)).astype(np.float32)
    sin = np.tile(np.sin(ang), (1, 2)).astype(np.float32)
    first = (np.arange(V_DIM) % (2 * ROT_SPAN)) < ROT_SPAN
    sina = np.where(first[None, :], -sin, 0.0).astype(np.float32)
    sinb = np.where(first[None, :], 0.0, sin).astype(np.float32)
    return jnp.asarray(cos), jnp.asarray(sina), jnp.asarray(sinb)


def _attn_kernel(*refs, lam_init, has_ext):
    if has_ext:
        lamp_ref, sub_ref, q_ref, k_ref, v_ref, ke_ref, ve_ref, o_ref = refs
    else:
        lamp_ref, sub_ref, q_ref, k_ref, v_ref, o_ref = refs
    lp = lamp_ref[...]
    lam = (jnp.exp(jnp.sum(lp[0:1] * lp[1:2], axis=-1, keepdims=True))
           - jnp.exp(jnp.sum(lp[2:3] * lp[3:4], axis=-1, keepdims=True)) + lam_init)
    lane = lax.broadcasted_iota(jnp.int32, (1, V_DIM), 1)
    nt = (((1,), (1,)), ((), ()))
    scale = QK_DIM ** -0.5
    map_scale = [jnp.where(lane < QK_DIM, scale, 0.0).astype(BF16),
                 jnp.where(lane < QK_DIM, 0.0, scale).astype(BF16)]

    tq = q_ref.shape[0]
    head_cols = [slice(V_DIM * hd, V_DIM * (hd + 1)) for hd in range(ATT_HEADS)]

    def q_map(hd, m):
        return q_ref[:, head_cols[hd]] * map_scale[m]

    def with_ones(v):
        return jnp.concatenate([v, jnp.ones_like(v)], axis=1)

    outs = []
    if has_ext:
        for hd in range(ATT_HEADS):
            kh = k_ref[:, head_cols[hd]]
            v_aug = with_ones(v_ref[:, head_cols[hd]])
            head_rows = pl.ds(hd, PAST_LEN, stride=ATT_HEADS)
            keh = ke_ref[head_rows, :].astype(BF16)
            ve_aug = with_ones(ve_ref[head_rows, :].astype(BF16))
            ratio = []
            for m in range(2):
                qm = q_map(hd, m)
                s = lax.dot_general(qm, kh, nt, preferred_element_type=F32)
                se = lax.dot_general(qm, keh, nt, preferred_element_type=F32)
                mx = jnp.maximum(jnp.max(s, axis=-1, keepdims=True), jnp.max(se, axis=-1, keepdims=True))
                pv = (jnp.dot(jnp.exp((s - mx).astype(BF16)), v_aug, preferred_element_type=F32)
                      + jnp.dot(jnp.exp((se - mx).astype(BF16)), ve_aug, preferred_element_type=F32))
                ratio.append(pv[:, :V_DIM] / pv[:, V_DIM:])
            outs.append(ratio[0] - lam * ratio[1])
    else:
        seqs = [slice(SEQ * sq, SEQ * (sq + 1)) for sq in range(tq // SEQ)]
        s = jnp.concatenate(
            [lax.dot_general(q_ref[rows, head_cols[hd]] * map_scale[m], k_ref[rows, head_cols[hd]], nt,
                             preferred_element_type=F32)
             for rows in seqs for hd in range(ATT_HEADS) for m in range(2)], axis=0)
        p = jnp.exp((s - jnp.max(s, axis=-1, keepdims=True)).astype(BF16))
        for hd in range(ATT_HEADS):
            per_seq = []
            for sq, rows in enumerate(seqs):
                first = (sq * ATT_HEADS + hd) * 2 * SEQ
                pv = jnp.dot(p[first:first + 2 * SEQ], with_ones(v_ref[rows, head_cols[hd]]),
                             preferred_element_type=F32)
                ratio = pv[:, :V_DIM] / pv[:, V_DIM:]
                per_seq.append(ratio[:SEQ] - lam * ratio[SEQ:])
            outs.append(jnp.concatenate(per_seq, axis=0))

    heads = [(_rms(o, sub_ref[...]) * (1.0 - lam_init)).astype(BF16) for o in outs]
    o_ref[...] = jnp.concatenate(heads, axis=1)


CTX_SEQS = 4


def _attention(l, lam_init, lam_params, subln_g, qkvg, cache_k, cache_v):
    small = [pl.BlockSpec((None, 4, QK_DIM), lambda *_: (l, 0, 0)),
             pl.BlockSpec((None, 1, V_DIM), lambda *_: (l, 0, 0))]
    sub3 = subln_g.reshape(DEPTH, 1, V_DIM)

    o_ctx = pl.pallas_call(
        functools.partial(_attn_kernel, lam_init=lam_init, has_ext=False),
        out_shape=jax.ShapeDtypeStruct((N_PROMPT, ATT_W), BF16),
        grid=(BATCH // CTX_SEQS,),
        in_specs=small + [pl.BlockSpec((CTX_SEQS * SEQ, ATT_W), lambda b: (b, 0)),
                          pl.BlockSpec((CTX_SEQS * SEQ, ATT_W), lambda b: (b, 1)),
                          pl.BlockSpec((CTX_SEQS * SEQ, ATT_W), lambda b: (b, 2))],
        out_specs=pl.BlockSpec((CTX_SEQS * SEQ, ATT_W), lambda b: (b, 0)),
        compiler_params=_cparams(1),
        name=f"attn_ctx_{l}",
    )(lam_params, sub3, qkvg, qkvg, qkvg)

    tq = 512
    q_tiles = DEC_SEQ // tq
    q_base = N_PROMPT // tq
    kv_base = N_PROMPT // DEC_SEQ

    def q_idx(b, i):
        return (q_base + b * q_tiles + i, 0)

    o_lat = pl.pallas_call(
        functools.partial(_attn_kernel, lam_init=lam_init, has_ext=True),
        out_shape=jax.ShapeDtypeStruct((N_SAMPLE, ATT_W), BF16),
        grid=(DEC_BATCH, q_tiles),
        in_specs=small + [pl.BlockSpec((tq, ATT_W), q_idx),
                          pl.BlockSpec((DEC_SEQ, ATT_W), lambda b, i: (kv_base + b, 1)),
                          pl.BlockSpec((DEC_SEQ, ATT_W), lambda b, i: (kv_base + b, 2)),
                          pl.BlockSpec((None, None, PAST_LEN * ATT_HEADS, V_DIM), lambda b, i: (b, l, 0, 0)),
                          pl.BlockSpec((None, None, PAST_LEN * ATT_HEADS, V_DIM), lambda b, i: (b, l, 0, 0))],
        out_specs=pl.BlockSpec((tq, ATT_W), lambda b, i: (b * q_tiles + i, 0)),
        compiler_params=_cparams(2),
        name=f"attn_lat_{l}",
    )(lam_params, sub3, qkvg, qkvg, qkvg, cache_k, cache_v)
    return o_ctx, o_lat


def _outproj_kernel(*refs, ffn):
    if ffn == "routed":
        (oc_ref, ol_ref, cv_ref, xa_ref, xb_ref, mod_ref, g_ref, w_ref, rw_ref,
         xo_ref, h2_ref, info_ref, info_t_ref, cb_ref, tot_ref, wbf_ref, carry_ref, tri_ref) = refs
    else:
        (oc_ref, ol_ref, cv_ref, xa_ref, xb_ref, mod_ref, g_ref, w_ref, w13_ref, w2_ref,
         xo_ref, wbf_ref, h2_ref) = refs
    i = pl.program_id(0)

    @pl.when(i == 0)
    def _():
        wbf_ref[...] = w_ref[...].astype(BF16)
        if ffn == "routed":
            carry_ref[...] = jnp.zeros_like(carry_ref)
            tri_ref[...] = _lower_triangle()

    half = TOK_TILE // 2
    for rows in (slice(0, half), slice(half, TOK_TILE)):
        o = jnp.where(i < P_TILES, oc_ref[rows, :], ol_ref[rows, :])
        m = (jnp.dot(o, wbf_ref[0:ATT_W, :], preferred_element_type=F32)
             + jnp.dot(cv_ref[rows, :], wbf_ref[ATT_W:, :], preferred_element_type=F32))
        x = jnp.where(i < P_TILES, xa_ref[rows, :], xb_ref[rows, :])
        xn = x + mod_ref[2:3, :] * _rms(m, g_ref[1:2, :])
        xo_ref[rows, :] = xn
        h2_ref[rows, :] = (_rms(xn, g_ref[2:3, :]) * (1.0 + mod_ref[4:5, :]) + mod_ref[3:4, :]).astype(BF16)

    if ffn == "routed":
        _route_tile(h2_ref[...], rw_ref, info_ref, info_t_ref, cb_ref, tot_ref, carry_ref, tri_ref)
    else:
        h = h2_ref[...]
        gt = jnp.dot(h, w13_ref[:, :D_FF], preferred_element_type=F32)
        up = jnp.dot(h, w13_ref[:, D_FF:], preferred_element_type=F32)
        a = (gt * _sigmoid(gt) * up).astype(BF16)
        y = jnp.dot(a, w2_ref[...], preferred_element_type=F32)
        xo_ref[...] = xo_ref[...] + mod_ref[5:6, :] * _rms(y, g_ref[3:4, :])


def _outproj(l, o_ctx, o_lat, cv, x, mod, norm_g, w_out, router_w=None, dense_w=None):
    x_specs, x_args = _x_specs(x)
    out_shape = [jax.ShapeDtypeStruct((N_TOK, D_MODEL), F32)]
    out_specs = [pl.BlockSpec((TOK_TILE, D_MODEL), lambda i: (i, 0))]
    scratch = [pltpu.VMEM((D_MODEL, D_MODEL), BF16)]
    if dense_w is not None:
        extra_specs = [pl.BlockSpec((D_MODEL, 2 * D_FF), lambda i: (0, 0), pipeline_mode=pl.Buffered(1)),
                       pl.BlockSpec((D_FF, D_MODEL), lambda i: (0, 0), pipeline_mode=pl.Buffered(1))]
        extra_args = tuple(dense_w)
        scratch += [pltpu.VMEM((TOK_TILE, D_MODEL), BF16)]
    else:
        rw_pad = jnp.zeros((D_MODEL, LANES), BF16).at[:, :N_EXPERTS].set(router_w.astype(BF16))
        extra_specs, extra_args = [pl.BlockSpec((D_MODEL, LANES), lambda i: (0, 0))], (rw_pad,)
        out_shape += [jax.ShapeDtypeStruct((N_TOK, D_MODEL), BF16)]
        out_specs += [pl.BlockSpec((TOK_TILE, D_MODEL), lambda i: (i, 0))]
        out_shape += [jax.ShapeDtypeStruct((N_TOK, LANES), F32),
                      jax.ShapeDtypeStruct((N_TILES, 8, TOK_TILE), F32),
                      jax.ShapeDtypeStruct((N_TILES, 8, LANES), F32),
                      jax.ShapeDtypeStruct((8, LANES), F32)]
        out_specs += [pl.BlockSpec((TOK_TILE, LANES), lambda i: (i, 0)),
                      pl.BlockSpec((None, 8, TOK_TILE), lambda i: (i, 0, 0)),
                      pl.BlockSpec((None, 8, LANES), lambda i: (i, 0, 0)),
                      pl.BlockSpec((8, LANES), lambda i: (0, 0))]
        scratch += [pltpu.VMEM((8, LANES), F32), pltpu.VMEM((TOK_TILE, TOK_TILE), BF16)]
    return pl.pallas_call(
        functools.partial(_outproj_kernel, ffn="dense" if dense_w is not None else "routed"),
        out_shape=tuple(out_shape),
        grid=(N_TILES,),
        in_specs=[pl.BlockSpec((TOK_TILE, ATT_W), lambda i: (jnp.minimum(i, P_TILES - 1), 0)),
                  pl.BlockSpec((TOK_TILE, ATT_W), lambda i: (jnp.maximum(i - P_TILES, 0), 0)),
                  pl.BlockSpec((TOK_TILE, CONV_CH), lambda i: (i, 0))] + x_specs + [
                  pl.BlockSpec((None, None, 6, D_MODEL), lambda i: (l, _cond_id(i), 0, 0)),
                  pl.BlockSpec((None, 4, D_MODEL), lambda i: (l, 0, 0)),
                  pl.BlockSpec((None, D_MODEL, D_MODEL), lambda i: (l, 0, 0),
                               pipeline_mode=pl.Buffered(1))] + extra_specs,
        out_specs=tuple(out_specs),
        scratch_shapes=scratch,
        compiler_params=_cparams(1),
        name=f"outproj_{l}",
    )(o_ctx, o_lat, cv, *x_args, mod, norm_g, w_out, *extra_args)


def _lower_triangle():
    r = lax.broadcasted_iota(jnp.int32, (TOK_TILE, TOK_TILE), 0)
    c = lax.broadcasted_iota(jnp.int32, (TOK_TILE, TOK_TILE), 1)
    return jnp.where(c <= r, 1.0, 0.0).astype(BF16)


def _route_tile(h, rw_ref, info_ref, info_t_ref, cb_ref, tot_ref, carry_ref, tri_ref):
    lane = lax.broadcasted_iota(jnp.int32, (TOK_TILE, LANES), 1)
    lanef = lane.astype(F32)
    logits = jnp.dot(h, rw_ref[...], preferred_element_type=F32)
    logits = jnp.where(lane < N_EXPERTS, logits, -jnp.inf)
    big = jnp.asarray(LANES, F32)
    m1 = jnp.max(logits, axis=-1, keepdims=True)
    e1 = jnp.min(jnp.where(logits == m1, lanef, big), axis=-1, keepdims=True)
    oh1 = lanef == e1
    rest = jnp.where(oh1, -jnp.inf, logits)
    m2 = jnp.max(rest, axis=-1, keepdims=True)
    e2 = jnp.min(jnp.where(rest == m2, lanef, big), axis=-1, keepdims=True)
    oh2 = lanef == e2
    ex = jnp.exp(m2 - m1)
    g1 = 1.0 / (1.0 + ex)
    g2 = ex / (1.0 + ex)

    oh = jnp.where(jnp.logical_or(oh1, oh2), 1.0, 0.0)
    incl = jnp.dot(tri_ref[...], oh.astype(BF16), preferred_element_type=F32)
    carry = carry_ref[0:1, :]
    excl = incl - oh + carry
    rank1 = jnp.sum(jnp.where(oh1, excl, 0.0), axis=-1, keepdims=True)
    rank2 = jnp.sum(jnp.where(oh2, excl, 0.0), axis=-1, keepdims=True)

    info = jnp.where(lane == 0, e1, 0.0)
    for k, col in enumerate((e2, rank1, rank2, g1, g2), start=1):
        info = jnp.where(lane == k, col, info)
    info_ref[...] = info
    info_t_ref[...] = info.T[0:8, :]

    cb_ref[...] = carry_ref[...]
    new_carry = carry + incl[TOK_TILE - 1:TOK_TILE, :]
    carry_ref[...] = jnp.broadcast_to(new_carry, carry_ref.shape)
    tot_ref[...] = jnp.broadcast_to(new_carry, tot_ref.shape)


def _sorted_pos(expert, rank, base_ref):
    start = jnp.zeros_like(rank)
    for e in range(N_EXPERTS):
        start = jnp.where(expert == float(e), base_ref[e].astype(F32), start)
    return start + rank


def _dispatch_kernel(clo_ref, cn_ref, base_ref, info_t_ref, x_ref, o_ref):
    r = pl.program_id(0)
    o_ref[...] = jnp.zeros_like(o_ref)
    rows = (r * DISPATCH_TILE + lax.broadcasted_iota(jnp.int32, (DISPATCH_TILE, TOK_TILE), 0)).astype(F32)

    def body(k, carry):
        c = clo_ref[r] + k
        it = info_t_ref[c]
        pos1 = _sorted_pos(it[0:1, :], it[2:3, :], base_ref)
        pos2 = _sorted_pos(it[1:2, :], it[3:4, :], base_ref)
        hit = jnp.logical_or(rows == pos1, rows == pos2)
        sel = jnp.where(hit, 1.0, 0.0).astype(BF16)
        off = pl.multiple_of(c * TOK_TILE, TOK_TILE)
        o_ref[...] += jnp.dot(sel, x_ref[pl.ds(off, TOK_TILE), :], preferred_element_type=F32).astype(BF16)
        return carry

    lax.fori_loop(0, cn_ref[r], body, 0)


def _dispatch(c_lo, c_n, base, info_t, h2):
    return pl.pallas_call(
        _dispatch_kernel,
        out_shape=jax.ShapeDtypeStruct((SORT_ROWS, D_MODEL), BF16),
        grid_spec=pltpu.PrefetchScalarGridSpec(
            num_scalar_prefetch=3,
            grid=(SORT_ROWS // DISPATCH_TILE,),
            in_specs=[pl.BlockSpec((N_TILES, 8, TOK_TILE), lambda r, *_: (0, 0, 0)),
                      pl.BlockSpec((N_TOK, D_MODEL), lambda r, *_: (0, 0),
                                   pipeline_mode=pl.Buffered(1))],
            out_specs=pl.BlockSpec((DISPATCH_TILE, D_MODEL), lambda r, *_: (r, 0))),
        compiler_params=_cparams(1),
        name="moe_dispatch",
    )(c_lo, c_n, base, info_t, h2)


TILE_UNUSED, TILE_HALF, TILE_FULL = 0, 1, 2
CAST_ROWS = 64


def _expert_weight_ring(c, r, n_c, te_ref, tf_ref, nx_ref, copies, cast):
    @pl.when(jnp.logical_and(c == 0, r == 0))
    def _():
        for cp in copies(te_ref[0], 0):
            cp.start()

    @pl.when(tf_ref[r] == 1)
    def _():
        for cp in copies(te_ref[r], c):
            cp.wait()
        cast()
        nr = nx_ref[r]
        nc = c + (nr <= r).astype(jnp.int32)

        @pl.when(nc < n_c)
        def _():
            for cp in copies(te_ref[nr], nc):
                cp.start()


def _moe_up_kernel(te_ref, tv_ref, tf_ref, nx_ref, x_ref, w_hbm, h_ref, wst_ref, wbf_ref, sem_ref, *, i_moe):
    f = pl.program_id(0)
    r = pl.program_id(1)

    def copies(e, ft):
        out = []
        for part in range(2):
            col = pl.multiple_of((part * UP_TILES + ft) * UP_TILE, LANES)
            out.append(pltpu.make_async_copy(w_hbm.at[i_moe, e, :, pl.ds(col, UP_TILE)],
                                             wst_ref.at[part], sem_ref.at[part]))
        return out

    def cast():
        def rows_block(b, carry):
            rows = pl.ds(pl.multiple_of(b * CAST_ROWS, CAST_ROWS), CAST_ROWS)
            wbf_ref[:, rows, :] = wst_ref[:, rows, :].astype(BF16)
            return carry

        lax.fori_loop(0, D_MODEL // CAST_ROWS, rows_block, 0)

    _expert_weight_ring(f, r, UP_TILES, te_ref, tf_ref, nx_ref, copies, cast)

    def hidden(x):
        gt = jnp.dot(x, wbf_ref[0], preferred_element_type=F32)
        up = jnp.dot(x, wbf_ref[1], preferred_element_type=F32)
        return (gt * _sigmoid(gt) * up).astype(BF16)

    half = ROW_TILE // 2

    @pl.when(tv_ref[r] == TILE_FULL)
    def _():
        h_ref[...] = hidden(x_ref[...])

    @pl.when(tv_ref[r] == TILE_HALF)
    def _():
        h_ref[0:half, :] = hidden(x_ref[0:half, :])
        h_ref[half:, :] = jnp.zeros((half, UP_TILE), BF16)

    @pl.when(tv_ref[r] == TILE_UNUSED)
    def _():
        h_ref[...] = jnp.zeros_like(h_ref)


def _moe_up(i_moe, tile_e, tile_valid, tile_first, tile_next, xs, w13):
    return pl.pallas_call(
        functools.partial(_moe_up_kernel, i_moe=i_moe),
        out_shape=jax.ShapeDtypeStruct((SORT_ROWS, D_FF_EXPERT), BF16),
        grid_spec=pltpu.PrefetchScalarGridSpec(
            num_scalar_prefetch=4,
            grid=(UP_TILES, SORT_TILES),
            in_specs=[pl.BlockSpec((ROW_TILE, D_MODEL), lambda f, r, *_: (r, 0)),
                      pl.BlockSpec(memory_space=pl.ANY)],
            out_specs=pl.BlockSpec((ROW_TILE, UP_TILE), lambda f, r, *_: (r, f)),
            scratch_shapes=[pltpu.VMEM((2, D_MODEL, UP_TILE), F32),
                            pltpu.VMEM((2, D_MODEL, UP_TILE), BF16),
                            pltpu.SemaphoreType.DMA((2,))]),
        compiler_params=_cparams(2),
        name="moe_up",
    )(tile_e, tile_valid, tile_first, tile_next, xs, w13)


def _moe_down_kernel(te_ref, tv_ref, tf_ref, nx_ref, h_ref, w_hbm, y_ref, wst_ref, wbf_ref, sem_ref, *, i_moe):
    n = pl.program_id(0)
    r = pl.program_id(1)

    def copies(e, nt):
        col = pl.multiple_of(nt * DOWN_TILE, LANES)
        return [pltpu.make_async_copy(w_hbm.at[i_moe, e, :, pl.ds(col, DOWN_TILE)], wst_ref, sem_ref.at[0])]

    def cast():
        wbf_ref[...] = wst_ref[...].astype(BF16)

    _expert_weight_ring(n, r, D_MODEL // DOWN_TILE, te_ref, tf_ref, nx_ref, copies, cast)

    half = ROW_TILE // 2

    @pl.when(tv_ref[r] == TILE_FULL)
    def _():
        y_ref[...] = jnp.dot(h_ref[...], wbf_ref[...], preferred_element_type=F32).astype(BF16)

    @pl.when(tv_ref[r] == TILE_HALF)
    def _():
        y_ref[0:half, :] = jnp.dot(h_ref[0:half, :], wbf_ref[...], preferred_element_type=F32).astype(BF16)
        y_ref[half:, :] = jnp.zeros((half, DOWN_TILE), BF16)

    @pl.when(tv_ref[r] == TILE_UNUSED)
    def _():
        y_ref[...] = jnp.zeros_like(y_ref)


def _moe_down(i_moe, tile_e, tile_valid, tile_first, tile_next, hs, w2):
    return pl.pallas_call(
        functools.partial(_moe_down_kernel, i_moe=i_moe),
        out_shape=jax.ShapeDtypeStruct((SORT_ROWS, D_MODEL), BF16),
        grid_spec=pltpu.PrefetchScalarGridSpec(
            num_scalar_prefetch=4,
            grid=(D_MODEL // DOWN_TILE, SORT_TILES),
            in_specs=[pl.BlockSpec((ROW_TILE, D_FF_EXPERT), lambda n, r, *_: (r, 0)),
                      pl.BlockSpec(memory_space=pl.ANY)],
            out_specs=pl.BlockSpec((ROW_TILE, DOWN_TILE), lambda n, r, *_: (r, n)),
            scratch_shapes=[pltpu.VMEM((D_FF_EXPERT, DOWN_TILE), F32),
                            pltpu.VMEM((D_FF_EXPERT, DOWN_TILE), BF16),
                            pltpu.SemaphoreType.DMA((1,))]),
        compiler_params=_cparams(2),
        name="moe_down",
    )(tile_e, tile_valid, tile_first, tile_next, hs, w2)


def _window_copy(y_hbm, ybuf_ref, sem_ref, src, buf, slot):
    return pltpu.make_async_copy(y_hbm.at[pl.ds(src, WIN), :],
                                 ybuf_ref.at[buf, pl.ds(slot * WIN, WIN), :], sem_ref.at[buf, slot])


def _start_windows(src_ref, y_hbm, ybuf_ref, sem_ref, tile, buf):
    for s in range(N_WIN):
        src = pl.multiple_of(src_ref[tile * N_WIN + s], BF16_SUBLANES)
        _window_copy(y_hbm, ybuf_ref, sem_ref, src, buf, s).start()


def _combine_kernel(src_ref, lo_ref, hi_ref, base_ref, info_ref, x_ref, mod_ref, g_ref, y_hbm,
                    yp_ref, ys_ref, ybuf_ref, sem_ref):
    j = pl.program_id(0)
    buf = j % 2

    @pl.when(j == 0)
    def _():
        _start_windows(src_ref, y_hbm, ybuf_ref, sem_ref, 0, 0)

    @pl.when(j + 1 < N_TILES)
    def _():
        _start_windows(src_ref, y_hbm, ybuf_ref, sem_ref, j + 1, 1 - buf)

    info = info_ref[...]
    pos1 = _sorted_pos(info[:, 0:1], info[:, 2:3], base_ref)
    pos2 = _sorted_pos(info[:, 1:2], info[:, 3:4], base_ref)
    g1, g2 = info[:, 4:5], info[:, 5:6]
    iota = lax.broadcasted_iota(jnp.int32, (1, WIN), 1)
    ids = []
    for s in range(N_WIN):
        row = src_ref[j * N_WIN + s] + iota
        ok = jnp.logical_and(row >= lo_ref[j * N_WIN + s], row < hi_ref[j * N_WIN + s])
        ids.append(jnp.where(ok, row, -1))
    row_id = jnp.concatenate(ids, axis=1).astype(F32)

    for s in range(N_WIN):
        _window_copy(y_hbm, ybuf_ref, sem_ref, 0, buf, s).wait()
    halves = []
    for rows in (slice(0, TOK_TILE // 2), slice(TOK_TILE // 2, TOK_TILE)):
        sel = (jnp.where(pos1[rows] == row_id, g1[rows], 0.0)
               + jnp.where(pos2[rows] == row_id, g2[rows], 0.0)).astype(BF16)
        ffn = jnp.dot(sel, ybuf_ref[buf], preferred_element_type=F32)
        halves.append(x_ref[rows, :] + mod_ref[5:6, :] * _rms(ffn, g_ref[3:4, :]))
    out = jnp.concatenate(halves, axis=0)

    @pl.when(j < P_TILES)
    def _():
        yp_ref[...] = out

    @pl.when(j >= P_TILES)
    def _():
        ys_ref[...] = out


def _combine(l, win_src, win_lo, win_hi, base, info, x, mod, norm_g, ys):
    return pl.pallas_call(
        _combine_kernel,
        out_shape=(jax.ShapeDtypeStruct((N_PROMPT, D_MODEL), F32),
                   jax.ShapeDtypeStruct((N_SAMPLE, D_MODEL), F32)),
        grid_spec=pltpu.PrefetchScalarGridSpec(
            num_scalar_prefetch=4,
            grid=(N_TILES,),
            in_specs=[pl.BlockSpec((TOK_TILE, LANES), lambda i, *_: (i, 0)),
                      pl.BlockSpec((TOK_TILE, D_MODEL), lambda i, *_: (i, 0)),
                      pl.BlockSpec((None, None, 6, D_MODEL), lambda i, *_: (l, _cond_id(i), 0, 0)),
                      pl.BlockSpec((None, 4, D_MODEL), lambda i, *_: (l, 0, 0)),
                      pl.BlockSpec(memory_space=pl.ANY)],
            out_specs=(pl.BlockSpec((TOK_TILE, D_MODEL), lambda i, *_: (jnp.minimum(i, P_TILES - 1), 0)),
                       pl.BlockSpec((TOK_TILE, D_MODEL), lambda i, *_: (jnp.maximum(i - P_TILES, 0), 0))),
            scratch_shapes=[pltpu.VMEM((2, N_WIN * WIN, D_MODEL), BF16),
                            pltpu.SemaphoreType.DMA((2, N_WIN))]),
        compiler_params=_cparams(1),
        name="moe_combine",
    )(win_src, win_lo, win_hi, base, info, x, mod, norm_g, ys)


def _moe_ffn(l, i_moe, h2, x, mod, norm_g, routing, moe_w13, moe_w2):
    info, info_t, cb, tot = routing

    ar8 = jnp.arange(N_EXPERTS, dtype=jnp.int32)

    def at8(vec, idx):
        return jnp.sum(jnp.where(idx[..., None] == ar8, vec, 0), axis=-1)

    def cumsum8(a):
        return jnp.sum(jnp.where(ar8[None, :] <= ar8[:, None], a[..., None, :], 0), axis=-1)

    counts = tot[0, :N_EXPERTS].astype(jnp.int32)
    padded = (counts + ROW_TILE - 1) // ROW_TILE * ROW_TILE
    seg_end = cumsum8(padded)
    base = (seg_end - padded).astype(jnp.int32)
    cbx = cb[:, 0, :N_EXPERTS].astype(jnp.int32)
    cb_end = jnp.concatenate([cbx[1:], counts[None, :]], axis=0)

    def tile_segments(n_tiles, rows):
        row0 = jnp.arange(n_tiles, dtype=jnp.int32) * rows
        e = jnp.minimum(jnp.sum(seg_end[None, :] <= row0[:, None], axis=1), N_EXPERTS - 1).astype(jnp.int32)
        k0 = row0 - at8(base, e)
        return e, k0, jnp.logical_and(k0 >= 0, k0 < at8(counts, e))

    tile_e, k0, tile_valid = tile_segments(SORT_TILES, ROW_TILE)
    tile_first = jnp.logical_and(tile_valid, k0 == 0)
    d_e, d_k0, d_valid = tile_segments(SORT_ROWS // DISPATCH_TILE, DISPATCH_TILE)
    d_kend = jnp.minimum(d_k0 + DISPATCH_TILE, at8(counts, d_e))
    cb_t = jnp.sum(jnp.where(d_e[:, None, None] == ar8[None, :, None], cbx.T[None, :, :], 0), axis=1)
    c_lo = jnp.sum(cb_t <= d_k0[:, None], axis=1) - 1
    c_hi = jnp.sum(cb_t < d_kend[:, None], axis=1) - 1
    c_n = jnp.where(d_valid, c_hi - c_lo + 1, 0).astype(jnp.int32)
    c_lo = jnp.where(d_valid, c_lo, 0).astype(jnp.int32)
    tile_idx = jnp.arange(SORT_TILES, dtype=jnp.int32)
    first_idx = jnp.where(tile_first, tile_idx, SORT_TILES)
    later = jnp.where(tile_idx[None, :] > tile_idx[:, None], first_idx[None, :], SORT_TILES)
    tile_next = jnp.min(later, axis=1)
    tile_next = jnp.where(tile_next >= SORT_TILES, 0, tile_next).astype(jnp.int32)
    tile_rows = at8(counts, tile_e) - k0
    tile_valid = jnp.where(tile_valid, jnp.where(tile_rows <= ROW_TILE // 2, TILE_HALF, TILE_FULL),
                           TILE_UNUSED).astype(jnp.int32)
    tile_first = tile_first.astype(jnp.int32)

    seg_lo = base[None, :] + cbx
    seg_hi = base[None, :] + cb_end
    seg_n = seg_hi - seg_lo
    w0 = seg_lo // BF16_SUBLANES * BF16_SUBLANES
    nw = jnp.where(seg_n > 0, (seg_lo - w0 + seg_n + WIN - 1) // WIN, 0)
    nw_end = cumsum8(nw)
    slot = jnp.arange(N_WIN, dtype=jnp.int32)
    slot_e = jnp.minimum(jnp.sum(nw_end[:, None, :] <= slot[None, :, None], axis=2), N_EXPERTS - 1)
    pick = slot_e[:, :, None] == jnp.arange(N_EXPERTS, dtype=jnp.int32)[None, None, :]
    take = lambda a: jnp.sum(jnp.where(pick, a[:, None, :], 0), axis=2)
    slot_k = slot[None, :] - (take(nw_end) - take(nw))
    slot_ok = slot[None, :] < nw_end[:, -1:]
    slot_src = take(w0) + WIN * slot_k
    win_src = jnp.where(slot_ok, slot_src, 0).astype(jnp.int32).reshape(-1)
    win_lo = jnp.where(slot_ok, take(seg_lo), 0).astype(jnp.int32).reshape(-1)
    win_hi = jnp.where(slot_ok, take(seg_hi), 0).astype(jnp.int32).reshape(-1)

    xs = _dispatch(c_lo, c_n, base, info_t, h2)
    hs = _moe_up(i_moe, tile_e, tile_valid, tile_first, tile_next, xs, moe_w13)
    ys = _moe_down(i_moe, tile_e, tile_valid, tile_first, tile_next, hs, moe_w2)
    return _combine(l, win_src, win_lo, win_hi, base, info, x, mod, norm_g, ys)


def kernel(x_prompt, x_sample, cache_k, cache_v, c, c_ctx, w_ada, b_ada, norm_g, w_in, w_out,
           lam_params, subln_g, dw_weight, dw_bias, conv_ln_g, conv_ln_b, dense_w13, dense_w2,
           router_w, moe_w13, moe_w2):
    x = (x_prompt.reshape(N_PROMPT, D_MODEL), x_sample.reshape(N_SAMPLE, D_MODEL))
    cond8 = jnp.zeros((8, D_MODEL), F32).at[0].set(c_ctx).at[1:1 + DEC_BATCH].set(c)
    mod = _ada_modulation(cond8, w_ada, b_ada).reshape(DEPTH, 8, 6, D_MODEL)
    rope = _rope_tables()
    ck = cache_k.reshape(DEC_BATCH, DEPTH, PAST_LEN * ATT_HEADS, V_DIM)
    cv = cache_v.reshape(DEC_BATCH, DEPTH, PAST_LEN * ATT_HEADS, V_DIM)

    caches = None
    for l in range(DEPTH):
        lam_init = 0.8 - 0.6 * math.exp(-0.3 * l)
        qkv, cvo, *caches = _mixer_in(l, x, mod, norm_g, w_in, rope,
                                      (dw_weight, dw_bias, conv_ln_g, conv_ln_b), caches)
        o_ctx, o_lat = _attention(l, lam_init, lam_params, subln_g, qkv, ck, cv)
        i = l // 2
        if l % 2 == 0:
            (x,) = _outproj(l, o_ctx, o_lat, cvo, x, mod, norm_g, w_out,
                            dense_w=(dense_w13[i].astype(BF16), dense_w2[i].astype(BF16)))
        else:
            x, h2, *routing = _outproj(l, o_ctx, o_lat, cvo, x, mod, norm_g, w_out, router_w=router_w[i])
            x = _moe_ffn(l, i, h2, x, mod, norm_g, routing, moe_w13, moe_w2)

    xp, xs = x if isinstance(x, tuple) else (x[:N_PROMPT], x[N_PROMPT:])
    new_k, new_v = (a.reshape(BATCH, DEPTH, SEQ, ATT_HEADS, V_DIM) for a in caches)
    return (xp.reshape(BATCH, SEQ, D_MODEL), xs.reshape(DEC_BATCH, DEC_SEQ, D_MODEL), new_k, new_v)
```

```python
import functools
import math

import jax
import jax.numpy as jnp
import numpy as np
from jax import lax
from jax.experimental import pallas as pl
from jax.experimental.pallas import tpu as pltpu

F32 = jnp.float32
BF16 = jnp.bfloat16

D_MODEL = 1024
BATCH = 32
SEQ = 256
DEPTH = 2
DEC_BATCH = 2
DEC_SEQ = 2048
PAST_LEN = 512
GRID_W = 64
ATT_HEADS = 4
QK_DIM = 64
V_DIM = 128
ATT_W = ATT_HEADS * V_DIM
IN_W = 5 * ATT_W
CONV_CH = 512
CONV_WIDTH = 31
D_FF = 2816
N_EXPERTS = 8
D_FF_EXPERT = 3584
ROPE_THETA = 10000.0
NORM_EPS = 1e-6
LN_EPS = 1e-5

N_PROMPT = BATCH * SEQ
N_SAMPLE = DEC_BATCH * DEC_SEQ
N_TOK = N_PROMPT + N_SAMPLE

LANES = 128
BF16_SUBLANES = 16
MXU_TILE = 256
VMEM_BYTES = 64 * 1024 * 1024
VMEM_LIMIT = VMEM_BYTES - 8 * 1024 * 1024

TOK_TILE = 512
N_TILES = N_TOK // TOK_TILE
P_TILES = N_PROMPT // TOK_TILE
S_TILES_PER_BATCH = DEC_SEQ // TOK_TILE

ROW_TILE = 512
SORT_TILES = (2 * N_TOK + N_EXPERTS * ROW_TILE) // ROW_TILE + 1
SORT_ROWS = SORT_TILES * ROW_TILE
DISPATCH_TILE = 256
UP_TILE = 7 * MXU_TILE
UP_TILES = D_FF_EXPERT // UP_TILE
DOWN_TILE = 4 * MXU_TILE
WIN = 128
N_WIN = (2 * TOK_TILE + N_EXPERTS * (BF16_SUBLANES - 1 + WIN - 1)) // WIN

assert D_FF_EXPERT % UP_TILE == 0 and D_MODEL % DOWN_TILE == 0 and D_FF % MXU_TILE == 0
assert ROW_TILE % DISPATCH_TILE == 0 and N_TOK % TOK_TILE == 0 and N_PROMPT % TOK_TILE == 0
assert DEPTH == 2, "layer 0 zero-fills exactly one later cache slab; the last layer must be the routed one"


def _cparams(n_axes):
    return pltpu.CompilerParams(dimension_semantics=("arbitrary",) * n_axes,
                                vmem_limit_bytes=VMEM_LIMIT)


def _cond_id(i):
    return jnp.where(i < P_TILES, 0, 1 + (i - P_TILES) // S_TILES_PER_BATCH)


def _sigmoid(x):
    return 1.0 / (1.0 + jnp.exp(-x))


def _rms(x, g):
    return x * lax.rsqrt(jnp.mean(x * x, axis=-1, keepdims=True) + NORM_EPS) * g


def _ada_kernel(c_ref, w_ref, b_ref, o_ref):
    c = c_ref[...]
    s = (c * _sigmoid(c)).astype(BF16)
    o_ref[...] = jnp.dot(s, w_ref[...].astype(BF16), preferred_element_type=F32) + b_ref[...]


def _ada_modulation(cond8, w_ada, b_ada):
    tn = 3072
    n = 6 * D_MODEL
    return pl.pallas_call(
        _ada_kernel,
        out_shape=jax.ShapeDtypeStruct((DEPTH, 8, n), F32),
        grid=(DEPTH, n // tn),
        in_specs=[pl.BlockSpec((8, D_MODEL), lambda l, j: (0, 0)),
                  pl.BlockSpec((None, D_MODEL, tn), lambda l, j: (l, 0, j)),
                  pl.BlockSpec((None, 1, tn), lambda l, j: (l, 0, j))],
        out_specs=pl.BlockSpec((None, 8, tn), lambda l, j: (l, 0, j)),
        compiler_params=_cparams(2),
        name="ada_modulation",
    )(cond8, w_ada, b_ada.reshape(DEPTH, 1, n))


def _tile_x(xa_ref, xb_ref):
    return jnp.where(pl.program_id(0) < P_TILES, xa_ref[...], xb_ref[...])


def _x_specs(x):
    last = N_TILES - 1
    if isinstance(x, tuple):
        xa, xb = x
        b_idx = lambda i, *_: (jnp.maximum(jnp.minimum(i, last) - P_TILES, 0), 0)
    else:
        xa = xb = x
        b_idx = lambda i, *_: (jnp.maximum(jnp.minimum(i, last), P_TILES), 0)
    a_idx = lambda i, *_: (jnp.minimum(i, P_TILES - 1), 0)
    return [pl.BlockSpec((TOK_TILE, D_MODEL), a_idx), pl.BlockSpec((TOK_TILE, D_MODEL), b_idx)], (xa, xb)


ROT_SPAN = QK_DIM // 4
SEQ_PER_TILE = TOK_TILE // SEQ
CACHE_ROWS = SEQ * ATT_HEADS


CONV_LAG = 2
CONV_RING = 3
CONV_TILE = 256
CONV_HALO = 16
CONV_SUB = 32


def _conv_pass(upad_ref, w_ref, bias_ref, lg_ref, lb_ref, o_ref, row0):
    rows = CONV_TILE + 2 * CONV_HALO - 8
    for s in range(1, 8):
        upad_ref[s, 0:rows, :] = upad_ref[0, s:s + rows, :]
    first_tap = CONV_HALO - CONV_WIDTH // 2
    groups = CONV_SUB // 8
    for t in range(CONV_TILE // CONV_SUB):
        base = t * CONV_SUB
        acc = jnp.zeros((groups, 8, CONV_CH), F32)
        for j in range(CONV_WIDTH):
            lo = base + (first_tap + j) // 8 * 8
            taps = upad_ref[(first_tap + j) % 8, lo:lo + CONV_SUB, :].reshape(groups, 8, CONV_CH)
            acc = acc + taps * w_ref[j][None]
        y = acc.reshape(CONV_SUB, CONV_CH) + bias_ref[...]
        mu = jnp.mean(y, axis=-1, keepdims=True)
        yc = y - mu
        var = jnp.mean(yc * yc, axis=-1, keepdims=True)
        z = yc * lax.rsqrt(var + LN_EPS) * lg_ref[...] + lb_ref[...]
        o_ref[row0 + base:row0 + base + CONV_SUB, :] = (z * _sigmoid(z)).astype(BF16)


def _mixer_in_kernel(*refs, layer):
    n_in = 12 + (2 if layer else 0)
    (xa_ref, xb_ref, mod_ref, g_ref, w_ref, cos_ref, sina_ref, sinb_ref,
     dw_ref, db_ref, lg_ref, lb_ref) = refs[:12]
    qkv_ref, cv_ref, kc_ref, vc_ref, wbf_ref, ring_ref, glu_ref, pj_ref, upad_ref, taps_ref = refs[n_in:]
    i = pl.program_id(0)
    t = jnp.minimum(i, N_TILES - 1)

    @pl.when(i == 0)
    def _():
        wbf_ref[...] = w_ref[...].astype(BF16)
        ring_ref[...] = jnp.zeros_like(ring_ref)
        glu_ref[...] = jnp.zeros_like(glu_ref)
        for j in range(CONV_WIDTH):
            taps_ref[j] = jnp.broadcast_to(dw_ref[j:j + 1, :], (8, CONV_CH))

    ring_ref[(i - 1) % CONV_RING] = glu_ref[...]

    h = _rms(_tile_x(xa_ref, xb_ref), g_ref[0:1, :]) * (1.0 + mod_ref[1:2, :]) + mod_ref[0:1, :]
    hb = h.astype(BF16)

    def proj(part):
        return jnp.dot(hb, wbf_ref[:, part * ATT_W:(part + 1) * ATT_W], preferred_element_type=F32)

    for part in range(3):
        pj_ref[:, part * ATT_W:(part + 1) * ATT_W] = proj(part)
    glu_ref[...] = proj(3) * _sigmoid(proj(4))

    j = i - CONV_LAG
    latent = j >= P_TILES
    q = (j - P_TILES) % S_TILES_PER_BATCH
    has_prev = jnp.logical_and(latent, q != 0)
    has_next = jnp.logical_and(latent, q != S_TILES_PER_BATCH - 1)
    cur = j % CONV_RING
    prev_tail = ring_ref[(j - 1) % CONV_RING, TOK_TILE - CONV_HALO:, :]
    next_head = ring_ref[(j + 1) % CONV_RING, 0:CONV_HALO, :]
    for s in range(TOK_TILE // CONV_TILE):
        lo = s * CONV_TILE
        if s == 0:
            before = jnp.where(has_prev, prev_tail, 0.0)
        else:
            before = jnp.where(latent, ring_ref[cur, lo - CONV_HALO:lo, :], 0.0)
        if s == TOK_TILE // CONV_TILE - 1:
            after = jnp.where(has_next, next_head, 0.0)
        else:
            after = jnp.where(latent, ring_ref[cur, lo + CONV_TILE:lo + CONV_TILE + CONV_HALO, :], 0.0)
        upad_ref[0, 0:CONV_HALO, :] = before
        upad_ref[0, CONV_HALO:CONV_HALO + CONV_TILE, :] = ring_ref[cur, lo:lo + CONV_TILE, :]
        upad_ref[0, CONV_HALO + CONV_TILE:, :] = after
        _conv_pass(upad_ref, taps_ref, db_ref, lg_ref, lb_ref, cv_ref, lo)

    qkv_ref[:, 2 * ATT_W:] = pj_ref[:, 2 * ATT_W:].astype(BF16)

    @pl.when(t < P_TILES)
    def _():
        qkv_ref[:, :2 * ATT_W] = pj_ref[:, :2 * ATT_W].astype(BF16)
        for ref, col0 in ((kc_ref, ATT_W), (vc_ref, 2 * ATT_W)):
            for s in range(SEQ_PER_TILE):
                for hd in range(ATT_HEADS):
                    val = pj_ref[SEQ * s:SEQ * (s + 1), col0 + V_DIM * hd:col0 + V_DIM * (hd + 1)]
                    rows = pl.ds(hd, SEQ, stride=ATT_HEADS)
                    if layer == 0:
                        ref[s, 0, rows, :] = val
                    else:
                        ref[s, rows, :] = val
            if layer == 0:
                ref[:, 1:] = jnp.zeros((SEQ_PER_TILE, DEPTH - 1, CACHE_ROWS, V_DIM), F32)

    @pl.when(t >= P_TILES)
    def _():
        cos = cos_ref[...]
        sina = sina_ref[...]
        sinb = sinb_ref[...]
        for c in range(2 * ATT_W // V_DIM):
            xg = pj_ref[:, V_DIM * c:V_DIM * (c + 1)]
            fwd = pltpu.roll(xg, V_DIM - ROT_SPAN, 1)
            bwd = pltpu.roll(xg, ROT_SPAN, 1)
            qkv_ref[:, V_DIM * c:V_DIM * (c + 1)] = (xg * cos + fwd * sina + bwd * sinb).astype(BF16)


def _mixer_in(l, x, mod, norm_g, w_in, rope, conv_params, caches):
    cos, sina, sinb = rope
    dw_w, dw_b, ln_g, ln_b = conv_params
    x_specs, x_args = _x_specs(x)

    def tile(i):
        return jnp.minimum(i, N_TILES - 1)

    def rope_idx(i):
        return (jnp.maximum(tile(i) - P_TILES, 0) % S_TILES_PER_BATCH, 0)

    def vec(a):
        return a.reshape(DEPTH, 1, CONV_CH)

    cache_shape = jax.ShapeDtypeStruct((BATCH, DEPTH, CACHE_ROWS, V_DIM), F32)
    if l == 0:
        cache_spec = pl.BlockSpec((SEQ_PER_TILE, DEPTH, CACHE_ROWS, V_DIM),
                                  lambda i: (jnp.minimum(i, P_TILES - 1), 0, 0, 0))
        extra_specs, extra_args, aliases = [], (), {}
    else:
        cache_spec = pl.BlockSpec((SEQ_PER_TILE, None, CACHE_ROWS, V_DIM),
                                  lambda i: (jnp.minimum(i, P_TILES - 1), l, 0, 0))
        extra_specs = [pl.BlockSpec(memory_space=pl.ANY)] * 2
        extra_args = tuple(caches)
        aliases = {12: 2, 13: 3}

    return pl.pallas_call(
        functools.partial(_mixer_in_kernel, layer=l),
        out_shape=(jax.ShapeDtypeStruct((N_TOK, 3 * ATT_W), BF16),
                   jax.ShapeDtypeStruct((N_TOK, CONV_CH), BF16), cache_shape, cache_shape),
        grid=(N_TILES + CONV_LAG,),
        in_specs=x_specs + [
            pl.BlockSpec((None, None, 6, D_MODEL), lambda i: (l, _cond_id(tile(i)), 0, 0)),
            pl.BlockSpec((None, 4, D_MODEL), lambda i: (l, 0, 0)),
            pl.BlockSpec((None, D_MODEL, IN_W), lambda i: (l, 0, 0), pipeline_mode=pl.Buffered(1)),
            pl.BlockSpec((TOK_TILE, V_DIM), rope_idx),
            pl.BlockSpec((TOK_TILE, V_DIM), rope_idx),
            pl.BlockSpec((TOK_TILE, V_DIM), rope_idx),
            pl.BlockSpec((None, CONV_WIDTH, CONV_CH), lambda i: (l, 0, 0)),
            pl.BlockSpec((None, 1, CONV_CH), lambda i: (l, 0, 0)),
            pl.BlockSpec((None, 1, CONV_CH), lambda i: (l, 0, 0)),
            pl.BlockSpec((None, 1, CONV_CH), lambda i: (l, 0, 0))] + extra_specs,
        out_specs=(pl.BlockSpec((TOK_TILE, 3 * ATT_W), lambda i: (tile(i), 0)),
                   pl.BlockSpec((TOK_TILE, CONV_CH), lambda i: (jnp.maximum(i - CONV_LAG, 0), 0)),
                   cache_spec, cache_spec),
        scratch_shapes=[pltpu.VMEM((D_MODEL, IN_W), BF16),
                        pltpu.VMEM((CONV_RING, TOK_TILE, CONV_CH), F32),
                        pltpu.VMEM((TOK_TILE, CONV_CH), F32),
                        pltpu.VMEM((TOK_TILE, 3 * ATT_W), F32),
                        pltpu.VMEM((8, CONV_TILE + 2 * CONV_HALO, CONV_CH), F32),
                        pltpu.VMEM((CONV_WIDTH, 8, CONV_CH), F32)],
        input_output_aliases=aliases,
        compiler_params=_cparams(1),
        name=f"mixer_in_{l}",
    )(*x_args, mod, norm_g, w_in, cos, sina, sinb, dw_w, vec(dw_b), vec(ln_g), vec(ln_b), *extra_args)


def _rope_tables():
    rows = DEC_SEQ // GRID_W
    row_pos = np.repeat(np.arange(rows, dtype=np.float64), GRID_W)
    col_pos = np.tile(np.arange(GRID_W, dtype=np.float64), rows)
    half = QK_DIM // 2
    inv_freq = 1.0 / (ROPE_THETA ** (np.arange(0, half, 2, dtype=np.float64) / half))
    ang_r = row_pos[:, None] * inv_freq
    ang_c = col_pos[:, None] * inv_freq
    ang = np.concatenate([ang_r, ang_r, ang_c, ang_c], axis=-1)
    cos = np.tile(np.cos(ang), (1, 2)).astype(np.float32)
    sin = np.tile(np.sin(ang), (1, 2)).astype(np.float32)
    first = (np.arange(V_DIM) % (2 * ROT_SPAN)) < ROT_SPAN
    sina = np.where(first[None, :], -sin, 0.0).astype(np.float32)
    sinb = np.where(first[None, :], 0.0, sin).astype(np.float32)
    return jnp.asarray(cos), jnp.asarray(sina), jnp.asarray(sinb)


def _attn_kernel(*refs, lam_init, has_ext):
    if has_ext:
        lamp_ref, sub_ref, q_ref, k_ref, v_ref, ke_ref, ve_ref, o_ref = refs
    else:
        lamp_ref, sub_ref, q_ref, k_ref, v_ref, o_ref = refs
    lp = lamp_ref[...]
    lam = (jnp.exp(jnp.sum(lp[0:1] * lp[1:2], axis=-1, keepdims=True))
           - jnp.exp(jnp.sum(lp[2:3] * lp[3:4], axis=-1, keepdims=True)) + lam_init)
    lane = lax.broadcasted_iota(jnp.int32, (1, V_DIM), 1)
    nt = (((1,), (1,)), ((), ()))
    scale = QK_DIM ** -0.5
    map_scale = [jnp.where(lane < QK_DIM, scale, 0.0).astype(BF16),
                 jnp.where(lane < QK_DIM, 0.0, scale).astype(BF16)]

    tq = q_ref.shape[0]
    head_cols = [slice(V_DIM * hd, V_DIM * (hd + 1)) for hd in range(ATT_HEADS)]

    def q_map(hd, m):
        return q_ref[:, head_cols[hd]] * map_scale[m]

    def with_ones(v):
        return jnp.concatenate([v, jnp.ones_like(v)], axis=1)

    outs = []
    if has_ext:
        for hd in range(ATT_HEADS):
            kh = k_ref[:, head_cols[hd]]
            v_aug = with_ones(v_ref[:, head_cols[hd]])
            head_rows = pl.ds(hd, PAST_LEN, stride=ATT_HEADS)
            keh = ke_ref[head_rows, :].astype(BF16)
            ve_aug = with_ones(ve_ref[head_rows, :].astype(BF16))
            ratio = []
            for m in range(2):
                qm = q_map(hd, m)
                s = lax.dot_general(qm, kh, nt, preferred_element_type=F32)
                se = lax.dot_general(qm, keh, nt, preferred_element_type=F32)
                mx = jnp.maximum(jnp.max(s, axis=-1, keepdims=True), jnp.max(se, axis=-1, keepdims=True))
                pv = (jnp.dot(jnp.exp((s - mx).astype(BF16)), v_aug, preferred_element_type=F32)
                      + jnp.dot(jnp.exp((se - mx).astype(BF16)), ve_aug, preferred_element_type=F32))
                ratio.append(pv[:, :V_DIM] / pv[:, V_DIM:])
            outs.append(ratio[0] - lam * ratio[1])
    else:
        seqs = [slice(SEQ * sq, SEQ * (sq + 1)) for sq in range(tq // SEQ)]
        s = jnp.concatenate(
            [lax.dot_general(q_ref[rows, head_cols[hd]] * map_scale[m], k_ref[rows, head_cols[hd]], nt,
                             preferred_element_type=F32)
             for rows in seqs for hd in range(ATT_HEADS) for m in range(2)], axis=0)
        p = jnp.exp((s - jnp.max(s, axis=-1, keepdims=True)).astype(BF16))
        for hd in range(ATT_HEADS):
            per_seq = []
            for sq, rows in enumerate(seqs):
                first = (sq * ATT_HEADS + hd) * 2 * SEQ
                pv = jnp.dot(p[first:first + 2 * SEQ], with_ones(v_ref[rows, head_cols[hd]]),
                             preferred_element_type=F32)
                ratio = pv[:, :V_DIM] / pv[:, V_DIM:]
                per_seq.append(ratio[:SEQ] - lam * ratio[SEQ:])
            outs.append(jnp.concatenate(per_seq, axis=0))

    heads = [(_rms(o, sub_ref[...]) * (1.0 - lam_init)).astype(BF16) for o in outs]
    o_ref[...] = jnp.concatenate(heads, axis=1)


CTX_SEQS = 4


def _attention(l, lam_init, lam_params, subln_g, qkvg, cache_k, cache_v):
    small = [pl.BlockSpec((None, 4, QK_DIM), lambda *_: (l, 0, 0)),
             pl.BlockSpec((None, 1, V_DIM), lambda *_: (l, 0, 0))]
    sub3 = subln_g.reshape(DEPTH, 1, V_DIM)

    o_ctx = pl.pallas_call(
        functools.partial(_attn_kernel, lam_init=lam_init, has_ext=False),
        out_shape=jax.ShapeDtypeStruct((N_PROMPT, ATT_W), BF16),
        grid=(BATCH // CTX_SEQS,),
        in_specs=small + [pl.BlockSpec((CTX_SEQS * SEQ, ATT_W), lambda b: (b, 0)),
                          pl.BlockSpec((CTX_SEQS * SEQ, ATT_W), lambda b: (b, 1)),
                          pl.BlockSpec((CTX_SEQS * SEQ, ATT_W), lambda b: (b, 2))],
        out_specs=pl.BlockSpec((CTX_SEQS * SEQ, ATT_W), lambda b: (b, 0)),
        compiler_params=_cparams(1),
        name=f"attn_ctx_{l}",
    )(lam_params, sub3, qkvg, qkvg, qkvg)

    tq = 512
    q_tiles = DEC_SEQ // tq
    q_base = N_PROMPT // tq
    kv_base = N_PROMPT // DEC_SEQ

    def q_idx(b, i):
        return (q_base + b * q_tiles + i, 0)

    o_lat = pl.pallas_call(
        functools.partial(_attn_kernel, lam_init=lam_init, has_ext=True),
        out_shape=jax.ShapeDtypeStruct((N_SAMPLE, ATT_W), BF16),
        grid=(DEC_BATCH, q_tiles),
        in_specs=small + [pl.BlockSpec((tq, ATT_W), q_idx),
                          pl.BlockSpec((DEC_SEQ, ATT_W), lambda b, i: (kv_base + b, 1)),
                          pl.BlockSpec((DEC_SEQ, ATT_W), lambda b, i: (kv_base + b, 2)),
                          pl.BlockSpec((None, None, PAST_LEN * ATT_HEADS, V_DIM), lambda b, i: (b, l, 0, 0)),
                          pl.BlockSpec((None, None, PAST_LEN * ATT_HEADS, V_DIM), lambda b, i: (b, l, 0, 0))],
        out_specs=pl.BlockSpec((tq, ATT_W), lambda b, i: (b * q_tiles + i, 0)),
        compiler_params=_cparams(2),
        name=f"attn_lat_{l}",
    )(lam_params, sub3, qkvg, qkvg, qkvg, cache_k, cache_v)
    return o_ctx, o_lat


DENSE_BLOCKS = 11
DENSE_COLS = 2 * D_FF // DENSE_BLOCKS
DENSE_ROWS = D_FF // DENSE_BLOCKS
assert DENSE_COLS * DENSE_BLOCKS == 2 * D_FF and DENSE_COLS % LANES == 0
assert DENSE_ROWS * DENSE_BLOCKS == D_FF and DENSE_ROWS % BF16_SUBLANES == 0


def _outproj_kernel(*refs, ffn, dense_index=0):
    if ffn == "routed":
        (oc_ref, ol_ref, cv_ref, xa_ref, xb_ref, mod_ref, g_ref, w_ref, rw_ref,
         xo_ref, h2_ref, info_ref, info_t_ref, cb_ref, tot_ref, wbf_ref, carry_ref, tri_ref) = refs
    else:
        (oc_ref, ol_ref, cv_ref, xa_ref, xb_ref, mod_ref, g_ref, w_ref, w13_hbm, w2_hbm,
         xo_ref, wbf_ref, h2_ref, w13_ref, w2_ref, st13_ref, st2_ref, sem_ref) = refs
    i = pl.program_id(0)

    @pl.when(i == 0)
    def _():
        wbf_ref[...] = w_ref[...].astype(BF16)
        if ffn == "routed":
            carry_ref[...] = jnp.zeros_like(carry_ref)
            tri_ref[...] = _lower_triangle()
        else:
            def w13_copy(k):
                return pltpu.make_async_copy(w13_hbm.at[dense_index, :, pl.ds(k * DENSE_COLS, DENSE_COLS)],
                                             st13_ref, sem_ref.at[0])

            def w2_copy(k):
                return pltpu.make_async_copy(w2_hbm.at[dense_index, pl.ds(k * DENSE_ROWS, DENSE_ROWS), :],
                                             st2_ref, sem_ref.at[1])

            w13_copy(0).start()
            w2_copy(0).start()
            for k in range(DENSE_BLOCKS):
                w13_copy(k).wait()
                w13_ref[:, k * DENSE_COLS:(k + 1) * DENSE_COLS] = st13_ref[...].astype(BF16)
                if k + 1 < DENSE_BLOCKS:
                    w13_copy(k + 1).start()
                w2_copy(k).wait()
                w2_ref[k * DENSE_ROWS:(k + 1) * DENSE_ROWS, :] = st2_ref[...].astype(BF16)
                if k + 1 < DENSE_BLOCKS:
                    w2_copy(k + 1).start()

    half = TOK_TILE // 2
    for rows in (slice(0, half), slice(half, TOK_TILE)):
        o = jnp.where(i < P_TILES, oc_ref[rows, :], ol_ref[rows, :])
        m = (jnp.dot(o, wbf_ref[0:ATT_W, :], preferred_element_type=F32)
             + jnp.dot(cv_ref[rows, :], wbf_ref[ATT_W:, :], preferred_element_type=F32))
        x = jnp.where(i < P_TILES, xa_ref[rows, :], xb_ref[rows, :])
        xn = x + mod_ref[2:3, :] * _rms(m, g_ref[1:2, :])
        xo_ref[rows, :] = xn
        h2_ref[rows, :] = (_rms(xn, g_ref[2:3, :]) * (1.0 + mod_ref[4:5, :]) + mod_ref[3:4, :]).astype(BF16)

    if ffn == "routed":
        _route_tile(h2_ref[...], rw_ref, info_ref, info_t_ref, cb_ref, tot_ref, carry_ref, tri_ref)
    else:
        h = h2_ref[...]
        gt = jnp.dot(h, w13_ref[:, :D_FF], preferred_element_type=F32)
        up = jnp.dot(h, w13_ref[:, D_FF:], preferred_element_type=F32)
        a = (gt * _sigmoid(gt) * up).astype(BF16)
        y = jnp.dot(a, w2_ref[...], preferred_element_type=F32)
        xo_ref[...] = xo_ref[...] + mod_ref[5:6, :] * _rms(y, g_ref[3:4, :])


def _outproj(l, o_ctx, o_lat, cv, x, mod, norm_g, w_out, router_w=None, dense_w=None):
    x_specs, x_args = _x_specs(x)
    out_shape = [jax.ShapeDtypeStruct((N_TOK, D_MODEL), F32)]
    out_specs = [pl.BlockSpec((TOK_TILE, D_MODEL), lambda i: (i, 0))]
    scratch = [pltpu.VMEM((D_MODEL, D_MODEL), BF16)]
    dense_index = 0
    if dense_w is not None:
        w13, w2, dense_index = dense_w
        extra_specs = [pl.BlockSpec(memory_space=pl.ANY), pl.BlockSpec(memory_space=pl.ANY)]
        extra_args = (w13, w2)
        scratch += [pltpu.VMEM((TOK_TILE, D_MODEL), BF16),
                    pltpu.VMEM((D_MODEL, 2 * D_FF), BF16),
                    pltpu.VMEM((D_FF, D_MODEL), BF16),
                    pltpu.VMEM((D_MODEL, DENSE_COLS), F32),
                    pltpu.VMEM((DENSE_ROWS, D_MODEL), F32),
                    pltpu.SemaphoreType.DMA((2,))]
    else:
        rw_pad = jnp.zeros((D_MODEL, LANES), BF16).at[:, :N_EXPERTS].set(router_w.astype(BF16))
        extra_specs, extra_args = [pl.BlockSpec((D_MODEL, LANES), lambda i: (0, 0))], (rw_pad,)
        out_shape += [jax.ShapeDtypeStruct((N_TOK, D_MODEL), BF16)]
        out_specs += [pl.BlockSpec((TOK_TILE, D_MODEL), lambda i: (i, 0))]
        out_shape += [jax.ShapeDtypeStruct((N_TOK, LANES), F32),
                      jax.ShapeDtypeStruct((N_TILES, 8, TOK_TILE), F32),
                      jax.ShapeDtypeStruct((N_TILES, 8, LANES), F32),
                      jax.ShapeDtypeStruct((8, LANES), F32)]
        out_specs += [pl.BlockSpec((TOK_TILE, LANES), lambda i: (i, 0)),
                      pl.BlockSpec((None, 8, TOK_TILE), lambda i: (i, 0, 0)),
                      pl.BlockSpec((None, 8, LANES), lambda i: (i, 0, 0)),
                      pl.BlockSpec((8, LANES), lambda i: (0, 0))]
        scratch += [pltpu.VMEM((8, LANES), F32), pltpu.VMEM((TOK_TILE, TOK_TILE), BF16)]
    return pl.pallas_call(
        functools.partial(_outproj_kernel, ffn="dense" if dense_w is not None else "routed",
                          dense_index=dense_index),
        out_shape=tuple(out_shape),
        grid=(N_TILES,),
        in_specs=[pl.BlockSpec((TOK_TILE, ATT_W), lambda i: (jnp.minimum(i, P_TILES - 1), 0)),
                  pl.BlockSpec((TOK_TILE, ATT_W), lambda i: (jnp.maximum(i - P_TILES, 0), 0)),
                  pl.BlockSpec((TOK_TILE, CONV_CH), lambda i: (i, 0))] + x_specs + [
                  pl.BlockSpec((None, None, 6, D_MODEL), lambda i: (l, _cond_id(i), 0, 0)),
                  pl.BlockSpec((None, 4, D_MODEL), lambda i: (l, 0, 0)),
                  pl.BlockSpec((None, D_MODEL, D_MODEL), lambda i: (l, 0, 0),
                               pipeline_mode=pl.Buffered(1))] + extra_specs,
        out_specs=tuple(out_specs),
        scratch_shapes=scratch,
        compiler_params=_cparams(1),
        name=f"outproj_{l}",
    )(o_ctx, o_lat, cv, *x_args, mod, norm_g, w_out, *extra_args)


def _lower_triangle():
    r = lax.broadcasted_iota(jnp.int32, (TOK_TILE, TOK_TILE), 0)
    c = lax.broadcasted_iota(jnp.int32, (TOK_TILE, TOK_TILE), 1)
    return jnp.where(c <= r, 1.0, 0.0).astype(BF16)


def _route_tile(h, rw_ref, info_ref, info_t_ref, cb_ref, tot_ref, carry_ref, tri_ref):
    lane = lax.broadcasted_iota(jnp.int32, (TOK_TILE, LANES), 1)
    lanef = lane.astype(F32)
    logits = jnp.dot(h, rw_ref[...], preferred_element_type=F32)
    logits = jnp.where(lane < N_EXPERTS, logits, -jnp.inf)
    big = jnp.asarray(LANES, F32)
    m1 = jnp.max(logits, axis=-1, keepdims=True)
    e1 = jnp.min(jnp.where(logits == m1, lanef, big), axis=-1, keepdims=True)
    oh1 = lanef == e1
    rest = jnp.where(oh1, -jnp.inf, logits)
    m2 = jnp.max(rest, axis=-1, keepdims=True)
    e2 = jnp.min(jnp.where(rest == m2, lanef, big), axis=-1, keepdims=True)
    oh2 = lanef == e2
    ex = jnp.exp(m2 - m1)
    g1 = 1.0 / (1.0 + ex)
    g2 = ex / (1.0 + ex)

    oh = jnp.where(jnp.logical_or(oh1, oh2), 1.0, 0.0)
    incl = jnp.dot(tri_ref[...], oh.astype(BF16), preferred_element_type=F32)
    carry = carry_ref[0:1, :]
    excl = incl - oh + carry
    rank1 = jnp.sum(jnp.where(oh1, excl, 0.0), axis=-1, keepdims=True)
    rank2 = jnp.sum(jnp.where(oh2, excl, 0.0), axis=-1, keepdims=True)

    info = jnp.where(lane == 0, e1, 0.0)
    for k, col in enumerate((e2, rank1, rank2, g1, g2), start=1):
        info = jnp.where(lane == k, col, info)
    info_ref[...] = info
    info_t_ref[...] = info.T[0:8, :]

    cb_ref[...] = carry_ref[...]
    new_carry = carry + incl[TOK_TILE - 1:TOK_TILE, :]
    carry_ref[...] = jnp.broadcast_to(new_carry, carry_ref.shape)
    tot_ref[...] = jnp.broadcast_to(new_carry, tot_ref.shape)


def _sorted_pos(expert, rank, base_ref):
    start = jnp.zeros_like(rank)
    for e in range(N_EXPERTS):
        start = jnp.where(expert == float(e), base_ref[e].astype(F32), start)
    return start + rank


def _dispatch_kernel(clo_ref, cn_ref, base_ref, info_t_ref, x_ref, o_ref):
    r = pl.program_id(0)
    o_ref[...] = jnp.zeros_like(o_ref)
    rows = (r * DISPATCH_TILE + lax.broadcasted_iota(jnp.int32, (DISPATCH_TILE, TOK_TILE), 0)).astype(F32)

    def body(k, carry):
        c = clo_ref[r] + k
        it = info_t_ref[c]
        pos1 = _sorted_pos(it[0:1, :], it[2:3, :], base_ref)
        pos2 = _sorted_pos(it[1:2, :], it[3:4, :], base_ref)
        hit = jnp.logical_or(rows == pos1, rows == pos2)
        sel = jnp.where(hit, 1.0, 0.0).astype(BF16)
        off = pl.multiple_of(c * TOK_TILE, TOK_TILE)
        o_ref[...] += jnp.dot(sel, x_ref[pl.ds(off, TOK_TILE), :], preferred_element_type=F32).astype(BF16)
        return carry

    lax.fori_loop(0, cn_ref[r], body, 0)


def _dispatch(c_lo, c_n, base, info_t, h2):
    return pl.pallas_call(
        _dispatch_kernel,
        out_shape=jax.ShapeDtypeStruct((SORT_ROWS, D_MODEL), BF16),
        grid_spec=pltpu.PrefetchScalarGridSpec(
            num_scalar_prefetch=3,
            grid=(SORT_ROWS // DISPATCH_TILE,),
            in_specs=[pl.BlockSpec((N_TILES, 8, TOK_TILE), lambda r, *_: (0, 0, 0)),
                      pl.BlockSpec((N_TOK, D_MODEL), lambda r, *_: (0, 0),
                                   pipeline_mode=pl.Buffered(1))],
            out_specs=pl.BlockSpec((DISPATCH_TILE, D_MODEL), lambda r, *_: (r, 0))),
        compiler_params=_cparams(1),
        name="moe_dispatch",
    )(c_lo, c_n, base, info_t, h2)


TILE_UNUSED, TILE_HALF, TILE_FULL = 0, 1, 2
CAST_ROWS = 64


def _expert_weight_ring(c, r, n_c, te_ref, tf_ref, nx_ref, copies, cast):
    @pl.when(jnp.logical_and(c == 0, r == 0))
    def _():
        for cp in copies(te_ref[0], 0):
            cp.start()

    @pl.when(tf_ref[r] == 1)
    def _():
        for cp in copies(te_ref[r], c):
            cp.wait()
        cast()
        nr = nx_ref[r]
        nc = c + (nr <= r).astype(jnp.int32)

        @pl.when(nc < n_c)
        def _():
            for cp in copies(te_ref[nr], nc):
                cp.start()


def _moe_up_kernel(te_ref, tv_ref, tf_ref, nx_ref, x_ref, w_hbm, h_ref, wst_ref, wbf_ref, sem_ref, *, i_moe):
    f = pl.program_id(0)
    r = pl.program_id(1)

    def copies(e, ft):
        out = []
        for part in range(2):
            col = pl.multiple_of((part * UP_TILES + ft) * UP_TILE, LANES)
            out.append(pltpu.make_async_copy(w_hbm.at[i_moe, e, :, pl.ds(col, UP_TILE)],
                                             wst_ref.at[part], sem_ref.at[part]))
        return out

    def cast():
        def rows_block(b, carry):
            rows = pl.ds(pl.multiple_of(b * CAST_ROWS, CAST_ROWS), CAST_ROWS)
            wbf_ref[:, rows, :] = wst_ref[:, rows, :].astype(BF16)
            return carry

        lax.fori_loop(0, D_MODEL // CAST_ROWS, rows_block, 0)

    _expert_weight_ring(f, r, UP_TILES, te_ref, tf_ref, nx_ref, copies, cast)

    def hidden(x):
        gt = jnp.dot(x, wbf_ref[0], preferred_element_type=F32)
        up = jnp.dot(x, wbf_ref[1], preferred_element_type=F32)
        return (gt * _sigmoid(gt) * up).astype(BF16)

    half = ROW_TILE // 2

    @pl.when(tv_ref[r] == TILE_FULL)
    def _():
        h_ref[...] = hidden(x_ref[...])

    @pl.when(tv_ref[r] == TILE_HALF)
    def _():
        h_ref[0:half, :] = hidden(x_ref[0:half, :])
        h_ref[half:, :] = jnp.zeros((half, UP_TILE), BF16)

    @pl.when(tv_ref[r] == TILE_UNUSED)
    def _():
        h_ref[...] = jnp.zeros_like(h_ref)


def _moe_up(i_moe, tile_e, tile_valid, tile_first, tile_next, xs, w13):
    return pl.pallas_call(
        functools.partial(_moe_up_kernel, i_moe=i_moe),
        out_shape=jax.ShapeDtypeStruct((SORT_ROWS, D_FF_EXPERT), BF16),
        grid_spec=pltpu.PrefetchScalarGridSpec(
            num_scalar_prefetch=4,
            grid=(UP_TILES, SORT_TILES),
            in_specs=[pl.BlockSpec((ROW_TILE, D_MODEL), lambda f, r, *_: (r, 0)),
                      pl.BlockSpec(memory_space=pl.ANY)],
            out_specs=pl.BlockSpec((ROW_TILE, UP_TILE), lambda f, r, *_: (r, f)),
            scratch_shapes=[pltpu.VMEM((2, D_MODEL, UP_TILE), F32),
                            pltpu.VMEM((2, D_MODEL, UP_TILE), BF16),
                            pltpu.SemaphoreType.DMA((2,))]),
        compiler_params=_cparams(2),
        name="moe_up",
    )(tile_e, tile_valid, tile_first, tile_next, xs, w13)


def _moe_down_kernel(te_ref, tv_ref, tf_ref, nx_ref, h_ref, w_hbm, y_ref, wst_ref, wbf_ref, sem_ref, *, i_moe):
    n = pl.program_id(0)
    r = pl.program_id(1)

    def copies(e, nt):
        col = pl.multiple_of(nt * DOWN_TILE, LANES)
        return [pltpu.make_async_copy(w_hbm.at[i_moe, e, :, pl.ds(col, DOWN_TILE)], wst_ref, sem_ref.at[0])]

    def cast():
        wbf_ref[...] = wst_ref[...].astype(BF16)

    _expert_weight_ring(n, r, D_MODEL // DOWN_TILE, te_ref, tf_ref, nx_ref, copies, cast)

    half = ROW_TILE // 2

    @pl.when(tv_ref[r] == TILE_FULL)
    def _():
        y_ref[...] = jnp.dot(h_ref[...], wbf_ref[...], preferred_element_type=F32).astype(BF16)

    @pl.when(tv_ref[r] == TILE_HALF)
    def _():
        y_ref[0:half, :] = jnp.dot(h_ref[0:half, :], wbf_ref[...], preferred_element_type=F32).astype(BF16)
        y_ref[half:, :] = jnp.zeros((half, DOWN_TILE), BF16)

    @pl.when(tv_ref[r] == TILE_UNUSED)
    def _():
        y_ref[...] = jnp.zeros_like(y_ref)


def _moe_down(i_moe, tile_e, tile_valid, tile_first, tile_next, hs, w2):
    return pl.pallas_call(
        functools.partial(_moe_down_kernel, i_moe=i_moe),
        out_shape=jax.ShapeDtypeStruct((SORT_ROWS, D_MODEL), BF16),
        grid_spec=pltpu.PrefetchScalarGridSpec(
            num_scalar_prefetch=4,
            grid=(D_MODEL // DOWN_TILE, SORT_TILES),
            in_specs=[pl.BlockSpec((ROW_TILE, D_FF_EXPERT), lambda n, r, *_: (r, 0)),
                      pl.BlockSpec(memory_space=pl.ANY)],
            out_specs=pl.BlockSpec((ROW_TILE, DOWN_TILE), lambda n, r, *_: (r, n)),
            scratch_shapes=[pltpu.VMEM((D_FF_EXPERT, DOWN_TILE), F32),
                            pltpu.VMEM((D_FF_EXPERT, DOWN_TILE), BF16),
                            pltpu.SemaphoreType.DMA((1,))]),
        compiler_params=_cparams(2),
        name="moe_down",
    )(tile_e, tile_valid, tile_first, tile_next, hs, w2)


def _window_copy(y_hbm, ybuf_ref, sem_ref, src, buf, slot):
    return pltpu.make_async_copy(y_hbm.at[pl.ds(src, WIN), :],
                                 ybuf_ref.at[buf, pl.ds(slot * WIN, WIN), :], sem_ref.at[buf, slot])


def _start_windows(src_ref, y_hbm, ybuf_ref, sem_ref, tile, buf):
    for s in range(N_WIN):
        src = pl.multiple_of(src_ref[tile * N_WIN + s], BF16_SUBLANES)
        _window_copy(y_hbm, ybuf_ref, sem_ref, src, buf, s).start()


def _combine_kernel(src_ref, lo_ref, hi_ref, base_ref, info_ref, x_ref, mod_ref, g_ref, y_hbm,
                    yp_ref, ys_ref, ybuf_ref, sem_ref):
    j = pl.program_id(0)
    buf = j % 2

    @pl.when(j == 0)
    def _():
        _start_windows(src_ref, y_hbm, ybuf_ref, sem_ref, 0, 0)

    @pl.when(j + 1 < N_TILES)
    def _():
        _start_windows(src_ref, y_hbm, ybuf_ref, sem_ref, j + 1, 1 - buf)

    info = info_ref[...]
    pos1 = _sorted_pos(info[:, 0:1], info[:, 2:3], base_ref)
    pos2 = _sorted_pos(info[:, 1:2], info[:, 3:4], base_ref)
    g1, g2 = info[:, 4:5], info[:, 5:6]
    iota = lax.broadcasted_iota(jnp.int32, (1, WIN), 1)
    ids = []
    for s in range(N_WIN):
        row = src_ref[j * N_WIN + s] + iota
        ok = jnp.logical_and(row >= lo_ref[j * N_WIN + s], row < hi_ref[j * N_WIN + s])
        ids.append(jnp.where(ok, row, -1))
    row_id = jnp.concatenate(ids, axis=1).astype(F32)

    for s in range(N_WIN):
        _window_copy(y_hbm, ybuf_ref, sem_ref, 0, buf, s).wait()
    halves = []
    for rows in (slice(0, TOK_TILE // 2), slice(TOK_TILE // 2, TOK_TILE)):
        sel = (jnp.where(pos1[rows] == row_id, g1[rows], 0.0)
               + jnp.where(pos2[rows] == row_id, g2[rows], 0.0)).astype(BF16)
        ffn = jnp.dot(sel, ybuf_ref[buf], preferred_element_type=F32)
        halves.append(x_ref[rows, :] + mod_ref[5:6, :] * _rms(ffn, g_ref[3:4, :]))
    out = jnp.concatenate(halves, axis=0)

    @pl.when(j < P_TILES)
    def _():
        yp_ref[...] = out

    @pl.when(j >= P_TILES)
    def _():
        ys_ref[...] = out


def _combine(l, win_src, win_lo, win_hi, base, info, x, mod, norm_g, ys):
    return pl.pallas_call(
        _combine_kernel,
        out_shape=(jax.ShapeDtypeStruct((N_PROMPT, D_MODEL), F32),
                   jax.ShapeDtypeStruct((N_SAMPLE, D_MODEL), F32)),
        grid_spec=pltpu.PrefetchScalarGridSpec(
            num_scalar_prefetch=4,
            grid=(N_TILES,),
            in_specs=[pl.BlockSpec((TOK_TILE, LANES), lambda i, *_: (i, 0)),
                      pl.BlockSpec((TOK_TILE, D_MODEL), lambda i, *_: (i, 0)),
                      pl.BlockSpec((None, None, 6, D_MODEL), lambda i, *_: (l, _cond_id(i), 0, 0)),
                      pl.BlockSpec((None, 4, D_MODEL), lambda i, *_: (l, 0, 0)),
                      pl.BlockSpec(memory_space=pl.ANY)],
            out_specs=(pl.BlockSpec((TOK_TILE, D_MODEL), lambda i, *_: (jnp.minimum(i, P_TILES - 1), 0)),
                       pl.BlockSpec((TOK_TILE, D_MODEL), lambda i, *_: (jnp.maximum(i - P_TILES, 0), 0))),
            scratch_shapes=[pltpu.VMEM((2, N_WIN * WIN, D_MODEL), BF16),
                            pltpu.SemaphoreType.DMA((2, N_WIN))]),
        compiler_params=_cparams(1),
        name="moe_combine",
    )(win_src, win_lo, win_hi, base, info, x, mod, norm_g, ys)


def _moe_ffn(l, i_moe, h2, x, mod, norm_g, routing, moe_w13, moe_w2):
    info, info_t, cb, tot = routing

    ar8 = jnp.arange(N_EXPERTS, dtype=jnp.int32)

    def at8(vec, idx):
        return jnp.sum(jnp.where(idx[..., None] == ar8, vec, 0), axis=-1)

    def cumsum8(a):
        return jnp.sum(jnp.where(ar8[None, :] <= ar8[:, None], a[..., None, :], 0), axis=-1)

    counts = tot[0, :N_EXPERTS].astype(jnp.int32)
    padded = (counts + ROW_TILE - 1) // ROW_TILE * ROW_TILE
    seg_end = cumsum8(padded)
    base = (seg_end - padded).astype(jnp.int32)
    cbx = cb[:, 0, :N_EXPERTS].astype(jnp.int32)
    cb_end = jnp.concatenate([cbx[1:], counts[None, :]], axis=0)

    def tile_segments(n_tiles, rows):
        row0 = jnp.arange(n_tiles, dtype=jnp.int32) * rows
        e = jnp.minimum(jnp.sum(seg_end[None, :] <= row0[:, None], axis=1), N_EXPERTS - 1).astype(jnp.int32)
        k0 = row0 - at8(base, e)
        return e, k0, jnp.logical_and(k0 >= 0, k0 < at8(counts, e))

    tile_e, k0, tile_valid = tile_segments(SORT_TILES, ROW_TILE)
    tile_first = jnp.logical_and(tile_valid, k0 == 0)
    d_e, d_k0, d_valid = tile_segments(SORT_ROWS // DISPATCH_TILE, DISPATCH_TILE)
    d_kend = jnp.minimum(d_k0 + DISPATCH_TILE, at8(counts, d_e))
    cb_t = jnp.sum(jnp.where(d_e[:, None, None] == ar8[None, :, None], cbx.T[None, :, :], 0), axis=1)
    c_lo = jnp.sum(cb_t <= d_k0[:, None], axis=1) - 1
    c_hi = jnp.sum(cb_t < d_kend[:, None], axis=1) - 1
    c_n = jnp.where(d_valid, c_hi - c_lo + 1, 0).astype(jnp.int32)
    c_lo = jnp.where(d_valid, c_lo, 0).astype(jnp.int32)
    tile_idx = jnp.arange(SORT_TILES, dtype=jnp.int32)
    first_idx = jnp.where(tile_first, tile_idx, SORT_TILES)
    later = jnp.where(tile_idx[None, :] > tile_idx[:, None], first_idx[None, :], SORT_TILES)
    tile_next = jnp.min(later, axis=1)
    tile_next = jnp.where(tile_next >= SORT_TILES, 0, tile_next).astype(jnp.int32)
    tile_rows = at8(counts, tile_e) - k0
    tile_valid = jnp.where(tile_valid, jnp.where(tile_rows <= ROW_TILE // 2, TILE_HALF, TILE_FULL),
                           TILE_UNUSED).astype(jnp.int32)
    tile_first = tile_first.astype(jnp.int32)

    seg_lo = base[None, :] + cbx
    seg_hi = base[None, :] + cb_end
    seg_n = seg_hi - seg_lo
    w0 = seg_lo // BF16_SUBLANES * BF16_SUBLANES
    nw = jnp.where(seg_n > 0, (seg_lo - w0 + seg_n + WIN - 1) // WIN, 0)
    nw_end = cumsum8(nw)
    slot = jnp.arange(N_WIN, dtype=jnp.int32)
    slot_e = jnp.minimum(jnp.sum(nw_end[:, None, :] <= slot[None, :, None], axis=2), N_EXPERTS - 1)
    pick = slot_e[:, :, None] == jnp.arange(N_EXPERTS, dtype=jnp.int32)[None, None, :]
    take = lambda a: jnp.sum(jnp.where(pick, a[:, None, :], 0), axis=2)
    slot_k = slot[None, :] - (take(nw_end) - take(nw))
    slot_ok = slot[None, :] < nw_end[:, -1:]
    slot_src = take(w0) + WIN * slot_k
    win_src = jnp.where(slot_ok, slot_src, 0).astype(jnp.int32).reshape(-1)
    win_lo = jnp.where(slot_ok, take(seg_lo), 0).astype(jnp.int32).reshape(-1)
    win_hi = jnp.where(slot_ok, take(seg_hi), 0).astype(jnp.int32).reshape(-1)

    xs = _dispatch(c_lo, c_n, base, info_t, h2)
    hs = _moe_up(i_moe, tile_e, tile_valid, tile_first, tile_next, xs, moe_w13)
    ys = _moe_down(i_moe, tile_e, tile_valid, tile_first, tile_next, hs, moe_w2)
    return _combine(l, win_src, win_lo, win_hi, base, info, x, mod, norm_g, ys)


def kernel(x_prompt, x_sample, cache_k, cache_v, c, c_ctx, w_ada, b_ada, norm_g, w_in, w_out,
           lam_params, subln_g, dw_weight, dw_bias, conv_ln_g, conv_ln_b, dense_w13, dense_w2,
           router_w, moe_w13, moe_w2):
    x = (x_prompt.reshape(N_PROMPT, D_MODEL), x_sample.reshape(N_SAMPLE, D_MODEL))
    cond8 = jnp.zeros((8, D_MODEL), F32).at[0].set(c_ctx).at[1:1 + DEC_BATCH].set(c)
    mod = _ada_modulation(cond8, w_ada, b_ada).reshape(DEPTH, 8, 6, D_MODEL)
    rope = _rope_tables()
    ck = cache_k.reshape(DEC_BATCH, DEPTH, PAST_LEN * ATT_HEADS, V_DIM)
    cv = cache_v.reshape(DEC_BATCH, DEPTH, PAST_LEN * ATT_HEADS, V_DIM)

    caches = None
    for l in range(DEPTH):
        lam_init = 0.8 - 0.6 * math.exp(-0.3 * l)
        qkv, cvo, *caches = _mixer_in(l, x, mod, norm_g, w_in, rope,
                                      (dw_weight, dw_bias, conv_ln_g, conv_ln_b), caches)
        o_ctx, o_lat = _attention(l, lam_init, lam_params, subln_g, qkv, ck, cv)
        i = l // 2
        if l % 2 == 0:
            (x,) = _outproj(l, o_ctx, o_lat, cvo, x, mod, norm_g, w_out,
                            dense_w=(dense_w13, dense_w2, i))
        else:
            x, h2, *routing = _outproj(l, o_ctx, o_lat, cvo, x, mod, norm_g, w_out, router_w=router_w[i])
            x = _moe_ffn(l, i, h2, x, mod, norm_g, routing, moe_w13, moe_w2)

    xp, xs = x if isinstance(x, tuple) else (x[:N_PROMPT], x[N_PROMPT:])
    new_k, new_v = (a.reshape(BATCH, DEPTH, SEQ, ATT_HEADS, V_DIM) for a in caches)
    return (xp.reshape(BATCH, SEQ, D_MODEL), xs.reshape(DEC_BATCH, DEC_SEQ, D_MODEL), new_k, new_v)
```

```python
import functools
import math

import jax
import jax.numpy as jnp
import numpy as np
from jax import lax
from jax.experimental import pallas as pl
from jax.experimental.pallas import tpu as pltpu

F32 = jnp.float32
BF16 = jnp.bfloat16

D_MODEL = 1024
BATCH = 32
SEQ = 256
DEPTH = 2
DEC_BATCH = 2
DEC_SEQ = 2048
PAST_LEN = 512
GRID_W = 64
ATT_HEADS = 4
QK_DIM = 64
V_DIM = 128
ATT_W = ATT_HEADS * V_DIM
IN_W = 5 * ATT_W
CONV_CH = 512
CONV_WIDTH = 31
D_FF = 2816
N_EXPERTS = 8
D_FF_EXPERT = 3584
ROPE_THETA = 10000.0
NORM_EPS = 1e-6
LN_EPS = 1e-5

N_PROMPT = BATCH * SEQ
N_SAMPLE = DEC_BATCH * DEC_SEQ
N_TOK = N_PROMPT + N_SAMPLE

LANES = 128
BF16_SUBLANES = 16
MXU_TILE = 256
VMEM_BYTES = 64 * 1024 * 1024
VMEM_LIMIT = VMEM_BYTES - 8 * 1024 * 1024

TOK_TILE = 512
N_TILES = N_TOK // TOK_TILE
P_TILES = N_PROMPT // TOK_TILE
S_TILES_PER_BATCH = DEC_SEQ // TOK_TILE

ROW_TILE = 512
SORT_TILES = (2 * N_TOK + N_EXPERTS * ROW_TILE) // ROW_TILE + 1
SORT_ROWS = SORT_TILES * ROW_TILE
DISPATCH_TILE = 256
UP_TILE = 7 * MXU_TILE
UP_TILES = D_FF_EXPERT // UP_TILE
DOWN_TILE = 4 * MXU_TILE
WIN = 128
N_WIN = (2 * TOK_TILE + N_EXPERTS * (BF16_SUBLANES - 1 + WIN - 1)) // WIN

assert D_FF_EXPERT % UP_TILE == 0 and D_MODEL % DOWN_TILE == 0 and D_FF % MXU_TILE == 0
assert ROW_TILE % DISPATCH_TILE == 0 and N_TOK % TOK_TILE == 0 and N_PROMPT % TOK_TILE == 0
assert DEPTH == 2, "layer 0 zero-fills exactly one later cache slab; the last layer must be the routed one"


def _cparams(n_axes):
    return pltpu.CompilerParams(dimension_semantics=("arbitrary",) * n_axes,
                                vmem_limit_bytes=VMEM_LIMIT)


def _cond_id(i):
    return jnp.where(i < P_TILES, 0, 1 + (i - P_TILES) // S_TILES_PER_BATCH)


def _sigmoid(x):
    return 1.0 / (1.0 + jnp.exp(-x))


def _rms(x, g):
    return x * lax.rsqrt(jnp.mean(x * x, axis=-1, keepdims=True) + NORM_EPS) * g


def _ada_kernel(c_ref, w_ref, b_ref, o_ref):
    c = c_ref[...]
    s = (c * _sigmoid(c)).astype(BF16)
    o_ref[...] = jnp.dot(s, w_ref[...].astype(BF16), preferred_element_type=F32) + b_ref[...]


def _ada_modulation(cond8, w_ada, b_ada):
    tn = 1536
    n = 6 * D_MODEL
    return pl.pallas_call(
        _ada_kernel,
        out_shape=jax.ShapeDtypeStruct((DEPTH, 8, n), F32),
        grid=(DEPTH, n // tn),
        in_specs=[pl.BlockSpec((8, D_MODEL), lambda l, j: (0, 0)),
                  pl.BlockSpec((None, D_MODEL, tn), lambda l, j: (l, 0, j)),
                  pl.BlockSpec((None, 1, tn), lambda l, j: (l, 0, j))],
        out_specs=pl.BlockSpec((None, 8, tn), lambda l, j: (l, 0, j)),
        compiler_params=_cparams(2),
        name="ada_modulation",
    )(cond8, w_ada, b_ada.reshape(DEPTH, 1, n))


def _tile_x(xa_ref, xb_ref):
    return jnp.where(pl.program_id(0) < P_TILES, xa_ref[...], xb_ref[...])


def _x_specs(x):
    last = N_TILES - 1
    if isinstance(x, tuple):
        xa, xb = x
        b_idx = lambda i, *_: (jnp.maximum(jnp.minimum(i, last) - P_TILES, 0), 0)
    else:
        xa = xb = x
        b_idx = lambda i, *_: (jnp.maximum(jnp.minimum(i, last), P_TILES), 0)
    a_idx = lambda i, *_: (jnp.minimum(i, P_TILES - 1), 0)
    return [pl.BlockSpec((TOK_TILE, D_MODEL), a_idx), pl.BlockSpec((TOK_TILE, D_MODEL), b_idx)], (xa, xb)


ROT_SPAN = QK_DIM // 4
SEQ_PER_TILE = TOK_TILE // SEQ
CACHE_ROWS = SEQ * ATT_HEADS


CONV_LAG = 2
CONV_RING = 3
CONV_TILE = 256
CONV_HALO = 16
CONV_SUB = 32


def _conv_pass(upad_ref, w_ref, bias_ref, lg_ref, lb_ref, o_ref, row0):
    rows = CONV_TILE + 2 * CONV_HALO - 8
    for s in range(1, 8):
        upad_ref[s, 0:rows, :] = upad_ref[0, s:s + rows, :]
    first_tap = CONV_HALO - CONV_WIDTH // 2
    groups = CONV_SUB // 8
    for t in range(CONV_TILE // CONV_SUB):
        base = t * CONV_SUB
        acc = jnp.zeros((groups, 8, CONV_CH), F32)
        for j in range(CONV_WIDTH):
            lo = base + (first_tap + j) // 8 * 8
            taps = upad_ref[(first_tap + j) % 8, lo:lo + CONV_SUB, :].reshape(groups, 8, CONV_CH)
            acc = acc + taps * w_ref[j][None]
        y = acc.reshape(CONV_SUB, CONV_CH) + bias_ref[...]
        mu = jnp.mean(y, axis=-1, keepdims=True)
        yc = y - mu
        var = jnp.mean(yc * yc, axis=-1, keepdims=True)
        z = yc * lax.rsqrt(var + LN_EPS) * lg_ref[...] + lb_ref[...]
        o_ref[row0 + base:row0 + base + CONV_SUB, :] = (z * _sigmoid(z)).astype(BF16)


def _mixer_in_kernel(*refs, layer):
    n_in = 12 + (2 if layer else 0)
    (xa_ref, xb_ref, mod_ref, g_ref, w_ref, cos_ref, sina_ref, sinb_ref,
     dw_ref, db_ref, lg_ref, lb_ref) = refs[:12]
    qkv_ref, cv_ref, kc_ref, vc_ref, wbf_ref, ring_ref, glu_ref, pj_ref, upad_ref, taps_ref = refs[n_in:]
    i = pl.program_id(0)
    t = jnp.minimum(i, N_TILES - 1)

    @pl.when(i == 0)
    def _():
        wbf_ref[...] = w_ref[...].astype(BF16)
        ring_ref[...] = jnp.zeros_like(ring_ref)
        glu_ref[...] = jnp.zeros_like(glu_ref)
        for j in range(CONV_WIDTH):
            taps_ref[j] = jnp.broadcast_to(dw_ref[j:j + 1, :], (8, CONV_CH))

    ring_ref[(i - 1) % CONV_RING] = glu_ref[...]

    h = _rms(_tile_x(xa_ref, xb_ref), g_ref[0:1, :]) * (1.0 + mod_ref[1:2, :]) + mod_ref[0:1, :]
    hb = h.astype(BF16)

    def proj(part):
        return jnp.dot(hb, wbf_ref[:, part * ATT_W:(part + 1) * ATT_W], preferred_element_type=F32)

    for part in range(3):
        pj_ref[:, part * ATT_W:(part + 1) * ATT_W] = proj(part)
    glu_ref[...] = proj(3) * _sigmoid(proj(4))

    j = i - CONV_LAG
    latent = j >= P_TILES
    q = (j - P_TILES) % S_TILES_PER_BATCH
    has_prev = jnp.logical_and(latent, q != 0)
    has_next = jnp.logical_and(latent, q != S_TILES_PER_BATCH - 1)
    cur = j % CONV_RING
    prev_tail = ring_ref[(j - 1) % CONV_RING, TOK_TILE - CONV_HALO:, :]
    next_head = ring_ref[(j + 1) % CONV_RING, 0:CONV_HALO, :]
    for s in range(TOK_TILE // CONV_TILE):
        lo = s * CONV_TILE
        if s == 0:
            before = jnp.where(has_prev, prev_tail, 0.0)
        else:
            before = jnp.where(latent, ring_ref[cur, lo - CONV_HALO:lo, :], 0.0)
        if s == TOK_TILE // CONV_TILE - 1:
            after = jnp.where(has_next, next_head, 0.0)
        else:
            after = jnp.where(latent, ring_ref[cur, lo + CONV_TILE:lo + CONV_TILE + CONV_HALO, :], 0.0)
        upad_ref[0, 0:CONV_HALO, :] = before
        upad_ref[0, CONV_HALO:CONV_HALO + CONV_TILE, :] = ring_ref[cur, lo:lo + CONV_TILE, :]
        upad_ref[0, CONV_HALO + CONV_TILE:, :] = after
        _conv_pass(upad_ref, taps_ref, db_ref, lg_ref, lb_ref, cv_ref, lo)

    qkv_ref[:, 2 * ATT_W:] = pj_ref[:, 2 * ATT_W:].astype(BF16)

    @pl.when(t < P_TILES)
    def _():
        qkv_ref[:, :2 * ATT_W] = pj_ref[:, :2 * ATT_W].astype(BF16)
        for ref, col0 in ((kc_ref, ATT_W), (vc_ref, 2 * ATT_W)):
            for s in range(SEQ_PER_TILE):
                for hd in range(ATT_HEADS):
                    val = pj_ref[SEQ * s:SEQ * (s + 1), col0 + V_DIM * hd:col0 + V_DIM * (hd + 1)]
                    rows = pl.ds(hd, SEQ, stride=ATT_HEADS)
                    if layer == 0:
                        ref[s, 0, rows, :] = val
                    else:
                        ref[s, rows, :] = val
            if layer == 0:
                ref[:, 1:] = jnp.zeros((SEQ_PER_TILE, DEPTH - 1, CACHE_ROWS, V_DIM), F32)

    @pl.when(t >= P_TILES)
    def _():
        cos = cos_ref[...]
        sina = sina_ref[...]
        sinb = sinb_ref[...]
        for c in range(2 * ATT_W // V_DIM):
            xg = pj_ref[:, V_DIM * c:V_DIM * (c + 1)]
            fwd = pltpu.roll(xg, V_DIM - ROT_SPAN, 1)
            bwd = pltpu.roll(xg, ROT_SPAN, 1)
            qkv_ref[:, V_DIM * c:V_DIM * (c + 1)] = (xg * cos + fwd * sina + bwd * sinb).astype(BF16)


def _mixer_in(l, x, mod, norm_g, w_in, rope, conv_params, caches):
    cos, sina, sinb = rope
    dw_w, dw_b, ln_g, ln_b = conv_params
    x_specs, x_args = _x_specs(x)

    def tile(i):
        return jnp.minimum(i, N_TILES - 1)

    def rope_idx(i):
        return (jnp.maximum(tile(i) - P_TILES, 0) % S_TILES_PER_BATCH, 0)

    def vec(a):
        return a.reshape(DEPTH, 1, CONV_CH)

    cache_shape = jax.ShapeDtypeStruct((BATCH, DEPTH, CACHE_ROWS, V_DIM), F32)
    if l == 0:
        cache_spec = pl.BlockSpec((SEQ_PER_TILE, DEPTH, CACHE_ROWS, V_DIM),
                                  lambda i: (jnp.minimum(i, P_TILES - 1), 0, 0, 0))
        extra_specs, extra_args, aliases = [], (), {}
    else:
        cache_spec = pl.BlockSpec((SEQ_PER_TILE, None, CACHE_ROWS, V_DIM),
                                  lambda i: (jnp.minimum(i, P_TILES - 1), l, 0, 0))
        extra_specs = [pl.BlockSpec(memory_space=pl.ANY)] * 2
        extra_args = tuple(caches)
        aliases = {12: 2, 13: 3}

    return pl.pallas_call(
        functools.partial(_mixer_in_kernel, layer=l),
        out_shape=(jax.ShapeDtypeStruct((N_TOK, 3 * ATT_W), BF16),
                   jax.ShapeDtypeStruct((N_TOK, CONV_CH), BF16), cache_shape, cache_shape),
        grid=(N_TILES + CONV_LAG,),
        in_specs=x_specs + [
            pl.BlockSpec((None, None, 6, D_MODEL), lambda i: (l, _cond_id(tile(i)), 0, 0)),
            pl.BlockSpec((None, 4, D_MODEL), lambda i: (l, 0, 0)),
            pl.BlockSpec((None, D_MODEL, IN_W), lambda i: (l, 0, 0), pipeline_mode=pl.Buffered(1)),
            pl.BlockSpec((TOK_TILE, V_DIM), rope_idx),
            pl.BlockSpec((TOK_TILE, V_DIM), rope_idx),
            pl.BlockSpec((TOK_TILE, V_DIM), rope_idx),
            pl.BlockSpec((None, CONV_WIDTH, CONV_CH), lambda i: (l, 0, 0)),
            pl.BlockSpec((None, 1, CONV_CH), lambda i: (l, 0, 0)),
            pl.BlockSpec((None, 1, CONV_CH), lambda i: (l, 0, 0)),
            pl.BlockSpec((None, 1, CONV_CH), lambda i: (l, 0, 0))] + extra_specs,
        out_specs=(pl.BlockSpec((TOK_TILE, 3 * ATT_W), lambda i: (tile(i), 0)),
                   pl.BlockSpec((TOK_TILE, CONV_CH), lambda i: (jnp.maximum(i - CONV_LAG, 0), 0)),
                   cache_spec, cache_spec),
        scratch_shapes=[pltpu.VMEM((D_MODEL, IN_W), BF16),
                        pltpu.VMEM((CONV_RING, TOK_TILE, CONV_CH), F32),
                        pltpu.VMEM((TOK_TILE, CONV_CH), F32),
                        pltpu.VMEM((TOK_TILE, 3 * ATT_W), F32),
                        pltpu.VMEM((8, CONV_TILE + 2 * CONV_HALO, CONV_CH), F32),
                        pltpu.VMEM((CONV_WIDTH, 8, CONV_CH), F32)],
        input_output_aliases=aliases,
        compiler_params=_cparams(1),
        name=f"mixer_in_{l}",
    )(*x_args, mod, norm_g, w_in, cos, sina, sinb, dw_w, vec(dw_b), vec(ln_g), vec(ln_b), *extra_args)


def _rope_tables():
    rows = DEC_SEQ // GRID_W
    row_pos = np.repeat(np.arange(rows, dtype=np.float64), GRID_W)
    col_pos = np.tile(np.arange(GRID_W, dtype=np.float64), rows)
    half = QK_DIM // 2
    inv_freq = 1.0 / (ROPE_THETA ** (np.arange(0, half, 2, dtype=np.float64) / half))
    ang_r = row_pos[:, None] * inv_freq
    ang_c = col_pos[:, None] * inv_freq
    ang = np.concatenate([ang_r, ang_r, ang_c, ang_c], axis=-1)
    cos = np.tile(np.cos(ang), (1, 2)).astype(np.float32)
    sin = np.tile(np.sin(ang), (1, 2)).astype(np.float32)
    first = (np.arange(V_DIM) % (2 * ROT_SPAN)) < ROT_SPAN
    sina = np.where(first[None, :], -sin, 0.0).astype(np.float32)
    sinb = np.where(first[None, :], 0.0, sin).astype(np.float32)
    return jnp.asarray(cos), jnp.asarray(sina), jnp.asarray(sinb)


def _attn_kernel(*refs, lam_init, has_ext):
    if has_ext:
        lamp_ref, sub_ref, q_ref, k_ref, v_ref, ke_ref, ve_ref, o_ref = refs
    else:
        lamp_ref, sub_ref, q_ref, k_ref, v_ref, o_ref = refs
    lp = lamp_ref[...]
    lam = (jnp.exp(jnp.sum(lp[0:1] * lp[1:2], axis=-1, keepdims=True))
           - jnp.exp(jnp.sum(lp[2:3] * lp[3:4], axis=-1, keepdims=True)) + lam_init)
    lane = lax.broadcasted_iota(jnp.int32, (1, V_DIM), 1)
    nt = (((1,), (1,)), ((), ()))
    scale = QK_DIM ** -0.5
    map_scale = [jnp.where(lane < QK_DIM, scale, 0.0).astype(BF16),
                 jnp.where(lane < QK_DIM, 0.0, scale).astype(BF16)]

    tq = q_ref.shape[0]
    head_cols = [slice(V_DIM * hd, V_DIM * (hd + 1)) for hd in range(ATT_HEADS)]

    def q_map(hd, m):
        return q_ref[:, head_cols[hd]] * map_scale[m]

    def with_ones(v):
        return jnp.concatenate([v, jnp.ones_like(v)], axis=1)

    outs = []
    if has_ext:
        for hd in range(ATT_HEADS):
            kh = k_ref[:, head_cols[hd]]
            v_aug = with_ones(v_ref[:, head_cols[hd]])
            head_rows = pl.ds(hd, PAST_LEN, stride=ATT_HEADS)
            keh = ke_ref[head_rows, :].astype(BF16)
            ve_aug = with_ones(ve_ref[head_rows, :].astype(BF16))
            ratio = []
            for m in range(2):
                qm = q_map(hd, m)
                s = lax.dot_general(qm, kh, nt, preferred_element_type=F32)
                se = lax.dot_general(qm, keh, nt, preferred_element_type=F32)
                mx = jnp.maximum(jnp.max(s, axis=-1, keepdims=True), jnp.max(se, axis=-1, keepdims=True))
                pv = (jnp.dot(jnp.exp((s - mx).astype(BF16)), v_aug, preferred_element_type=F32)
                      + jnp.dot(jnp.exp((se - mx).astype(BF16)), ve_aug, preferred_element_type=F32))
                ratio.append(pv[:, :V_DIM] / pv[:, V_DIM:])
            outs.append(ratio[0] - lam * ratio[1])
    else:
        seqs = [slice(SEQ * sq, SEQ * (sq + 1)) for sq in range(tq // SEQ)]
        s = jnp.concatenate(
            [lax.dot_general(q_ref[rows, head_cols[hd]] * map_scale[m], k_ref[rows, head_cols[hd]], nt,
                             preferred_element_type=F32)
             for rows in seqs for hd in range(ATT_HEADS) for m in range(2)], axis=0)
        p = jnp.exp((s - jnp.max(s, axis=-1, keepdims=True)).astype(BF16))
        for hd in range(ATT_HEADS):
            per_seq = []
            for sq, rows in enumerate(seqs):
                first = (sq * ATT_HEADS + hd) * 2 * SEQ
                pv = jnp.dot(p[first:first + 2 * SEQ], with_ones(v_ref[rows, head_cols[hd]]),
                             preferred_element_type=F32)
                ratio = pv[:, :V_DIM] / pv[:, V_DIM:]
                per_seq.append(ratio[:SEQ] - lam * ratio[SEQ:])
            outs.append(jnp.concatenate(per_seq, axis=0))

    heads = [(_rms(o, sub_ref[...]) * (1.0 - lam_init)).astype(BF16) for o in outs]
    o_ref[...] = jnp.concatenate(heads, axis=1)


CTX_SEQS = 4


def _attention(l, lam_init, lam_params, subln_g, qkvg, cache_k, cache_v):
    small = [pl.BlockSpec((None, 4, QK_DIM), lambda *_: (l, 0, 0)),
             pl.BlockSpec((None, 1, V_DIM), lambda *_: (l, 0, 0))]
    sub3 = subln_g.reshape(DEPTH, 1, V_DIM)

    o_ctx = pl.pallas_call(
        functools.partial(_attn_kernel, lam_init=lam_init, has_ext=False),
        out_shape=jax.ShapeDtypeStruct((N_PROMPT, ATT_W), BF16),
        grid=(BATCH // CTX_SEQS,),
        in_specs=small + [pl.BlockSpec((CTX_SEQS * SEQ, ATT_W), lambda b: (b, 0)),
                          pl.BlockSpec((CTX_SEQS * SEQ, ATT_W), lambda b: (b, 1)),
                          pl.BlockSpec((CTX_SEQS * SEQ, ATT_W), lambda b: (b, 2))],
        out_specs=pl.BlockSpec((CTX_SEQS * SEQ, ATT_W), lambda b: (b, 0)),
        compiler_params=_cparams(1),
        name=f"attn_ctx_{l}",
    )(lam_params, sub3, qkvg, qkvg, qkvg)

    tq = 512
    q_tiles = DEC_SEQ // tq
    q_base = N_PROMPT // tq
    kv_base = N_PROMPT // DEC_SEQ

    def q_idx(b, i):
        return (q_base + b * q_tiles + i, 0)

    o_lat = pl.pallas_call(
        functools.partial(_attn_kernel, lam_init=lam_init, has_ext=True),
        out_shape=jax.ShapeDtypeStruct((N_SAMPLE, ATT_W), BF16),
        grid=(DEC_BATCH, q_tiles),
        in_specs=small + [pl.BlockSpec((tq, ATT_W), q_idx),
                          pl.BlockSpec((DEC_SEQ, ATT_W), lambda b, i: (kv_base + b, 1)),
                          pl.BlockSpec((DEC_SEQ, ATT_W), lambda b, i: (kv_base + b, 2)),
                          pl.BlockSpec((None, None, PAST_LEN * ATT_HEADS, V_DIM), lambda b, i: (b, l, 0, 0)),
                          pl.BlockSpec((None, None, PAST_LEN * ATT_HEADS, V_DIM), lambda b, i: (b, l, 0, 0))],
        out_specs=pl.BlockSpec((tq, ATT_W), lambda b, i: (b * q_tiles + i, 0)),
        compiler_params=_cparams(2),
        name=f"attn_lat_{l}",
    )(lam_params, sub3, qkvg, qkvg, qkvg, cache_k, cache_v)
    return o_ctx, o_lat


def _outproj_kernel(*refs, ffn):
    if ffn == "routed":
        (oc_ref, ol_ref, cv_ref, xa_ref, xb_ref, mod_ref, g_ref, w_ref, rw_ref,
         xo_ref, h2_ref, info_ref, info_t_ref, cb_ref, tot_ref, wbf_ref, carry_ref, tri_ref) = refs
    else:
        (oc_ref, ol_ref, cv_ref, xa_ref, xb_ref, mod_ref, g_ref, w_ref, w13_ref, w2_ref,
         xo_ref, wbf_ref, h2_ref) = refs
    i = pl.program_id(0)

    @pl.when(i == 0)
    def _():
        wbf_ref[...] = w_ref[...].astype(BF16)
        if ffn == "routed":
            carry_ref[...] = jnp.zeros_like(carry_ref)
            tri_ref[...] = _lower_triangle()

    half = TOK_TILE // 2
    for rows in (slice(0, half), slice(half, TOK_TILE)):
        o = jnp.where(i < P_TILES, oc_ref[rows, :], ol_ref[rows, :])
        m = (jnp.dot(o, wbf_ref[0:ATT_W, :], preferred_element_type=F32)
             + jnp.dot(cv_ref[rows, :], wbf_ref[ATT_W:, :], preferred_element_type=F32))
        x = jnp.where(i < P_TILES, xa_ref[rows, :], xb_ref[rows, :])
        xn = x + mod_ref[2:3, :] * _rms(m, g_ref[1:2, :])
        xo_ref[rows, :] = xn
        h2_ref[rows, :] = (_rms(xn, g_ref[2:3, :]) * (1.0 + mod_ref[4:5, :]) + mod_ref[3:4, :]).astype(BF16)

    if ffn == "routed":
        _route_tile(h2_ref[...], rw_ref, info_ref, info_t_ref, cb_ref, tot_ref, carry_ref, tri_ref)
    else:
        h = h2_ref[...]
        gt = jnp.dot(h, w13_ref[:, :D_FF], preferred_element_type=F32)
        up = jnp.dot(h, w13_ref[:, D_FF:], preferred_element_type=F32)
        a = (gt * _sigmoid(gt) * up).astype(BF16)
        y = jnp.dot(a, w2_ref[...], preferred_element_type=F32)
        xo_ref[...] = xo_ref[...] + mod_ref[5:6, :] * _rms(y, g_ref[3:4, :])


def _outproj(l, o_ctx, o_lat, cv, x, mod, norm_g, w_out, router_w=None, dense_w=None):
    x_specs, x_args = _x_specs(x)
    out_shape = [jax.ShapeDtypeStruct((N_TOK, D_MODEL), F32)]
    out_specs = [pl.BlockSpec((TOK_TILE, D_MODEL), lambda i: (i, 0))]
    scratch = [pltpu.VMEM((D_MODEL, D_MODEL), BF16)]
    if dense_w is not None:
        extra_specs = [pl.BlockSpec((D_MODEL, 2 * D_FF), lambda i: (0, 0), pipeline_mode=pl.Buffered(1)),
                       pl.BlockSpec((D_FF, D_MODEL), lambda i: (0, 0), pipeline_mode=pl.Buffered(1))]
        extra_args = tuple(dense_w)
        scratch += [pltpu.VMEM((TOK_TILE, D_MODEL), BF16)]
    else:
        rw_pad = jnp.zeros((D_MODEL, LANES), BF16).at[:, :N_EXPERTS].set(router_w.astype(BF16))
        extra_specs, extra_args = [pl.BlockSpec((D_MODEL, LANES), lambda i: (0, 0))], (rw_pad,)
        out_shape += [jax.ShapeDtypeStruct((N_TOK, D_MODEL), BF16)]
        out_specs += [pl.BlockSpec((TOK_TILE, D_MODEL), lambda i: (i, 0))]
        out_shape += [jax.ShapeDtypeStruct((N_TOK, LANES), F32),
                      jax.ShapeDtypeStruct((N_TILES, 8, TOK_TILE), F32),
                      jax.ShapeDtypeStruct((N_TILES, 8, LANES), F32),
                      jax.ShapeDtypeStruct((8, LANES), F32)]
        out_specs += [pl.BlockSpec((TOK_TILE, LANES), lambda i: (i, 0)),
                      pl.BlockSpec((None, 8, TOK_TILE), lambda i: (i, 0, 0)),
                      pl.BlockSpec((None, 8, LANES), lambda i: (i, 0, 0)),
                      pl.BlockSpec((8, LANES), lambda i: (0, 0))]
        scratch += [pltpu.VMEM((8, LANES), F32), pltpu.VMEM((TOK_TILE, TOK_TILE), BF16)]
    return pl.pallas_call(
        functools.partial(_outproj_kernel, ffn="dense" if dense_w is not None else "routed"),
        out_shape=tuple(out_shape),
        grid=(N_TILES,),
        in_specs=[pl.BlockSpec((TOK_TILE, ATT_W), lambda i: (jnp.minimum(i, P_TILES - 1), 0)),
                  pl.BlockSpec((TOK_TILE, ATT_W), lambda i: (jnp.maximum(i - P_TILES, 0), 0)),
                  pl.BlockSpec((TOK_TILE, CONV_CH), lambda i: (i, 0))] + x_specs + [
                  pl.BlockSpec((None, None, 6, D_MODEL), lambda i: (l, _cond_id(i), 0, 0)),
                  pl.BlockSpec((None, 4, D_MODEL), lambda i: (l, 0, 0)),
                  pl.BlockSpec((None, D_MODEL, D_MODEL), lambda i: (l, 0, 0),
                               pipeline_mode=pl.Buffered(1))] + extra_specs,
        out_specs=tuple(out_specs),
        scratch_shapes=scratch,
        compiler_params=_cparams(1),
        name=f"outproj_{l}",
    )(o_ctx, o_lat, cv, *x_args, mod, norm_g, w_out, *extra_args)


def _lower_triangle():
    r = lax.broadcasted_iota(jnp.int32, (TOK_TILE, TOK_TILE), 0)
    c = lax.broadcasted_iota(jnp.int32, (TOK_TILE, TOK_TILE), 1)
    return jnp.where(c <= r, 1.0, 0.0).astype(BF16)


def _route_tile(h, rw_ref, info_ref, info_t_ref, cb_ref, tot_ref, carry_ref, tri_ref):
    lane = lax.broadcasted_iota(jnp.int32, (TOK_TILE, LANES), 1)
    lanef = lane.astype(F32)
    logits = jnp.dot(h, rw_ref[...], preferred_element_type=F32)
    logits = jnp.where(lane < N_EXPERTS, logits, -jnp.inf)
    big = jnp.asarray(LANES, F32)
    m1 = jnp.max(logits, axis=-1, keepdims=True)
    e1 = jnp.min(jnp.where(logits == m1, lanef, big), axis=-1, keepdims=True)
    oh1 = lanef == e1
    rest = jnp.where(oh1, -jnp.inf, logits)
    m2 = jnp.max(rest, axis=-1, keepdims=True)
    e2 = jnp.min(jnp.where(rest == m2, lanef, big), axis=-1, keepdims=True)
    oh2 = lanef == e2
    ex = jnp.exp(m2 - m1)
    g1 = 1.0 / (1.0 + ex)
    g2 = ex / (1.0 + ex)

    oh = jnp.where(jnp.logical_or(oh1, oh2), 1.0, 0.0)
    incl = jnp.dot(tri_ref[...], oh.astype(BF16), preferred_element_type=F32)
    carry = carry_ref[0:1, :]
    excl = incl - oh + carry
    rank1 = jnp.sum(jnp.where(oh1, excl, 0.0), axis=-1, keepdims=True)
    rank2 = jnp.sum(jnp.where(oh2, excl, 0.0), axis=-1, keepdims=True)

    info = jnp.where(lane == 0, e1, 0.0)
    for k, col in enumerate((e2, rank1, rank2, g1, g2), start=1):
        info = jnp.where(lane == k, col, info)
    info_ref[...] = info
    info_t_ref[...] = info.T[0:8, :]

    cb_ref[...] = carry_ref[...]
    new_carry = carry + incl[TOK_TILE - 1:TOK_TILE, :]
    carry_ref[...] = jnp.broadcast_to(new_carry, carry_ref.shape)
    tot_ref[...] = jnp.broadcast_to(new_carry, tot_ref.shape)


def _sorted_pos(expert, rank, base_ref):
    start = jnp.zeros_like(rank)
    for e in range(N_EXPERTS):
        start = jnp.where(expert == float(e), base_ref[e].astype(F32), start)
    return start + rank


def _dispatch_kernel(clo_ref, cn_ref, base_ref, info_t_ref, x_ref, o_ref):
    r = pl.program_id(0)
    o_ref[...] = jnp.zeros_like(o_ref)
    rows = (r * DISPATCH_TILE + lax.broadcasted_iota(jnp.int32, (DISPATCH_TILE, TOK_TILE), 0)).astype(F32)

    def body(k, carry):
        c = clo_ref[r] + k
        it = info_t_ref[c]
        pos1 = _sorted_pos(it[0:1, :], it[2:3, :], base_ref)
        pos2 = _sorted_pos(it[1:2, :], it[3:4, :], base_ref)
        hit = jnp.logical_or(rows == pos1, rows == pos2)
        sel = jnp.where(hit, 1.0, 0.0).astype(BF16)
        off = pl.multiple_of(c * TOK_TILE, TOK_TILE)
        o_ref[...] += jnp.dot(sel, x_ref[pl.ds(off, TOK_TILE), :], preferred_element_type=F32).astype(BF16)
        return carry

    lax.fori_loop(0, cn_ref[r], body, 0)


def _dispatch(c_lo, c_n, base, info_t, h2):
    return pl.pallas_call(
        _dispatch_kernel,
        out_shape=jax.ShapeDtypeStruct((SORT_ROWS, D_MODEL), BF16),
        grid_spec=pltpu.PrefetchScalarGridSpec(
            num_scalar_prefetch=3,
            grid=(SORT_ROWS // DISPATCH_TILE,),
            in_specs=[pl.BlockSpec((N_TILES, 8, TOK_TILE), lambda r, *_: (0, 0, 0)),
                      pl.BlockSpec((N_TOK, D_MODEL), lambda r, *_: (0, 0),
                                   pipeline_mode=pl.Buffered(1))],
            out_specs=pl.BlockSpec((DISPATCH_TILE, D_MODEL), lambda r, *_: (r, 0))),
        compiler_params=_cparams(1),
        name="moe_dispatch",
    )(c_lo, c_n, base, info_t, h2)


TILE_PARTS = 4
CAST_ROWS = 64


def _expert_weight_ring(c, r, n_c, te_ref, tf_ref, nx_ref, copies, cast):
    @pl.when(jnp.logical_and(c == 0, r == 0))
    def _():
        for cp in copies(te_ref[0], 0):
            cp.start()

    @pl.when(tf_ref[r] == 1)
    def _():
        for cp in copies(te_ref[r], c):
            cp.wait()
        cast()
        nr = nx_ref[r]
        nc = c + (nr <= r).astype(jnp.int32)

        @pl.when(nc < n_c)
        def _():
            for cp in copies(te_ref[nr], nc):
                cp.start()


def _moe_up_kernel(te_ref, tv_ref, tf_ref, nx_ref, x_ref, w_hbm, h_ref, wst_ref, wbf_ref, sem_ref, *, i_moe):
    f = pl.program_id(0)
    r = pl.program_id(1)

    def copies(e, ft):
        out = []
        for part in range(2):
            col = pl.multiple_of((part * UP_TILES + ft) * UP_TILE, LANES)
            out.append(pltpu.make_async_copy(w_hbm.at[i_moe, e, :, pl.ds(col, UP_TILE)],
                                             wst_ref.at[part], sem_ref.at[part]))
        return out

    def cast():
        def rows_block(b, carry):
            rows = pl.ds(pl.multiple_of(b * CAST_ROWS, CAST_ROWS), CAST_ROWS)
            wbf_ref[:, rows, :] = wst_ref[:, rows, :].astype(BF16)
            return carry

        lax.fori_loop(0, D_MODEL // CAST_ROWS, rows_block, 0)

    _expert_weight_ring(f, r, UP_TILES, te_ref, tf_ref, nx_ref, copies, cast)

    def hidden(x):
        gt = jnp.dot(x, wbf_ref[0], preferred_element_type=F32)
        up = jnp.dot(x, wbf_ref[1], preferred_element_type=F32)
        return (gt * _sigmoid(gt) * up).astype(BF16)

    for parts in range(TILE_PARTS + 1):
        rows = parts * (ROW_TILE // TILE_PARTS)

        @pl.when(tv_ref[r] == parts)
        def _(rows=rows):
            if rows:
                h_ref[0:rows, :] = hidden(x_ref[0:rows, :])
            if rows < ROW_TILE:
                h_ref[rows:, :] = jnp.zeros((ROW_TILE - rows, UP_TILE), BF16)


def _moe_up(i_moe, tile_e, tile_valid, tile_first, tile_next, xs, w13):
    return pl.pallas_call(
        functools.partial(_moe_up_kernel, i_moe=i_moe),
        out_shape=jax.ShapeDtypeStruct((SORT_ROWS, D_FF_EXPERT), BF16),
        grid_spec=pltpu.PrefetchScalarGridSpec(
            num_scalar_prefetch=4,
            grid=(UP_TILES, SORT_TILES),
            in_specs=[pl.BlockSpec((ROW_TILE, D_MODEL), lambda f, r, *_: (r, 0)),
                      pl.BlockSpec(memory_space=pl.ANY)],
            out_specs=pl.BlockSpec((ROW_TILE, UP_TILE), lambda f, r, *_: (r, f)),
            scratch_shapes=[pltpu.VMEM((2, D_MODEL, UP_TILE), F32),
                            pltpu.VMEM((2, D_MODEL, UP_TILE), BF16),
                            pltpu.SemaphoreType.DMA((2,))]),
        compiler_params=_cparams(2),
        name="moe_up",
    )(tile_e, tile_valid, tile_first, tile_next, xs, w13)


def _moe_down_kernel(te_ref, tv_ref, tf_ref, nx_ref, h_ref, w_hbm, y_ref, wst_ref, wbf_ref, sem_ref, *, i_moe):
    n = pl.program_id(0)
    r = pl.program_id(1)

    def copies(e, nt):
        col = pl.multiple_of(nt * DOWN_TILE, LANES)
        return [pltpu.make_async_copy(w_hbm.at[i_moe, e, :, pl.ds(col, DOWN_TILE)], wst_ref, sem_ref.at[0])]

    def cast():
        wbf_ref[...] = wst_ref[...].astype(BF16)

    _expert_weight_ring(n, r, D_MODEL // DOWN_TILE, te_ref, tf_ref, nx_ref, copies, cast)

    for parts in range(TILE_PARTS + 1):
        rows = parts * (ROW_TILE // TILE_PARTS)

        @pl.when(tv_ref[r] == parts)
        def _(rows=rows):
            if rows:
                y_ref[0:rows, :] = jnp.dot(h_ref[0:rows, :], wbf_ref[...],
                                           preferred_element_type=F32).astype(BF16)
            if rows < ROW_TILE:
                y_ref[rows:, :] = jnp.zeros((ROW_TILE - rows, DOWN_TILE), BF16)


def _moe_down(i_moe, tile_e, tile_valid, tile_first, tile_next, hs, w2):
    return pl.pallas_call(
        functools.partial(_moe_down_kernel, i_moe=i_moe),
        out_shape=jax.ShapeDtypeStruct((SORT_ROWS, D_MODEL), BF16),
        grid_spec=pltpu.PrefetchScalarGridSpec(
            num_scalar_prefetch=4,
            grid=(D_MODEL // DOWN_TILE, SORT_TILES),
            in_specs=[pl.BlockSpec((ROW_TILE, D_FF_EXPERT), lambda n, r, *_: (r, 0)),
                      pl.BlockSpec(memory_space=pl.ANY)],
            out_specs=pl.BlockSpec((ROW_TILE, DOWN_TILE), lambda n, r, *_: (r, n)),
            scratch_shapes=[pltpu.VMEM((D_FF_EXPERT, DOWN_TILE), F32),
                            pltpu.VMEM((D_FF_EXPERT, DOWN_TILE), BF16),
                            pltpu.SemaphoreType.DMA((1,))]),
        compiler_params=_cparams(2),
        name="moe_down",
    )(tile_e, tile_valid, tile_first, tile_next, hs, w2)


def _window_copy(y_hbm, ybuf_ref, sem_ref, src, buf, slot):
    return pltpu.make_async_copy(y_hbm.at[pl.ds(src, WIN), :],
                                 ybuf_ref.at[buf, pl.ds(slot * WIN, WIN), :], sem_ref.at[buf, slot])


def _start_windows(src_ref, y_hbm, ybuf_ref, sem_ref, tile, buf):
    for s in range(N_WIN):
        src = pl.multiple_of(src_ref[tile * N_WIN + s], BF16_SUBLANES)
        _window_copy(y_hbm, ybuf_ref, sem_ref, src, buf, s).start()


def _combine_kernel(src_ref, lo_ref, hi_ref, base_ref, info_ref, x_ref, mod_ref, g_ref, y_hbm,
                    yp_ref, ys_ref, ybuf_ref, sem_ref):
    j = pl.program_id(0)
    buf = j % 2

    @pl.when(j == 0)
    def _():
        _start_windows(src_ref, y_hbm, ybuf_ref, sem_ref, 0, 0)

    @pl.when(j + 1 < N_TILES)
    def _():
        _start_windows(src_ref, y_hbm, ybuf_ref, sem_ref, j + 1, 1 - buf)

    info = info_ref[...]
    pos1 = _sorted_pos(info[:, 0:1], info[:, 2:3], base_ref)
    pos2 = _sorted_pos(info[:, 1:2], info[:, 3:4], base_ref)
    g1, g2 = info[:, 4:5], info[:, 5:6]
    iota = lax.broadcasted_iota(jnp.int32, (1, WIN), 1)
    ids = []
    for s in range(N_WIN):
        row = src_ref[j * N_WIN + s] + iota
        ok = jnp.logical_and(row >= lo_ref[j * N_WIN + s], row < hi_ref[j * N_WIN + s])
        ids.append(jnp.where(ok, row, -1))
    row_id = jnp.concatenate(ids, axis=1).astype(F32)

    for s in range(N_WIN):
        _window_copy(y_hbm, ybuf_ref, sem_ref, 0, buf, s).wait()
    halves = []
    for rows in (slice(0, TOK_TILE // 2), slice(TOK_TILE // 2, TOK_TILE)):
        sel = (jnp.where(pos1[rows] == row_id, g1[rows], 0.0)
               + jnp.where(pos2[rows] == row_id, g2[rows], 0.0)).astype(BF16)
        ffn = jnp.dot(sel, ybuf_ref[buf], preferred_element_type=F32)
        halves.append(x_ref[rows, :] + mod_ref[5:6, :] * _rms(ffn, g_ref[3:4, :]))
    out = jnp.concatenate(halves, axis=0)

    @pl.when(j < P_TILES)
    def _():
        yp_ref[...] = out

    @pl.when(j >= P_TILES)
    def _():
        ys_ref[...] = out


def _combine(l, win_src, win_lo, win_hi, base, info, x, mod, norm_g, ys):
    return pl.pallas_call(
        _combine_kernel,
        out_shape=(jax.ShapeDtypeStruct((N_PROMPT, D_MODEL), F32),
                   jax.ShapeDtypeStruct((N_SAMPLE, D_MODEL), F32)),
        grid_spec=pltpu.PrefetchScalarGridSpec(
            num_scalar_prefetch=4,
            grid=(N_TILES,),
            in_specs=[pl.BlockSpec((TOK_TILE, LANES), lambda i, *_: (i, 0)),
                      pl.BlockSpec((TOK_TILE, D_MODEL), lambda i, *_: (i, 0)),
                      pl.BlockSpec((None, None, 6, D_MODEL), lambda i, *_: (l, _cond_id(i), 0, 0)),
                      pl.BlockSpec((None, 4, D_MODEL), lambda i, *_: (l, 0, 0)),
                      pl.BlockSpec(memory_space=pl.ANY)],
            out_specs=(pl.BlockSpec((TOK_TILE, D_MODEL), lambda i, *_: (jnp.minimum(i, P_TILES - 1), 0)),
                       pl.BlockSpec((TOK_TILE, D_MODEL), lambda i, *_: (jnp.maximum(i - P_TILES, 0), 0))),
            scratch_shapes=[pltpu.VMEM((2, N_WIN * WIN, D_MODEL), BF16),
                            pltpu.SemaphoreType.DMA((2, N_WIN))]),
        compiler_params=_cparams(1),
        name="moe_combine",
    )(win_src, win_lo, win_hi, base, info, x, mod, norm_g, ys)


def _moe_ffn(l, i_moe, h2, x, mod, norm_g, routing, moe_w13, moe_w2):
    info, info_t, cb, tot = routing

    ar8 = jnp.arange(N_EXPERTS, dtype=jnp.int32)

    def at8(vec, idx):
        return jnp.sum(jnp.where(idx[..., None] == ar8, vec, 0), axis=-1)

    def cumsum8(a):
        return jnp.sum(jnp.where(ar8[None, :] <= ar8[:, None], a[..., None, :], 0), axis=-1)

    counts = tot[0, :N_EXPERTS].astype(jnp.int32)
    padded = (counts + ROW_TILE - 1) // ROW_TILE * ROW_TILE
    seg_end = cumsum8(padded)
    base = (seg_end - padded).astype(jnp.int32)
    cbx = cb[:, 0, :N_EXPERTS].astype(jnp.int32)
    cb_end = jnp.concatenate([cbx[1:], counts[None, :]], axis=0)

    def tile_segments(n_tiles, rows):
        row0 = jnp.arange(n_tiles, dtype=jnp.int32) * rows
        e = jnp.minimum(jnp.sum(seg_end[None, :] <= row0[:, None], axis=1), N_EXPERTS - 1).astype(jnp.int32)
        k0 = row0 - at8(base, e)
        return e, k0, jnp.logical_and(k0 >= 0, k0 < at8(counts, e))

    tile_e, k0, tile_valid = tile_segments(SORT_TILES, ROW_TILE)
    tile_first = jnp.logical_and(tile_valid, k0 == 0)
    d_e, d_k0, d_valid = tile_segments(SORT_ROWS // DISPATCH_TILE, DISPATCH_TILE)
    d_kend = jnp.minimum(d_k0 + DISPATCH_TILE, at8(counts, d_e))
    cb_t = jnp.sum(jnp.where(d_e[:, None, None] == ar8[None, :, None], cbx.T[None, :, :], 0), axis=1)
    c_lo = jnp.sum(cb_t <= d_k0[:, None], axis=1) - 1
    c_hi = jnp.sum(cb_t < d_kend[:, None], axis=1) - 1
    c_n = jnp.where(d_valid, c_hi - c_lo + 1, 0).astype(jnp.int32)
    c_lo = jnp.where(d_valid, c_lo, 0).astype(jnp.int32)
    tile_idx = jnp.arange(SORT_TILES, dtype=jnp.int32)
    first_idx = jnp.where(tile_first, tile_idx, SORT_TILES)
    later = jnp.where(tile_idx[None, :] > tile_idx[:, None], first_idx[None, :], SORT_TILES)
    tile_next = jnp.min(later, axis=1)
    tile_next = jnp.where(tile_next >= SORT_TILES, 0, tile_next).astype(jnp.int32)
    tile_rows = at8(counts, tile_e) - k0
    part_rows = ROW_TILE // TILE_PARTS
    tile_valid = jnp.where(tile_valid, jnp.clip((tile_rows + part_rows - 1) // part_rows, 1, TILE_PARTS),
                           0).astype(jnp.int32)
    tile_first = tile_first.astype(jnp.int32)

    seg_lo = base[None, :] + cbx
    seg_hi = base[None, :] + cb_end
    seg_n = seg_hi - seg_lo
    w0 = seg_lo // BF16_SUBLANES * BF16_SUBLANES
    nw = jnp.where(seg_n > 0, (seg_lo - w0 + seg_n + WIN - 1) // WIN, 0)
    nw_end = cumsum8(nw)
    slot = jnp.arange(N_WIN, dtype=jnp.int32)
    slot_e = jnp.minimum(jnp.sum(nw_end[:, None, :] <= slot[None, :, None], axis=2), N_EXPERTS - 1)
    pick = slot_e[:, :, None] == jnp.arange(N_EXPERTS, dtype=jnp.int32)[None, None, :]
    take = lambda a: jnp.sum(jnp.where(pick, a[:, None, :], 0), axis=2)
    slot_k = slot[None, :] - (take(nw_end) - take(nw))
    slot_ok = slot[None, :] < nw_end[:, -1:]
    slot_src = take(w0) + WIN * slot_k
    win_src = jnp.where(slot_ok, slot_src, 0).astype(jnp.int32).reshape(-1)
    win_lo = jnp.where(slot_ok, take(seg_lo), 0).astype(jnp.int32).reshape(-1)
    win_hi = jnp.where(slot_ok, take(seg_hi), 0).astype(jnp.int32).reshape(-1)

    xs = _dispatch(c_lo, c_n, base, info_t, h2)
    hs = _moe_up(i_moe, tile_e, tile_valid, tile_first, tile_next, xs, moe_w13)
    ys = _moe_down(i_moe, tile_e, tile_valid, tile_first, tile_next, hs, moe_w2)
    return _combine(l, win_src, win_lo, win_hi, base, info, x, mod, norm_g, ys)


def kernel(x_prompt, x_sample, cache_k, cache_v, c, c_ctx, w_ada, b_ada, norm_g, w_in, w_out,
           lam_params, subln_g, dw_weight, dw_bias, conv_ln_g, conv_ln_b, dense_w13, dense_w2,
           router_w, moe_w13, moe_w2):
    x = (x_prompt.reshape(N_PROMPT, D_MODEL), x_sample.reshape(N_SAMPLE, D_MODEL))
    cond8 = jnp.zeros((8, D_MODEL), F32).at[0].set(c_ctx).at[1:1 + DEC_BATCH].set(c)
    mod = _ada_modulation(cond8, w_ada, b_ada).reshape(DEPTH, 8, 6, D_MODEL)
    rope = _rope_tables()
    ck = cache_k.reshape(DEC_BATCH, DEPTH, PAST_LEN * ATT_HEADS, V_DIM)
    cv = cache_v.reshape(DEC_BATCH, DEPTH, PAST_LEN * ATT_HEADS, V_DIM)

    caches = None
    for l in range(DEPTH):
        lam_init = 0.8 - 0.6 * math.exp(-0.3 * l)
        qkv, cvo, *caches = _mixer_in(l, x, mod, norm_g, w_in, rope,
                                      (dw_weight, dw_bias, conv_ln_g, conv_ln_b), caches)
        o_ctx, o_lat = _attention(l, lam_init, lam_params, subln_g, qkv, ck, cv)
        i = l // 2
        if l % 2 == 0:
            (x,) = _outproj(l, o_ctx, o_lat, cvo, x, mod, norm_g, w_out,
                            dense_w=(dense_w13[i].astype(BF16), dense_w2[i].astype(BF16)))
        else:
            x, h2, *routing = _outproj(l, o_ctx, o_lat, cvo, x, mod, norm_g, w_out, router_w=router_w[i])
            x = _moe_ffn(l, i, h2, x, mod, norm_g, routing, moe_w13, moe_w2)

    xp, xs = x if isinstance(x, tuple) else (x[:N_PROMPT], x[N_PROMPT:])
    new_k, new_v = (a.reshape(BATCH, DEPTH, SEQ, ATT_HEADS, V_DIM) for a in caches)
    return (xp.reshape(BATCH, SEQ, D_MODEL), xs.reshape(DEC_BATCH, DEC_SEQ, D_MODEL), new_k, new_v)
```

```python
import functools
import math

import jax
import jax.numpy as jnp
import numpy as np
from jax import lax
from jax.experimental import pallas as pl
from jax.experimental.pallas import tpu as pltpu

F32 = jnp.float32
BF16 = jnp.bfloat16

D_MODEL = 1024
BATCH = 32
SEQ = 256
DEPTH = 2
DEC_BATCH = 2
DEC_SEQ = 2048
PAST_LEN = 512
GRID_W = 64
ATT_HEADS = 4
QK_DIM = 64
V_DIM = 128
ATT_W = ATT_HEADS * V_DIM
IN_W = 5 * ATT_W
CONV_CH = 512
CONV_WIDTH = 31
D_FF = 2816
N_EXPERTS = 8
D_FF_EXPERT = 3584
ROPE_THETA = 10000.0
NORM_EPS = 1e-6
LN_EPS = 1e-5

N_PROMPT = BATCH * SEQ
N_SAMPLE = DEC_BATCH * DEC_SEQ
N_TOK = N_PROMPT + N_SAMPLE

LANES = 128
BF16_SUBLANES = 16
MXU_TILE = 256
VMEM_BYTES = 64 * 1024 * 1024
VMEM_LIMIT = VMEM_BYTES - 8 * 1024 * 1024

TOK_TILE = 512
N_TILES = N_TOK // TOK_TILE
P_TILES = N_PROMPT // TOK_TILE
S_TILES_PER_BATCH = DEC_SEQ // TOK_TILE

ROW_TILE = 512
SORT_TILES = (2 * N_TOK + N_EXPERTS * ROW_TILE) // ROW_TILE + 1
SORT_ROWS = SORT_TILES * ROW_TILE
DISPATCH_TILE = 256
UP_TILE = 7 * MXU_TILE
UP_TILES = D_FF_EXPERT // UP_TILE
DOWN_TILE = 4 * MXU_TILE
WIN = 128
N_WIN = (2 * TOK_TILE + N_EXPERTS * (BF16_SUBLANES - 1 + WIN - 1)) // WIN

assert D_FF_EXPERT % UP_TILE == 0 and D_MODEL % DOWN_TILE == 0 and D_FF % MXU_TILE == 0
assert ROW_TILE % DISPATCH_TILE == 0 and N_TOK % TOK_TILE == 0 and N_PROMPT % TOK_TILE == 0
assert DEPTH == 2, "layer 0 zero-fills exactly one later cache slab; the last layer must be the routed one"


def _cparams(n_axes):
    return pltpu.CompilerParams(dimension_semantics=("arbitrary",) * n_axes,
                                vmem_limit_bytes=VMEM_LIMIT)


def _cond_id(i):
    return jnp.where(i < P_TILES, 0, 1 + (i - P_TILES) // S_TILES_PER_BATCH)


def _sigmoid(x):
    return 1.0 / (1.0 + jnp.exp(-x))


def _rms(x, g):
    return x * lax.rsqrt(jnp.mean(x * x, axis=-1, keepdims=True) + NORM_EPS) * g


def _ada_kernel(c_ref, w_ref, b_ref, o_ref):
    c = c_ref[...]
    s = (c * _sigmoid(c)).astype(BF16)
    o_ref[...] = jnp.dot(s, w_ref[...].astype(BF16), preferred_element_type=F32) + b_ref[...]


def _ada_modulation(cond8, w_ada, b_ada):
    tn = 1536
    n = 6 * D_MODEL
    return pl.pallas_call(
        _ada_kernel,
        out_shape=jax.ShapeDtypeStruct((DEPTH, 8, n), F32),
        grid=(DEPTH, n // tn),
        in_specs=[pl.BlockSpec((8, D_MODEL), lambda l, j: (0, 0)),
                  pl.BlockSpec((None, D_MODEL, tn), lambda l, j: (l, 0, j)),
                  pl.BlockSpec((None, 1, tn), lambda l, j: (l, 0, j))],
        out_specs=pl.BlockSpec((None, 8, tn), lambda l, j: (l, 0, j)),
        compiler_params=_cparams(2),
        name="ada_modulation",
    )(cond8, w_ada, b_ada.reshape(DEPTH, 1, n))


def _tile_x(xa_ref, xb_ref):
    return jnp.where(pl.program_id(0) < P_TILES, xa_ref[...], xb_ref[...])


def _x_specs(x):
    last = N_TILES - 1
    if isinstance(x, tuple):
        xa, xb = x
        b_idx = lambda i, *_: (jnp.maximum(jnp.minimum(i, last) - P_TILES, 0), 0)
    else:
        xa = xb = x
        b_idx = lambda i, *_: (jnp.maximum(jnp.minimum(i, last), P_TILES), 0)
    a_idx = lambda i, *_: (jnp.minimum(i, P_TILES - 1), 0)
    return [pl.BlockSpec((TOK_TILE, D_MODEL), a_idx), pl.BlockSpec((TOK_TILE, D_MODEL), b_idx)], (xa, xb)


ROT_SPAN = QK_DIM // 4
SEQ_PER_TILE = TOK_TILE // SEQ
CACHE_ROWS = SEQ * ATT_HEADS


CONV_LAG = 2
CONV_RING = 3
CONV_TILE = 256
CONV_HALO = 16
CONV_SUB = 32


def _conv_pass(upad_ref, w_ref, bias_ref, lg_ref, lb_ref, o_ref, row0):
    rows = CONV_TILE + 2 * CONV_HALO - 8
    for s in range(1, 8):
        upad_ref[s, 0:rows, :] = upad_ref[0, s:s + rows, :]
    first_tap = CONV_HALO - CONV_WIDTH // 2
    groups = CONV_SUB // 8
    for t in range(CONV_TILE // CONV_SUB):
        base = t * CONV_SUB
        acc = jnp.zeros((groups, 8, CONV_CH), F32)
        for j in range(CONV_WIDTH):
            lo = base + (first_tap + j) // 8 * 8
            taps = upad_ref[(first_tap + j) % 8, lo:lo + CONV_SUB, :].reshape(groups, 8, CONV_CH)
            acc = acc + taps * w_ref[j][None]
        y = acc.reshape(CONV_SUB, CONV_CH) + bias_ref[...]
        mu = jnp.mean(y, axis=-1, keepdims=True)
        yc = y - mu
        var = jnp.mean(yc * yc, axis=-1, keepdims=True)
        z = yc * lax.rsqrt(var + LN_EPS) * lg_ref[...] + lb_ref[...]
        o_ref[row0 + base:row0 + base + CONV_SUB, :] = (z * _sigmoid(z)).astype(BF16)


def _mixer_in_kernel(*refs, layer):
    n_in = 12 + (2 if layer else 0)
    (xa_ref, xb_ref, mod_ref, g_ref, w_ref, cos_ref, sina_ref, sinb_ref,
     dw_ref, db_ref, lg_ref, lb_ref) = refs[:12]
    qkv_ref, cv_ref, kc_ref, vc_ref, wbf_ref, ring_ref, glu_ref, pj_ref, upad_ref, taps_ref = refs[n_in:]
    i = pl.program_id(0)
    t = jnp.minimum(i, N_TILES - 1)

    @pl.when(i == 0)
    def _():
        wbf_ref[...] = w_ref[...].astype(BF16)
        ring_ref[...] = jnp.zeros_like(ring_ref)
        glu_ref[...] = jnp.zeros_like(glu_ref)
        for j in range(CONV_WIDTH):
            taps_ref[j] = jnp.broadcast_to(dw_ref[j:j + 1, :], (8, CONV_CH))

    ring_ref[(i - 1) % CONV_RING] = glu_ref[...]

    h = _rms(_tile_x(xa_ref, xb_ref), g_ref[0:1, :]) * (1.0 + mod_ref[1:2, :]) + mod_ref[0:1, :]
    hb = h.astype(BF16)

    def proj(part):
        return jnp.dot(hb, wbf_ref[:, part * ATT_W:(part + 1) * ATT_W], preferred_element_type=F32)

    for part in range(3):
        pj_ref[:, part * ATT_W:(part + 1) * ATT_W] = proj(part)
    glu_ref[...] = proj(3) * _sigmoid(proj(4))

    j = i - CONV_LAG
    latent = j >= P_TILES
    q = (j - P_TILES) % S_TILES_PER_BATCH
    has_prev = jnp.logical_and(latent, q != 0)
    has_next = jnp.logical_and(latent, q != S_TILES_PER_BATCH - 1)
    cur = j % CONV_RING
    prev_tail = ring_ref[(j - 1) % CONV_RING, TOK_TILE - CONV_HALO:, :]
    next_head = ring_ref[(j + 1) % CONV_RING, 0:CONV_HALO, :]
    for s in range(TOK_TILE // CONV_TILE):
        lo = s * CONV_TILE
        if s == 0:
            before = jnp.where(has_prev, prev_tail, 0.0)
        else:
            before = jnp.where(latent, ring_ref[cur, lo - CONV_HALO:lo, :], 0.0)
        if s == TOK_TILE // CONV_TILE - 1:
            after = jnp.where(has_next, next_head, 0.0)
        else:
            after = jnp.where(latent, ring_ref[cur, lo + CONV_TILE:lo + CONV_TILE + CONV_HALO, :], 0.0)
        upad_ref[0, 0:CONV_HALO, :] = before
        upad_ref[0, CONV_HALO:CONV_HALO + CONV_TILE, :] = ring_ref[cur, lo:lo + CONV_TILE, :]
        upad_ref[0, CONV_HALO + CONV_TILE:, :] = after
        _conv_pass(upad_ref, taps_ref, db_ref, lg_ref, lb_ref, cv_ref, lo)

    qkv_ref[:, 2 * ATT_W:] = pj_ref[:, 2 * ATT_W:].astype(BF16)

    @pl.when(t < P_TILES)
    def _():
        qkv_ref[:, :2 * ATT_W] = pj_ref[:, :2 * ATT_W].astype(BF16)
        for ref, col0 in ((kc_ref, ATT_W), (vc_ref, 2 * ATT_W)):
            for s in range(SEQ_PER_TILE):
                for hd in range(ATT_HEADS):
                    val = pj_ref[SEQ * s:SEQ * (s + 1), col0 + V_DIM * hd:col0 + V_DIM * (hd + 1)]
                    rows = pl.ds(hd, SEQ, stride=ATT_HEADS)
                    if layer == 0:
                        ref[s, 0, rows, :] = val
                    else:
                        ref[s, rows, :] = val
            if layer == 0:
                ref[:, 1:] = jnp.zeros((SEQ_PER_TILE, DEPTH - 1, CACHE_ROWS, V_DIM), F32)

    @pl.when(t >= P_TILES)
    def _():
        cos = cos_ref[...]
        sina = sina_ref[...]
        sinb = sinb_ref[...]
        for c in range(2 * ATT_W // V_DIM):
            xg = pj_ref[:, V_DIM * c:V_DIM * (c + 1)]
            fwd = pltpu.roll(xg, V_DIM - ROT_SPAN, 1)
            bwd = pltpu.roll(xg, ROT_SPAN, 1)
            qkv_ref[:, V_DIM * c:V_DIM * (c + 1)] = (xg * cos + fwd * sina + bwd * sinb).astype(BF16)


def _mixer_in(l, x, mod, norm_g, w_in, rope, conv_params, caches):
    cos, sina, sinb = rope
    dw_w, dw_b, ln_g, ln_b = conv_params
    x_specs, x_args = _x_specs(x)

    def tile(i):
        return jnp.minimum(i, N_TILES - 1)

    def rope_idx(i):
        return (jnp.maximum(tile(i) - P_TILES, 0) % S_TILES_PER_BATCH, 0)

    def vec(a):
        return a.reshape(DEPTH, 1, CONV_CH)

    cache_shape = jax.ShapeDtypeStruct((BATCH, DEPTH, CACHE_ROWS, V_DIM), F32)
    if l == 0:
        cache_spec = pl.BlockSpec((SEQ_PER_TILE, DEPTH, CACHE_ROWS, V_DIM),
                                  lambda i: (jnp.minimum(i, P_TILES - 1), 0, 0, 0))
        extra_specs, extra_args, aliases = [], (), {}
    else:
        cache_spec = pl.BlockSpec((SEQ_PER_TILE, None, CACHE_ROWS, V_DIM),
                                  lambda i: (jnp.minimum(i, P_TILES - 1), l, 0, 0))
        extra_specs = [pl.BlockSpec(memory_space=pl.ANY)] * 2
        extra_args = tuple(caches)
        aliases = {12: 2, 13: 3}

    return pl.pallas_call(
        functools.partial(_mixer_in_kernel, layer=l),
        out_shape=(jax.ShapeDtypeStruct((N_TOK, 3 * ATT_W), BF16),
                   jax.ShapeDtypeStruct((N_TOK, CONV_CH), BF16), cache_shape, cache_shape),
        grid=(N_TILES + CONV_LAG,),
        in_specs=x_specs + [
            pl.BlockSpec((None, None, 6, D_MODEL), lambda i: (l, _cond_id(tile(i)), 0, 0)),
            pl.BlockSpec((None, 4, D_MODEL), lambda i: (l, 0, 0)),
            pl.BlockSpec((None, D_MODEL, IN_W), lambda i: (l, 0, 0), pipeline_mode=pl.Buffered(1)),
            pl.BlockSpec((TOK_TILE, V_DIM), rope_idx),
            pl.BlockSpec((TOK_TILE, V_DIM), rope_idx),
            pl.BlockSpec((TOK_TILE, V_DIM), rope_idx),
            pl.BlockSpec((None, CONV_WIDTH, CONV_CH), lambda i: (l, 0, 0)),
            pl.BlockSpec((None, 1, CONV_CH), lambda i: (l, 0, 0)),
            pl.BlockSpec((None, 1, CONV_CH), lambda i: (l, 0, 0)),
            pl.BlockSpec((None, 1, CONV_CH), lambda i: (l, 0, 0))] + extra_specs,
        out_specs=(pl.BlockSpec((TOK_TILE, 3 * ATT_W), lambda i: (tile(i), 0)),
                   pl.BlockSpec((TOK_TILE, CONV_CH), lambda i: (jnp.maximum(i - CONV_LAG, 0), 0)),
                   cache_spec, cache_spec),
        scratch_shapes=[pltpu.VMEM((D_MODEL, IN_W), BF16),
                        pltpu.VMEM((CONV_RING, TOK_TILE, CONV_CH), F32),
                        pltpu.VMEM((TOK_TILE, CONV_CH), F32),
                        pltpu.VMEM((TOK_TILE, 3 * ATT_W), F32),
                        pltpu.VMEM((8, CONV_TILE + 2 * CONV_HALO, CONV_CH), F32),
                        pltpu.VMEM((CONV_WIDTH, 8, CONV_CH), F32)],
        input_output_aliases=aliases,
        compiler_params=_cparams(1),
        name=f"mixer_in_{l}",
    )(*x_args, mod, norm_g, w_in, cos, sina, sinb, dw_w, vec(dw_b), vec(ln_g), vec(ln_b), *extra_args)


def _rope_tables():
    rows = DEC_SEQ // GRID_W
    row_pos = np.repeat(np.arange(rows, dtype=np.float64), GRID_W)
    col_pos = np.tile(np.arange(GRID_W, dtype=np.float64), rows)
    half = QK_DIM // 2
    inv_freq = 1.0 / (ROPE_THETA ** (np.arange(0, half, 2, dtype=np.float64) / half))
    ang_r = row_pos[:, None] * inv_freq
    ang_c = col_pos[:, None] * inv_freq
    ang = np.concatenate([ang_r, ang_r, ang_c, ang_c], axis=-1)
    cos = np.tile(np.cos(ang), (1, 2)).astype(np.float32)
    sin = np.tile(np.sin(ang), (1, 2)).astype(np.float32)
    first = (np.arange(V_DIM) % (2 * ROT_SPAN)) < ROT_SPAN
    sina = np.where(first[None, :], -sin, 0.0).astype(np.float32)
    sinb = np.where(first[None, :], 0.0, sin).astype(np.float32)
    return jnp.asarray(cos), jnp.asarray(sina), jnp.asarray(sinb)


def _attn_kernel(*refs, lam_init, has_ext):
    if has_ext:
        lamp_ref, sub_ref, q_ref, k_ref, v_ref, ke_ref, ve_ref, o_ref = refs
    else:
        lamp_ref, sub_ref, q_ref, k_ref, v_ref, o_ref = refs
    lp = lamp_ref[...]
    lam = (jnp.exp(jnp.sum(lp[0:1] * lp[1:2], axis=-1, keepdims=True))
           - jnp.exp(jnp.sum(lp[2:3] * lp[3:4], axis=-1, keepdims=True)) + lam_init)
    lane = lax.broadcasted_iota(jnp.int32, (1, V_DIM), 1)
    nt = (((1,), (1,)), ((), ()))
    scale = QK_DIM ** -0.5
    map_scale = [jnp.where(lane < QK_DIM, scale, 0.0).astype(BF16),
                 jnp.where(lane < QK_DIM, 0.0, scale).astype(BF16)]

    tq = q_ref.shape[0]
    head_cols = [slice(V_DIM * hd, V_DIM * (hd + 1)) for hd in range(ATT_HEADS)]

    def q_map(hd, m):
        return q_ref[:, head_cols[hd]] * map_scale[m]

    def with_ones(v):
        return jnp.concatenate([v, jnp.ones_like(v)], axis=1)

    outs = []
    if has_ext:
        for hd in range(ATT_HEADS):
            kh = k_ref[:, head_cols[hd]]
            v_aug = with_ones(v_ref[:, head_cols[hd]])
            head_rows = pl.ds(hd, PAST_LEN, stride=ATT_HEADS)
            keh = ke_ref[head_rows, :].astype(BF16)
            ve_aug = with_ones(ve_ref[head_rows, :].astype(BF16))
            ratio = []
            for m in range(2):
                qm = q_map(hd, m)
                s = lax.dot_general(qm, kh, nt, preferred_element_type=F32)
                se = lax.dot_general(qm, keh, nt, preferred_element_type=F32)
                mx = jnp.maximum(jnp.max(s, axis=-1, keepdims=True), jnp.max(se, axis=-1, keepdims=True))
                pv = (jnp.dot(jnp.exp((s - mx).astype(BF16)), v_aug, preferred_element_type=F32)
                      + jnp.dot(jnp.exp((se - mx).astype(BF16)), ve_aug, preferred_element_type=F32))
                ratio.append(pv[:, :V_DIM] / pv[:, V_DIM:])
            outs.append(ratio[0] - lam * ratio[1])
    else:
        seqs = [slice(SEQ * sq, SEQ * (sq + 1)) for sq in range(tq // SEQ)]
        s = jnp.concatenate(
            [lax.dot_general(q_ref[rows, head_cols[hd]] * map_scale[m], k_ref[rows, head_cols[hd]], nt,
                             preferred_element_type=F32)
             for rows in seqs for hd in range(ATT_HEADS) for m in range(2)], axis=0)
        p = jnp.exp((s - jnp.max(s, axis=-1, keepdims=True)).astype(BF16))
        for hd in range(ATT_HEADS):
            per_seq = []
            for sq, rows in enumerate(seqs):
                first = (sq * ATT_HEADS + hd) * 2 * SEQ
                pv = jnp.dot(p[first:first + 2 * SEQ], with_ones(v_ref[rows, head_cols[hd]]),
                             preferred_element_type=F32)
                ratio = pv[:, :V_DIM] / pv[:, V_DIM:]
                per_seq.append(ratio[:SEQ] - lam * ratio[SEQ:])
            outs.append(jnp.concatenate(per_seq, axis=0))

    heads = [(_rms(o, sub_ref[...]) * (1.0 - lam_init)).astype(BF16) for o in outs]
    o_ref[...] = jnp.concatenate(heads, axis=1)


CTX_SEQS = 4


def _attention(l, lam_init, lam_params, subln_g, qkvg, cache_k, cache_v):
    small = [pl.BlockSpec((None, 4, QK_DIM), lambda *_: (l, 0, 0)),
             pl.BlockSpec((None, 1, V_DIM), lambda *_: (l, 0, 0))]
    sub3 = subln_g.reshape(DEPTH, 1, V_DIM)

    o_ctx = pl.pallas_call(
        functools.partial(_attn_kernel, lam_init=lam_init, has_ext=False),
        out_shape=jax.ShapeDtypeStruct((N_PROMPT, ATT_W), BF16),
        grid=(BATCH // CTX_SEQS,),
        in_specs=small + [pl.BlockSpec((CTX_SEQS * SEQ, ATT_W), lambda b: (b, 0)),
                          pl.BlockSpec((CTX_SEQS * SEQ, ATT_W), lambda b: (b, 1)),
                          pl.BlockSpec((CTX_SEQS * SEQ, ATT_W), lambda b: (b, 2))],
        out_specs=pl.BlockSpec((CTX_SEQS * SEQ, ATT_W), lambda b: (b, 0)),
        compiler_params=_cparams(1),
        name=f"attn_ctx_{l}",
    )(lam_params, sub3, qkvg, qkvg, qkvg)

    tq = 512
    q_tiles = DEC_SEQ // tq
    q_base = N_PROMPT // tq
    kv_base = N_PROMPT // DEC_SEQ

    def q_idx(b, i):
        return (q_base + b * q_tiles + i, 0)

    o_lat = pl.pallas_call(
        functools.partial(_attn_kernel, lam_init=lam_init, has_ext=True),
        out_shape=jax.ShapeDtypeStruct((N_SAMPLE, ATT_W), BF16),
        grid=(DEC_BATCH, q_tiles),
        in_specs=small + [pl.BlockSpec((tq, ATT_W), q_idx),
                          pl.BlockSpec((DEC_SEQ, ATT_W), lambda b, i: (kv_base + b, 1)),
                          pl.BlockSpec((DEC_SEQ, ATT_W), lambda b, i: (kv_base + b, 2)),
                          pl.BlockSpec((None, None, PAST_LEN * ATT_HEADS, V_DIM), lambda b, i: (b, l, 0, 0)),
                          pl.BlockSpec((None, None, PAST_LEN * ATT_HEADS, V_DIM), lambda b, i: (b, l, 0, 0))],
        out_specs=pl.BlockSpec((tq, ATT_W), lambda b, i: (b * q_tiles + i, 0)),
        compiler_params=_cparams(2),
        name=f"attn_lat_{l}",
    )(lam_params, sub3, qkvg, qkvg, qkvg, cache_k, cache_v)
    return o_ctx, o_lat


def _outproj_kernel(*refs, ffn):
    if ffn == "routed":
        (oc_ref, ol_ref, cv_ref, xa_ref, xb_ref, mod_ref, g_ref, w_ref, rw_ref,
         xo_ref, h2_ref, info_ref, info_t_ref, cb_ref, tot_ref, wbf_ref, carry_ref, tri_ref) = refs
    else:
        (oc_ref, ol_ref, cv_ref, xa_ref, xb_ref, mod_ref, g_ref, w_ref, w13_ref, w2_ref,
         xo_ref, wbf_ref, h2_ref) = refs
    i = pl.program_id(0)

    @pl.when(i == 0)
    def _():
        wbf_ref[...] = w_ref[...].astype(BF16)
        if ffn == "routed":
            carry_ref[...] = jnp.zeros_like(carry_ref)
            tri_ref[...] = _lower_triangle()

    half = TOK_TILE // 2
    for rows in (slice(0, half), slice(half, TOK_TILE)):
        o = jnp.where(i < P_TILES, oc_ref[rows, :], ol_ref[rows, :])
        m = (jnp.dot(o, wbf_ref[0:ATT_W, :], preferred_element_type=F32)
             + jnp.dot(cv_ref[rows, :], wbf_ref[ATT_W:, :], preferred_element_type=F32))
        x = jnp.where(i < P_TILES, xa_ref[rows, :], xb_ref[rows, :])
        xn = x + mod_ref[2:3, :] * _rms(m, g_ref[1:2, :])
        xo_ref[rows, :] = xn
        h2_ref[rows, :] = (_rms(xn, g_ref[2:3, :]) * (1.0 + mod_ref[4:5, :]) + mod_ref[3:4, :]).astype(BF16)

    if ffn == "routed":
        _route_tile(h2_ref[...], rw_ref, info_ref, info_t_ref, cb_ref, tot_ref, carry_ref, tri_ref)
    else:
        h = h2_ref[...]
        gt = jnp.dot(h, w13_ref[:, :D_FF], preferred_element_type=F32)
        up = jnp.dot(h, w13_ref[:, D_FF:], preferred_element_type=F32)
        a = (gt * _sigmoid(gt) * up).astype(BF16)
        y = jnp.dot(a, w2_ref[...], preferred_element_type=F32)
        xo_ref[...] = xo_ref[...] + mod_ref[5:6, :] * _rms(y, g_ref[3:4, :])


def _outproj(l, o_ctx, o_lat, cv, x, mod, norm_g, w_out, router_w=None, dense_w=None):
    x_specs, x_args = _x_specs(x)
    out_shape = [jax.ShapeDtypeStruct((N_TOK, D_MODEL), F32)]
    out_specs = [pl.BlockSpec((TOK_TILE, D_MODEL), lambda i: (i, 0))]
    scratch = [pltpu.VMEM((D_MODEL, D_MODEL), BF16)]
    if dense_w is not None:
        extra_specs = [pl.BlockSpec((D_MODEL, 2 * D_FF), lambda i: (0, 0), pipeline_mode=pl.Buffered(1)),
                       pl.BlockSpec((D_FF, D_MODEL), lambda i: (0, 0), pipeline_mode=pl.Buffered(1))]
        extra_args = tuple(dense_w)
        scratch += [pltpu.VMEM((TOK_TILE, D_MODEL), BF16)]
    else:
        rw_pad = jnp.zeros((D_MODEL, LANES), BF16).at[:, :N_EXPERTS].set(router_w.astype(BF16))
        extra_specs, extra_args = [pl.BlockSpec((D_MODEL, LANES), lambda i: (0, 0))], (rw_pad,)
        out_shape += [jax.ShapeDtypeStruct((N_TOK, D_MODEL), BF16)]
        out_specs += [pl.BlockSpec((TOK_TILE, D_MODEL), lambda i: (i, 0))]
        out_shape += [jax.ShapeDtypeStruct((N_TOK, LANES), F32),
                      jax.ShapeDtypeStruct((N_TILES, 8, TOK_TILE), F32),
                      jax.ShapeDtypeStruct((N_TILES, 8, LANES), F32),
                      jax.ShapeDtypeStruct((8, LANES), F32)]
        out_specs += [pl.BlockSpec((TOK_TILE, LANES), lambda i: (i, 0)),
                      pl.BlockSpec((None, 8, TOK_TILE), lambda i: (i, 0, 0)),
                      pl.BlockSpec((None, 8, LANES), lambda i: (i, 0, 0)),
                      pl.BlockSpec((8, LANES), lambda i: (0, 0))]
        scratch += [pltpu.VMEM((8, LANES), F32), pltpu.VMEM((TOK_TILE, TOK_TILE), BF16)]
    return pl.pallas_call(
        functools.partial(_outproj_kernel, ffn="dense" if dense_w is not None else "routed"),
        out_shape=tuple(out_shape),
        grid=(N_TILES,),
        in_specs=[pl.BlockSpec((TOK_TILE, ATT_W), lambda i: (jnp.minimum(i, P_TILES - 1), 0)),
                  pl.BlockSpec((TOK_TILE, ATT_W), lambda i: (jnp.maximum(i - P_TILES, 0), 0)),
                  pl.BlockSpec((TOK_TILE, CONV_CH), lambda i: (i, 0))] + x_specs + [
                  pl.BlockSpec((None, None, 6, D_MODEL), lambda i: (l, _cond_id(i), 0, 0)),
                  pl.BlockSpec((None, 4, D_MODEL), lambda i: (l, 0, 0)),
                  pl.BlockSpec((None, D_MODEL, D_MODEL), lambda i: (l, 0, 0),
                               pipeline_mode=pl.Buffered(1))] + extra_specs,
        out_specs=tuple(out_specs),
        scratch_shapes=scratch,
        compiler_params=_cparams(1),
        name=f"outproj_{l}",
    )(o_ctx, o_lat, cv, *x_args, mod, norm_g, w_out, *extra_args)


def _lower_triangle():
    r = lax.broadcasted_iota(jnp.int32, (TOK_TILE, TOK_TILE), 0)
    c = lax.broadcasted_iota(jnp.int32, (TOK_TILE, TOK_TILE), 1)
    return jnp.where(c <= r, 1.0, 0.0).astype(BF16)


def _route_tile(h, rw_ref, info_ref, info_t_ref, cb_ref, tot_ref, carry_ref, tri_ref):
    lane = lax.broadcasted_iota(jnp.int32, (TOK_TILE, LANES), 1)
    lanef = lane.astype(F32)
    logits = jnp.dot(h, rw_ref[...], preferred_element_type=F32)
    logits = jnp.where(lane < N_EXPERTS, logits, -jnp.inf)
    big = jnp.asarray(LANES, F32)
    m1 = jnp.max(logits, axis=-1, keepdims=True)
    e1 = jnp.min(jnp.where(logits == m1, lanef, big), axis=-1, keepdims=True)
    oh1 = lanef == e1
    rest = jnp.where(oh1, -jnp.inf, logits)
    m2 = jnp.max(rest, axis=-1, keepdims=True)
    e2 = jnp.min(jnp.where(rest == m2, lanef, big), axis=-1, keepdims=True)
    oh2 = lanef == e2
    ex = jnp.exp(m2 - m1)
    g1 = 1.0 / (1.0 + ex)
    g2 = ex / (1.0 + ex)

    oh = jnp.where(jnp.logical_or(oh1, oh2), 1.0, 0.0)
    incl = jnp.dot(tri_ref[...], oh.astype(BF16), preferred_element_type=F32)
    carry = carry_ref[0:1, :]
    excl = incl - oh + carry
    rank1 = jnp.sum(jnp.where(oh1, excl, 0.0), axis=-1, keepdims=True)
    rank2 = jnp.sum(jnp.where(oh2, excl, 0.0), axis=-1, keepdims=True)

    info = jnp.where(lane == 0, e1, 0.0)
    for k, col in enumerate((e2, rank1, rank2, g1, g2), start=1):
        info = jnp.where(lane == k, col, info)
    info_ref[...] = info
    info_t_ref[...] = info.T[0:8, :]

    cb_ref[...] = carry_ref[...]
    new_carry = carry + incl[TOK_TILE - 1:TOK_TILE, :]
    carry_ref[...] = jnp.broadcast_to(new_carry, carry_ref.shape)
    tot_ref[...] = jnp.broadcast_to(new_carry, tot_ref.shape)


def _sorted_pos(expert, rank, base_ref):
    start = jnp.zeros_like(rank)
    for e in range(N_EXPERTS):
        start = jnp.where(expert == float(e), base_ref[e].astype(F32), start)
    return start + rank


def _dispatch_kernel(clo_ref, cn_ref, base_ref, info_t_ref, x_hbm, o_ref, x_ref, sem_ref, seen_ref):
    r = pl.program_id(0)

    def chunk_copy(c):
        start = c * TOK_TILE
        rows_c = pl.ds(start if isinstance(c, int) else pl.multiple_of(start, TOK_TILE), TOK_TILE)
        return pltpu.make_async_copy(x_hbm.at[rows_c, :], x_ref.at[rows_c, :], sem_ref.at[c])

    def arrive(c):
        @pl.when(seen_ref[c] == 0)
        def _():
            chunk_copy(c).wait()
            seen_ref[c] = 1

    @pl.when(r == 0)
    def _():
        for c in range(N_TILES):
            seen_ref[c] = 0
            chunk_copy(c).start()

    o_ref[...] = jnp.zeros_like(o_ref)
    rows = (r * DISPATCH_TILE + lax.broadcasted_iota(jnp.int32, (DISPATCH_TILE, TOK_TILE), 0)).astype(F32)

    def body(k, carry):
        c = clo_ref[r] + k
        arrive(c)
        it = info_t_ref[c]
        pos1 = _sorted_pos(it[0:1, :], it[2:3, :], base_ref)
        pos2 = _sorted_pos(it[1:2, :], it[3:4, :], base_ref)
        hit = jnp.logical_or(rows == pos1, rows == pos2)
        sel = jnp.where(hit, 1.0, 0.0).astype(BF16)
        off = pl.multiple_of(c * TOK_TILE, TOK_TILE)
        o_ref[...] += jnp.dot(sel, x_ref[pl.ds(off, TOK_TILE), :], preferred_element_type=F32).astype(BF16)
        return carry

    lax.fori_loop(0, cn_ref[r], body, 0)

    @pl.when(r == pl.num_programs(0) - 1)
    def _():
        for c in range(N_TILES):
            arrive(c)


def _dispatch(c_lo, c_n, base, info_t, h2):
    return pl.pallas_call(
        _dispatch_kernel,
        out_shape=jax.ShapeDtypeStruct((SORT_ROWS, D_MODEL), BF16),
        grid_spec=pltpu.PrefetchScalarGridSpec(
            num_scalar_prefetch=3,
            grid=(SORT_ROWS // DISPATCH_TILE,),
            in_specs=[pl.BlockSpec((N_TILES, 8, TOK_TILE), lambda r, *_: (0, 0, 0)),
                      pl.BlockSpec(memory_space=pl.ANY)],
            out_specs=pl.BlockSpec((DISPATCH_TILE, D_MODEL), lambda r, *_: (r, 0)),
            scratch_shapes=[pltpu.VMEM((N_TOK, D_MODEL), BF16),
                            pltpu.SemaphoreType.DMA((N_TILES,)),
                            pltpu.SMEM((N_TILES,), jnp.int32)]),
        compiler_params=_cparams(1),
        name="moe_dispatch",
    )(c_lo, c_n, base, info_t, h2)


TILE_PARTS = 4
CAST_ROWS = 64


def _expert_weight_ring(c, r, n_c, te_ref, tf_ref, nx_ref, copies, cast):
    @pl.when(jnp.logical_and(c == 0, r == 0))
    def _():
        for cp in copies(te_ref[0], 0):
            cp.start()

    @pl.when(tf_ref[r] == 1)
    def _():
        for cp in copies(te_ref[r], c):
            cp.wait()
        cast()
        nr = nx_ref[r]
        nc = c + (nr <= r).astype(jnp.int32)

        @pl.when(nc < n_c)
        def _():
            for cp in copies(te_ref[nr], nc):
                cp.start()


def _moe_up_kernel(te_ref, tv_ref, tf_ref, nx_ref, x_ref, w_hbm, h_ref, wst_ref, wbf_ref, sem_ref, *, i_moe):
    f = pl.program_id(0)
    r = pl.program_id(1)

    def copies(e, ft):
        out = []
        for part in range(2):
            col = pl.multiple_of((part * UP_TILES + ft) * UP_TILE, LANES)
            out.append(pltpu.make_async_copy(w_hbm.at[i_moe, e, :, pl.ds(col, UP_TILE)],
                                             wst_ref.at[part], sem_ref.at[part]))
        return out

    def cast():
        def rows_block(b, carry):
            rows = pl.ds(pl.multiple_of(b * CAST_ROWS, CAST_ROWS), CAST_ROWS)
            wbf_ref[:, rows, :] = wst_ref[:, rows, :].astype(BF16)
            return carry

        lax.fori_loop(0, D_MODEL // CAST_ROWS, rows_block, 0)

    _expert_weight_ring(f, r, UP_TILES, te_ref, tf_ref, nx_ref, copies, cast)

    def hidden(x):
        gt = jnp.dot(x, wbf_ref[0], preferred_element_type=F32)
        up = jnp.dot(x, wbf_ref[1], preferred_element_type=F32)
        return (gt * _sigmoid(gt) * up).astype(BF16)

    for parts in range(TILE_PARTS + 1):
        rows = parts * (ROW_TILE // TILE_PARTS)

        @pl.when(tv_ref[r] == parts)
        def _(rows=rows):
            if rows:
                h_ref[0:rows, :] = hidden(x_ref[0:rows, :])
            if rows < ROW_TILE:
                h_ref[rows:, :] = jnp.zeros((ROW_TILE - rows, UP_TILE), BF16)


def _moe_up(i_moe, tile_e, tile_valid, tile_first, tile_next, xs, w13):
    return pl.pallas_call(
        functools.partial(_moe_up_kernel, i_moe=i_moe),
        out_shape=jax.ShapeDtypeStruct((SORT_ROWS, D_FF_EXPERT), BF16),
        grid_spec=pltpu.PrefetchScalarGridSpec(
            num_scalar_prefetch=4,
            grid=(UP_TILES, SORT_TILES),
            in_specs=[pl.BlockSpec((ROW_TILE, D_MODEL), lambda f, r, *_: (r, 0)),
                      pl.BlockSpec(memory_space=pl.ANY)],
            out_specs=pl.BlockSpec((ROW_TILE, UP_TILE), lambda f, r, *_: (r, f)),
            scratch_shapes=[pltpu.VMEM((2, D_MODEL, UP_TILE), F32),
                            pltpu.VMEM((2, D_MODEL, UP_TILE), BF16),
                            pltpu.SemaphoreType.DMA((2,))]),
        compiler_params=_cparams(2),
        name="moe_up",
    )(tile_e, tile_valid, tile_first, tile_next, xs, w13)


def _moe_down_kernel(te_ref, tv_ref, tf_ref, nx_ref, h_ref, w_hbm, y_ref, wst_ref, wbf_ref, sem_ref, *, i_moe):
    n = pl.program_id(0)
    r = pl.program_id(1)

    def copies(e, nt):
        col = pl.multiple_of(nt * DOWN_TILE, LANES)
        return [pltpu.make_async_copy(w_hbm.at[i_moe, e, :, pl.ds(col, DOWN_TILE)], wst_ref, sem_ref.at[0])]

    def cast():
        wbf_ref[...] = wst_ref[...].astype(BF16)

    _expert_weight_ring(n, r, D_MODEL // DOWN_TILE, te_ref, tf_ref, nx_ref, copies, cast)

    for parts in range(TILE_PARTS + 1):
        rows = parts * (ROW_TILE // TILE_PARTS)

        @pl.when(tv_ref[r] == parts)
        def _(rows=rows):
            if rows:
                y_ref[0:rows, :] = jnp.dot(h_ref[0:rows, :], wbf_ref[...],
                                           preferred_element_type=F32).astype(BF16)
            if rows < ROW_TILE:
                y_ref[rows:, :] = jnp.zeros((ROW_TILE - rows, DOWN_TILE), BF16)


def _moe_down(i_moe, tile_e, tile_valid, tile_first, tile_next, hs, w2):
    return pl.pallas_call(
        functools.partial(_moe_down_kernel, i_moe=i_moe),
        out_shape=jax.ShapeDtypeStruct((SORT_ROWS, D_MODEL), BF16),
        grid_spec=pltpu.PrefetchScalarGridSpec(
            num_scalar_prefetch=4,
            grid=(D_MODEL // DOWN_TILE, SORT_TILES),
            in_specs=[pl.BlockSpec((ROW_TILE, D_FF_EXPERT), lambda n, r, *_: (r, 0)),
                      pl.BlockSpec(memory_space=pl.ANY)],
            out_specs=pl.BlockSpec((ROW_TILE, DOWN_TILE), lambda n, r, *_: (r, n)),
            scratch_shapes=[pltpu.VMEM((D_FF_EXPERT, DOWN_TILE), F32),
                            pltpu.VMEM((D_FF_EXPERT, DOWN_TILE), BF16),
                            pltpu.SemaphoreType.DMA((1,))]),
        compiler_params=_cparams(2),
        name="moe_down",
    )(tile_e, tile_valid, tile_first, tile_next, hs, w2)


def _window_copy(y_hbm, ybuf_ref, sem_ref, src, buf, slot):
    return pltpu.make_async_copy(y_hbm.at[pl.ds(src, WIN), :],
                                 ybuf_ref.at[buf, pl.ds(slot * WIN, WIN), :], sem_ref.at[buf, slot])


def _start_windows(src_ref, y_hbm, ybuf_ref, sem_ref, tile, buf):
    for s in range(N_WIN):
        src = pl.multiple_of(src_ref[tile * N_WIN + s], BF16_SUBLANES)
        _window_copy(y_hbm, ybuf_ref, sem_ref, src, buf, s).start()


def _combine_kernel(src_ref, lo_ref, hi_ref, base_ref, info_ref, x_ref, mod_ref, g_ref, y_hbm,
                    yp_ref, ys_ref, ybuf_ref, sem_ref):
    j = pl.program_id(0)
    buf = j % 2

    @pl.when(j == 0)
    def _():
        _start_windows(src_ref, y_hbm, ybuf_ref, sem_ref, 0, 0)

    @pl.when(j + 1 < N_TILES)
    def _():
        _start_windows(src_ref, y_hbm, ybuf_ref, sem_ref, j + 1, 1 - buf)

    info = info_ref[...]
    pos1 = _sorted_pos(info[:, 0:1], info[:, 2:3], base_ref)
    pos2 = _sorted_pos(info[:, 1:2], info[:, 3:4], base_ref)
    g1, g2 = info[:, 4:5], info[:, 5:6]
    iota = lax.broadcasted_iota(jnp.int32, (1, WIN), 1)
    ids = []
    for s in range(N_WIN):
        row = src_ref[j * N_WIN + s] + iota
        ok = jnp.logical_and(row >= lo_ref[j * N_WIN + s], row < hi_ref[j * N_WIN + s])
        ids.append(jnp.where(ok, row, -1))
    row_id = jnp.concatenate(ids, axis=1).astype(F32)

    for s in range(N_WIN):
        _window_copy(y_hbm, ybuf_ref, sem_ref, 0, buf, s).wait()
    halves = []
    for rows in (slice(0, TOK_TILE // 2), slice(TOK_TILE // 2, TOK_TILE)):
        sel = (jnp.where(pos1[rows] == row_id, g1[rows], 0.0)
               + jnp.where(pos2[rows] == row_id, g2[rows], 0.0)).astype(BF16)
        ffn = jnp.dot(sel, ybuf_ref[buf], preferred_element_type=F32)
        halves.append(x_ref[rows, :] + mod_ref[5:6, :] * _rms(ffn, g_ref[3:4, :]))
    out = jnp.concatenate(halves, axis=0)

    @pl.when(j < P_TILES)
    def _():
        yp_ref[...] = out

    @pl.when(j >= P_TILES)
    def _():
        ys_ref[...] = out


def _combine(l, win_src, win_lo, win_hi, base, info, x, mod, norm_g, ys):
    return pl.pallas_call(
        _combine_kernel,
        out_shape=(jax.ShapeDtypeStruct((N_PROMPT, D_MODEL), F32),
                   jax.ShapeDtypeStruct((N_SAMPLE, D_MODEL), F32)),
        grid_spec=pltpu.PrefetchScalarGridSpec(
            num_scalar_prefetch=4,
            grid=(N_TILES,),
            in_specs=[pl.BlockSpec((TOK_TILE, LANES), lambda i, *_: (i, 0)),
                      pl.BlockSpec((TOK_TILE, D_MODEL), lambda i, *_: (i, 0)),
                      pl.BlockSpec((None, None, 6, D_MODEL), lambda i, *_: (l, _cond_id(i), 0, 0)),
                      pl.BlockSpec((None, 4, D_MODEL), lambda i, *_: (l, 0, 0)),
                      pl.BlockSpec(memory_space=pl.ANY)],
            out_specs=(pl.BlockSpec((TOK_TILE, D_MODEL), lambda i, *_: (jnp.minimum(i, P_TILES - 1), 0)),
                       pl.BlockSpec((TOK_TILE, D_MODEL), lambda i, *_: (jnp.maximum(i - P_TILES, 0), 0))),
            scratch_shapes=[pltpu.VMEM((2, N_WIN * WIN, D_MODEL), BF16),
                            pltpu.SemaphoreType.DMA((2, N_WIN))]),
        compiler_params=_cparams(1),
        name="moe_combine",
    )(win_src, win_lo, win_hi, base, info, x, mod, norm_g, ys)


def _moe_ffn(l, i_moe, h2, x, mod, norm_g, routing, moe_w13, moe_w2):
    info, info_t, cb, tot = routing

    ar8 = jnp.arange(N_EXPERTS, dtype=jnp.int32)

    def at8(vec, idx):
        return jnp.sum(jnp.where(idx[..., None] == ar8, vec, 0), axis=-1)

    def cumsum8(a):
        return jnp.sum(jnp.where(ar8[None, :] <= ar8[:, None], a[..., None, :], 0), axis=-1)

    counts = tot[0, :N_EXPERTS].astype(jnp.int32)
    padded = (counts + ROW_TILE - 1) // ROW_TILE * ROW_TILE
    seg_end = cumsum8(padded)
    base = (seg_end - padded).astype(jnp.int32)
    cbx = cb[:, 0, :N_EXPERTS].astype(jnp.int32)
    cb_end = jnp.concatenate([cbx[1:], counts[None, :]], axis=0)

    def tile_segments(n_tiles, rows):
        row0 = jnp.arange(n_tiles, dtype=jnp.int32) * rows
        e = jnp.minimum(jnp.sum(seg_end[None, :] <= row0[:, None], axis=1), N_EXPERTS - 1).astype(jnp.int32)
        k0 = row0 - at8(base, e)
        return e, k0, jnp.logical_and(k0 >= 0, k0 < at8(counts, e))

    tile_e, k0, tile_valid = tile_segments(SORT_TILES, ROW_TILE)
    tile_first = jnp.logical_and(tile_valid, k0 == 0)
    d_e, d_k0, d_valid = tile_segments(SORT_ROWS // DISPATCH_TILE, DISPATCH_TILE)
    d_kend = jnp.minimum(d_k0 + DISPATCH_TILE, at8(counts, d_e))
    cb_t = jnp.sum(jnp.where(d_e[:, None, None] == ar8[None, :, None], cbx.T[None, :, :], 0), axis=1)
    c_lo = jnp.sum(cb_t <= d_k0[:, None], axis=1) - 1
    c_hi = jnp.sum(cb_t < d_kend[:, None], axis=1) - 1
    c_n = jnp.where(d_valid, c_hi - c_lo + 1, 0).astype(jnp.int32)
    c_lo = jnp.where(d_valid, c_lo, 0).astype(jnp.int32)
    tile_idx = jnp.arange(SORT_TILES, dtype=jnp.int32)
    first_idx = jnp.where(tile_first, tile_idx, SORT_TILES)
    later = jnp.where(tile_idx[None, :] > tile_idx[:, None], first_idx[None, :], SORT_TILES)
    tile_next = jnp.min(later, axis=1)
    tile_next = jnp.where(tile_next >= SORT_TILES, 0, tile_next).astype(jnp.int32)
    tile_rows = at8(counts, tile_e) - k0
    part_rows = ROW_TILE // TILE_PARTS
    tile_valid = jnp.where(tile_valid, jnp.clip((tile_rows + part_rows - 1) // part_rows, 1, TILE_PARTS),
                           0).astype(jnp.int32)
    tile_first = tile_first.astype(jnp.int32)

    seg_lo = base[None, :] + cbx
    seg_hi = base[None, :] + cb_end
    seg_n = seg_hi - seg_lo
    w0 = seg_lo // BF16_SUBLANES * BF16_SUBLANES
    nw = jnp.where(seg_n > 0, (seg_lo - w0 + seg_n + WIN - 1) // WIN, 0)
    nw_end = cumsum8(nw)
    slot = jnp.arange(N_WIN, dtype=jnp.int32)
    slot_e = jnp.minimum(jnp.sum(nw_end[:, None, :] <= slot[None, :, None], axis=2), N_EXPERTS - 1)
    pick = slot_e[:, :, None] == jnp.arange(N_EXPERTS, dtype=jnp.int32)[None, None, :]
    take = lambda a: jnp.sum(jnp.where(pick, a[:, None, :], 0), axis=2)
    slot_k = slot[None, :] - (take(nw_end) - take(nw))
    slot_ok = slot[None, :] < nw_end[:, -1:]
    slot_src = take(w0) + WIN * slot_k
    win_src = jnp.where(slot_ok, slot_src, 0).astype(jnp.int32).reshape(-1)
    win_lo = jnp.where(slot_ok, take(seg_lo), 0).astype(jnp.int32).reshape(-1)
    win_hi = jnp.where(slot_ok, take(seg_hi), 0).astype(jnp.int32).reshape(-1)

    xs = _dispatch(c_lo, c_n, base, info_t, h2)
    hs = _moe_up(i_moe, tile_e, tile_valid, tile_first, tile_next, xs, moe_w13)
    ys = _moe_down(i_moe, tile_e, tile_valid, tile_first, tile_next, hs, moe_w2)
    return _combine(l, win_src, win_lo, win_hi, base, info, x, mod, norm_g, ys)


def kernel(x_prompt, x_sample, cache_k, cache_v, c, c_ctx, w_ada, b_ada, norm_g, w_in, w_out,
           lam_params, subln_g, dw_weight, dw_bias, conv_ln_g, conv_ln_b, dense_w13, dense_w2,
           router_w, moe_w13, moe_w2):
    x = (x_prompt.reshape(N_PROMPT, D_MODEL), x_sample.reshape(N_SAMPLE, D_MODEL))
    cond8 = jnp.zeros((8, D_MODEL), F32).at[0].set(c_ctx).at[1:1 + DEC_BATCH].set(c)
    mod = _ada_modulation(cond8, w_ada, b_ada).reshape(DEPTH, 8, 6, D_MODEL)
    rope = _rope_tables()
    ck = cache_k.reshape(DEC_BATCH, DEPTH, PAST_LEN * ATT_HEADS, V_DIM)
    cv = cache_v.reshape(DEC_BATCH, DEPTH, PAST_LEN * ATT_HEADS, V_DIM)

    caches = None
    for l in range(DEPTH):
        lam_init = 0.8 - 0.6 * math.exp(-0.3 * l)
        qkv, cvo, *caches = _mixer_in(l, x, mod, norm_g, w_in, rope,
                                      (dw_weight, dw_bias, conv_ln_g, conv_ln_b), caches)
        o_ctx, o_lat = _attention(l, lam_init, lam_params, subln_g, qkv, ck, cv)
        i = l // 2
        if l % 2 == 0:
            (x,) = _outproj(l, o_ctx, o_lat, cvo, x, mod, norm_g, w_out,
                            dense_w=(dense_w13[i].astype(BF16), dense_w2[i].astype(BF16)))
        else:
            x, h2, *routing = _outproj(l, o_ctx, o_lat, cvo, x, mod, norm_g, w_out, router_w=router_w[i])
            x = _moe_ffn(l, i, h2, x, mod, norm_g, routing, moe_w13, moe_w2)

    xp, xs = x if isinstance(x, tuple) else (x[:N_PROMPT], x[N_PROMPT:])
    new_k, new_v = (a.reshape(BATCH, DEPTH, SEQ, ATT_HEADS, V_DIM) for a in caches)
    return (xp.reshape(BATCH, SEQ, D_MODEL), xs.reshape(DEC_BATCH, DEC_SEQ, D_MODEL), new_k, new_v)
```
